```python
import math
import jax
import jax.numpy as jnp
from jax import lax
import numpy as np

D_MODEL = 1024
BATCH = 32
SEQ = 256
DEPTH = 4
DEC_BATCH = 8
DEC_SEQ = 4096
PAST_LEN = 512

GRID_W = 64
EPS = 1e-6
N_HEADS = 8
N_KV_HEADS = 2
Q_PER_KV = N_HEADS // N_KV_HEADS
HEAD_DIM = 128
ATTN_Q = N_HEADS * HEAD_DIM
ATTN_KV = N_KV_HEADS * HEAD_DIM
ROPE_THETA = 10000.0
Q_BLOCK = 128
CONV_WIDTH = 512
CONV_K = 3
DN_HEADS = 4
DN_DK = 128
DN_DV = 128
DN_QK = DN_HEADS * DN_DK
DN_VW = DN_HEADS * DN_DV
DN_CONV_K = 3
DN_CHUNK = 64
D_FF = (((8 * D_MODEL + 2) // 3 + 255) // 256) * 256
IN_SIZES = (CONV_WIDTH, CONV_WIDTH, CONV_WIDTH,
            ATTN_Q, ATTN_KV, ATTN_KV,
            DN_QK, DN_QK, DN_VW, DN_VW, 2 * DN_HEADS, 2 * DN_HEADS,
            3 * D_MODEL)
D_IN = sum(IN_SIZES)
SPLIT_IDX = tuple(int(s) for s in np.cumsum(IN_SIZES)[:-1])

kernel_name = 'hybrid_diffusion_trunk_step'


def rms_norm(x, g):
    xf = x.astype(jnp.float32)
    y = xf * lax.rsqrt(jnp.mean(xf * xf, axis=-1, keepdims=True) + EPS)
    return (y * g.astype(jnp.float32)).astype(x.dtype)


def l2_normalize(x):
    return x * lax.rsqrt(jnp.sum(x * x, axis=-1, keepdims=True) + EPS)


def depthwise_conv(x, w):
    pad = w.shape[0] // 2
    return lax.conv_general_dilated(
        x, w[:, None, :].astype(x.dtype), window_strides=(1,), padding=((pad, pad),),
        dimension_numbers=('NWC', 'WIO', 'NWC'), feature_group_count=x.shape[-1])


def axial_rope_angles(rows):
    row_id = jnp.repeat(jnp.arange(rows, dtype=jnp.float32), GRID_W)
    col_id = jnp.tile(jnp.arange(GRID_W, dtype=jnp.float32), rows)
    n_freq = HEAD_DIM // 4
    inv_freq = ROPE_THETA ** (-jnp.arange(n_freq, dtype=jnp.float32) / n_freq)
    ang = jnp.concatenate([row_id[:, None] * inv_freq, col_id[:, None] * inv_freq], axis=-1)
    return jnp.cos(ang), jnp.sin(ang)


def apply_rope(x, cos, sin):
    xf = x.astype(jnp.float32).reshape(*x.shape[:-1], HEAD_DIM // 2, 2)
    x1, x2 = xf[..., 0], xf[..., 1]
    c, s = cos[:, None, :], sin[:, None, :]
    out = jnp.stack([x1 * c - x2 * s, x1 * s + x2 * c], axis=-1).reshape(x.shape)
    return out.astype(x.dtype)


def block_attention(q, k, v):
    b, tq = q.shape[:2]
    nb = tq // Q_BLOCK
    qb = jnp.moveaxis(q.reshape(b, nb, Q_BLOCK, N_KV_HEADS, Q_PER_KV, HEAD_DIM), 1, 0)
    scale = HEAD_DIM ** -0.5

    def one_block(qi):
        s = jnp.einsum('bqhgd,bshd->bhgqs', qi, k, preferred_element_type=jnp.float32) * scale
        p = jax.nn.softmax(s, axis=-1)
        return jnp.einsum('bhgqs,bshd->bqhgd', p.astype(v.dtype), v)

    o = lax.map(one_block, qb)
    return jnp.moveaxis(o, 0, 1).reshape(b, tq, ATTN_Q)


def chunked_gated_delta(q, k, v, g, beta, s0):
    b, t, h, _ = k.shape
    n = t // DN_CHUNK

    def chunks(a):
        a = a.reshape(b, n, DN_CHUNK, h, *a.shape[3:])
        return jnp.moveaxis(jnp.moveaxis(a, 1, 0), 3, 2)

    qc, kc, vc, gc, bc = chunks(q), chunks(k), chunks(v), chunks(g), chunks(beta)
    G = jnp.cumsum(gc, axis=-1)
    idx = jnp.arange(DN_CHUNK)
    lower_incl = idx[:, None] >= idx[None, :]
    strict = (idx[:, None] > idx[None, :]).astype(jnp.float32)
    decay = jnp.exp(jnp.where(lower_incl, G[..., :, None] - G[..., None, :], -jnp.inf))
    kb = kc * bc[..., None]
    lmat = jnp.einsum('nbhik,nbhjk->nbhij', kb, kc) * decay * strict
    amat = lmat + jnp.eye(DN_CHUNK, dtype=jnp.float32)
    rhs = jnp.concatenate([vc * bc[..., None], kb * jnp.exp(G)[..., None]], axis=-1)
    sol = lax.linalg.triangular_solve(amat, rhs, left_side=True, lower=True, unit_diagonal=True)
    u, w = sol[..., :DN_DV], sol[..., DN_DV:]

    def step(S, inp):
        q_i, k_i, u_i, w_i, G_i, dec_i = inp
        v_new = u_i - jnp.einsum('bhck,bhkv->bhcv', w_i, S)
        attn = jnp.einsum('bhik,bhjk->bhij', q_i, k_i) * dec_i
        o = (jnp.einsum('bhck,bhkv->bhcv', q_i * jnp.exp(G_i)[..., None], S)
             + jnp.einsum('bhij,bhjv->bhiv', attn, v_new))
        g_last = G_i[..., -1:]
        S = (S * jnp.exp(g_last)[..., None]
             + jnp.einsum('bhck,bhcv->bhkv', k_i * jnp.exp(g_last - G_i)[..., None], v_new))
        return S, o

    s_fin, o = lax.scan(step, s0, (qc, kc, u, w, G, decay))
    o = jnp.moveaxis(jnp.moveaxis(o, 2, 3), 0, 1).reshape(b, t, h, DN_DV)
    return o, s_fin


def gated_deltanet_bidir(q_raw, k_raw, v_raw, z, beta_logit, a_logit, conv_w, a_log, dt_bias, norm_g, s0):
    b, t, _ = q_raw.shape
    qkv = jax.nn.silu(depthwise_conv(jnp.concatenate([q_raw, k_raw, v_raw], axis=-1), conv_w))
    q, k, v = jnp.split(qkv.astype(jnp.float32), 3, axis=-1)
    q = l2_normalize(q.reshape(b, t, DN_HEADS, DN_DK)) * (DN_DK ** -0.5)
    k = l2_normalize(k.reshape(b, t, DN_HEADS, DN_DK))
    v = v.reshape(b, t, DN_HEADS, DN_DV)
    beta = jax.nn.sigmoid(beta_logit.astype(jnp.float32)).reshape(b, t, 2, DN_HEADS)
    g = -jnp.exp(a_log.astype(jnp.float32)) * jax.nn.softplus(
        a_logit.astype(jnp.float32).reshape(b, t, 2, DN_HEADS) + dt_bias.astype(jnp.float32))
    s0 = s0.astype(jnp.float32)
    o_f, s_f = chunked_gated_delta(q, k, v, g[:, :, 0], beta[:, :, 0], s0[:, 0])
    rev = lambda a: jnp.flip(a, axis=1)
    o_b, s_b = chunked_gated_delta(rev(q), rev(k), rev(v), rev(g[:, :, 1]), rev(beta[:, :, 1]), s0[:, 1])
    o = o_f + rev(o_b)
    o = rms_norm(o, norm_g) * jax.nn.silu(z.astype(jnp.float32).reshape(b, t, DN_HEADS, DN_DV))
    return o.reshape(b, t, DN_VW).astype(q_raw.dtype), jnp.stack([s_f, s_b], axis=1)


def trunk_layer(x, cvec, lp, rope=None, ctx=None):
    b, t, _ = x.shape
    mod = jax.nn.silu(cvec) @ lp['w_mod'] + lp['b_mod']
    sh1, sc1, gt1, sh2, sc2, gt2 = jnp.split(mod[:, None, :], 6, axis=-1)
    h = rms_norm(x, lp['g_pre1']) * (1 + sc1) + sh1
    (cb, cc, cx, aq, ak, av, dn_q, dn_k, dn_v, dn_z, dn_beta, dn_a, gate_logits) = jnp.split(
        h @ lp['w_in'], SPLIT_IDX, axis=-1)

    ya = (cb * depthwise_conv(cc * cx, lp['conv_w'])) @ lp['w_pa']

    q = rms_norm(aq.reshape(b, t, N_HEADS, HEAD_DIM), lp['g_qn'])
    k = rms_norm(ak.reshape(b, t, N_KV_HEADS, HEAD_DIM), lp['g_kn'])
    v = av.reshape(b, t, N_KV_HEADS, HEAD_DIM)
    if ctx is None:
        keys, vals = k, v
        s0 = jnp.zeros((b, 2, DN_HEADS, DN_DK, DN_DV), jnp.float32)
    else:
        cos, sin = rope
        q = apply_rope(q, cos, sin)
        k = apply_rope(k, cos, sin)
        keys = jnp.concatenate([ctx[0].astype(k.dtype), k], axis=1)
        vals = jnp.concatenate([ctx[1].astype(v.dtype), v], axis=1)
        s0 = ctx[2]
    yb = block_attention(q.reshape(b, t, N_KV_HEADS, Q_PER_KV, HEAD_DIM), keys, vals) @ lp['w_pb']

    yc, s_fin = gated_deltanet_bidir(dn_q, dn_k, dn_v, dn_z, dn_beta, dn_a, lp['dn_conv_w'],
                                     lp['dn_a_log'], lp['dn_dt_bias'], lp['dn_norm_g'], s0)
    yc = yc @ lp['w_pc']

    ga, gb, gc = jnp.split(jax.nn.sigmoid(gate_logits), 3, axis=-1)
    mix = (ga * ya + gb * yb + gc * yc) @ lp['w_o']
    x = x + gt1 * rms_norm(mix, lp['g_post1'])

    h2 = rms_norm(x, lp['g_pre2']) * (1 + sc2) + sh2
    ffn = (jax.nn.silu(h2 @ lp['w_gate']) * (h2 @ lp['w_up'])) @ lp['w_down']
    x = x + gt2 * rms_norm(ffn, lp['g_post2'])
    return x, (k, v, s_fin.astype(x.dtype))


def setup_inputs(seed: int = 0) -> dict:
    key = jax.random.key(seed)
    ks = jax.random.split(key, 32)
    f32 = jnp.float32
    D = D_MODEL

    def nrm(k, shape, scale):
        return jax.random.normal(k, shape, f32) * scale

    def gain(k, shape):
        return 1.0 + 0.05 * jax.random.normal(k, shape, f32)

    dt = jnp.exp(jax.random.uniform(ks[20], (DEPTH, 2, DN_HEADS), f32, math.log(1e-3), math.log(1e-1)))
    return {
        'x_prompt': nrm(ks[0], (BATCH, SEQ, D), 1.0),
        'x_sample': nrm(ks[1], (DEC_BATCH, DEC_SEQ, D), 1.0),
        'cache_k': nrm(ks[2], (DEC_BATCH, DEPTH, PAST_LEN, N_KV_HEADS, HEAD_DIM), 1.0),
        'cache_v': nrm(ks[3], (DEC_BATCH, DEPTH, PAST_LEN, N_KV_HEADS, HEAD_DIM), 1.0),
        'state_dn': nrm(ks[4], (DEC_BATCH, DEPTH, 2, DN_HEADS, DN_DK, DN_DV), 0.1),
        'c': nrm(ks[5], (DEC_BATCH, D), 1.0),
        'c_ctx': nrm(ks[6], (D,), 1.0),
        'w_mod': nrm(ks[7], (DEPTH, D, 6 * D), D ** -0.5),
        'b_mod': nrm(ks[8], (DEPTH, 6 * D), 0.02),
        'g_pre1': gain(ks[9], (DEPTH, D)),
        'g_post1': gain(ks[10], (DEPTH, D)),
        'g_pre2': gain(ks[11], (DEPTH, D)),
        'g_post2': gain(ks[12], (DEPTH, D)),
        'w_in': nrm(ks[13], (DEPTH, D, D_IN), D ** -0.5),
        'conv_w': nrm(ks[14], (DEPTH, CONV_K, CONV_WIDTH), CONV_K ** -0.5),
        'g_qn': gain(ks[15], (DEPTH, HEAD_DIM)),
        'g_kn': gain(ks[16], (DEPTH, HEAD_DIM)),
        'dn_conv_w': nrm(ks[17], (DEPTH, DN_CONV_K, 2 * DN_QK + DN_VW), DN_CONV_K ** -0.5),
        'dn_a_log': jnp.log(jax.random.uniform(ks[18], (DEPTH, 2, DN_HEADS), f32, 1.0, 16.0)),
        'dn_dt_bias': dt + jnp.log(-jnp.expm1(-dt)),
        'dn_norm_g': gain(ks[19], (DEPTH, DN_DV)),
        'w_pa': nrm(ks[21], (DEPTH, CONV_WIDTH, D), CONV_WIDTH ** -0.5),
        'w_pb': nrm(ks[22], (DEPTH, ATTN_Q, D), ATTN_Q ** -0.5),
        'w_pc': nrm(ks[23], (DEPTH, DN_VW, D), DN_VW ** -0.5),
        'w_o': nrm(ks[24], (DEPTH, D, D), D ** -0.5),
        'w_gate': nrm(ks[25], (DEPTH, D, D_FF), D ** -0.5),
        'w_up': nrm(ks[26], (DEPTH, D, D_FF), D ** -0.5),
        'w_down': nrm(ks[27], (DEPTH, D_FF, D), D_FF ** -0.5),
    }


def reference(x_prompt, x_sample, cache_k, cache_v, state_dn, c, c_ctx, w_mod, b_mod,
              g_pre1, g_post1, g_pre2, g_post2, w_in, conv_w, g_qn, g_kn, dn_conv_w,
              dn_a_log, dn_dt_bias, dn_norm_g, w_pa, w_pb, w_pc, w_o, w_gate, w_up, w_down):
    rows = x_sample.shape[1] // GRID_W
    rope = axial_rope_angles(rows)
    y_prompt, y_sample = x_prompt, x_sample
    ks, vs, ss = [], [], []
    for l in range(DEPTH):
        lp = {'w_mod': w_mod[l], 'b_mod': b_mod[l], 'g_pre1': g_pre1[l], 'g_post1': g_post1[l],
              'g_pre2': g_pre2[l], 'g_post2': g_post2[l], 'w_in': w_in[l], 'conv_w': conv_w[l],
              'g_qn': g_qn[l], 'g_kn': g_kn[l], 'dn_conv_w': dn_conv_w[l], 'dn_a_log': dn_a_log[l],
              'dn_dt_bias': dn_dt_bias[l], 'dn_norm_g': dn_norm_g[l], 'w_pa': w_pa[l], 'w_pb': w_pb[l],
              'w_pc': w_pc[l], 'w_o': w_o[l], 'w_gate': w_gate[l], 'w_up': w_up[l], 'w_down': w_down[l]}
        y_prompt, (k_l, v_l, s_l) = trunk_layer(y_prompt, c_ctx[None, :], lp)
        ks.append(k_l)
        vs.append(v_l)
        ss.append(s_l)
        y_sample, _ = trunk_layer(y_sample, c, lp, rope=rope,
                                  ctx=(cache_k[:, l], cache_v[:, l], state_dn[:, l]))
    new_k = jnp.stack(ks, axis=1)
    new_v = jnp.stack(vs, axis=1)
    new_state_dn = jnp.stack(ss, axis=1)
    return (y_prompt, y_sample, new_k, new_v, new_state_dn)
```

```python
import functools

import jax
import jax.numpy as jnp
from jax import lax
from jax.experimental import pallas as pl
from jax.experimental.pallas import tpu as pltpu

F32 = jnp.float32
BF16 = jnp.bfloat16

D_MODEL = 1024
EPS = 1e-6
GRID_W = 64
N_HEADS = 8
N_KV_HEADS = 2
Q_PER_KV = N_HEADS // N_KV_HEADS
HEAD_DIM = 128
ATTN_Q = N_HEADS * HEAD_DIM
ATTN_KV = N_KV_HEADS * HEAD_DIM
ROPE_THETA = 10000.0
CONV_WIDTH = 512
DN_HEADS = 4
DN_DK = 128
DN_DV = 128
DN_QK = DN_HEADS * DN_DK
DN_VW = DN_HEADS * DN_DV
D_FF = 2816

SUBLANES = 8
LANES = 128

COL_Q = 0
COL_CONV = ATTN_Q
COL_K = COL_CONV + 3 * CONV_WIDTH
COL_V = COL_K + ATTN_KV
COL_DN = COL_V + ATTN_KV
COL_GATE = COL_DN + 4 * DN_QK
COL_SMALL = COL_GATE + 3 * D_MODEL
D_INP = COL_SMALL + LANES
ORIG_CONV, ORIG_Q, ORIG_K = 0, 3 * CONV_WIDTH, 3 * CONV_WIDTH + ATTN_Q
ORIG_SMALL = ORIG_K + 2 * ATTN_KV + 4 * DN_QK
ORIG_GATE = ORIG_SMALL + 4 * DN_HEADS
D_IN = ORIG_GATE + 3 * D_MODEL

DN_CHUNK = 128
DN_BASE = 16
VMEM_LIMIT = 56 * 1024 * 1024


def _cparams(sem):
    return pltpu.CompilerParams(dimension_semantics=sem, vmem_limit_bytes=VMEM_LIMIT)


def _const_spec(shape):
    nd = len(shape)
    return pl.BlockSpec(shape, lambda *_: (0,) * nd, pipeline_mode=pl.Buffered(1))


def _dot(a, b):
    return jnp.dot(a, b, preferred_element_type=F32)


def _dot_nt(a, b):
    return lax.dot_general(a, b, (((1,), (1,)), ((), ())), preferred_element_type=F32)


def _dot_tn(a, b):
    return lax.dot_general(a, b, (((0,), (0,)), ((), ())), preferred_element_type=F32)


def _mm(a, b):
    return _dot(a.astype(BF16), b.astype(BF16))


def _rms(x, g):
    return x * lax.rsqrt(jnp.mean(x * x, axis=-1, keepdims=True) + EPS) * g


def _sigmoid(x):
    return 1.0 / (1.0 + jnp.exp(-x))


def _silu(x):
    return x * _sigmoid(x)


def _softplus(x):
    return jnp.maximum(x, 0.0) + jnp.log1p(jnp.exp(-jnp.abs(x)))


def _split3(x):
    hi = x.astype(BF16)
    r = x - hi.astype(F32)
    mid = r.astype(BF16)
    lo = (r - mid.astype(F32)).astype(BF16)
    return hi, mid, lo


def _mod_kernel(cv_ref, w_ref, b_ref, o_ref):
    cv = cv_ref[...]
    o_ref[0] = _dot(_silu(cv).astype(BF16), w_ref[0].astype(BF16)) + b_ref[0]


def _modulation(cv, w_mod, b_mod):
    depth, d, n6 = w_mod.shape
    rows = cv.shape[0]
    tn = 1536
    return pl.pallas_call(
        _mod_kernel,
        grid=(depth, n6 // tn),
        in_specs=[pl.BlockSpec((rows, d), lambda l, j: (0, 0)),
                  pl.BlockSpec((1, d, tn), lambda l, j: (l, 0, j)),
                  pl.BlockSpec((1, 1, tn), lambda l, j: (l, 0, j))],
        out_specs=pl.BlockSpec((1, rows, tn), lambda l, j: (l, 0, j)),
        out_shape=jax.ShapeDtypeStruct((depth, rows, n6), F32),
        compiler_params=_cparams(("arbitrary", "arbitrary")),
    )(cv, w_mod, b_mod.reshape(depth, 1, n6))


def _inproj_kernel(x_ref, mod_ref, g_ref, w_ref, o_ref, *, tn):
    y = _rms(x_ref[...], g_ref[...])
    h = (y * (1.0 + mod_ref[0, 1:2, :]) + mod_ref[0, 0:1, :]).astype(BF16)
    for j in range(0, D_INP, tn):
        o_ref[:, j:j + tn] = _dot(h, w_ref[:, j:j + tn])


def _inproj(x, mod3, mod_base, rows_per_mod, g_pre1, w_in_r):
    n = x.shape[0]
    tm = 256
    return pl.pallas_call(
        functools.partial(_inproj_kernel, tn=640),
        grid=(n // tm,),
        in_specs=[pl.BlockSpec((tm, D_MODEL), lambda i: (i, 0)),
                  pl.BlockSpec((1, 6, D_MODEL), lambda i: (mod_base + (i * tm) // rows_per_mod, 0, 0)),
                  _const_spec((1, D_MODEL)),
                  _const_spec((D_MODEL, D_INP))],
        out_specs=pl.BlockSpec((tm, D_INP), lambda i: (i, 0)),
        out_shape=jax.ShapeDtypeStruct((n, D_INP), F32),
        compiler_params=_cparams(("arbitrary",)),
    )(x, mod3, g_pre1, w_in_r)


def _swap_pairs(y):
    lane = lax.broadcasted_iota(jnp.int32, y.shape, 1)
    return jnp.where(lane % 2 == 0, pltpu.roll(y, LANES - 1, axis=1), pltpu.roll(y, 1, axis=1))


def _qkprep_kernel(*refs, rope):
    if rope:
        q_ref, k_ref, gq_ref, gk_ref, cos_ref, sin_ref, qo_ref, ko_ref = refs
        cos, sin = cos_ref[...], sin_ref[...]
    else:
        q_ref, k_ref, gq_ref, gk_ref, qo_ref, ko_ref = refs

    def head(x, g):
        y = _rms(x, g)
        if rope:
            y = y * cos + _swap_pairs(y) * sin
        return y

    for h in range(N_HEADS):
        sl = slice(h * HEAD_DIM, (h + 1) * HEAD_DIM)
        qo_ref[:, sl] = (head(q_ref[:, sl], gq_ref[...]) * (HEAD_DIM ** -0.5)).astype(qo_ref.dtype)
    for h in range(N_KV_HEADS):
        sl = slice(h * HEAD_DIM, (h + 1) * HEAD_DIM)
        ko_ref[:, sl] = head(k_ref[:, sl], gk_ref[...]).astype(ko_ref.dtype)


def _qkprep(proj, t, g_qn, g_kn, rope_tabs, k_dtype):
    n = proj.shape[0]
    tm = 256
    tps = t // tm
    in_specs = [pl.BlockSpec((tm, ATTN_Q), lambda i: (i, COL_Q // ATTN_Q)),
                pl.BlockSpec((tm, ATTN_KV), lambda i: (i, COL_K // ATTN_KV)),
                _const_spec((1, HEAD_DIM)), _const_spec((1, HEAD_DIM))]
    args = [proj, proj, g_qn, g_kn]
    if rope_tabs is not None:
        in_specs += [pl.BlockSpec((tm, HEAD_DIM), lambda i: (i % tps, 0))] * 2
        args += list(rope_tabs)
    return pl.pallas_call(
        functools.partial(_qkprep_kernel, rope=rope_tabs is not None),
        grid=(n // tm,),
        in_specs=in_specs,
        out_specs=[pl.BlockSpec((tm, ATTN_Q), lambda i: (i, 0)),
                   pl.BlockSpec((tm, ATTN_KV), lambda i: (i, 0))],
        out_shape=[jax.ShapeDtypeStruct((n, ATTN_Q), BF16),
                   jax.ShapeDtypeStruct((n, ATTN_KV), k_dtype)],
        compiler_params=_cparams(("arbitrary",)),
    )(*args)


def _attn_kernel(*refs, has_cache, t, tk, tq):
    if has_cache:
        q_ref, kc_ref, vc_ref, k_ref, v_ref, o_ref = refs
    else:
        q_ref, k_ref, v_ref, o_ref = refs
    q = q_ref[...]
    qs = jnp.concatenate([q[:, g * HEAD_DIM:(g + 1) * HEAD_DIM] for g in range(Q_PER_KV)], axis=0)
    rows = Q_PER_KV * tq

    def update(kb, vb, carry):
        m, l, acc = carry
        s = _dot_nt(qs, kb.astype(BF16))
        m_new = jnp.maximum(m, jnp.max(s, axis=-1, keepdims=True))
        alpha = jnp.exp(m - m_new)
        p = jnp.exp(s - m_new)
        l = alpha * l + jnp.sum(p, axis=-1, keepdims=True)
        acc = alpha * acc + _dot(p.astype(BF16), vb.astype(BF16))
        return m_new, l, acc

    carry = (jnp.full((rows, 1), -jnp.inf, F32), jnp.zeros((rows, 1), F32), jnp.zeros((rows, HEAD_DIM), F32))
    if has_cache:
        carry = update(kc_ref[...], vc_ref[...], carry)

    def body(c, carry):
        r0 = pl.multiple_of(c * tk, tk)
        return update(k_ref[pl.ds(r0, tk), :], v_ref[pl.ds(r0, tk), :], carry)

    _, l, acc = lax.fori_loop(0, t // tk, body, carry)
    o = acc / l
    for g in range(Q_PER_KV):
        o_ref[:, g * HEAD_DIM:(g + 1) * HEAD_DIM] = o[g * tq:(g + 1) * tq].astype(o_ref.dtype)


def _attention(q_p, k_p, proj, b, t, cache):
    n = q_p.shape[0]
    tq = 128 if t > 256 else 256
    tk = min(t, 512)
    nq = t // tq
    qw = Q_PER_KV * HEAD_DIM
    in_specs = [pl.BlockSpec((tq, qw), lambda bi, j, qi: (bi * nq + qi, j))]
    args = [q_p]
    if cache is not None:
        cache_k, cache_v, layer = cache
        past = cache_k.shape[2]
        cspec = pl.BlockSpec((None, None, past, HEAD_DIM), lambda bi, j, qi: (bi, layer, 0, j))
        in_specs += [cspec, cspec]
        args += [cache_k, cache_v]
    in_specs += [pl.BlockSpec((t, HEAD_DIM), lambda bi, j, qi: (bi, j)),
                 pl.BlockSpec((t, HEAD_DIM), lambda bi, j, qi: (bi, COL_V // HEAD_DIM + j))]
    args += [k_p, proj]
    return pl.pallas_call(
        functools.partial(_attn_kernel, has_cache=cache is not None, t=t, tk=tk, tq=tq),
        grid=(b, N_KV_HEADS, nq),
        in_specs=in_specs,
        out_specs=pl.BlockSpec((tq, qw), lambda bi, j, qi: (bi * nq + qi, j)),
        out_shape=jax.ShapeDtypeStruct((n, ATTN_Q), BF16),
        compiler_params=_cparams(("arbitrary", "arbitrary", "arbitrary")),
    )(*args)


def _lane_pick(x, lane):
    idx = lax.broadcasted_iota(jnp.int32, x.shape, 1)
    return jnp.sum(jnp.where(idx == lane, x, 0.0), axis=-1, keepdims=True)


def _unit_tri_inverse(lmat, ri, ci):
    def blk(s):
        return (ri ^ ci) < s

    eye = jnp.where(ri == ci, 1.0, 0.0)
    p = jnp.where(blk(DN_BASE), -lmat, 0.0)
    x = eye + p
    s = 2
    while s < DN_BASE:
        p = _mm(p, p)
        x = x + _mm(x, p)
        s *= 2
    s = DN_BASE
    while s < DN_CHUNK:
        off = jnp.where(blk(2 * s) & jnp.logical_not(blk(s)), lmat, 0.0)
        x = x - _mm(x, _mm(off, x))
        s *= 2
    return x


def _dn_kernel(*refs, reverse, has_s0, want_state, nc, d, nt):
    refs = list(refs)
    x_ref, xp_ref, xn_ref, cw_ref, gc_ref, gr_ref, ac_ref, bc_ref, ar_ref, br_ref = refs[:10]
    refs = refs[10:]
    s0_ref = refs.pop(0) if has_s0 else None
    o_ref = refs.pop(0)
    sfin_ref = refs.pop(0) if want_state else None
    s_sc = refs.pop(0)

    i = pl.program_id(1)
    tt = nc * DN_CHUNK
    ti = (nt - 1 - i) if reverse else i

    @pl.when(i == 0)
    def _():
        if has_s0:
            s_sc[...] = s0_ref[...]
        else:
            s_sc[...] = jnp.zeros_like(s_sc)

    x = x_ref[...]
    rows = lax.broadcasted_iota(jnp.int32, (tt, 1), 0)
    prev_row = jnp.where(ti == 0, 0.0, xp_ref[SUBLANES - 1:SUBLANES, :])
    next_row = jnp.where(ti == nt - 1, 0.0, xn_ref[0:1, :])
    x_prev = jnp.where(rows == 0, prev_row, pltpu.roll(x, 1, axis=0))
    x_next = jnp.where(rows == tt - 1, next_row, pltpu.roll(x, tt - 1, axis=0))
    cw = cw_ref[...]
    y = _silu(cw[0:1, :] * x_prev + cw[1:2, :] * x + cw[2:3, :] * x_next)

    gl = gc_ref[...]
    beta_c = _sigmoid(gl)
    g_c = -jnp.exp(ac_ref[...]) * _softplus(gl + bc_ref[...])
    g_r = -jnp.exp(ar_ref[...]) * _softplus(gr_ref[...] + br_ref[...])
    bi = lax.broadcasted_iota(jnp.int32, (tt, tt), 0)
    bj = lax.broadcasted_iota(jnp.int32, (tt, tt), 1)
    same = (bi ^ bj) < DN_CHUNK
    if reverse:
        tri_c = jnp.where(same & (bj >= bi), 1.0, 0.0).astype(BF16)
        tri_r = jnp.where(same & (bi >= bj), 1.0, 0.0).astype(BF16)
    else:
        tri_c = jnp.where(same & (bj <= bi), 1.0, 0.0).astype(BF16)
        tri_r = jnp.where(same & (bi <= bj), 1.0, 0.0).astype(BF16)
    gcum_c = sum(_dot(tri_c, part) for part in _split3(g_c))
    gcum_r = sum(_dot(part, tri_r) for part in _split3(g_r))

    ri = lax.broadcasted_iota(jnp.int32, (DN_CHUNK, DN_CHUNK), 0)
    ci = lax.broadcasted_iota(jnp.int32, (DN_CHUNK, DN_CHUNK), 1)
    incl = (ci >= ri) if reverse else (ci <= ri)
    strict = (ci > ri) if reverse else (ci < ri)

    prepared = {}
    for c in range(nc):
        r0 = c * DN_CHUNK
        rs = slice(r0, r0 + DN_CHUNK)
        gc_chunk = gcum_c[rs, :]
        beta_chunk = beta_c[rs, :]
        for h in range(DN_HEADS):
            lane_b = d * DN_HEADS + h
            lane_g = 2 * DN_HEADS + d * DN_HEADS + h
            cs = slice(h * DN_DK, (h + 1) * DN_DK)
            qh = y[rs, cs]
            kh = y[rs, DN_QK + h * DN_DK:DN_QK + (h + 1) * DN_DK]
            vh = y[rs, 2 * DN_QK + h * DN_DV:2 * DN_QK + (h + 1) * DN_DV]
            qh = qh * lax.rsqrt(jnp.sum(qh * qh, axis=-1, keepdims=True) + EPS) * (DN_DK ** -0.5)
            kh = kh * lax.rsqrt(jnp.sum(kh * kh, axis=-1, keepdims=True) + EPS)
            g_i = _lane_pick(gc_chunk, lane_g)
            b_i = _lane_pick(beta_chunk, lane_b)
            g_j = gcum_r[lane_g:lane_g + 1, rs]
            g_tot = g_i[0:1, :] if reverse else g_i[DN_CHUNK - 1:DN_CHUNK, :]
            decay = jnp.where(incl, jnp.exp(jnp.where(incl, g_i - g_j, 0.0)), 0.0)
            kb = kh * b_i
            k16 = kh.astype(BF16)
            a2 = _dot_nt(jnp.concatenate([kb, qh], axis=0).astype(BF16), k16)
            lmat = jnp.where(strict, a2[:DN_CHUNK] * decay, 0.0)
            attn = a2[DN_CHUNK:] * decay
            tinv = _unit_tri_inverse(lmat, ri, ci)
            gam = jnp.exp(g_i)
            rhs = jnp.concatenate([vh * b_i, kb * gam], axis=1)
            sol = _mm(tinv, rhs)
            prepared[(c, h)] = dict(
                u=sol[:, :DN_DV],
                wq=jnp.concatenate([sol[:, DN_DV:], qh * gam], axis=0).astype(BF16),
                attn=attn.astype(BF16),
                kt=(kh * jnp.exp(g_tot - g_i)).astype(BF16),
                gtot=jnp.exp(g_tot))

    order = range(nc - 1, -1, -1) if reverse else range(nc)
    for h in range(DN_HEADS):
        s = s_sc[h]
        for c in order:
            pr = prepared[(c, h)]
            m1 = _dot(pr["wq"], s.astype(BF16))
            v_new = (pr["u"] - m1[:DN_CHUNK]).astype(BF16)
            o = m1[DN_CHUNK:] + _dot(pr["attn"], v_new)
            s = s * pr["gtot"] + _dot_tn(pr["kt"], v_new)
            o_ref[c * DN_CHUNK:(c + 1) * DN_CHUNK, h * DN_DV:(h + 1) * DN_DV] = o
        s_sc[h] = s

    if want_state:
        @pl.when(i == nt - 1)
        def _():
            sfin_ref[...] = s_sc[...]


def _deltanet_dir(proj, gates_row, cw, ac, bc, ar, br, s0, b, t, d, want_state):
    n = proj.shape[0]
    nc = 2
    tt = nc * DN_CHUNK
    nt = t // tt
    reverse = d == 1
    wx = 3 * DN_QK
    halo = tt // SUBLANES

    def tile(bi, i):
        return bi * nt + ((nt - 1 - i) if reverse else i)

    in_specs = [
        pl.BlockSpec((tt, wx), lambda bi, i: (tile(bi, i), COL_DN // wx)),
        pl.BlockSpec((SUBLANES, wx), lambda bi, i: (jnp.maximum(tile(bi, i) * halo - 1, 0), COL_DN // wx)),
        pl.BlockSpec((SUBLANES, wx),
                     lambda bi, i: (jnp.minimum((tile(bi, i) + 1) * halo, n // SUBLANES - 1), COL_DN // wx)),
        _const_spec((3, wx)),
        pl.BlockSpec((tt, LANES), lambda bi, i: (tile(bi, i), COL_SMALL // LANES)),
        pl.BlockSpec((None, 4 * DN_HEADS, tt), lambda bi, i: (bi, 0, tile(bi, i) - bi * nt)),
        _const_spec((1, LANES)), _const_spec((1, LANES)),
        _const_spec((4 * DN_HEADS, tt)), _const_spec((4 * DN_HEADS, tt)),
    ]
    args = [proj, proj, proj, cw, proj, gates_row, ac, bc, ar, br]
    if s0 is not None:
        in_specs.append(pl.BlockSpec((None, None, DN_HEADS, DN_DK, DN_DV), lambda bi, i: (bi, d, 0, 0, 0)))
        args.append(s0)
    out_specs = [pl.BlockSpec((tt, DN_VW), lambda bi, i: (tile(bi, i), 0))]
    out_shape = [jax.ShapeDtypeStruct((n, DN_VW), F32)]
    if want_state:
        out_specs.append(pl.BlockSpec((None, DN_HEADS, DN_DK, DN_DV), lambda bi, i: (bi, 0, 0, 0)))
        out_shape.append(jax.ShapeDtypeStruct((b, DN_HEADS, DN_DK, DN_DV), F32))
    res = pl.pallas_call(
        functools.partial(_dn_kernel, reverse=reverse, has_s0=s0 is not None, want_state=want_state,
                          nc=nc, d=d, nt=nt),
        grid=(b, nt),
        in_specs=in_specs,
        out_specs=out_specs,
        out_shape=out_shape,
        scratch_shapes=[pltpu.VMEM((DN_HEADS, DN_DK, DN_DV), F32)],
        compiler_params=_cparams(("arbitrary", "arbitrary")),
    )(*args)
    return res if want_state else (res[0], None)


def _post_kernel(x_ref, mod_ref, cb_ref, cc_ref, cx_ref, ccp_ref, cxp_ref, ccn_ref, cxn_ref, cw_ref,
                 att_ref, of_ref, ob_ref, z_ref, ng_ref, ga_ref, gb_ref, gc_ref, wpa_ref, wpb_ref, wpc_ref, wo_ref,
                 gpost_ref, o_ref, *, tm, tps):
    i = pl.program_id(0)
    first = (i % tps) == 0
    last = (i % tps) == tps - 1
    u = cc_ref[...] * cx_ref[...]
    prev_row = jnp.where(first, 0.0, ccp_ref[SUBLANES - 1:SUBLANES, :] * cxp_ref[SUBLANES - 1:SUBLANES, :])
    next_row = jnp.where(last, 0.0, ccn_ref[0:1, :] * cxn_ref[0:1, :])
    rows = lax.broadcasted_iota(jnp.int32, (tm, 1), 0)
    u_prev = jnp.where(rows == 0, prev_row, pltpu.roll(u, 1, axis=0))
    u_next = jnp.where(rows == tm - 1, next_row, pltpu.roll(u, tm - 1, axis=0))
    cw = cw_ref[...]
    conv = cw[0:1, :] * u_prev + cw[1:2, :] * u + cw[2:3, :] * u_next
    ya = _dot((cb_ref[...] * conv).astype(BF16), wpa_ref[...])
    yb = _dot(att_ref[...], wpb_ref[...])
    o = of_ref[...] + ob_ref[...]
    z = z_ref[...]
    parts = []
    for h in range(DN_HEADS):
        sl = slice(h * DN_DV, (h + 1) * DN_DV)
        parts.append((_rms(o[:, sl], ng_ref[...]) * _silu(z[:, sl])).astype(BF16))
    yc = _dot(jnp.concatenate(parts, axis=1), wpc_ref[...])
    mix_in = _sigmoid(ga_ref[...]) * ya + _sigmoid(gb_ref[...]) * yb + _sigmoid(gc_ref[...]) * yc
    mix = _dot(mix_in.astype(BF16), wo_ref[...])
    o_ref[...] = x_ref[...] + mod_ref[0, 2:3, :] * _rms(mix, gpost_ref[...])


def _post_mixer(x, mod3, mod_base, rows_per_mod, t, proj, att, o_f, o_b, conv_w, ng, wpa, wpb, wpc, wo, g_post1):
    n = x.shape[0]
    tm = 256
    tps = t // tm
    hb = tm // SUBLANES
    cwid = CONV_WIDTH
    c0 = COL_CONV // cwid

    def prev(i):
        return jnp.maximum(i * hb - 1, 0)

    def nxt(i):
        return jnp.minimum((i + 1) * hb, n // SUBLANES - 1)

    g0 = COL_GATE // D_MODEL
    in_specs = [
        pl.BlockSpec((tm, D_MODEL), lambda i: (i, 0)),
        pl.BlockSpec((1, 6, D_MODEL), lambda i: (mod_base + (i * tm) // rows_per_mod, 0, 0)),
        pl.BlockSpec((tm, cwid), lambda i: (i, c0)),
        pl.BlockSpec((tm, cwid), lambda i: (i, c0 + 1)),
        pl.BlockSpec((tm, cwid), lambda i: (i, c0 + 2)),
        pl.BlockSpec((SUBLANES, cwid), lambda i: (prev(i), c0 + 1)),
        pl.BlockSpec((SUBLANES, cwid), lambda i: (prev(i), c0 + 2)),
        pl.BlockSpec((SUBLANES, cwid), lambda i: (nxt(i), c0 + 1)),
        pl.BlockSpec((SUBLANES, cwid), lambda i: (nxt(i), c0 + 2)),
        _const_spec((3, cwid)),
        pl.BlockSpec((tm, ATTN_Q), lambda i: (i, 0)),
        pl.BlockSpec((tm, DN_VW), lambda i: (i, 0)),
        pl.BlockSpec((tm, DN_VW), lambda i: (i, 0)),
        pl.BlockSpec((tm, DN_VW), lambda i: (i, (COL_DN + 3 * DN_QK) // DN_VW)),
        _const_spec((1, DN_DV)),
        pl.BlockSpec((tm, D_MODEL), lambda i: (i, g0)),
        pl.BlockSpec((tm, D_MODEL), lambda i: (i, g0 + 1)),
        pl.BlockSpec((tm, D_MODEL), lambda i: (i, g0 + 2)),
        _const_spec((CONV_WIDTH, D_MODEL)), _const_spec((ATTN_Q, D_MODEL)),
        _const_spec((DN_VW, D_MODEL)), _const_spec((D_MODEL, D_MODEL)),
        _const_spec((1, D_MODEL)),
    ]
    return pl.pallas_call(
        functools.partial(_post_kernel, tm=tm, tps=tps),
        grid=(n // tm,),
        in_specs=in_specs,
        out_specs=pl.BlockSpec((tm, D_MODEL), lambda i: (i, 0)),
        out_shape=jax.ShapeDtypeStruct((n, D_MODEL), F32),
        compiler_params=_cparams(("arbitrary",)),
    )(x, mod3, proj, proj, proj, proj, proj, proj, proj, conv_w, att, o_f, o_b, proj, ng,
      proj, proj, proj, wpa, wpb, wpc, wo, g_post1)


def _ffn_kernel(x_ref, mod_ref, gpre_ref, wg_ref, wu_ref, wd_ref, gpost_ref, o_ref, act_sc, *, tf):
    x = x_ref[...]
    h2 = (_rms(x, gpre_ref[...]) * (1.0 + mod_ref[0, 4:5, :]) + mod_ref[0, 3:4, :]).astype(BF16)
    for j in range(0, D_FF, tf):
        gate = _dot(h2, wg_ref[:, j:j + tf])
        up = _dot(h2, wu_ref[:, j:j + tf])
        act_sc[:, j:j + tf] = (_silu(gate) * up).astype(BF16)
    ffn = _dot(act_sc[...], wd_ref[...])
    o_ref[...] = x + mod_ref[0, 5:6, :] * _rms(ffn, gpost_ref[...])


def _ffn(x, mod3, mod_base, rows_per_mod, g_pre2, wg, wu, wd, g_post2):
    n = x.shape[0]
    tm = 256
    return pl.pallas_call(
        functools.partial(_ffn_kernel, tf=256),
        grid=(n // tm,),
        in_specs=[pl.BlockSpec((tm, D_MODEL), lambda i: (i, 0)),
                  pl.BlockSpec((1, 6, D_MODEL), lambda i: (mod_base + (i * tm) // rows_per_mod, 0, 0)),
                  _const_spec((1, D_MODEL)),
                  _const_spec((D_MODEL, D_FF)), _const_spec((D_MODEL, D_FF)), _const_spec((D_FF, D_MODEL)),
                  _const_spec((1, D_MODEL))],
        out_specs=pl.BlockSpec((tm, D_MODEL), lambda i: (i, 0)),
        out_shape=jax.ShapeDtypeStruct((n, D_MODEL), F32),
        scratch_shapes=[pltpu.VMEM((tm, D_FF), BF16)],
        compiler_params=_cparams(("arbitrary",)),
    )(x, mod3, g_pre2, wg, wu, wd, g_post2)


def _rope_tables(t):
    rows = t // GRID_W
    row_id = jnp.repeat(jnp.arange(rows, dtype=F32), GRID_W)
    col_id = jnp.tile(jnp.arange(GRID_W, dtype=F32), rows)
    n_freq = HEAD_DIM // 4
    inv_freq = ROPE_THETA ** (-jnp.arange(n_freq, dtype=F32) / n_freq)
    ang = jnp.concatenate([row_id[:, None] * inv_freq, col_id[:, None] * inv_freq], axis=-1)
    cos = jnp.repeat(jnp.cos(ang), 2, axis=-1)
    sin = jnp.repeat(jnp.sin(ang), 2, axis=-1)
    sign = jnp.tile(jnp.array([-1.0, 1.0], F32), HEAD_DIM // 2)
    return cos, sin * sign


def _layer(x, b, t, mod3, mod_base, rows_per_mod, lw, rope_tabs, cache, state0, want_state):
    proj = _inproj(x, mod3, mod_base, rows_per_mod, lw["g_pre1"], lw["w_in"])
    q_p, k_p = _qkprep(proj, t, lw["g_qn"], lw["g_kn"], rope_tabs, F32 if want_state else BF16)
    att = _attention(q_p, k_p, proj, b, t, cache)
    gates_row = jnp.swapaxes(proj[:, COL_SMALL:COL_SMALL + 4 * DN_HEADS].reshape(b, t, 4 * DN_HEADS), 1, 2)
    dn = [_deltanet_dir(proj, gates_row, lw["dn_conv_w"], lw["dn_ac"], lw["dn_bc"], lw["dn_ar"], lw["dn_br"],
                        state0, b, t, d, want_state) for d in range(2)]
    x = _post_mixer(x, mod3, mod_base, rows_per_mod, t, proj, att, dn[0][0], dn[1][0], lw["conv_w"],
                    lw["dn_norm_g"], lw["w_pa"], lw["w_pb"], lw["w_pc"], lw["w_o"], lw["g_post1"])
    x = _ffn(x, mod3, mod_base, rows_per_mod, lw["g_pre2"], lw["w_gate"], lw["w_up"], lw["w_down"],
             lw["g_post2"])
    if not want_state:
        return x, None
    v = proj[:, COL_V:COL_V + ATTN_KV]
    s_fin = jnp.stack([dn[0][1], dn[1][1]], axis=1)
    return x, (k_p, v, s_fin)


def kernel(x_prompt, x_sample, cache_k, cache_v, state_dn, c, c_ctx, w_mod, b_mod, g_pre1, g_post1, g_pre2, g_post2, w_in, conv_w, g_qn, g_kn, dn_conv_w, dn_a_log, dn_dt_bias, dn_norm_g, w_pa, w_pb, w_pc, w_o, w_gate, w_up, w_down):
    bp, tp, d = x_prompt.shape
    bs, ts, _ = x_sample.shape
    depth = w_mod.shape[0]
    past = cache_k.shape[2]

    mod_rows = -(-(bs + 1) // SUBLANES) * SUBLANES
    cv = jnp.zeros((mod_rows, d), F32).at[:bs].set(c).at[bs].set(c_ctx)
    mod_all = _modulation(cv, w_mod, b_mod).reshape(depth, mod_rows, 6, d)

    w_in_r = jnp.concatenate(
        [w_in[:, :, ORIG_Q:ORIG_K], w_in[:, :, ORIG_CONV:ORIG_Q], w_in[:, :, ORIG_K:ORIG_SMALL],
         w_in[:, :, ORIG_GATE:D_IN], w_in[:, :, ORIG_SMALL:ORIG_GATE],
         jnp.zeros((depth, d, LANES - 4 * DN_HEADS), w_in.dtype)], axis=-1).astype(BF16)
    lane_pad = ((0, 0), (2 * DN_HEADS, LANES - 4 * DN_HEADS))
    a_flat = dn_a_log.reshape(depth, 2 * DN_HEADS)
    b_flat = dn_dt_bias.reshape(depth, 2 * DN_HEADS)
    dn_tt = 2 * DN_CHUNK
    row_pad = ((0, 0), (2 * DN_HEADS, 0))
    rope_tabs = _rope_tables(ts)
    cache_k4 = cache_k.reshape(bs, depth, past, ATTN_KV)
    cache_v4 = cache_v.reshape(bs, depth, past, ATTN_KV)

    xp = x_prompt.reshape(bp * tp, d)
    xs = x_sample.reshape(bs * ts, d)
    ks, vs, ss = [], [], []
    for l in range(depth):
        lw = {
            "g_pre1": g_pre1[l][None], "g_post1": g_post1[l][None], "g_pre2": g_pre2[l][None],
            "g_post2": g_post2[l][None], "w_in": w_in_r[l], "conv_w": conv_w[l],
            "g_qn": g_qn[l][None], "g_kn": g_kn[l][None], "dn_conv_w": dn_conv_w[l],
            "dn_ac": jnp.pad(a_flat[l][None], lane_pad), "dn_bc": jnp.pad(b_flat[l][None], lane_pad),
            "dn_ar": jnp.broadcast_to(jnp.pad(a_flat[l][None], row_pad).T, (4 * DN_HEADS, dn_tt)),
            "dn_br": jnp.broadcast_to(jnp.pad(b_flat[l][None], row_pad).T, (4 * DN_HEADS, dn_tt)),
            "dn_norm_g": dn_norm_g[l][None],
            "w_pa": w_pa[l].astype(BF16), "w_pb": w_pb[l].astype(BF16), "w_pc": w_pc[l].astype(BF16),
            "w_o": w_o[l].astype(BF16), "w_gate": w_gate[l].astype(BF16), "w_up": w_up[l].astype(BF16),
            "w_down": w_down[l].astype(BF16),
        }
        mod3 = mod_all[l]
        xp, (k_l, v_l, s_l) = _layer(xp, bp, tp, mod3, bs, bp * tp, lw, None, None, None, True)
        ks.append(k_l.reshape(bp, tp, N_KV_HEADS, HEAD_DIM))
        vs.append(v_l.reshape(bp, tp, N_KV_HEADS, HEAD_DIM))
        ss.append(s_l)
        xs, _ = _layer(xs, bs, ts, mod3, 0, ts, lw, rope_tabs, (cache_k4, cache_v4, l), state_dn[:, l], False)
    return (xp.reshape(bp, tp, d), xs.reshape(bs, ts, d), jnp.stack(ks, axis=1), jnp.stack(vs, axis=1),
            jnp.stack(ss, axis=1))
```

```python
import functools

import jax
import jax.numpy as jnp
from jax import lax
from jax.experimental import pallas as pl
from jax.experimental.pallas import tpu as pltpu

F32 = jnp.float32
BF16 = jnp.bfloat16

D_MODEL = 1024
EPS = 1e-6
GRID_W = 64
N_HEADS = 8
N_KV_HEADS = 2
Q_PER_KV = N_HEADS // N_KV_HEADS
HEAD_DIM = 128
ATTN_Q = N_HEADS * HEAD_DIM
ATTN_KV = N_KV_HEADS * HEAD_DIM
ROPE_THETA = 10000.0
CONV_WIDTH = 512
DN_HEADS = 4
DN_DK = 128
DN_DV = 128
DN_QK = DN_HEADS * DN_DK
DN_VW = DN_HEADS * DN_DV
D_FF = 2816

SUBLANES = 8
LANES = 128

COL_Q = 0
COL_CONV = ATTN_Q
COL_K = COL_CONV + 3 * CONV_WIDTH
COL_V = COL_K + ATTN_KV
COL_DN = COL_V + ATTN_KV
COL_GATE = COL_DN + 4 * DN_QK
COL_SMALL = COL_GATE + 3 * D_MODEL
D_INP = COL_SMALL + LANES
ORIG_CONV, ORIG_Q, ORIG_K = 0, 3 * CONV_WIDTH, 3 * CONV_WIDTH + ATTN_Q
ORIG_SMALL = ORIG_K + 2 * ATTN_KV + 4 * DN_QK
ORIG_GATE = ORIG_SMALL + 4 * DN_HEADS
D_IN = ORIG_GATE + 3 * D_MODEL

Q_SCALE = 1.4426950408889634 * HEAD_DIM ** -0.5
DN_CHUNK = 128
DN_BASE = 16
VMEM_LIMIT = 56 * 1024 * 1024


def _cparams(sem):
    return pltpu.CompilerParams(dimension_semantics=sem, vmem_limit_bytes=VMEM_LIMIT)


def _const_spec(shape):
    nd = len(shape)
    return pl.BlockSpec(shape, lambda *_: (0,) * nd, pipeline_mode=pl.Buffered(1))


def _dot(a, b):
    return jnp.dot(a, b, preferred_element_type=F32)


def _dot_nt(a, b):
    return lax.dot_general(a, b, (((1,), (1,)), ((), ())), preferred_element_type=F32)


def _dot_tn(a, b):
    return lax.dot_general(a, b, (((0,), (0,)), ((), ())), preferred_element_type=F32)


def _mm(a, b):
    return _dot(a.astype(BF16), b.astype(BF16))


def _rms(x, g):
    return x * lax.rsqrt(jnp.mean(x * x, axis=-1, keepdims=True) + EPS) * g


def _sigmoid(x):
    return 1.0 / (1.0 + jnp.exp(-x))


def _silu(x):
    return x * _sigmoid(x)


def _softplus(x):
    return jnp.maximum(x, 0.0) + jnp.log1p(jnp.exp(-jnp.abs(x)))


def _split3(x):
    hi = x.astype(BF16)
    r = x - hi.astype(F32)
    mid = r.astype(BF16)
    lo = (r - mid.astype(F32)).astype(BF16)
    return hi, mid, lo


def _mod_kernel(cv_ref, w_ref, b_ref, o_ref):
    cv = cv_ref[...]
    o_ref[0] = _dot(_silu(cv).astype(BF16), w_ref[0].astype(BF16)) + b_ref[0]


def _modulation(cv, w_mod, b_mod):
    depth, d, n6 = w_mod.shape
    rows = cv.shape[0]
    tn = 1536
    return pl.pallas_call(
        _mod_kernel,
        grid=(depth, n6 // tn),
        in_specs=[pl.BlockSpec((rows, d), lambda l, j: (0, 0)),
                  pl.BlockSpec((1, d, tn), lambda l, j: (l, 0, j)),
                  pl.BlockSpec((1, 1, tn), lambda l, j: (l, 0, j))],
        out_specs=pl.BlockSpec((1, rows, tn), lambda l, j: (l, 0, j)),
        out_shape=jax.ShapeDtypeStruct((depth, rows, n6), F32),
        compiler_params=_cparams(("arbitrary", "arbitrary")),
    )(cv, w_mod, b_mod.reshape(depth, 1, n6))


def _inproj_kernel(x_ref, mod_ref, g_ref, w_ref, o_ref, *, tn):
    y = _rms(x_ref[...], g_ref[...])
    h = (y * (1.0 + mod_ref[0, 1:2, :]) + mod_ref[0, 0:1, :]).astype(BF16)
    for j in range(0, D_INP, tn):
        o_ref[:, j:j + tn] = _dot(h, w_ref[:, j:j + tn])


def _inproj(x, mod3, mod_base, rows_per_mod, g_pre1, w_in_r):
    n = x.shape[0]
    tm = 256
    return pl.pallas_call(
        functools.partial(_inproj_kernel, tn=640),
        grid=(n // tm,),
        in_specs=[pl.BlockSpec((tm, D_MODEL), lambda i: (i, 0)),
                  pl.BlockSpec((1, 6, D_MODEL), lambda i: (mod_base + (i * tm) // rows_per_mod, 0, 0)),
                  _const_spec((1, D_MODEL)),
                  _const_spec((D_MODEL, D_INP))],
        out_specs=pl.BlockSpec((tm, D_INP), lambda i: (i, 0)),
        out_shape=jax.ShapeDtypeStruct((n, D_INP), F32),
        compiler_params=_cparams(("arbitrary",)),
    )(x, mod3, g_pre1, w_in_r)


def _swap_pairs(y):
    lane = lax.broadcasted_iota(jnp.int32, y.shape, 1)
    return jnp.where(lane % 2 == 0, pltpu.roll(y, LANES - 1, axis=1), pltpu.roll(y, 1, axis=1))


def _qkprep_kernel(*refs, rope):
    if rope:
        q_ref, k_ref, v_ref, gq_ref, gk_ref, cos_ref, sin_ref, qo_ref, ko_ref, vo_ref = refs
        cos, sin = cos_ref[...], sin_ref[...]
    else:
        q_ref, k_ref, v_ref, gq_ref, gk_ref, qo_ref, ko_ref, vo_ref = refs
    vo_ref[...] = v_ref[...].astype(vo_ref.dtype)

    def head(x, g):
        y = _rms(x, g)
        if rope:
            y = y * cos + _swap_pairs(y) * sin
        return y

    for h in range(N_HEADS):
        sl = slice(h * HEAD_DIM, (h + 1) * HEAD_DIM)
        qo_ref[:, sl] = (head(q_ref[:, sl], gq_ref[...]) * Q_SCALE).astype(qo_ref.dtype)
    for h in range(N_KV_HEADS):
        sl = slice(h * HEAD_DIM, (h + 1) * HEAD_DIM)
        ko_ref[:, sl] = head(k_ref[:, sl], gk_ref[...]).astype(ko_ref.dtype)


def _qkprep(proj, t, g_qn, g_kn, rope_tabs, k_dtype):
    n = proj.shape[0]
    tm = 256
    tps = t // tm
    in_specs = [pl.BlockSpec((tm, ATTN_Q), lambda i: (i, COL_Q // ATTN_Q)),
                pl.BlockSpec((tm, ATTN_KV), lambda i: (i, COL_K // ATTN_KV)),
                pl.BlockSpec((tm, ATTN_KV), lambda i: (i, COL_V // ATTN_KV)),
                _const_spec((1, HEAD_DIM)), _const_spec((1, HEAD_DIM))]
    args = [proj, proj, proj, g_qn, g_kn]
    if rope_tabs is not None:
        in_specs += [pl.BlockSpec((tm, HEAD_DIM), lambda i: (i % tps, 0))] * 2
        args += list(rope_tabs)
    return pl.pallas_call(
        functools.partial(_qkprep_kernel, rope=rope_tabs is not None),
        grid=(n // tm,),
        in_specs=in_specs,
        out_specs=[pl.BlockSpec((tm, ATTN_Q), lambda i: (i, 0)),
                   pl.BlockSpec((tm, ATTN_KV), lambda i: (i, 0)),
                   pl.BlockSpec((tm, ATTN_KV), lambda i: (i, 0))],
        out_shape=[jax.ShapeDtypeStruct((n, ATTN_Q), BF16),
                   jax.ShapeDtypeStruct((n, ATTN_KV), k_dtype),
                   jax.ShapeDtypeStruct((n, ATTN_KV), BF16)],
        compiler_params=_cparams(("arbitrary",)),
    )(*args)


def _attn_kernel(*refs, has_cache, t, tk, tq, nq):
    refs = list(refs)
    q_ref = refs.pop(0)
    qn_ref = refs.pop(0) if nq > 1 else None
    kc_ref, vc_ref = (refs.pop(0), refs.pop(0)) if has_cache else (None, None)
    k_ref, v_ref, o_ref = refs[:3]
    s_bufs, m_bufs = refs[3:3 + len(refs[3:]) // 2], refs[3 + len(refs[3:]) // 2:]
    qi = pl.program_id(2)

    segs = []
    if has_cache:
        past = kc_ref.shape[0]
        segs += [(kc_ref, vc_ref, r, min(tk, past - r)) for r in range(0, past, tk)]
    segs += [(k_ref, v_ref, r, tk) for r in range(0, t, tk)]
    offs = [sum(w for _, _, _, w in segs[:i]) for i in range(len(segs))]

    def stack(ref):
        q = ref[...]
        return jnp.concatenate([q[:, g * HEAD_DIM:(g + 1) * HEAD_DIM] for g in range(Q_PER_KV)], axis=0)

    def scores(qs, s_ref, i, m_run):
        kr, _, r, w = segs[i]
        s = _dot_nt(qs, kr[r:r + w, :].astype(BF16))
        s_ref[:, offs[i]:offs[i] + w] = s
        for b0 in range(0, w, LANES):
            blk = s[:, b0:b0 + LANES]
            m_run = blk if m_run is None else jnp.maximum(m_run, blk)
        return m_run

    def weighted(s_ref, i, m, acc):
        _, vr, r, w = segs[i]
        p = jnp.exp2(s_ref[:, offs[i]:offs[i] + w] - m).astype(BF16)
        v_ext = jnp.concatenate([vr[r:r + w, :].astype(BF16), jnp.ones((w, HEAD_DIM), BF16)], axis=1)
        pv = _dot(p, v_ext)
        return pv if acc is None else acc + pv

    def finish(acc):
        o = acc[:, :HEAD_DIM] / acc[:, HEAD_DIM:]
        for g in range(Q_PER_KV):
            o_ref[:, g * HEAD_DIM:(g + 1) * HEAD_DIM] = o[g * tq:(g + 1) * tq].astype(o_ref.dtype)

    if nq > 1:
        @pl.when(qi == 0)
        def _():
            qs0 = stack(q_ref)
            m_run = None
            for i in range(len(segs)):
                m_run = scores(qs0, s_bufs[0], i, m_run)
            m_bufs[0][...] = m_run

        def step(s_cur, m_cur, s_nxt, m_nxt):
            qs_next = stack(qn_ref)
            m = jnp.max(m_cur[...], axis=-1, keepdims=True)
            acc, m_run = None, None
            for i in range(len(segs)):
                m_run = scores(qs_next, s_nxt, i, m_run)
                acc = weighted(s_cur, i, m, acc)
            m_nxt[...] = m_run
            finish(acc)

        pl.when(qi % 2 == 0)(functools.partial(step, s_bufs[0], m_bufs[0], s_bufs[1], m_bufs[1]))
        pl.when(qi % 2 == 1)(functools.partial(step, s_bufs[1], m_bufs[1], s_bufs[0], m_bufs[0]))
    else:
        qs = stack(q_ref)
        m_run = None
        for i in range(len(segs)):
            m_run = scores(qs, s_bufs[0], i, m_run)
        m = jnp.max(m_run, axis=-1, keepdims=True)
        acc = None
        for i in range(len(segs)):
            acc = weighted(s_bufs[0], i, m, acc)
        finish(acc)


def _attention(q_p, k_p, v_p, b, t, cache):
    n = q_p.shape[0]
    tq = 128 if t > 256 else 256
    tk = min(t, 512)
    nq = t // tq
    qw = Q_PER_KV * HEAD_DIM
    in_specs = [pl.BlockSpec((tq, qw), lambda bi, j, qi: (bi * nq + qi, j))]
    args = [q_p]
    if nq > 1:
        in_specs.append(pl.BlockSpec((tq, qw), lambda bi, j, qi: (bi * nq + jnp.minimum(qi + 1, nq - 1), j)))
        args.append(q_p)
    if cache is not None:
        cache_k, cache_v, layer = cache
        past = cache_k.shape[2]
        cspec = pl.BlockSpec((None, None, past, HEAD_DIM), lambda bi, j, qi: (bi, layer, 0, j))
        in_specs += [cspec, cspec]
        args += [cache_k, cache_v]
    kv_spec = pl.BlockSpec((t, HEAD_DIM), lambda bi, j, qi: (bi, j))
    in_specs += [kv_spec, kv_spec]
    args += [k_p, v_p]
    n_keys = t + (cache[0].shape[2] if cache is not None else 0)
    slots = 2 if nq > 1 else 1
    return pl.pallas_call(
        functools.partial(_attn_kernel, has_cache=cache is not None, t=t, tk=tk, tq=tq, nq=nq),
        grid=(b, N_KV_HEADS, nq),
        in_specs=in_specs,
        out_specs=pl.BlockSpec((tq, qw), lambda bi, j, qi: (bi * nq + qi, j)),
        out_shape=jax.ShapeDtypeStruct((n, ATTN_Q), BF16),
        scratch_shapes=([pltpu.VMEM((Q_PER_KV * tq, n_keys), F32)] * slots
                        + [pltpu.VMEM((Q_PER_KV * tq, LANES), F32)] * slots),
        compiler_params=_cparams(("arbitrary", "arbitrary", "arbitrary")),
    )(*args)


def _lane_pick(x, lane):
    idx = lax.broadcasted_iota(jnp.int32, x.shape, 1)
    return jnp.sum(jnp.where(idx == lane, x, 0.0), axis=-1, keepdims=True)


def _unit_tri_inverses(lmats, ri, ci):
    def blk(s):
        return (ri ^ ci) < s

    eye = jnp.where(ri == ci, 1.0, 0.0)
    ps = [jnp.where(blk(DN_BASE), -lm, 0.0) for lm in lmats]
    xs = [eye + p for p in ps]
    s = 2
    while s < DN_BASE:
        ps = [_mm(p, p) for p in ps]
        xs = [x + _mm(x, p) for x, p in zip(xs, ps)]
        s *= 2
    s = DN_BASE
    while s < DN_CHUNK:
        sel = blk(2 * s) & jnp.logical_not(blk(s))
        ts = [_mm(jnp.where(sel, lm, 0.0), x) for lm, x in zip(lmats, xs)]
        xs = [x - _mm(x, t) for x, t in zip(xs, ts)]
        s *= 2
    return xs


def _dn_kernel(*refs, reverse, has_s0, want_state, nc, d, nt):
    refs = list(refs)
    x_ref, xp_ref, xn_ref, cw_ref, gc_ref, gr_ref, ac_ref, bc_ref, ar_ref, br_ref = refs[:10]
    refs = refs[10:]
    s0_ref = refs.pop(0) if has_s0 else None
    o_ref = refs.pop(0)
    sfin_ref = refs.pop(0) if want_state else None
    s_sc = refs.pop(0)

    i = pl.program_id(1)
    tt = nc * DN_CHUNK
    ti = (nt - 1 - i) if reverse else i

    @pl.when(i == 0)
    def _():
        if has_s0:
            s_sc[...] = s0_ref[...]
        else:
            s_sc[...] = jnp.zeros_like(s_sc)

    x = x_ref[...]
    rows = lax.broadcasted_iota(jnp.int32, (tt, 1), 0)
    prev_row = jnp.where(ti == 0, 0.0, xp_ref[SUBLANES - 1:SUBLANES, :])
    next_row = jnp.where(ti == nt - 1, 0.0, xn_ref[0:1, :])
    x_prev = jnp.where(rows == 0, prev_row, pltpu.roll(x, 1, axis=0))
    x_next = jnp.where(rows == tt - 1, next_row, pltpu.roll(x, tt - 1, axis=0))
    cw = cw_ref[...]
    y = _silu(cw[0:1, :] * x_prev + cw[1:2, :] * x + cw[2:3, :] * x_next)

    gl = gc_ref[...]
    beta_c = _sigmoid(gl)
    g_c = -jnp.exp(ac_ref[...]) * _softplus(gl + bc_ref[...])
    g_r = -jnp.exp(ar_ref[...]) * _softplus(gr_ref[...] + br_ref[...])
    bi = lax.broadcasted_iota(jnp.int32, (tt, tt), 0)
    bj = lax.broadcasted_iota(jnp.int32, (tt, tt), 1)
    same = (bi ^ bj) < DN_CHUNK
    if reverse:
        tri_c = jnp.where(same & (bj >= bi), 1.0, 0.0).astype(BF16)
        tri_r = jnp.where(same & (bi >= bj), 1.0, 0.0).astype(BF16)
    else:
        tri_c = jnp.where(same & (bj <= bi), 1.0, 0.0).astype(BF16)
        tri_r = jnp.where(same & (bi <= bj), 1.0, 0.0).astype(BF16)
    gcum_c = sum(_dot(tri_c, part) for part in _split3(g_c))
    gcum_r = sum(_dot(part, tri_r) for part in _split3(g_r))

    ri = lax.broadcasted_iota(jnp.int32, (DN_CHUNK, DN_CHUNK), 0)
    ci = lax.broadcasted_iota(jnp.int32, (DN_CHUNK, DN_CHUNK), 1)
    incl = (ci >= ri) if reverse else (ci <= ri)
    strict = (ci > ri) if reverse else (ci < ri)

    insts = [(c, h) for c in range(nc) for h in range(DN_HEADS)]
    pre = []
    for c, h in insts:
        rs = slice(c * DN_CHUNK, (c + 1) * DN_CHUNK)
        lane_b = d * DN_HEADS + h
        lane_g = 2 * DN_HEADS + d * DN_HEADS + h
        qh = y[rs, h * DN_DK:(h + 1) * DN_DK]
        kh = y[rs, DN_QK + h * DN_DK:DN_QK + (h + 1) * DN_DK]
        vh = y[rs, 2 * DN_QK + h * DN_DV:2 * DN_QK + (h + 1) * DN_DV]
        qh = qh * lax.rsqrt(jnp.sum(qh * qh, axis=-1, keepdims=True) + EPS) * (DN_DK ** -0.5)
        kh = kh * lax.rsqrt(jnp.sum(kh * kh, axis=-1, keepdims=True) + EPS)
        g_i = _lane_pick(gcum_c[rs, :], lane_g)
        b_i = _lane_pick(beta_c[rs, :], lane_b)
        g_j = gcum_r[lane_g:lane_g + 1, rs]
        g_tot = g_i[0:1, :] if reverse else g_i[DN_CHUNK - 1:DN_CHUNK, :]
        decay = jnp.where(incl, jnp.exp(jnp.where(incl, g_i - g_j, 0.0)), 0.0)
        kb = kh * b_i
        pre.append(dict(qh=qh, kh=kh, vh=vh, g_i=g_i, b_i=b_i, g_tot=g_tot, decay=decay, kb=kb))
    a2s = [_dot_nt(jnp.concatenate([p["kb"], p["qh"]], axis=0).astype(BF16), p["kh"].astype(BF16))
           for p in pre]
    lmats = [jnp.where(strict, a2[:DN_CHUNK] * p["decay"], 0.0) for a2, p in zip(a2s, pre)]
    attns = [(a2[DN_CHUNK:] * p["decay"]).astype(BF16) for a2, p in zip(a2s, pre)]
    tinvs = _unit_tri_inverses(lmats, ri, ci)
    gams = [jnp.exp(p["g_i"]) for p in pre]
    sols = [_mm(t, jnp.concatenate([p["vh"] * p["b_i"], p["kb"] * gam], axis=1))
            for t, p, gam in zip(tinvs, pre, gams)]
    prepared = {}
    for key, p, sol, attn, gam in zip(insts, pre, sols, attns, gams):
        prepared[key] = dict(
            u=sol[:, :DN_DV],
            wq=jnp.concatenate([sol[:, DN_DV:], p["qh"] * gam], axis=0).astype(BF16),
            attn=attn,
            kt=(p["kh"] * jnp.exp(p["g_tot"] - p["g_i"])).astype(BF16),
            gtot=jnp.exp(p["g_tot"]))

    heads = range(DN_HEADS)
    states = [s_sc[h] for h in heads]
    for c in (range(nc - 1, -1, -1) if reverse else range(nc)):
        prs = [prepared[(c, h)] for h in heads]
        m1s = [_dot(pr["wq"], s.astype(BF16)) for pr, s in zip(prs, states)]
        v_news = [(pr["u"] - m1[:DN_CHUNK]).astype(BF16) for pr, m1 in zip(prs, m1s)]
        outs = [m1[DN_CHUNK:] + _dot(pr["attn"], v) for pr, m1, v in zip(prs, m1s, v_news)]
        states = [s * pr["gtot"] + _dot_tn(pr["kt"], v) for pr, s, v in zip(prs, states, v_news)]
        for h in heads:
            o_ref[c * DN_CHUNK:(c + 1) * DN_CHUNK, h * DN_DV:(h + 1) * DN_DV] = outs[h]
    for h in heads:
        s_sc[h] = states[h]

    if want_state:
        @pl.when(i == nt - 1)
        def _():
            sfin_ref[...] = s_sc[...]


def _deltanet_dir(proj, gates_row, cw, ac, bc, ar, br, s0, b, t, d, want_state):
    n = proj.shape[0]
    nc = 2
    tt = nc * DN_CHUNK
    nt = t // tt
    reverse = d == 1
    wx = 3 * DN_QK
    halo = tt // SUBLANES

    def tile(bi, i):
        return bi * nt + ((nt - 1 - i) if reverse else i)

    in_specs = [
        pl.BlockSpec((tt, wx), lambda bi, i: (tile(bi, i), COL_DN // wx)),
        pl.BlockSpec((SUBLANES, wx), lambda bi, i: (jnp.maximum(tile(bi, i) * halo - 1, 0), COL_DN // wx)),
        pl.BlockSpec((SUBLANES, wx),
                     lambda bi, i: (jnp.minimum((tile(bi, i) + 1) * halo, n // SUBLANES - 1), COL_DN // wx)),
        _const_spec((3, wx)),
        pl.BlockSpec((tt, LANES), lambda bi, i: (tile(bi, i), COL_SMALL // LANES)),
        pl.BlockSpec((None, 4 * DN_HEADS, tt), lambda bi, i: (bi, 0, tile(bi, i) - bi * nt)),
        _const_spec((1, LANES)), _const_spec((1, LANES)),
        _const_spec((4 * DN_HEADS, tt)), _const_spec((4 * DN_HEADS, tt)),
    ]
    args = [proj, proj, proj, cw, proj, gates_row, ac, bc, ar, br]
    if s0 is not None:
        in_specs.append(pl.BlockSpec((None, None, DN_HEADS, DN_DK, DN_DV), lambda bi, i: (bi, d, 0, 0, 0)))
        args.append(s0)
    out_specs = [pl.BlockSpec((tt, DN_VW), lambda bi, i: (tile(bi, i), 0))]
    out_shape = [jax.ShapeDtypeStruct((n, DN_VW), F32)]
    if want_state:
        out_specs.append(pl.BlockSpec((None, DN_HEADS, DN_DK, DN_DV), lambda bi, i: (bi, 0, 0, 0)))
        out_shape.append(jax.ShapeDtypeStruct((b, DN_HEADS, DN_DK, DN_DV), F32))
    res = pl.pallas_call(
        functools.partial(_dn_kernel, reverse=reverse, has_s0=s0 is not None, want_state=want_state,
                          nc=nc, d=d, nt=nt),
        grid=(b, nt),
        in_specs=in_specs,
        out_specs=out_specs,
        out_shape=out_shape,
        scratch_shapes=[pltpu.VMEM((DN_HEADS, DN_DK, DN_DV), F32)],
        compiler_params=_cparams(("arbitrary", "arbitrary")),
    )(*args)
    return res if want_state else (res[0], None)


def _post_kernel(x_ref, mod_ref, cb_ref, cc_ref, cx_ref, ccp_ref, cxp_ref, ccn_ref, cxn_ref, cw_ref,
                 att_ref, of_ref, ob_ref, z_ref, ng_ref, ga_ref, gb_ref, gc_ref, wpa_ref, wpb_ref, wpc_ref, wo_ref,
                 gpost_ref, o_ref, *, tm, tps):
    i = pl.program_id(0)
    first = (i % tps) == 0
    last = (i % tps) == tps - 1
    u = cc_ref[...] * cx_ref[...]
    prev_row = jnp.where(first, 0.0, ccp_ref[SUBLANES - 1:SUBLANES, :] * cxp_ref[SUBLANES - 1:SUBLANES, :])
    next_row = jnp.where(last, 0.0, ccn_ref[0:1, :] * cxn_ref[0:1, :])
    rows = lax.broadcasted_iota(jnp.int32, (tm, 1), 0)
    u_prev = jnp.where(rows == 0, prev_row, pltpu.roll(u, 1, axis=0))
    u_next = jnp.where(rows == tm - 1, next_row, pltpu.roll(u, tm - 1, axis=0))
    cw = cw_ref[...]
    conv = cw[0:1, :] * u_prev + cw[1:2, :] * u + cw[2:3, :] * u_next
    ya = _dot((cb_ref[...] * conv).astype(BF16), wpa_ref[...])
    yb = _dot(att_ref[...], wpb_ref[...])
    o = of_ref[...] + ob_ref[...]
    z = z_ref[...]
    parts = []
    for h in range(DN_HEADS):
        sl = slice(h * DN_DV, (h + 1) * DN_DV)
        parts.append((_rms(o[:, sl], ng_ref[...]) * _silu(z[:, sl])).astype(BF16))
    yc = _dot(jnp.concatenate(parts, axis=1), wpc_ref[...])
    mix_in = _sigmoid(ga_ref[...]) * ya + _sigmoid(gb_ref[...]) * yb + _sigmoid(gc_ref[...]) * yc
    mix = _dot(mix_in.astype(BF16), wo_ref[...])
    o_ref[...] = x_ref[...] + mod_ref[0, 2:3, :] * _rms(mix, gpost_ref[...])


def _post_mixer(x, mod3, mod_base, rows_per_mod, t, proj, att, o_f, o_b, conv_w, ng, wpa, wpb, wpc, wo, g_post1):
    n = x.shape[0]
    tm = 256
    tps = t // tm
    hb = tm // SUBLANES
    cwid = CONV_WIDTH
    c0 = COL_CONV // cwid

    def prev(i):
        return jnp.maximum(i * hb - 1, 0)

    def nxt(i):
        return jnp.minimum((i + 1) * hb, n // SUBLANES - 1)

    g0 = COL_GATE // D_MODEL
    in_specs = [
        pl.BlockSpec((tm, D_MODEL), lambda i: (i, 0)),
        pl.BlockSpec((1, 6, D_MODEL), lambda i: (mod_base + (i * tm) // rows_per_mod, 0, 0)),
        pl.BlockSpec((tm, cwid), lambda i: (i, c0)),
        pl.BlockSpec((tm, cwid), lambda i: (i, c0 + 1)),
        pl.BlockSpec((tm, cwid), lambda i: (i, c0 + 2)),
        pl.BlockSpec((SUBLANES, cwid), lambda i: (prev(i), c0 + 1)),
        pl.BlockSpec((SUBLANES, cwid), lambda i: (prev(i), c0 + 2)),
        pl.BlockSpec((SUBLANES, cwid), lambda i: (nxt(i), c0 + 1)),
        pl.BlockSpec((SUBLANES, cwid), lambda i: (nxt(i), c0 + 2)),
        _const_spec((3, cwid)),
        pl.BlockSpec((tm, ATTN_Q), lambda i: (i, 0)),
        pl.BlockSpec((tm, DN_VW), lambda i: (i, 0)),
        pl.BlockSpec((tm, DN_VW), lambda i: (i, 0)),
        pl.BlockSpec((tm, DN_VW), lambda i: (i, (COL_DN + 3 * DN_QK) // DN_VW)),
        _const_spec((1, DN_DV)),
        pl.BlockSpec((tm, D_MODEL), lambda i: (i, g0)),
        pl.BlockSpec((tm, D_MODEL), lambda i: (i, g0 + 1)),
        pl.BlockSpec((tm, D_MODEL), lambda i: (i, g0 + 2)),
        _const_spec((CONV_WIDTH, D_MODEL)), _const_spec((ATTN_Q, D_MODEL)),
        _const_spec((DN_VW, D_MODEL)), _const_spec((D_MODEL, D_MODEL)),
        _const_spec((1, D_MODEL)),
    ]
    return pl.pallas_call(
        functools.partial(_post_kernel, tm=tm, tps=tps),
        grid=(n // tm,),
        in_specs=in_specs,
        out_specs=pl.BlockSpec((tm, D_MODEL), lambda i: (i, 0)),
        out_shape=jax.ShapeDtypeStruct((n, D_MODEL), F32),
        compiler_params=_cparams(("arbitrary",)),
    )(x, mod3, proj, proj, proj, proj, proj, proj, proj, conv_w, att, o_f, o_b, proj, ng,
      proj, proj, proj, wpa, wpb, wpc, wo, g_post1)


def _ffn_kernel(x_ref, mod_ref, gpre_ref, wg_ref, wu_ref, wd_ref, gpost_ref, o_ref, act_sc, *, tf):
    x = x_ref[...]
    h2 = (_rms(x, gpre_ref[...]) * (1.0 + mod_ref[0, 4:5, :]) + mod_ref[0, 3:4, :]).astype(BF16)
    for j in range(0, D_FF, tf):
        gate = _dot(h2, wg_ref[:, j:j + tf])
        up = _dot(h2, wu_ref[:, j:j + tf])
        act_sc[:, j:j + tf] = (_silu(gate) * up).astype(BF16)
    ffn = _dot(act_sc[...], wd_ref[...])
    o_ref[...] = x + mod_ref[0, 5:6, :] * _rms(ffn, gpost_ref[...])


def _ffn(x, mod3, mod_base, rows_per_mod, g_pre2, wg, wu, wd, g_post2):
    n = x.shape[0]
    tm = 256
    return pl.pallas_call(
        functools.partial(_ffn_kernel, tf=256),
        grid=(n // tm,),
        in_specs=[pl.BlockSpec((tm, D_MODEL), lambda i: (i, 0)),
                  pl.BlockSpec((1, 6, D_MODEL), lambda i: (mod_base + (i * tm) // rows_per_mod, 0, 0)),
                  _const_spec((1, D_MODEL)),
                  _const_spec((D_MODEL, D_FF)), _const_spec((D_MODEL, D_FF)), _const_spec((D_FF, D_MODEL)),
                  _const_spec((1, D_MODEL))],
        out_specs=pl.BlockSpec((tm, D_MODEL), lambda i: (i, 0)),
        out_shape=jax.ShapeDtypeStruct((n, D_MODEL), F32),
        scratch_shapes=[pltpu.VMEM((tm, D_FF), BF16)],
        compiler_params=_cparams(("arbitrary",)),
    )(x, mod3, g_pre2, wg, wu, wd, g_post2)


def _rope_tables(t):
    rows = t // GRID_W
    row_id = jnp.repeat(jnp.arange(rows, dtype=F32), GRID_W)
    col_id = jnp.tile(jnp.arange(GRID_W, dtype=F32), rows)
    n_freq = HEAD_DIM // 4
    inv_freq = ROPE_THETA ** (-jnp.arange(n_freq, dtype=F32) / n_freq)
    ang = jnp.concatenate([row_id[:, None] * inv_freq, col_id[:, None] * inv_freq], axis=-1)
    cos = jnp.repeat(jnp.cos(ang), 2, axis=-1)
    sin = jnp.repeat(jnp.sin(ang), 2, axis=-1)
    sign = jnp.tile(jnp.array([-1.0, 1.0], F32), HEAD_DIM // 2)
    return cos, sin * sign


def _layer(x, b, t, mod3, mod_base, rows_per_mod, lw, rope_tabs, cache, state0, want_state):
    proj = _inproj(x, mod3, mod_base, rows_per_mod, lw["g_pre1"], lw["w_in"])
    q_p, k_p, v_p = _qkprep(proj, t, lw["g_qn"], lw["g_kn"], rope_tabs, F32 if want_state else BF16)
    att = _attention(q_p, k_p, v_p, b, t, cache)
    gates_row = jnp.swapaxes(proj[:, COL_SMALL:COL_SMALL + 4 * DN_HEADS].reshape(b, t, 4 * DN_HEADS), 1, 2)
    dn = [_deltanet_dir(proj, gates_row, lw["dn_conv_w"], lw["dn_ac"], lw["dn_bc"], lw["dn_ar"], lw["dn_br"],
                        state0, b, t, d, want_state) for d in range(2)]
    x = _post_mixer(x, mod3, mod_base, rows_per_mod, t, proj, att, dn[0][0], dn[1][0], lw["conv_w"],
                    lw["dn_norm_g"], lw["w_pa"], lw["w_pb"], lw["w_pc"], lw["w_o"], lw["g_post1"])
    x = _ffn(x, mod3, mod_base, rows_per_mod, lw["g_pre2"], lw["w_gate"], lw["w_up"], lw["w_down"],
             lw["g_post2"])
    if not want_state:
        return x, None
    v = proj[:, COL_V:COL_V + ATTN_KV]
    s_fin = jnp.stack([dn[0][1], dn[1][1]], axis=1)
    return x, (k_p, v, s_fin)


def kernel(x_prompt, x_sample, cache_k, cache_v, state_dn, c, c_ctx, w_mod, b_mod, g_pre1, g_post1, g_pre2, g_post2, w_in, conv_w, g_qn, g_kn, dn_conv_w, dn_a_log, dn_dt_bias, dn_norm_g, w_pa, w_pb, w_pc, w_o, w_gate, w_up, w_down):
    bp, tp, d = x_prompt.shape
    bs, ts, _ = x_sample.shape
    depth = w_mod.shape[0]
    past = cache_k.shape[2]

    mod_rows = -(-(bs + 1) // SUBLANES) * SUBLANES
    cv = jnp.zeros((mod_rows, d), F32).at[:bs].set(c).at[bs].set(c_ctx)
    mod_all = _modulation(cv, w_mod, b_mod).reshape(depth, mod_rows, 6, d)

    w_in_r = jnp.concatenate(
        [w_in[:, :, ORIG_Q:ORIG_K], w_in[:, :, ORIG_CONV:ORIG_Q], w_in[:, :, ORIG_K:ORIG_SMALL],
         w_in[:, :, ORIG_GATE:D_IN], w_in[:, :, ORIG_SMALL:ORIG_GATE],
         jnp.zeros((depth, d, LANES - 4 * DN_HEADS), w_in.dtype)], axis=-1).astype(BF16)
    lane_pad = ((0, 0), (2 * DN_HEADS, LANES - 4 * DN_HEADS))
    a_flat = dn_a_log.reshape(depth, 2 * DN_HEADS)
    b_flat = dn_dt_bias.reshape(depth, 2 * DN_HEADS)
    dn_tt = 2 * DN_CHUNK
    row_pad = ((0, 0), (2 * DN_HEADS, 0))
    rope_tabs = _rope_tables(ts)
    cache_k4 = cache_k.reshape(bs, depth, past, ATTN_KV)
    cache_v4 = cache_v.reshape(bs, depth, past, ATTN_KV)

    xp = x_prompt.reshape(bp * tp, d)
    xs = x_sample.reshape(bs * ts, d)
    ks, vs, ss = [], [], []
    for l in range(depth):
        lw = {
            "g_pre1": g_pre1[l][None], "g_post1": g_post1[l][None], "g_pre2": g_pre2[l][None],
            "g_post2": g_post2[l][None], "w_in": w_in_r[l], "conv_w": conv_w[l],
            "g_qn": g_qn[l][None], "g_kn": g_kn[l][None], "dn_conv_w": dn_conv_w[l],
            "dn_ac": jnp.pad(a_flat[l][None], lane_pad), "dn_bc": jnp.pad(b_flat[l][None], lane_pad),
            "dn_ar": jnp.broadcast_to(jnp.pad(a_flat[l][None], row_pad).T, (4 * DN_HEADS, dn_tt)),
            "dn_br": jnp.broadcast_to(jnp.pad(b_flat[l][None], row_pad).T, (4 * DN_HEADS, dn_tt)),
            "dn_norm_g": dn_norm_g[l][None],
            "w_pa": w_pa[l].astype(BF16), "w_pb": w_pb[l].astype(BF16), "w_pc": w_pc[l].astype(BF16),
            "w_o": w_o[l].astype(BF16), "w_gate": w_gate[l].astype(BF16), "w_up": w_up[l].astype(BF16),
            "w_down": w_down[l].astype(BF16),
        }
        mod3 = mod_all[l]
        xp, (k_l, v_l, s_l) = _layer(xp, bp, tp, mod3, bs, bp * tp, lw, None, None, None, True)
        ks.append(k_l.reshape(bp, tp, N_KV_HEADS, HEAD_DIM))
        vs.append(v_l.reshape(bp, tp, N_KV_HEADS, HEAD_DIM))
        ss.append(s_l)
        xs, _ = _layer(xs, bs, ts, mod3, 0, ts, lw, rope_tabs, (cache_k4, cache_v4, l), state_dn[:, l], False)
    return (xp.reshape(bp, tp, d), xs.reshape(bs, ts, d), jnp.stack(ks, axis=1), jnp.stack(vs, axis=1),
            jnp.stack(ss, axis=1))
```

```python
import functools

import jax
import jax.numpy as jnp
from jax import lax
from jax.experimental import pallas as pl
from jax.experimental.pallas import tpu as pltpu

F32 = jnp.float32
BF16 = jnp.bfloat16

D_MODEL = 1024
EPS = 1e-6
GRID_W = 64
N_HEADS = 8
N_KV_HEADS = 2
Q_PER_KV = N_HEADS // N_KV_HEADS
HEAD_DIM = 128
ATTN_Q = N_HEADS * HEAD_DIM
ATTN_KV = N_KV_HEADS * HEAD_DIM
ROPE_THETA = 10000.0
CONV_WIDTH = 512
DN_HEADS = 4
DN_DK = 128
DN_DV = 128
DN_QK = DN_HEADS * DN_DK
DN_VW = DN_HEADS * DN_DV
D_FF = 2816

SUBLANES = 8
LANES = 128

COL_Q = 0
COL_CONV = ATTN_Q
COL_K = COL_CONV + 3 * CONV_WIDTH
COL_V = COL_K + ATTN_KV
COL_DN = COL_V + ATTN_KV
COL_GATE = COL_DN + 4 * DN_QK
COL_SMALL = COL_GATE + 3 * D_MODEL
D_INP = COL_SMALL + LANES
ORIG_CONV, ORIG_Q, ORIG_K = 0, 3 * CONV_WIDTH, 3 * CONV_WIDTH + ATTN_Q
ORIG_SMALL = ORIG_K + 2 * ATTN_KV + 4 * DN_QK
ORIG_GATE = ORIG_SMALL + 4 * DN_HEADS
D_IN = ORIG_GATE + 3 * D_MODEL

Q_SCALE = 1.4426950408889634 * HEAD_DIM ** -0.5
DN_CHUNK = 128
DN_BASE = 16
VMEM_LIMIT = 56 * 1024 * 1024


def _cparams(sem):
    return pltpu.CompilerParams(dimension_semantics=sem, vmem_limit_bytes=VMEM_LIMIT)


def _const_spec(shape):
    nd = len(shape)
    return pl.BlockSpec(shape, lambda *_: (0,) * nd, pipeline_mode=pl.Buffered(1))


def _dot(a, b):
    return jnp.dot(a, b, preferred_element_type=F32)


def _dot_nt(a, b):
    return lax.dot_general(a, b, (((1,), (1,)), ((), ())), preferred_element_type=F32)


def _dot_tn(a, b):
    return lax.dot_general(a, b, (((0,), (0,)), ((), ())), preferred_element_type=F32)


def _mm(a, b):
    return _dot(a.astype(BF16), b.astype(BF16))


def _rms(x, g):
    return x * lax.rsqrt(jnp.mean(x * x, axis=-1, keepdims=True) + EPS) * g


def _sigmoid(x):
    return 1.0 / (1.0 + jnp.exp(-x))


def _silu(x):
    return x * _sigmoid(x)


def _softplus(x):
    return jnp.maximum(x, 0.0) + jnp.log1p(jnp.exp(-jnp.abs(x)))


def _split3(x):
    hi = x.astype(BF16)
    r = x - hi.astype(F32)
    mid = r.astype(BF16)
    lo = (r - mid.astype(F32)).astype(BF16)
    return hi, mid, lo


def _mod_kernel(cv_ref, w_ref, b_ref, o_ref):
    cv = cv_ref[...]
    o_ref[0] = _dot(_silu(cv).astype(BF16), w_ref[0].astype(BF16)) + b_ref[0]


def _modulation(cv, w_mod, b_mod):
    depth, d, n6 = w_mod.shape
    rows = cv.shape[0]
    tn = 1536
    return pl.pallas_call(
        _mod_kernel,
        grid=(depth, n6 // tn),
        in_specs=[pl.BlockSpec((rows, d), lambda l, j: (0, 0)),
                  pl.BlockSpec((1, d, tn), lambda l, j: (l, 0, j)),
                  pl.BlockSpec((1, 1, tn), lambda l, j: (l, 0, j))],
        out_specs=pl.BlockSpec((1, rows, tn), lambda l, j: (l, 0, j)),
        out_shape=jax.ShapeDtypeStruct((depth, rows, n6), F32),
        compiler_params=_cparams(("arbitrary", "arbitrary")),
    )(cv, w_mod, b_mod.reshape(depth, 1, n6))


def _inproj_kernel(x_ref, mod_ref, g_ref, w_ref, o_ref, *, tn):
    y = _rms(x_ref[...], g_ref[...])
    h = (y * (1.0 + mod_ref[0, 1:2, :]) + mod_ref[0, 0:1, :]).astype(BF16)
    for j in range(0, D_INP, tn):
        o_ref[:, j:j + tn] = _dot(h, w_ref[:, j:j + tn])


def _inproj(x, mod3, mod_base, rows_per_mod, g_pre1, w_in_r):
    n = x.shape[0]
    tm = 256
    return pl.pallas_call(
        functools.partial(_inproj_kernel, tn=640),
        grid=(n // tm,),
        in_specs=[pl.BlockSpec((tm, D_MODEL), lambda i: (i, 0)),
                  pl.BlockSpec((1, 6, D_MODEL), lambda i: (mod_base + (i * tm) // rows_per_mod, 0, 0)),
                  _const_spec((1, D_MODEL)),
                  _const_spec((D_MODEL, D_INP))],
        out_specs=pl.BlockSpec((tm, D_INP), lambda i: (i, 0)),
        out_shape=jax.ShapeDtypeStruct((n, D_INP), F32),
        compiler_params=_cparams(("arbitrary",)),
    )(x, mod3, g_pre1, w_in_r)


def _swap_pairs(y):
    lane = lax.broadcasted_iota(jnp.int32, y.shape, 1)
    return jnp.where(lane % 2 == 0, pltpu.roll(y, LANES - 1, axis=1), pltpu.roll(y, 1, axis=1))


def _qkprep_kernel(*refs, rope):
    if rope:
        q_ref, k_ref, v_ref, gq_ref, gk_ref, cos_ref, sin_ref, qo_ref, ko_ref, vo_ref = refs
        cos, sin = cos_ref[...], sin_ref[...]
    else:
        q_ref, k_ref, v_ref, gq_ref, gk_ref, qo_ref, ko_ref, vo_ref = refs
    vo_ref[...] = v_ref[...].astype(vo_ref.dtype)

    def head(x, g):
        y = _rms(x, g)
        if rope:
            y = y * cos + _swap_pairs(y) * sin
        return y

    for h in range(N_HEADS):
        sl = slice(h * HEAD_DIM, (h + 1) * HEAD_DIM)
        qo_ref[:, sl] = (head(q_ref[:, sl], gq_ref[...]) * Q_SCALE).astype(qo_ref.dtype)
    for h in range(N_KV_HEADS):
        sl = slice(h * HEAD_DIM, (h + 1) * HEAD_DIM)
        ko_ref[:, sl] = head(k_ref[:, sl], gk_ref[...]).astype(ko_ref.dtype)


def _qkprep(proj, t, g_qn, g_kn, rope_tabs, k_dtype):
    n = proj.shape[0]
    tm = 256
    tps = t // tm
    in_specs = [pl.BlockSpec((tm, ATTN_Q), lambda i: (i, COL_Q // ATTN_Q)),
                pl.BlockSpec((tm, ATTN_KV), lambda i: (i, COL_K // ATTN_KV)),
                pl.BlockSpec((tm, ATTN_KV), lambda i: (i, COL_V // ATTN_KV)),
                _const_spec((1, HEAD_DIM)), _const_spec((1, HEAD_DIM))]
    args = [proj, proj, proj, g_qn, g_kn]
    if rope_tabs is not None:
        in_specs += [pl.BlockSpec((tm, HEAD_DIM), lambda i: (i % tps, 0))] * 2
        args += list(rope_tabs)
    return pl.pallas_call(
        functools.partial(_qkprep_kernel, rope=rope_tabs is not None),
        grid=(n // tm,),
        in_specs=in_specs,
        out_specs=[pl.BlockSpec((tm, ATTN_Q), lambda i: (i, 0)),
                   pl.BlockSpec((tm, ATTN_KV), lambda i: (i, 0)),
                   pl.BlockSpec((tm, ATTN_KV), lambda i: (i, 0))],
        out_shape=[jax.ShapeDtypeStruct((n, ATTN_Q), BF16),
                   jax.ShapeDtypeStruct((n, ATTN_KV), k_dtype),
                   jax.ShapeDtypeStruct((n, ATTN_KV), BF16)],
        compiler_params=_cparams(("arbitrary",)),
    )(*args)


def _attn_kernel(*refs, has_cache, t, tk, tq, nq):
    refs = list(refs)
    q_ref = refs.pop(0)
    qn_ref = refs.pop(0) if nq > 1 else None
    kc_ref, vc_ref = (refs.pop(0), refs.pop(0)) if has_cache else (None, None)
    k_ref, v_ref, o_ref = refs[:3]
    s_bufs, m_bufs = refs[3:3 + len(refs[3:]) // 2], refs[3 + len(refs[3:]) // 2:]
    qi = pl.program_id(2)

    segs = []
    if has_cache:
        past = kc_ref.shape[0]
        segs += [(kc_ref, vc_ref, r, min(tk, past - r)) for r in range(0, past, tk)]
    segs += [(k_ref, v_ref, r, tk) for r in range(0, t, tk)]
    offs = [sum(w for _, _, _, w in segs[:i]) for i in range(len(segs))]

    def stack(ref):
        q = ref[...]
        return jnp.concatenate([q[:, g * HEAD_DIM:(g + 1) * HEAD_DIM] for g in range(Q_PER_KV)], axis=0)

    def scores(qs, s_ref, i, m_run):
        kr, _, r, w = segs[i]
        s = _dot_nt(qs, kr[r:r + w, :].astype(BF16))
        s_ref[:, offs[i]:offs[i] + w] = s
        for b0 in range(0, w, LANES):
            blk = s[:, b0:b0 + LANES]
            m_run = blk if m_run is None else jnp.maximum(m_run, blk)
        return m_run

    def weighted(s_ref, i, m, acc):
        _, vr, r, w = segs[i]
        p = jnp.exp2(s_ref[:, offs[i]:offs[i] + w] - m).astype(BF16)
        v_ext = jnp.concatenate([vr[r:r + w, :].astype(BF16), jnp.ones((w, HEAD_DIM), BF16)], axis=1)
        pv = _dot(p, v_ext)
        return pv if acc is None else acc + pv

    def finish(acc):
        o = acc[:, :HEAD_DIM] / acc[:, HEAD_DIM:]
        for g in range(Q_PER_KV):
            o_ref[:, g * HEAD_DIM:(g + 1) * HEAD_DIM] = o[g * tq:(g + 1) * tq].astype(o_ref.dtype)

    if nq > 1:
        @pl.when(qi == 0)
        def _():
            qs0 = stack(q_ref)
            m_run = None
            for i in range(len(segs)):
                m_run = scores(qs0, s_bufs[0], i, m_run)
            m_bufs[0][...] = m_run

        def step(s_cur, m_cur, s_nxt, m_nxt):
            qs_next = stack(qn_ref)
            m = jnp.max(m_cur[...], axis=-1, keepdims=True)
            acc, m_run = None, None
            for i in range(len(segs)):
                m_run = scores(qs_next, s_nxt, i, m_run)
                acc = weighted(s_cur, i, m, acc)
            m_nxt[...] = m_run
            finish(acc)

        pl.when(qi % 2 == 0)(functools.partial(step, s_bufs[0], m_bufs[0], s_bufs[1], m_bufs[1]))
        pl.when(qi % 2 == 1)(functools.partial(step, s_bufs[1], m_bufs[1], s_bufs[0], m_bufs[0]))
    else:
        qs = stack(q_ref)
        m_run = None
        for i in range(len(segs)):
            m_run = scores(qs, s_bufs[0], i, m_run)
        m = jnp.max(m_run, axis=-1, keepdims=True)
        acc = None
        for i in range(len(segs)):
            acc = weighted(s_bufs[0], i, m, acc)
        finish(acc)


def _attention(q_p, k_p, v_p, b, t, cache):
    n = q_p.shape[0]
    tq = 128 if t > 256 else 256
    tk = min(t, 512)
    nq = t // tq
    qw = Q_PER_KV * HEAD_DIM
    in_specs = [pl.BlockSpec((tq, qw), lambda bi, j, qi: (bi * nq + qi, j))]
    args = [q_p]
    if nq > 1:
        in_specs.append(pl.BlockSpec((tq, qw), lambda bi, j, qi: (bi * nq + jnp.minimum(qi + 1, nq - 1), j)))
        args.append(q_p)
    if cache is not None:
        cache_k, cache_v, layer = cache
        past = cache_k.shape[2]
        cspec = pl.BlockSpec((None, None, past, HEAD_DIM), lambda bi, j, qi: (bi, layer, 0, j))
        in_specs += [cspec, cspec]
        args += [cache_k, cache_v]
    kv_spec = pl.BlockSpec((t, HEAD_DIM), lambda bi, j, qi: (bi, j))
    in_specs += [kv_spec, kv_spec]
    args += [k_p, v_p]
    n_keys = t + (cache[0].shape[2] if cache is not None else 0)
    slots = 2 if nq > 1 else 1
    return pl.pallas_call(
        functools.partial(_attn_kernel, has_cache=cache is not None, t=t, tk=tk, tq=tq, nq=nq),
        grid=(b, N_KV_HEADS, nq),
        in_specs=in_specs,
        out_specs=pl.BlockSpec((tq, qw), lambda bi, j, qi: (bi * nq + qi, j)),
        out_shape=jax.ShapeDtypeStruct((n, ATTN_Q), BF16),
        scratch_shapes=([pltpu.VMEM((Q_PER_KV * tq, n_keys), F32)] * slots
                        + [pltpu.VMEM((Q_PER_KV * tq, LANES), F32)] * slots),
        compiler_params=_cparams(("arbitrary", "arbitrary", "arbitrary")),
    )(*args)


def _lane_pick(x, lane):
    idx = lax.broadcasted_iota(jnp.int32, x.shape, 1)
    return jnp.sum(jnp.where(idx == lane, x, 0.0), axis=-1, keepdims=True)


def _block_diag(x2):
    xb = x2.astype(BF16)
    z = jnp.zeros((DN_CHUNK, DN_CHUNK), BF16)
    return jnp.concatenate([jnp.concatenate([xb[:, :DN_CHUNK], z], axis=1),
                            jnp.concatenate([z, xb[:, DN_CHUNK:]], axis=1)], axis=0)


def _mm_pair(x2, y2):
    return _dot(x2.astype(BF16), _block_diag(y2))


def _unit_tri_inverses(lmats, ri, ci):
    def blk(s):
        return (ri ^ ci) < s

    eye = jnp.where(ri == ci, 1.0, 0.0)
    ps = [jnp.where(blk(DN_BASE), -lm, 0.0) for lm in lmats]
    xs = [eye + p for p in ps]
    s = 2
    while s < DN_BASE:
        ps = [_mm_pair(p, p) for p in ps]
        yield
        xs = [x + _mm_pair(x, p) for x, p in zip(xs, ps)]
        yield
        s *= 2
    s = DN_BASE
    while s < DN_CHUNK:
        sel = blk(2 * s) & jnp.logical_not(blk(s))
        ts = [_mm_pair(jnp.where(sel, lm, 0.0), x) for lm, x in zip(lmats, xs)]
        yield
        xs = [x - _mm_pair(x, t) for x, t in zip(xs, ts)]
        yield
        s *= 2
    return xs


def _interleave(*gens):
    gens = list(gens)
    while gens:
        for gen in list(gens):
            try:
                next(gen)
            except StopIteration:
                gens.remove(gen)


def _dn_kernel(*refs, reverse, has_s0, want_state, nc, d, nt):
    refs = list(refs)
    q_ref, k_ref, v_ref, gb_ref, gr_ref = refs[:5]
    refs = refs[5:]
    s0_ref = refs.pop(0) if has_s0 else None
    o_ref = refs.pop(0)
    sfin_ref = refs.pop(0) if want_state else None
    s_sc = refs.pop(0)
    bufs = (refs[:5], refs[5:10])
    n_pairs = DN_HEADS // 2
    c2 = 2 * DN_CHUNK
    tt = nc * DN_CHUNK
    insts = [(c, p) for c in range(nc) for p in range(n_pairs)]

    g = pl.program_id(0)
    i_scan = (g - 1) % nt

    def pair_cols(x, lane_a, lane_b):
        shape = (x.shape[0], DN_CHUNK)
        return jnp.concatenate([jnp.broadcast_to(_lane_pick(x, lane_a), shape),
                                jnp.broadcast_to(_lane_pick(x, lane_b), shape)], axis=1)

    def prepare(buf):
        u_buf, wq_buf, attn_buf, kt_buf, gt_buf = buf
        gb = gb_ref[...]
        g_r = gr_ref[...]
        bi = lax.broadcasted_iota(jnp.int32, (tt, tt), 0)
        bj = lax.broadcasted_iota(jnp.int32, (tt, tt), 1)
        same = (bi ^ bj) < DN_CHUNK
        if reverse:
            tri_c = jnp.where(same & (bj >= bi), 1.0, 0.0).astype(BF16)
            tri_r = jnp.where(same & (bi >= bj), 1.0, 0.0).astype(BF16)
        else:
            tri_c = jnp.where(same & (bj <= bi), 1.0, 0.0).astype(BF16)
            tri_r = jnp.where(same & (bi <= bj), 1.0, 0.0).astype(BF16)
        gcum_c = sum(_dot(tri_c, part) for part in _split3(gb))
        gcum_r = sum(_dot(part, tri_r) for part in _split3(g_r))
        yield

        ri = lax.broadcasted_iota(jnp.int32, (DN_CHUNK, c2), 0)
        ci = lax.broadcasted_iota(jnp.int32, (DN_CHUNK, c2), 1) & (DN_CHUNK - 1)
        incl = (ci >= ri) if reverse else (ci <= ri)
        strict = (ci > ri) if reverse else (ci < ri)
        pre = []
        for c, p in insts:
            rs = slice(c * DN_CHUNK, (c + 1) * DN_CHUNK)
            cs = slice(p * c2, (p + 1) * c2)
            lane_b = d * DN_HEADS + 2 * p
            lane_g = 2 * DN_HEADS + d * DN_HEADS + 2 * p
            q2 = q_ref[rs, cs].astype(F32)
            k2 = k_ref[rs, cs].astype(F32)
            v2 = v_ref[rs, cs].astype(F32)
            g_i = pair_cols(gcum_c[rs, :], lane_g, lane_g + 1)
            b_i = pair_cols(gb[rs, :], lane_b, lane_b + 1)
            g_j = jnp.concatenate([gcum_r[lane_g:lane_g + 1, rs], gcum_r[lane_g + 1:lane_g + 2, rs]], axis=1)
            g_tot = g_i[0:1, :] if reverse else g_i[DN_CHUNK - 1:DN_CHUNK, :]
            decay = jnp.where(incl, jnp.exp(jnp.where(incl, g_i - g_j, 0.0)), 0.0)
            pre.append(dict(q2=q2, k2=k2, v2=v2, g_i=g_i, b_i=b_i, g_tot=g_tot, decay=decay, kb=k2 * b_i))
        a2s = [_dot_nt(jnp.concatenate([p["kb"], p["q2"]], axis=0).astype(BF16), _block_diag(p["k2"]))
               for p in pre]
        yield
        lmats = [jnp.where(strict, a2[:DN_CHUNK] * p["decay"], 0.0) for a2, p in zip(a2s, pre)]
        for n, (a2, p) in enumerate(zip(a2s, pre)):
            attn_buf[n] = (a2[DN_CHUNK:] * p["decay"]).astype(BF16)
        tinvs = yield from _unit_tri_inverses(lmats, ri, ci)
        gams = [jnp.exp(p["g_i"]) for p in pre]
        us = [_mm_pair(t, p["v2"] * p["b_i"]) for t, p in zip(tinvs, pre)]
        yield
        ws = [_mm_pair(t, p["kb"] * gam) for t, p, gam in zip(tinvs, pre, gams)]
        yield
        for n, (p, u, w, gam) in enumerate(zip(pre, us, ws, gams)):
            u_buf[n] = u
            wq_buf[n] = jnp.concatenate([w, p["q2"] * gam], axis=0).astype(BF16)
            kt_buf[n] = (p["k2"] * jnp.exp(p["g_tot"] - p["g_i"])).astype(BF16)
            gt_buf[n] = jnp.broadcast_to(jnp.exp(p["g_tot"]), (SUBLANES, c2))

    def scan(buf):
        u_buf, wq_buf, attn_buf, kt_buf, gt_buf = buf
        si = lax.broadcasted_iota(jnp.int32, (c2, c2), 0)
        sj = lax.broadcasted_iota(jnp.int32, (c2, c2), 1)
        on_diag = (si < DN_CHUNK) == (sj < DN_CHUNK)
        pairs = range(n_pairs)
        states = [s_sc[p] for p in pairs]
        for c in (range(nc - 1, -1, -1) if reverse else range(nc)):
            ns = [c * n_pairs + p for p in pairs]
            m1s = [_dot(wq_buf[n], s.astype(BF16)) for n, s in zip(ns, states)]
            yield
            v_news = [u_buf[n] - m1[:DN_CHUNK] for n, m1 in zip(ns, m1s)]
            outs = [m1[DN_CHUNK:] + _mm_pair(attn_buf[n], v) for n, m1, v in zip(ns, m1s, v_news)]
            yield
            states = [s * gt_buf[n][0:1, :] + jnp.where(on_diag, _dot_tn(kt_buf[n], v.astype(BF16)), 0.0)
                      for n, s, v in zip(ns, states, v_news)]
            for p in pairs:
                o_ref[c * DN_CHUNK:(c + 1) * DN_CHUNK, p * c2:(p + 1) * c2] = outs[p]
            yield
        for p in pairs:
            s_sc[p] = states[p]

    def step(par):
        @pl.when(i_scan == 0)
        def _():
            s_sc[...] = jnp.zeros_like(s_sc)
            if has_s0:
                for h in range(DN_HEADS):
                    lo = (h % 2) * DN_CHUNK
                    s_sc[h // 2, lo:lo + DN_CHUNK, lo:lo + DN_CHUNK] = s0_ref[h]

        _interleave(prepare(bufs[par]), scan(bufs[1 - par]))

        if want_state:
            @pl.when(i_scan == nt - 1)
            def _():
                for h in range(DN_HEADS):
                    lo = (h % 2) * DN_CHUNK
                    sfin_ref[h] = s_sc[h // 2, lo:lo + DN_CHUNK, lo:lo + DN_CHUNK]

    pl.when(g == 0)(lambda: _interleave(prepare(bufs[0])))
    pl.when((g > 0) & (g % 2 == 0))(functools.partial(step, 0))
    pl.when(g % 2 == 1)(functools.partial(step, 1))


def _dnprep_kernel(x_ref, xp_ref, xn_ref, cw_ref, gl_ref, ac_ref, bc_ref, q_ref, k_ref, v_ref, gb_ref,
                   *, tm, tps):
    i = pl.program_id(0)
    x = x_ref[...]
    rows = lax.broadcasted_iota(jnp.int32, (tm, 1), 0)
    prev_row = jnp.where((i % tps) == 0, 0.0, xp_ref[SUBLANES - 1:SUBLANES, :])
    next_row = jnp.where((i % tps) == tps - 1, 0.0, xn_ref[0:1, :])
    x_prev = jnp.where(rows == 0, prev_row, pltpu.roll(x, 1, axis=0))
    x_next = jnp.where(rows == tm - 1, next_row, pltpu.roll(x, tm - 1, axis=0))
    cw = cw_ref[...]
    y = _silu(cw[0:1, :] * x_prev + cw[1:2, :] * x + cw[2:3, :] * x_next)
    for h in range(DN_HEADS):
        sl = slice(h * DN_DK, (h + 1) * DN_DK)
        qh = y[:, sl]
        kh = y[:, DN_QK + h * DN_DK:DN_QK + (h + 1) * DN_DK]
        q_ref[:, sl] = (qh * lax.rsqrt(jnp.sum(qh * qh, axis=-1, keepdims=True) + EPS)
                        * (DN_DK ** -0.5)).astype(q_ref.dtype)
        k_ref[:, sl] = (kh * lax.rsqrt(jnp.sum(kh * kh, axis=-1, keepdims=True) + EPS)).astype(k_ref.dtype)
    v_ref[...] = y[:, 2 * DN_QK:].astype(v_ref.dtype)
    gl = gl_ref[...]
    lane = lax.broadcasted_iota(jnp.int32, gl.shape, 1)
    gb_ref[...] = jnp.where(lane < 2 * DN_HEADS, _sigmoid(gl),
                            -jnp.exp(ac_ref[...]) * _softplus(gl + bc_ref[...]))


def _dn_prep(proj, t, cw, ac, bc):
    n = proj.shape[0]
    tm = 256
    tps = t // tm
    wx = 3 * DN_QK
    halo = tm // SUBLANES
    qkv_spec = pl.BlockSpec((tm, DN_QK), lambda i: (i, 0))
    return pl.pallas_call(
        functools.partial(_dnprep_kernel, tm=tm, tps=tps),
        grid=(n // tm,),
        in_specs=[pl.BlockSpec((tm, wx), lambda i: (i, COL_DN // wx)),
                  pl.BlockSpec((SUBLANES, wx), lambda i: (jnp.maximum(i * halo - 1, 0), COL_DN // wx)),
                  pl.BlockSpec((SUBLANES, wx),
                               lambda i: (jnp.minimum((i + 1) * halo, n // SUBLANES - 1), COL_DN // wx)),
                  _const_spec((3, wx)),
                  pl.BlockSpec((tm, LANES), lambda i: (i, COL_SMALL // LANES)),
                  _const_spec((1, LANES)), _const_spec((1, LANES))],
        out_specs=[qkv_spec, qkv_spec, qkv_spec, pl.BlockSpec((tm, LANES), lambda i: (i, 0))],
        out_shape=[jax.ShapeDtypeStruct((n, DN_QK), BF16)] * 3 + [jax.ShapeDtypeStruct((n, LANES), F32)],
        compiler_params=_cparams(("arbitrary",)),
    )(proj, proj, proj, cw, proj, ac, bc)


def _deltanet_dir(qn, kn, vn, gb, gates_row, s0, b, t, d, want_state):
    n = qn.shape[0]
    nc = min(4, t // DN_CHUNK)
    tt = nc * DN_CHUNK
    nt = t // tt
    n_tiles = b * nt
    reverse = d == 1
    c2 = 2 * DN_CHUNK
    n_inst = nc * (DN_HEADS // 2)

    def seq(gt):
        bi, i = gt // nt, gt % nt
        return bi, ((nt - 1 - i) if reverse else i)

    def prep_tile(g):
        return seq(jnp.minimum(g, n_tiles - 1))

    def scan_tile(g):
        return seq(jnp.maximum(g - 1, 0))

    def row_block(bt):
        return bt[0] * nt + bt[1]

    qkv_spec = pl.BlockSpec((tt, DN_QK), lambda g: (row_block(prep_tile(g)), 0))
    in_specs = [
        qkv_spec, qkv_spec, qkv_spec,
        pl.BlockSpec((tt, LANES), lambda g: (row_block(prep_tile(g)), 0)),
        pl.BlockSpec((None, 4 * DN_HEADS, tt), lambda g: (prep_tile(g)[0], 0, prep_tile(g)[1])),
    ]
    args = [qn, kn, vn, gb, gates_row]
    if s0 is not None:
        in_specs.append(pl.BlockSpec((None, None, DN_HEADS, DN_DK, DN_DV),
                                     lambda g: (scan_tile(g)[0], d, 0, 0, 0)))
        args.append(s0)
    out_specs = [pl.BlockSpec((tt, DN_VW), lambda g: (row_block(scan_tile(g)), 0))]
    out_shape = [jax.ShapeDtypeStruct((n, DN_VW), F32)]
    if want_state:
        out_specs.append(pl.BlockSpec((None, DN_HEADS, DN_DK, DN_DV), lambda g: (scan_tile(g)[0], 0, 0, 0)))
        out_shape.append(jax.ShapeDtypeStruct((b, DN_HEADS, DN_DK, DN_DV), F32))
    prepared = [pltpu.VMEM((n_inst, DN_CHUNK, c2), F32), pltpu.VMEM((n_inst, c2, c2), BF16),
                pltpu.VMEM((n_inst, DN_CHUNK, c2), BF16), pltpu.VMEM((n_inst, DN_CHUNK, c2), BF16),
                pltpu.VMEM((n_inst, SUBLANES, c2), F32)]
    res = pl.pallas_call(
        functools.partial(_dn_kernel, reverse=reverse, has_s0=s0 is not None, want_state=want_state,
                          nc=nc, d=d, nt=nt),
        grid=(n_tiles + 1,),
        in_specs=in_specs,
        out_specs=out_specs,
        out_shape=out_shape,
        scratch_shapes=[pltpu.VMEM((DN_HEADS // 2, c2, c2), F32)] + prepared + prepared,
        compiler_params=_cparams(("arbitrary",)),
    )(*args)
    return res if want_state else (res[0], None)


def _post_kernel(x_ref, mod_ref, cb_ref, cc_ref, cx_ref, ccp_ref, cxp_ref, ccn_ref, cxn_ref, cw_ref,
                 att_ref, of_ref, ob_ref, z_ref, ng_ref, ga_ref, gb_ref, gc_ref, wpa_ref, wpb_ref, wpc_ref, wo_ref,
                 gpost_ref, o_ref, *, tm, tps):
    i = pl.program_id(0)
    first = (i % tps) == 0
    last = (i % tps) == tps - 1
    u = cc_ref[...] * cx_ref[...]
    prev_row = jnp.where(first, 0.0, ccp_ref[SUBLANES - 1:SUBLANES, :] * cxp_ref[SUBLANES - 1:SUBLANES, :])
    next_row = jnp.where(last, 0.0, ccn_ref[0:1, :] * cxn_ref[0:1, :])
    rows = lax.broadcasted_iota(jnp.int32, (tm, 1), 0)
    u_prev = jnp.where(rows == 0, prev_row, pltpu.roll(u, 1, axis=0))
    u_next = jnp.where(rows == tm - 1, next_row, pltpu.roll(u, tm - 1, axis=0))
    cw = cw_ref[...]
    conv = cw[0:1, :] * u_prev + cw[1:2, :] * u + cw[2:3, :] * u_next
    ya = _dot((cb_ref[...] * conv).astype(BF16), wpa_ref[...])
    yb = _dot(att_ref[...], wpb_ref[...])
    o = of_ref[...] + ob_ref[...]
    z = z_ref[...]
    parts = []
    for h in range(DN_HEADS):
        sl = slice(h * DN_DV, (h + 1) * DN_DV)
        parts.append((_rms(o[:, sl], ng_ref[...]) * _silu(z[:, sl])).astype(BF16))
    yc = _dot(jnp.concatenate(parts, axis=1), wpc_ref[...])
    mix_in = _sigmoid(ga_ref[...]) * ya + _sigmoid(gb_ref[...]) * yb + _sigmoid(gc_ref[...]) * yc
    mix = _dot(mix_in.astype(BF16), wo_ref[...])
    o_ref[...] = x_ref[...] + mod_ref[0, 2:3, :] * _rms(mix, gpost_ref[...])


def _post_mixer(x, mod3, mod_base, rows_per_mod, t, proj, att, o_f, o_b, conv_w, ng, wpa, wpb, wpc, wo, g_post1):
    n = x.shape[0]
    tm = 256
    tps = t // tm
    hb = tm // SUBLANES
    cwid = CONV_WIDTH
    c0 = COL_CONV // cwid

    def prev(i):
        return jnp.maximum(i * hb - 1, 0)

    def nxt(i):
        return jnp.minimum((i + 1) * hb, n // SUBLANES - 1)

    g0 = COL_GATE // D_MODEL
    in_specs = [
        pl.BlockSpec((tm, D_MODEL), lambda i: (i, 0)),
        pl.BlockSpec((1, 6, D_MODEL), lambda i: (mod_base + (i * tm) // rows_per_mod, 0, 0)),
        pl.BlockSpec((tm, cwid), lambda i: (i, c0)),
        pl.BlockSpec((tm, cwid), lambda i: (i, c0 + 1)),
        pl.BlockSpec((tm, cwid), lambda i: (i, c0 + 2)),
        pl.BlockSpec((SUBLANES, cwid), lambda i: (prev(i), c0 + 1)),
        pl.BlockSpec((SUBLANES, cwid), lambda i: (prev(i), c0 + 2)),
        pl.BlockSpec((SUBLANES, cwid), lambda i: (nxt(i), c0 + 1)),
        pl.BlockSpec((SUBLANES, cwid), lambda i: (nxt(i), c0 + 2)),
        _const_spec((3, cwid)),
        pl.BlockSpec((tm, ATTN_Q), lambda i: (i, 0)),
        pl.BlockSpec((tm, DN_VW), lambda i: (i, 0)),
        pl.BlockSpec((tm, DN_VW), lambda i: (i, 0)),
        pl.BlockSpec((tm, DN_VW), lambda i: (i, (COL_DN + 3 * DN_QK) // DN_VW)),
        _const_spec((1, DN_DV)),
        pl.BlockSpec((tm, D_MODEL), lambda i: (i, g0)),
        pl.BlockSpec((tm, D_MODEL), lambda i: (i, g0 + 1)),
        pl.BlockSpec((tm, D_MODEL), lambda i: (i, g0 + 2)),
        _const_spec((CONV_WIDTH, D_MODEL)), _const_spec((ATTN_Q, D_MODEL)),
        _const_spec((DN_VW, D_MODEL)), _const_spec((D_MODEL, D_MODEL)),
        _const_spec((1, D_MODEL)),
    ]
    return pl.pallas_call(
        functools.partial(_post_kernel, tm=tm, tps=tps),
        grid=(n // tm,),
        in_specs=in_specs,
        out_specs=pl.BlockSpec((tm, D_MODEL), lambda i: (i, 0)),
        out_shape=jax.ShapeDtypeStruct((n, D_MODEL), F32),
        compiler_params=_cparams(("arbitrary",)),
    )(x, mod3, proj, proj, proj, proj, proj, proj, proj, conv_w, att, o_f, o_b, proj, ng,
      proj, proj, proj, wpa, wpb, wpc, wo, g_post1)


def _ffn_kernel(x_ref, mod_ref, gpre_ref, wg_ref, wu_ref, wd_ref, gpost_ref, o_ref, act_sc, *, tf):
    x = x_ref[...]
    h2 = (_rms(x, gpre_ref[...]) * (1.0 + mod_ref[0, 4:5, :]) + mod_ref[0, 3:4, :]).astype(BF16)
    for j in range(0, D_FF, tf):
        gate = _dot(h2, wg_ref[:, j:j + tf])
        up = _dot(h2, wu_ref[:, j:j + tf])
        act_sc[:, j:j + tf] = (_silu(gate) * up).astype(BF16)
    ffn = _dot(act_sc[...], wd_ref[...])
    o_ref[...] = x + mod_ref[0, 5:6, :] * _rms(ffn, gpost_ref[...])


def _ffn(x, mod3, mod_base, rows_per_mod, g_pre2, wg, wu, wd, g_post2):
    n = x.shape[0]
    tm = 256
    return pl.pallas_call(
        functools.partial(_ffn_kernel, tf=256),
        grid=(n // tm,),
        in_specs=[pl.BlockSpec((tm, D_MODEL), lambda i: (i, 0)),
                  pl.BlockSpec((1, 6, D_MODEL), lambda i: (mod_base + (i * tm) // rows_per_mod, 0, 0)),
                  _const_spec((1, D_MODEL)),
                  _const_spec((D_MODEL, D_FF)), _const_spec((D_MODEL, D_FF)), _const_spec((D_FF, D_MODEL)),
                  _const_spec((1, D_MODEL))],
        out_specs=pl.BlockSpec((tm, D_MODEL), lambda i: (i, 0)),
        out_shape=jax.ShapeDtypeStruct((n, D_MODEL), F32),
        scratch_shapes=[pltpu.VMEM((tm, D_FF), BF16)],
        compiler_params=_cparams(("arbitrary",)),
    )(x, mod3, g_pre2, wg, wu, wd, g_post2)


def _rope_tables(t):
    rows = t // GRID_W
    row_id = jnp.repeat(jnp.arange(rows, dtype=F32), GRID_W)
    col_id = jnp.tile(jnp.arange(GRID_W, dtype=F32), rows)
    n_freq = HEAD_DIM // 4
    inv_freq = ROPE_THETA ** (-jnp.arange(n_freq, dtype=F32) / n_freq)
    ang = jnp.concatenate([row_id[:, None] * inv_freq, col_id[:, None] * inv_freq], axis=-1)
    cos = jnp.repeat(jnp.cos(ang), 2, axis=-1)
    sin = jnp.repeat(jnp.sin(ang), 2, axis=-1)
    sign = jnp.tile(jnp.array([-1.0, 1.0], F32), HEAD_DIM // 2)
    return cos, sin * sign


def _layer(x, b, t, mod3, mod_base, rows_per_mod, lw, rope_tabs, cache, state0, want_state):
    proj = _inproj(x, mod3, mod_base, rows_per_mod, lw["g_pre1"], lw["w_in"])
    q_p, k_p, v_p = _qkprep(proj, t, lw["g_qn"], lw["g_kn"], rope_tabs, F32 if want_state else BF16)
    att = _attention(q_p, k_p, v_p, b, t, cache)
    qn, kn, vn, gb = _dn_prep(proj, t, lw["dn_conv_w"], lw["dn_ac"], lw["dn_bc"])
    gates_row = jnp.swapaxes(gb[:, :4 * DN_HEADS].reshape(b, t, 4 * DN_HEADS), 1, 2)
    dn = [_deltanet_dir(qn, kn, vn, gb, gates_row, state0, b, t, d, want_state) for d in range(2)]
    x = _post_mixer(x, mod3, mod_base, rows_per_mod, t, proj, att, dn[0][0], dn[1][0], lw["conv_w"],
                    lw["dn_norm_g"], lw["w_pa"], lw["w_pb"], lw["w_pc"], lw["w_o"], lw["g_post1"])
    x = _ffn(x, mod3, mod_base, rows_per_mod, lw["g_pre2"], lw["w_gate"], lw["w_up"], lw["w_down"],
             lw["g_post2"])
    if not want_state:
        return x, None
    v = proj[:, COL_V:COL_V + ATTN_KV]
    s_fin = jnp.stack([dn[0][1], dn[1][1]], axis=1)
    return x, (k_p, v, s_fin)


def kernel(x_prompt, x_sample, cache_k, cache_v, state_dn, c, c_ctx, w_mod, b_mod, g_pre1, g_post1, g_pre2, g_post2, w_in, conv_w, g_qn, g_kn, dn_conv_w, dn_a_log, dn_dt_bias, dn_norm_g, w_pa, w_pb, w_pc, w_o, w_gate, w_up, w_down):
    bp, tp, d = x_prompt.shape
    bs, ts, _ = x_sample.shape
    depth = w_mod.shape[0]
    past = cache_k.shape[2]

    mod_rows = -(-(bs + 1) // SUBLANES) * SUBLANES
    cv = jnp.zeros((mod_rows, d), F32).at[:bs].set(c).at[bs].set(c_ctx)
    mod_all = _modulation(cv, w_mod, b_mod).reshape(depth, mod_rows, 6, d)

    w_in_r = jnp.concatenate(
        [w_in[:, :, ORIG_Q:ORIG_K], w_in[:, :, ORIG_CONV:ORIG_Q], w_in[:, :, ORIG_K:ORIG_SMALL],
         w_in[:, :, ORIG_GATE:D_IN], w_in[:, :, ORIG_SMALL:ORIG_GATE],
         jnp.zeros((depth, d, LANES - 4 * DN_HEADS), w_in.dtype)], axis=-1).astype(BF16)
    lane_pad = ((0, 0), (2 * DN_HEADS, LANES - 4 * DN_HEADS))
    a_flat = dn_a_log.reshape(depth, 2 * DN_HEADS)
    b_flat = dn_dt_bias.reshape(depth, 2 * DN_HEADS)
    rope_tabs = _rope_tables(ts)
    cache_k4 = cache_k.reshape(bs, depth, past, ATTN_KV)
    cache_v4 = cache_v.reshape(bs, depth, past, ATTN_KV)

    xp = x_prompt.reshape(bp * tp, d)
    xs = x_sample.reshape(bs * ts, d)
    ks, vs, ss = [], [], []
    for l in range(depth):
        lw = {
            "g_pre1": g_pre1[l][None], "g_post1": g_post1[l][None], "g_pre2": g_pre2[l][None],
            "g_post2": g_post2[l][None], "w_in": w_in_r[l], "conv_w": conv_w[l],
            "g_qn": g_qn[l][None], "g_kn": g_kn[l][None], "dn_conv_w": dn_conv_w[l],
            "dn_ac": jnp.pad(a_flat[l][None], lane_pad), "dn_bc": jnp.pad(b_flat[l][None], lane_pad),
            "dn_norm_g": dn_norm_g[l][None],
            "w_pa": w_pa[l].astype(BF16), "w_pb": w_pb[l].astype(BF16), "w_pc": w_pc[l].astype(BF16),
            "w_o": w_o[l].astype(BF16), "w_gate": w_gate[l].astype(BF16), "w_up": w_up[l].astype(BF16),
            "w_down": w_down[l].astype(BF16),
        }
        mod3 = mod_all[l]
        xp, (k_l, v_l, s_l) = _layer(xp, bp, tp, mod3, bs, bp * tp, lw, None, None, None, True)
        ks.append(k_l.reshape(bp, tp, N_KV_HEADS, HEAD_DIM))
        vs.append(v_l.reshape(bp, tp, N_KV_HEADS, HEAD_DIM))
        ss.append(s_l)
        xs, _ = _layer(xs, bs, ts, mod3, 0, ts, lw, rope_tabs, (cache_k4, cache_v4, l), state_dn[:, l], False)
    return (xp.reshape(bp, tp, d), xs.reshape(bs, ts, d), jnp.stack(ks, axis=1), jnp.stack(vs, axis=1),
            jnp.stack(ss, axis=1))
```

```python
import functools

import jax
import jax.numpy as jnp
from jax import lax
from jax.experimental import pallas as pl
from jax.experimental.pallas import tpu as pltpu

F32 = jnp.float32
BF16 = jnp.bfloat16

D_MODEL = 1024
EPS = 1e-6
GRID_W = 64
N_HEADS = 8
N_KV_HEADS = 2
Q_PER_KV = N_HEADS // N_KV_HEADS
HEAD_DIM = 128
ATTN_Q = N_HEADS * HEAD_DIM
ATTN_KV = N_KV_HEADS * HEAD_DIM
ROPE_THETA = 10000.0
CONV_WIDTH = 512
DN_HEADS = 4
DN_DK = 128
DN_DV = 128
DN_QK = DN_HEADS * DN_DK
DN_VW = DN_HEADS * DN_DV
D_FF = 2816

SUBLANES = 8
BF16_SUBLANES = 16
LANES = 128

COL_Q = 0
COL_K = COL_Q + ATTN_Q
COL_V = COL_K + ATTN_KV
COL_DN = COL_V + ATTN_KV
COL_SMALL = COL_DN + 3 * DN_QK
COL_REST = COL_SMALL + LANES
REST_GATE = 0
REST_CONV = REST_GATE + 3 * D_MODEL
REST_Z = REST_CONV + 3 * CONV_WIDTH
D_REST = REST_Z + DN_VW
D_INP = COL_REST + D_REST
ORIG_CONV, ORIG_Q, ORIG_K = 0, 3 * CONV_WIDTH, 3 * CONV_WIDTH + ATTN_Q
ORIG_DN = ORIG_K + 2 * ATTN_KV
ORIG_Z = ORIG_DN + 3 * DN_QK
ORIG_SMALL = ORIG_Z + DN_VW
ORIG_GATE = ORIG_SMALL + 4 * DN_HEADS
D_IN = ORIG_GATE + 3 * D_MODEL

Q_SCALE = 1.4426950408889634 * HEAD_DIM ** -0.5
DN_CHUNK = 128
DN_BASE = 16
VMEM_LIMIT = 56 * 1024 * 1024


def _cparams(sem):
    return pltpu.CompilerParams(dimension_semantics=sem, vmem_limit_bytes=VMEM_LIMIT)


def _const_spec(shape):
    nd = len(shape)
    return pl.BlockSpec(shape, lambda *_: (0,) * nd, pipeline_mode=pl.Buffered(1))


def _dot(a, b):
    return jnp.dot(a, b, preferred_element_type=F32)


def _dot_nt(a, b):
    return lax.dot_general(a, b, (((1,), (1,)), ((), ())), preferred_element_type=F32)


def _dot_tn(a, b):
    return lax.dot_general(a, b, (((0,), (0,)), ((), ())), preferred_element_type=F32)


def _mm(a, b):
    return _dot(a.astype(BF16), b.astype(BF16))


def _rms(x, g):
    return x * lax.rsqrt(jnp.mean(x * x, axis=-1, keepdims=True) + EPS) * g


def _sigmoid(x):
    return 1.0 / (1.0 + jnp.exp(-x))


def _silu(x):
    return x * _sigmoid(x)


def _softplus(x):
    return jnp.maximum(x, 0.0) + jnp.log1p(jnp.exp(-jnp.abs(x)))


def _split3(x):
    hi = x.astype(BF16)
    r = x - hi.astype(F32)
    mid = r.astype(BF16)
    lo = (r - mid.astype(F32)).astype(BF16)
    return hi, mid, lo


def _mod_kernel(cv_ref, w_ref, b_ref, o_ref):
    cv = cv_ref[...]
    o_ref[0] = _dot(_silu(cv).astype(BF16), w_ref[0].astype(BF16)) + b_ref[0]


def _modulation(cv, w_mod, b_mod):
    depth, d, n6 = w_mod.shape
    rows = cv.shape[0]
    tn = 1536
    return pl.pallas_call(
        _mod_kernel,
        grid=(depth, n6 // tn),
        in_specs=[pl.BlockSpec((rows, d), lambda l, j: (0, 0)),
                  pl.BlockSpec((1, d, tn), lambda l, j: (l, 0, j)),
                  pl.BlockSpec((1, 1, tn), lambda l, j: (l, 0, j))],
        out_specs=pl.BlockSpec((1, rows, tn), lambda l, j: (l, 0, j)),
        out_shape=jax.ShapeDtypeStruct((depth, rows, n6), F32),
        compiler_params=_cparams(("arbitrary", "arbitrary")),
    )(cv, w_mod, b_mod.reshape(depth, 1, n6))


def _swap_pairs(y):
    lane = lax.broadcasted_iota(jnp.int32, y.shape, 1)
    return jnp.where(lane % 2 == 0, pltpu.roll(y, LANES - 1, axis=1), pltpu.roll(y, 1, axis=1))


def _interleave(*gens):
    gens = list(gens)
    while gens:
        for gen in list(gens):
            try:
                next(gen)
            except StopIteration:
                gens.remove(gen)


def _inproj_kernel(*refs, rope, want_v, tm, tps, tn):
    refs = list(refs)
    x_ref, xp_ref, xn_ref, mod_ref, g_ref, w_ref, gq_ref, gk_ref, cw_ref, ac_ref, bc_ref = refs[:11]
    refs = refs[11:]
    cos_ref, sin_ref = (refs.pop(0), refs.pop(0)) if rope else (None, None)
    q_out, k_out, v_out = refs[:3]
    refs = refs[3:]
    vf_out = refs.pop(0) if want_v else None
    qn_out, kn_out, vn_out, gb_out, rest_out = refs
    i = pl.program_id(0)

    def modnorm(x):
        return (_rms(x, g_ref[...]) * (1.0 + mod_ref[0, 1:2, :]) + mod_ref[0, 0:1, :]).astype(BF16)

    h = modnorm(x_ref[...])
    h_halo = modnorm(jnp.concatenate([xp_ref[...], xn_ref[...]], axis=0))

    def qk_head(x, g):
        y = _rms(x, g)
        if rope:
            y = y * cos_ref[...] + _swap_pairs(y) * sin_ref[...]
        return y

    def epilogue():
        qkv = _dot(h, w_ref[:, COL_Q:COL_DN])
        yield
        for hh in range(N_HEADS):
            sl = slice(hh * HEAD_DIM, (hh + 1) * HEAD_DIM)
            q_out[:, sl] = (qk_head(qkv[:, sl], gq_ref[...]) * Q_SCALE).astype(q_out.dtype)
            if hh % 2 == 1:
                yield
        for hh in range(N_KV_HEADS):
            sl = slice(hh * HEAD_DIM, (hh + 1) * HEAD_DIM)
            k_out[:, sl] = qk_head(qkv[:, COL_K + hh * HEAD_DIM:COL_K + (hh + 1) * HEAD_DIM],
                                   gk_ref[...]).astype(k_out.dtype)
        v = qkv[:, COL_V:COL_DN]
        v_out[...] = v.astype(v_out.dtype)
        if want_v:
            vf_out[...] = v
        yield
        dn = _dot(h, w_ref[:, COL_DN:COL_SMALL])
        dn_halo = _dot(h_halo, w_ref[:, COL_DN:COL_SMALL])
        gl = _dot(h, w_ref[:, COL_SMALL:COL_REST])
        yield
        rows = lax.broadcasted_iota(jnp.int32, (tm, 1), 0)
        prev_row = jnp.where((i % tps) == 0, 0.0, dn_halo[SUBLANES - 1:SUBLANES, :])
        next_row = jnp.where((i % tps) == tps - 1, 0.0, dn_halo[SUBLANES:SUBLANES + 1, :])
        dn_prev = jnp.where(rows == 0, prev_row, pltpu.roll(dn, 1, axis=0))
        dn_next = jnp.where(rows == tm - 1, next_row, pltpu.roll(dn, tm - 1, axis=0))
        cw = cw_ref[...]
        y = _silu(cw[0:1, :] * dn_prev + cw[1:2, :] * dn + cw[2:3, :] * dn_next)
        yield
        for hh in range(DN_HEADS):
            sl = slice(hh * DN_DK, (hh + 1) * DN_DK)
            qh = y[:, sl]
            kh = y[:, DN_QK + hh * DN_DK:DN_QK + (hh + 1) * DN_DK]
            qn_out[:, sl] = (qh * lax.rsqrt(jnp.sum(qh * qh, axis=-1, keepdims=True) + EPS)
                             * (DN_DK ** -0.5)).astype(qn_out.dtype)
            kn_out[:, sl] = (kh * lax.rsqrt(jnp.sum(kh * kh, axis=-1, keepdims=True) + EPS)).astype(kn_out.dtype)
            if hh % 2 == 1:
                yield
        vn_out[...] = y[:, 2 * DN_QK:].astype(vn_out.dtype)
        lane = lax.broadcasted_iota(jnp.int32, gl.shape, 1)
        gb_out[...] = jnp.where(lane < 2 * DN_HEADS, _sigmoid(gl),
                                -jnp.exp(ac_ref[...]) * _softplus(gl + bc_ref[...]))

    def remaining():
        for j in range(0, D_REST, tn):
            rest_out[:, j:j + tn] = _dot(h, w_ref[:, COL_REST + j:COL_REST + j + tn]).astype(rest_out.dtype)
            yield

    _interleave(epilogue(), remaining())


def _inproj(x, t, mod3, mod_base, rows_per_mod, lw, rope_tabs, want_v):
    n = x.shape[0]
    tm = 256
    tps = t // tm
    halo = tm // SUBLANES
    in_specs = [pl.BlockSpec((tm, D_MODEL), lambda i: (i, 0)),
                pl.BlockSpec((SUBLANES, D_MODEL), lambda i: (jnp.maximum(i * halo - 1, 0), 0)),
                pl.BlockSpec((SUBLANES, D_MODEL), lambda i: (jnp.minimum((i + 1) * halo, n // SUBLANES - 1), 0)),
                pl.BlockSpec((1, 6, D_MODEL), lambda i: (mod_base + (i * tm) // rows_per_mod, 0, 0)),
                _const_spec((1, D_MODEL)),
                _const_spec((D_MODEL, D_INP)),
                _const_spec((1, HEAD_DIM)), _const_spec((1, HEAD_DIM)),
                _const_spec((3, 3 * DN_QK)), _const_spec((1, LANES)), _const_spec((1, LANES))]
    args = [x, x, x, mod3, lw["g_pre1"], lw["w_in"], lw["g_qn"], lw["g_kn"], lw["dn_conv_w"],
            lw["dn_ac"], lw["dn_bc"]]
    if rope_tabs is not None:
        in_specs += [pl.BlockSpec((tm, HEAD_DIM), lambda i: (i % tps, 0))] * 2
        args += list(rope_tabs)

    def rows(width):
        return pl.BlockSpec((tm, width), lambda i: (i, 0))

    outs = [(ATTN_Q, BF16), (ATTN_KV, F32 if want_v else BF16), (ATTN_KV, BF16)]
    if want_v:
        outs.append((ATTN_KV, F32))
    outs += [(DN_QK, BF16), (DN_QK, BF16), (DN_VW, BF16), (LANES, F32), (D_REST, BF16)]
    res = pl.pallas_call(
        functools.partial(_inproj_kernel, rope=rope_tabs is not None, want_v=want_v, tm=tm, tps=tps, tn=640),
        grid=(n // tm,),
        in_specs=in_specs,
        out_specs=[rows(w) for w, _ in outs],
        out_shape=[jax.ShapeDtypeStruct((n, w), dt) for w, dt in outs],
        compiler_params=_cparams(("arbitrary",)),
    )(*args)
    res = list(res)
    q_p, k_p, v_p = res[:3]
    v_f32 = res[3] if want_v else None
    qn, kn, vn, gb, rest = res[-5:]
    return q_p, k_p, v_p, v_f32, qn, kn, vn, gb, rest


def _attn_kernel(*refs, has_cache, t, tk, tq, nq):
    refs = list(refs)
    q_ref = refs.pop(0)
    qn_ref = refs.pop(0) if nq > 1 else None
    kc_ref, vc_ref = (refs.pop(0), refs.pop(0)) if has_cache else (None, None)
    k_ref, v_ref, o_ref = refs[:3]
    s_bufs, m_bufs = refs[3:3 + len(refs[3:]) // 2], refs[3 + len(refs[3:]) // 2:]
    qi = pl.program_id(2)

    segs = []
    if has_cache:
        past = kc_ref.shape[0]
        segs += [(kc_ref, vc_ref, r, min(tk, past - r)) for r in range(0, past, tk)]
    segs += [(k_ref, v_ref, r, tk) for r in range(0, t, tk)]
    offs = [sum(w for _, _, _, w in segs[:i]) for i in range(len(segs))]

    def stack(ref):
        q = ref[...]
        return jnp.concatenate([q[:, g * HEAD_DIM:(g + 1) * HEAD_DIM] for g in range(Q_PER_KV)], axis=0)

    def scores(qs, s_ref, i, m_run):
        kr, _, r, w = segs[i]
        s = _dot_nt(qs, kr[r:r + w, :].astype(BF16))
        s_ref[:, offs[i]:offs[i] + w] = s
        for b0 in range(0, w, LANES):
            blk = s[:, b0:b0 + LANES]
            m_run = blk if m_run is None else jnp.maximum(m_run, blk)
        return m_run

    def weighted(s_ref, i, m, acc):
        _, vr, r, w = segs[i]
        p = jnp.exp2(s_ref[:, offs[i]:offs[i] + w] - m).astype(BF16)
        v_ext = jnp.concatenate([vr[r:r + w, :].astype(BF16), jnp.ones((w, HEAD_DIM), BF16)], axis=1)
        pv = _dot(p, v_ext)
        return pv if acc is None else acc + pv

    def finish(acc):
        o = acc[:, :HEAD_DIM] / acc[:, HEAD_DIM:]
        for g in range(Q_PER_KV):
            o_ref[:, g * HEAD_DIM:(g + 1) * HEAD_DIM] = o[g * tq:(g + 1) * tq].astype(o_ref.dtype)

    if nq > 1:
        @pl.when(qi == 0)
        def _():
            qs0 = stack(q_ref)
            m_run = None
            for i in range(len(segs)):
                m_run = scores(qs0, s_bufs[0], i, m_run)
            m_bufs[0][...] = m_run

        def step(s_cur, m_cur, s_nxt, m_nxt):
            qs_next = stack(qn_ref)
            m = jnp.max(m_cur[...], axis=-1, keepdims=True)
            acc, m_run = None, None
            for i in range(len(segs)):
                m_run = scores(qs_next, s_nxt, i, m_run)
                acc = weighted(s_cur, i, m, acc)
            m_nxt[...] = m_run
            finish(acc)

        pl.when(qi % 2 == 0)(functools.partial(step, s_bufs[0], m_bufs[0], s_bufs[1], m_bufs[1]))
        pl.when(qi % 2 == 1)(functools.partial(step, s_bufs[1], m_bufs[1], s_bufs[0], m_bufs[0]))
    else:
        qs = stack(q_ref)
        m_run = None
        for i in range(len(segs)):
            m_run = scores(qs, s_bufs[0], i, m_run)
        m = jnp.max(m_run, axis=-1, keepdims=True)
        acc = None
        for i in range(len(segs)):
            acc = weighted(s_bufs[0], i, m, acc)
        finish(acc)


def _attention(q_p, k_p, v_p, b, t, cache):
    n = q_p.shape[0]
    tq = 128 if t > 256 else 256
    tk = min(t, 512)
    nq = t // tq
    qw = Q_PER_KV * HEAD_DIM
    in_specs = [pl.BlockSpec((tq, qw), lambda bi, j, qi: (bi * nq + qi, j))]
    args = [q_p]
    if nq > 1:
        in_specs.append(pl.BlockSpec((tq, qw), lambda bi, j, qi: (bi * nq + jnp.minimum(qi + 1, nq - 1), j)))
        args.append(q_p)
    if cache is not None:
        cache_k, cache_v, layer = cache
        past = cache_k.shape[2]
        cspec = pl.BlockSpec((None, None, past, HEAD_DIM), lambda bi, j, qi: (bi, layer, 0, j))
        in_specs += [cspec, cspec]
        args += [cache_k, cache_v]
    kv_spec = pl.BlockSpec((t, HEAD_DIM), lambda bi, j, qi: (bi, j))
    in_specs += [kv_spec, kv_spec]
    args += [k_p, v_p]
    n_keys = t + (cache[0].shape[2] if cache is not None else 0)
    slots = 2 if nq > 1 else 1
    return pl.pallas_call(
        functools.partial(_attn_kernel, has_cache=cache is not None, t=t, tk=tk, tq=tq, nq=nq),
        grid=(b, N_KV_HEADS, nq),
        in_specs=in_specs,
        out_specs=pl.BlockSpec((tq, qw), lambda bi, j, qi: (bi * nq + qi, j)),
        out_shape=jax.ShapeDtypeStruct((n, ATTN_Q), BF16),
        scratch_shapes=([pltpu.VMEM((Q_PER_KV * tq, n_keys), F32)] * slots
                        + [pltpu.VMEM((Q_PER_KV * tq, LANES), F32)] * slots),
        compiler_params=_cparams(("arbitrary", "arbitrary", "arbitrary")),
    )(*args)


def _lane_pick(x, lane):
    idx = lax.broadcasted_iota(jnp.int32, x.shape, 1)
    return jnp.sum(jnp.where(idx == lane, x, 0.0), axis=-1, keepdims=True)


def _block_diag(x2):
    xb = x2.astype(BF16)
    z = jnp.zeros((DN_CHUNK, DN_CHUNK), BF16)
    return jnp.concatenate([jnp.concatenate([xb[:, :DN_CHUNK], z], axis=1),
                            jnp.concatenate([z, xb[:, DN_CHUNK:]], axis=1)], axis=0)


def _mm_pair(x2, y2):
    return _dot(x2.astype(BF16), _block_diag(y2))


def _unit_tri_inverses(lmats, ri, ci):
    def blk(s):
        return (ri ^ ci) < s

    eye = jnp.where(ri == ci, 1.0, 0.0)
    ps = [jnp.where(blk(DN_BASE), -lm, 0.0) for lm in lmats]
    xs = [eye + p for p in ps]
    s = 2
    while s < DN_BASE:
        ps = [_mm_pair(p, p) for p in ps]
        yield
        xs = [x + _mm_pair(x, p) for x, p in zip(xs, ps)]
        yield
        s *= 2
    s = DN_BASE
    while s < DN_CHUNK:
        sel = blk(2 * s) & jnp.logical_not(blk(s))
        ts = [_mm_pair(jnp.where(sel, lm, 0.0), x) for lm, x in zip(lmats, xs)]
        yield
        xs = [x - _mm_pair(x, t) for x, t in zip(xs, ts)]
        yield
        s *= 2
    return xs


def _dn_kernel(*refs, reverse, has_s0, want_state, nc, d, nt):
    refs = list(refs)
    q_ref, k_ref, v_ref, gb_ref, gr_ref = refs[:5]
    refs = refs[5:]
    s0_ref = refs.pop(0) if has_s0 else None
    o_ref = refs.pop(0)
    sfin_ref = refs.pop(0) if want_state else None
    s_sc = refs.pop(0)
    bufs = (refs[:5], refs[5:10])
    n_pairs = DN_HEADS // 2
    c2 = 2 * DN_CHUNK
    tt = nc * DN_CHUNK
    insts = [(c, p) for c in range(nc) for p in range(n_pairs)]

    g = pl.program_id(0)
    i_scan = (g - 1) % nt

    def pair_cols(x, lane_a, lane_b):
        shape = (x.shape[0], DN_CHUNK)
        return jnp.concatenate([jnp.broadcast_to(_lane_pick(x, lane_a), shape),
                                jnp.broadcast_to(_lane_pick(x, lane_b), shape)], axis=1)

    def prepare(buf):
        u_buf, wq_buf, attn_buf, kt_buf, gt_buf = buf
        gb = gb_ref[...]
        g_r = gr_ref[...]
        bi = lax.broadcasted_iota(jnp.int32, (tt, tt), 0)
        bj = lax.broadcasted_iota(jnp.int32, (tt, tt), 1)
        same = (bi ^ bj) < DN_CHUNK
        if reverse:
            tri_c = jnp.where(same & (bj >= bi), 1.0, 0.0).astype(BF16)
            tri_r = jnp.where(same & (bi >= bj), 1.0, 0.0).astype(BF16)
        else:
            tri_c = jnp.where(same & (bj <= bi), 1.0, 0.0).astype(BF16)
            tri_r = jnp.where(same & (bi <= bj), 1.0, 0.0).astype(BF16)
        gcum_c = sum(_dot(tri_c, part) for part in _split3(gb))
        gcum_r = sum(_dot(part, tri_r) for part in _split3(g_r))
        yield

        ri = lax.broadcasted_iota(jnp.int32, (DN_CHUNK, c2), 0)
        ci = lax.broadcasted_iota(jnp.int32, (DN_CHUNK, c2), 1) & (DN_CHUNK - 1)
        incl = (ci >= ri) if reverse else (ci <= ri)
        strict = (ci > ri) if reverse else (ci < ri)
        pre = []
        for c, p in insts:
            rs = slice(c * DN_CHUNK, (c + 1) * DN_CHUNK)
            cs = slice(p * c2, (p + 1) * c2)
            lane_b = d * DN_HEADS + 2 * p
            lane_g = 2 * DN_HEADS + d * DN_HEADS + 2 * p
            q2 = q_ref[rs, cs].astype(F32)
            k2 = k_ref[rs, cs].astype(F32)
            v2 = v_ref[rs, cs].astype(F32)
            g_i = pair_cols(gcum_c[rs, :], lane_g, lane_g + 1)
            b_i = pair_cols(gb[rs, :], lane_b, lane_b + 1)
            g_j = jnp.concatenate([gcum_r[lane_g:lane_g + 1, rs], gcum_r[lane_g + 1:lane_g + 2, rs]], axis=1)
            g_tot = g_i[0:1, :] if reverse else g_i[DN_CHUNK - 1:DN_CHUNK, :]
            decay = jnp.where(incl, jnp.exp(jnp.where(incl, g_i - g_j, 0.0)), 0.0)
            pre.append(dict(q2=q2, k2=k2, v2=v2, g_i=g_i, b_i=b_i, g_tot=g_tot, decay=decay, kb=k2 * b_i))
        a2s = [_dot_nt(jnp.concatenate([p["kb"], p["q2"]], axis=0).astype(BF16), _block_diag(p["k2"]))
               for p in pre]
        yield
        lmats = [jnp.where(strict, a2[:DN_CHUNK] * p["decay"], 0.0) for a2, p in zip(a2s, pre)]
        for n, (a2, p) in enumerate(zip(a2s, pre)):
            attn_buf[n] = (a2[DN_CHUNK:] * p["decay"]).astype(BF16)
        tinvs = yield from _unit_tri_inverses(lmats, ri, ci)
        gams = [jnp.exp(p["g_i"]) for p in pre]
        us = [_mm_pair(t, p["v2"] * p["b_i"]) for t, p in zip(tinvs, pre)]
        yield
        ws = [_mm_pair(t, p["kb"] * gam) for t, p, gam in zip(tinvs, pre, gams)]
        yield
        for n, (p, u, w, gam) in enumerate(zip(pre, us, ws, gams)):
            u_buf[n] = u
            wq_buf[n] = jnp.concatenate([w, p["q2"] * gam], axis=0).astype(BF16)
            kt_buf[n] = (p["k2"] * jnp.exp(p["g_tot"] - p["g_i"])).astype(BF16)
            gt_buf[n] = jnp.broadcast_to(jnp.exp(p["g_tot"]), (SUBLANES, c2))

    def scan(buf):
        u_buf, wq_buf, attn_buf, kt_buf, gt_buf = buf
        si = lax.broadcasted_iota(jnp.int32, (c2, c2), 0)
        sj = lax.broadcasted_iota(jnp.int32, (c2, c2), 1)
        on_diag = (si < DN_CHUNK) == (sj < DN_CHUNK)
        pairs = range(n_pairs)
        states = [s_sc[p] for p in pairs]
        for c in (range(nc - 1, -1, -1) if reverse else range(nc)):
            ns = [c * n_pairs + p for p in pairs]
            m1s = [_dot(wq_buf[n], s.astype(BF16)) for n, s in zip(ns, states)]
            yield
            v_news = [u_buf[n] - m1[:DN_CHUNK] for n, m1 in zip(ns, m1s)]
            outs = [m1[DN_CHUNK:] + _mm_pair(attn_buf[n], v) for n, m1, v in zip(ns, m1s, v_news)]
            yield
            states = [s * gt_buf[n][0:1, :] + jnp.where(on_diag, _dot_tn(kt_buf[n], v.astype(BF16)), 0.0)
                      for n, s, v in zip(ns, states, v_news)]
            for p in pairs:
                o_ref[c * DN_CHUNK:(c + 1) * DN_CHUNK, p * c2:(p + 1) * c2] = outs[p]
            yield
        for p in pairs:
            s_sc[p] = states[p]

    def step(par):
        @pl.when(i_scan == 0)
        def _():
            s_sc[...] = jnp.zeros_like(s_sc)
            if has_s0:
                for h in range(DN_HEADS):
                    lo = (h % 2) * DN_CHUNK
                    s_sc[h // 2, lo:lo + DN_CHUNK, lo:lo + DN_CHUNK] = s0_ref[h]

        _interleave(prepare(bufs[par]), scan(bufs[1 - par]))

        if want_state:
            @pl.when(i_scan == nt - 1)
            def _():
                for h in range(DN_HEADS):
                    lo = (h % 2) * DN_CHUNK
                    sfin_ref[h] = s_sc[h // 2, lo:lo + DN_CHUNK, lo:lo + DN_CHUNK]

    pl.when(g == 0)(lambda: _interleave(prepare(bufs[0])))
    pl.when((g > 0) & (g % 2 == 0))(functools.partial(step, 0))
    pl.when(g % 2 == 1)(functools.partial(step, 1))


def _deltanet_dir(qn, kn, vn, gb, gates_row, s0, b, t, d, want_state):
    n = qn.shape[0]
    nc = min(4, t // DN_CHUNK)
    tt = nc * DN_CHUNK
    nt = t // tt
    n_tiles = b * nt
    reverse = d == 1
    c2 = 2 * DN_CHUNK
    n_inst = nc * (DN_HEADS // 2)

    def seq(gt):
        bi, i = gt // nt, gt % nt
        return bi, ((nt - 1 - i) if reverse else i)

    def prep_tile(g):
        return seq(jnp.minimum(g, n_tiles - 1))

    def scan_tile(g):
        return seq(jnp.maximum(g - 1, 0))

    def row_block(bt):
        return bt[0] * nt + bt[1]

    qkv_spec = pl.BlockSpec((tt, DN_QK), lambda g: (row_block(prep_tile(g)), 0))
    in_specs = [
        qkv_spec, qkv_spec, qkv_spec,
        pl.BlockSpec((tt, LANES), lambda g: (row_block(prep_tile(g)), 0)),
        pl.BlockSpec((None, 4 * DN_HEADS, tt), lambda g: (prep_tile(g)[0], 0, prep_tile(g)[1])),
    ]
    args = [qn, kn, vn, gb, gates_row]
    if s0 is not None:
        in_specs.append(pl.BlockSpec((None, None, DN_HEADS, DN_DK, DN_DV),
                                     lambda g: (scan_tile(g)[0], d, 0, 0, 0)))
        args.append(s0)
    out_specs = [pl.BlockSpec((tt, DN_VW), lambda g: (row_block(scan_tile(g)), 0))]
    out_shape = [jax.ShapeDtypeStruct((n, DN_VW), F32)]
    if want_state:
        out_specs.append(pl.BlockSpec((None, DN_HEADS, DN_DK, DN_DV), lambda g: (scan_tile(g)[0], 0, 0, 0)))
        out_shape.append(jax.ShapeDtypeStruct((b, DN_HEADS, DN_DK, DN_DV), F32))
    prepared = [pltpu.VMEM((n_inst, DN_CHUNK, c2), F32), pltpu.VMEM((n_inst, c2, c2), BF16),
                pltpu.VMEM((n_inst, DN_CHUNK, c2), BF16), pltpu.VMEM((n_inst, DN_CHUNK, c2), BF16),
                pltpu.VMEM((n_inst, SUBLANES, c2), F32)]
    res = pl.pallas_call(
        functools.partial(_dn_kernel, reverse=reverse, has_s0=s0 is not None, want_state=want_state,
                          nc=nc, d=d, nt=nt),
        grid=(n_tiles + 1,),
        in_specs=in_specs,
        out_specs=out_specs,
        out_shape=out_shape,
        scratch_shapes=[pltpu.VMEM((DN_HEADS // 2, c2, c2), F32)] + prepared + prepared,
        compiler_params=_cparams(("arbitrary",)),
    )(*args)
    return res if want_state else (res[0], None)


def _post_kernel(x_ref, mod_ref, cb_ref, cc_ref, cx_ref, ccp_ref, cxp_ref, ccn_ref, cxn_ref, cw_ref,
                 att_ref, of_ref, ob_ref, z_ref, ng_ref, ga_ref, gb_ref, gc_ref, wpa_ref, wpb_ref, wpc_ref, wo_ref,
                 gpost_ref, o_ref, *, tm, tps):
    i = pl.program_id(0)
    first = (i % tps) == 0
    last = (i % tps) == tps - 1
    u = cc_ref[...].astype(F32) * cx_ref[...].astype(F32)
    hl = BF16_SUBLANES - 1
    prev_row = jnp.where(first, 0.0, ccp_ref[hl:hl + 1, :].astype(F32) * cxp_ref[hl:hl + 1, :].astype(F32))
    next_row = jnp.where(last, 0.0, ccn_ref[0:1, :].astype(F32) * cxn_ref[0:1, :].astype(F32))
    rows = lax.broadcasted_iota(jnp.int32, (tm, 1), 0)
    u_prev = jnp.where(rows == 0, prev_row, pltpu.roll(u, 1, axis=0))
    u_next = jnp.where(rows == tm - 1, next_row, pltpu.roll(u, tm - 1, axis=0))
    cw = cw_ref[...]
    conv = cw[0:1, :] * u_prev + cw[1:2, :] * u + cw[2:3, :] * u_next
    ya = _dot((cb_ref[...].astype(F32) * conv).astype(BF16), wpa_ref[...])
    yb = _dot(att_ref[...], wpb_ref[...])
    o = of_ref[...] + ob_ref[...]
    z = z_ref[...].astype(F32)
    parts = []
    for h in range(DN_HEADS):
        sl = slice(h * DN_DV, (h + 1) * DN_DV)
        parts.append((_rms(o[:, sl], ng_ref[...]) * _silu(z[:, sl])).astype(BF16))
    yc = _dot(jnp.concatenate(parts, axis=1), wpc_ref[...])
    mix_in = (_sigmoid(ga_ref[...].astype(F32)) * ya + _sigmoid(gb_ref[...].astype(F32)) * yb
              + _sigmoid(gc_ref[...].astype(F32)) * yc)
    mix = _dot(mix_in.astype(BF16), wo_ref[...])
    o_ref[...] = x_ref[...] + mod_ref[0, 2:3, :] * _rms(mix, gpost_ref[...])


def _post_mixer(x, mod3, mod_base, rows_per_mod, t, proj, att, o_f, o_b, conv_w, ng, wpa, wpb, wpc, wo, g_post1):
    n = x.shape[0]
    tm = 256
    tps = t // tm
    hb = tm // BF16_SUBLANES
    cwid = CONV_WIDTH
    c0 = REST_CONV // cwid

    def prev(i):
        return jnp.maximum(i * hb - 1, 0)

    def nxt(i):
        return jnp.minimum((i + 1) * hb, n // BF16_SUBLANES - 1)

    g0 = REST_GATE // D_MODEL
    in_specs = [
        pl.BlockSpec((tm, D_MODEL), lambda i: (i, 0)),
        pl.BlockSpec((1, 6, D_MODEL), lambda i: (mod_base + (i * tm) // rows_per_mod, 0, 0)),
        pl.BlockSpec((tm, cwid), lambda i: (i, c0)),
        pl.BlockSpec((tm, cwid), lambda i: (i, c0 + 1)),
        pl.BlockSpec((tm, cwid), lambda i: (i, c0 + 2)),
        pl.BlockSpec((BF16_SUBLANES, cwid), lambda i: (prev(i), c0 + 1)),
        pl.BlockSpec((BF16_SUBLANES, cwid), lambda i: (prev(i), c0 + 2)),
        pl.BlockSpec((BF16_SUBLANES, cwid), lambda i: (nxt(i), c0 + 1)),
        pl.BlockSpec((BF16_SUBLANES, cwid), lambda i: (nxt(i), c0 + 2)),
        _const_spec((3, cwid)),
        pl.BlockSpec((tm, ATTN_Q), lambda i: (i, 0)),
        pl.BlockSpec((tm, DN_VW), lambda i: (i, 0)),
        pl.BlockSpec((tm, DN_VW), lambda i: (i, 0)),
        pl.BlockSpec((tm, DN_VW), lambda i: (i, REST_Z // DN_VW)),
        _const_spec((1, DN_DV)),
        pl.BlockSpec((tm, D_MODEL), lambda i: (i, g0)),
        pl.BlockSpec((tm, D_MODEL), lambda i: (i, g0 + 1)),
        pl.BlockSpec((tm, D_MODEL), lambda i: (i, g0 + 2)),
        _const_spec((CONV_WIDTH, D_MODEL)), _const_spec((ATTN_Q, D_MODEL)),
        _const_spec((DN_VW, D_MODEL)), _const_spec((D_MODEL, D_MODEL)),
        _const_spec((1, D_MODEL)),
    ]
    return pl.pallas_call(
        functools.partial(_post_kernel, tm=tm, tps=tps),
        grid=(n // tm,),
        in_specs=in_specs,
        out_specs=pl.BlockSpec((tm, D_MODEL), lambda i: (i, 0)),
        out_shape=jax.ShapeDtypeStruct((n, D_MODEL), F32),
        compiler_params=_cparams(("arbitrary",)),
    )(x, mod3, proj, proj, proj, proj, proj, proj, proj, conv_w, att, o_f, o_b, proj, ng,
      proj, proj, proj, wpa, wpb, wpc, wo, g_post1)


def _ffn_kernel(x_ref, mod_ref, gpre_ref, wg_ref, wu_ref, wd_ref, gpost_ref, o_ref, act_sc, *, tf):
    x = x_ref[...]
    h2 = (_rms(x, gpre_ref[...]) * (1.0 + mod_ref[0, 4:5, :]) + mod_ref[0, 3:4, :]).astype(BF16)
    for j in range(0, D_FF, tf):
        gate = _dot(h2, wg_ref[:, j:j + tf])
        up = _dot(h2, wu_ref[:, j:j + tf])
        act_sc[:, j:j + tf] = (_silu(gate) * up).astype(BF16)
    ffn = _dot(act_sc[...], wd_ref[...])
    o_ref[...] = x + mod_ref[0, 5:6, :] * _rms(ffn, gpost_ref[...])


def _ffn(x, mod3, mod_base, rows_per_mod, g_pre2, wg, wu, wd, g_post2):
    n = x.shape[0]
    tm = 256
    return pl.pallas_call(
        functools.partial(_ffn_kernel, tf=256),
        grid=(n // tm,),
        in_specs=[pl.BlockSpec((tm, D_MODEL), lambda i: (i, 0)),
                  pl.BlockSpec((1, 6, D_MODEL), lambda i: (mod_base + (i * tm) // rows_per_mod, 0, 0)),
                  _const_spec((1, D_MODEL)),
                  _const_spec((D_MODEL, D_FF)), _const_spec((D_MODEL, D_FF)), _const_spec((D_FF, D_MODEL)),
                  _const_spec((1, D_MODEL))],
        out_specs=pl.BlockSpec((tm, D_MODEL), lambda i: (i, 0)),
        out_shape=jax.ShapeDtypeStruct((n, D_MODEL), F32),
        scratch_shapes=[pltpu.VMEM((tm, D_FF), BF16)],
        compiler_params=_cparams(("arbitrary",)),
    )(x, mod3, g_pre2, wg, wu, wd, g_post2)


def _rope_tables(t):
    rows = t // GRID_W
    row_id = jnp.repeat(jnp.arange(rows, dtype=F32), GRID_W)
    col_id = jnp.tile(jnp.arange(GRID_W, dtype=F32), rows)
    n_freq = HEAD_DIM // 4
    inv_freq = ROPE_THETA ** (-jnp.arange(n_freq, dtype=F32) / n_freq)
    ang = jnp.concatenate([row_id[:, None] * inv_freq, col_id[:, None] * inv_freq], axis=-1)
    cos = jnp.repeat(jnp.cos(ang), 2, axis=-1)
    sin = jnp.repeat(jnp.sin(ang), 2, axis=-1)
    sign = jnp.tile(jnp.array([-1.0, 1.0], F32), HEAD_DIM // 2)
    return cos, sin * sign


def _layer(x, b, t, mod3, mod_base, rows_per_mod, lw, rope_tabs, cache, state0, want_state):
    q_p, k_p, v_p, v_f32, qn, kn, vn, gb, proj = _inproj(x, t, mod3, mod_base, rows_per_mod, lw, rope_tabs,
                                                         want_state)
    att = _attention(q_p, k_p, v_p, b, t, cache)
    gates_row = jnp.swapaxes(gb[:, :4 * DN_HEADS].reshape(b, t, 4 * DN_HEADS), 1, 2)
    dn = [_deltanet_dir(qn, kn, vn, gb, gates_row, state0, b, t, d, want_state) for d in range(2)]
    x = _post_mixer(x, mod3, mod_base, rows_per_mod, t, proj, att, dn[0][0], dn[1][0], lw["conv_w"],
                    lw["dn_norm_g"], lw["w_pa"], lw["w_pb"], lw["w_pc"], lw["w_o"], lw["g_post1"])
    x = _ffn(x, mod3, mod_base, rows_per_mod, lw["g_pre2"], lw["w_gate"], lw["w_up"], lw["w_down"],
             lw["g_post2"])
    if not want_state:
        return x, None
    s_fin = jnp.stack([dn[0][1], dn[1][1]], axis=1)
    return x, (k_p, v_f32, s_fin)


def kernel(x_prompt, x_sample, cache_k, cache_v, state_dn, c, c_ctx, w_mod, b_mod, g_pre1, g_post1, g_pre2, g_post2, w_in, conv_w, g_qn, g_kn, dn_conv_w, dn_a_log, dn_dt_bias, dn_norm_g, w_pa, w_pb, w_pc, w_o, w_gate, w_up, w_down):
    bp, tp, d = x_prompt.shape
    bs, ts, _ = x_sample.shape
    depth = w_mod.shape[0]
    past = cache_k.shape[2]

    mod_rows = -(-(bs + 1) // SUBLANES) * SUBLANES
    cv = jnp.zeros((mod_rows, d), F32).at[:bs].set(c).at[bs].set(c_ctx)
    mod_all = _modulation(cv, w_mod, b_mod).reshape(depth, mod_rows, 6, d)

    w_in_r = jnp.concatenate(
        [w_in[:, :, ORIG_Q:ORIG_Z],
         w_in[:, :, ORIG_SMALL:ORIG_GATE], jnp.zeros((depth, d, LANES - 4 * DN_HEADS), w_in.dtype),
         w_in[:, :, ORIG_GATE:D_IN], w_in[:, :, ORIG_CONV:ORIG_Q], w_in[:, :, ORIG_Z:ORIG_SMALL]],
        axis=-1).astype(BF16)
    lane_pad = ((0, 0), (2 * DN_HEADS, LANES - 4 * DN_HEADS))
    a_flat = dn_a_log.reshape(depth, 2 * DN_HEADS)
    b_flat = dn_dt_bias.reshape(depth, 2 * DN_HEADS)
    rope_tabs = _rope_tables(ts)
    cache_k4 = cache_k.reshape(bs, depth, past, ATTN_KV)
    cache_v4 = cache_v.reshape(bs, depth, past, ATTN_KV)

    xp = x_prompt.reshape(bp * tp, d)
    xs = x_sample.reshape(bs * ts, d)
    ks, vs, ss = [], [], []
    for l in range(depth):
        lw = {
            "g_pre1": g_pre1[l][None], "g_post1": g_post1[l][None], "g_pre2": g_pre2[l][None],
            "g_post2": g_post2[l][None], "w_in": w_in_r[l], "conv_w": conv_w[l],
            "g_qn": g_qn[l][None], "g_kn": g_kn[l][None], "dn_conv_w": dn_conv_w[l],
            "dn_ac": jnp.pad(a_flat[l][None], lane_pad), "dn_bc": jnp.pad(b_flat[l][None], lane_pad),
            "dn_norm_g": dn_norm_g[l][None],
            "w_pa": w_pa[l].astype(BF16), "w_pb": w_pb[l].astype(BF16), "w_pc": w_pc[l].astype(BF16),
            "w_o": w_o[l].astype(BF16), "w_gate": w_gate[l].astype(BF16), "w_up": w_up[l].astype(BF16),
            "w_down": w_down[l].astype(BF16),
        }
        mod3 = mod_all[l]
        xp, (k_l, v_l, s_l) = _layer(xp, bp, tp, mod3, bs, bp * tp, lw, None, None, None, True)
        ks.append(k_l.reshape(bp, tp, N_KV_HEADS, HEAD_DIM))
        vs.append(v_l.reshape(bp, tp, N_KV_HEADS, HEAD_DIM))
        ss.append(s_l)
        xs, _ = _layer(xs, bs, ts, mod3, 0, ts, lw, rope_tabs, (cache_k4, cache_v4, l), state_dn[:, l], False)
    return (xp.reshape(bp, tp, d), xs.reshape(bs, ts, d), jnp.stack(ks, axis=1), jnp.stack(vs, axis=1),
            jnp.stack(ss, axis=1))
```

```python
import functools

import jax
import jax.numpy as jnp
from jax import lax
from jax.experimental import pallas as pl
from jax.experimental.pallas import tpu as pltpu

F32 = jnp.float32
BF16 = jnp.bfloat16

D_MODEL = 1024
EPS = 1e-6
GRID_W = 64
N_HEADS = 8
N_KV_HEADS = 2
Q_PER_KV = N_HEADS // N_KV_HEADS
HEAD_DIM = 128
ATTN_Q = N_HEADS * HEAD_DIM
ATTN_KV = N_KV_HEADS * HEAD_DIM
ROPE_THETA = 10000.0
CONV_WIDTH = 512
DN_HEADS = 4
DN_DK = 128
DN_DV = 128
DN_QK = DN_HEADS * DN_DK
DN_VW = DN_HEADS * DN_DV
D_FF = 2816

SUBLANES = 8
BF16_SUBLANES = 16
LANES = 128

COL_Q = 0
COL_K = COL_Q + ATTN_Q
COL_V = COL_K + ATTN_KV
COL_DN = COL_V + ATTN_KV
COL_SMALL = COL_DN + 3 * DN_QK
COL_REST = COL_SMALL + LANES
REST_GATE = 0
REST_CONV = REST_GATE + 3 * D_MODEL
REST_Z = REST_CONV + 3 * CONV_WIDTH
D_REST = REST_Z + DN_VW
D_INP = COL_REST + D_REST
ORIG_CONV, ORIG_Q, ORIG_K = 0, 3 * CONV_WIDTH, 3 * CONV_WIDTH + ATTN_Q
ORIG_DN = ORIG_K + 2 * ATTN_KV
ORIG_Z = ORIG_DN + 3 * DN_QK
ORIG_SMALL = ORIG_Z + DN_VW
ORIG_GATE = ORIG_SMALL + 4 * DN_HEADS
D_IN = ORIG_GATE + 3 * D_MODEL

Q_SCALE = 1.4426950408889634 * HEAD_DIM ** -0.5
DN_CHUNK = 128
DN_BASE = 16
VMEM_LIMIT = 56 * 1024 * 1024


def _cparams(sem):
    return pltpu.CompilerParams(dimension_semantics=sem, vmem_limit_bytes=VMEM_LIMIT)


def _const_spec(shape):
    nd = len(shape)
    return pl.BlockSpec(shape, lambda *_: (0,) * nd, pipeline_mode=pl.Buffered(1))


def _dot(a, b):
    return jnp.dot(a, b, preferred_element_type=F32)


def _dot_nt(a, b):
    return lax.dot_general(a, b, (((1,), (1,)), ((), ())), preferred_element_type=F32)


def _dot_tn(a, b):
    return lax.dot_general(a, b, (((0,), (0,)), ((), ())), preferred_element_type=F32)


def _mm(a, b):
    return _dot(a.astype(BF16), b.astype(BF16))


def _rms(x, g):
    return x * lax.rsqrt(jnp.mean(x * x, axis=-1, keepdims=True) + EPS) * g


def _sigmoid(x):
    return 1.0 / (1.0 + jnp.exp(-x))


def _silu(x):
    return x * _sigmoid(x)


def _softplus(x):
    return jnp.maximum(x, 0.0) + jnp.log1p(jnp.exp(-jnp.abs(x)))


def _split3(x):
    hi = x.astype(BF16)
    r = x - hi.astype(F32)
    mid = r.astype(BF16)
    lo = (r - mid.astype(F32)).astype(BF16)
    return hi, mid, lo


def _mod_kernel(cv_ref, w_ref, b_ref, o_ref):
    cv = cv_ref[...]
    o_ref[0] = _dot(_silu(cv).astype(BF16), w_ref[0].astype(BF16)) + b_ref[0]


def _modulation(cv, w_mod, b_mod):
    depth, d, n6 = w_mod.shape
    rows = cv.shape[0]
    tn = 1536
    return pl.pallas_call(
        _mod_kernel,
        grid=(depth, n6 // tn),
        in_specs=[pl.BlockSpec((rows, d), lambda l, j: (0, 0)),
                  pl.BlockSpec((1, d, tn), lambda l, j: (l, 0, j)),
                  pl.BlockSpec((1, 1, tn), lambda l, j: (l, 0, j))],
        out_specs=pl.BlockSpec((1, rows, tn), lambda l, j: (l, 0, j)),
        out_shape=jax.ShapeDtypeStruct((depth, rows, n6), F32),
        compiler_params=_cparams(("arbitrary", "arbitrary")),
    )(cv, w_mod, b_mod.reshape(depth, 1, n6))


def _swap_pairs(y):
    lane = lax.broadcasted_iota(jnp.int32, y.shape, 1)
    return jnp.where(lane % 2 == 0, pltpu.roll(y, LANES - 1, axis=1), pltpu.roll(y, 1, axis=1))


def _interleave(*gens):
    gens = list(gens)
    while gens:
        for gen in list(gens):
            try:
                next(gen)
            except StopIteration:
                gens.remove(gen)


def _inproj_kernel(*refs, rope, want_v, tm, tps, tn):
    refs = list(refs)
    x_ref, xp_ref, xn_ref, mod_ref, g_ref, w_ref, gq_ref, gk_ref, cw_ref, ac_ref, bc_ref = refs[:11]
    refs = refs[11:]
    cos_ref, sin_ref = (refs.pop(0), refs.pop(0)) if rope else (None, None)
    q_out, k_out, v_out = refs[:3]
    refs = refs[3:]
    vf_out = refs.pop(0) if want_v else None
    qn_out, kn_out, vn_out, gb_out, rest_out = refs
    i = pl.program_id(0)

    def modnorm(x):
        return (_rms(x, g_ref[...]) * (1.0 + mod_ref[0, 1:2, :]) + mod_ref[0, 0:1, :]).astype(BF16)

    h = modnorm(x_ref[...])
    h_halo = modnorm(jnp.concatenate([xp_ref[...], xn_ref[...]], axis=0))

    def qk_head(x, g):
        y = _rms(x, g)
        if rope:
            y = y * cos_ref[...] + _swap_pairs(y) * sin_ref[...]
        return y

    def attn_epilogue():
        qkv = _dot(h, w_ref[:, COL_Q:COL_DN])
        yield
        for hh in range(N_HEADS):
            sl = slice(hh * HEAD_DIM, (hh + 1) * HEAD_DIM)
            q_out[:, sl] = (qk_head(qkv[:, sl], gq_ref[...]) * Q_SCALE).astype(q_out.dtype)
            if hh % 2 == 1:
                yield
        for hh in range(N_KV_HEADS):
            sl = slice(hh * HEAD_DIM, (hh + 1) * HEAD_DIM)
            k_out[:, sl] = qk_head(qkv[:, COL_K + hh * HEAD_DIM:COL_K + (hh + 1) * HEAD_DIM],
                                   gk_ref[...]).astype(k_out.dtype)
        v = qkv[:, COL_V:COL_DN]
        v_out[...] = v.T.astype(v_out.dtype)
        if want_v:
            vf_out[...] = v

    def dn_epilogue():
        dn = _dot(h, w_ref[:, COL_DN:COL_SMALL])
        dn_halo = _dot(h_halo, w_ref[:, COL_DN:COL_SMALL])
        gl = _dot(h, w_ref[:, COL_SMALL:COL_REST])
        yield
        rows = lax.broadcasted_iota(jnp.int32, (tm, 1), 0)
        prev_row = jnp.where((i % tps) == 0, 0.0, dn_halo[SUBLANES - 1:SUBLANES, :])
        next_row = jnp.where((i % tps) == tps - 1, 0.0, dn_halo[SUBLANES:SUBLANES + 1, :])
        dn_prev = jnp.where(rows == 0, prev_row, pltpu.roll(dn, 1, axis=0))
        dn_next = jnp.where(rows == tm - 1, next_row, pltpu.roll(dn, tm - 1, axis=0))
        cw = cw_ref[...]

        def conv_silu(sl):
            return _silu(cw[0:1, sl] * dn_prev[:, sl] + cw[1:2, sl] * dn[:, sl] + cw[2:3, sl] * dn_next[:, sl])

        for hh in range(DN_HEADS):
            sl = slice(hh * DN_DK, (hh + 1) * DN_DK)
            qh = conv_silu(sl)
            kh = conv_silu(slice(DN_QK + hh * DN_DK, DN_QK + (hh + 1) * DN_DK))
            qn_out[:, sl] = (qh * lax.rsqrt(jnp.sum(qh * qh, axis=-1, keepdims=True) + EPS)
                             * (DN_DK ** -0.5)).astype(qn_out.dtype)
            kn_out[:, sl] = (kh * lax.rsqrt(jnp.sum(kh * kh, axis=-1, keepdims=True) + EPS)).astype(kn_out.dtype)
            vn_out[:, sl] = conv_silu(slice(2 * DN_QK + hh * DN_DV, 2 * DN_QK + (hh + 1) * DN_DV)
                                      ).astype(vn_out.dtype)
            yield
        lane = lax.broadcasted_iota(jnp.int32, gl.shape, 1)
        gb_out[...] = jnp.where(lane < 2 * DN_HEADS, _sigmoid(gl),
                                -jnp.exp(ac_ref[...]) * _softplus(gl + bc_ref[...]))

    def remaining():
        for j in range(0, D_REST, tn):
            rest_out[:, j:j + tn] = _dot(h, w_ref[:, COL_REST + j:COL_REST + j + tn]).astype(rest_out.dtype)
            yield

    _interleave(dn_epilogue(), attn_epilogue(), remaining())


def _inproj(x, t, mod3, mod_base, rows_per_mod, lw, rope_tabs, want_v):
    n = x.shape[0]
    tm = 256
    tps = t // tm
    halo = tm // SUBLANES
    in_specs = [pl.BlockSpec((tm, D_MODEL), lambda i: (i, 0)),
                pl.BlockSpec((SUBLANES, D_MODEL), lambda i: (jnp.maximum(i * halo - 1, 0), 0)),
                pl.BlockSpec((SUBLANES, D_MODEL), lambda i: (jnp.minimum((i + 1) * halo, n // SUBLANES - 1), 0)),
                pl.BlockSpec((1, 6, D_MODEL), lambda i: (mod_base + (i * tm) // rows_per_mod, 0, 0)),
                _const_spec((1, D_MODEL)),
                _const_spec((D_MODEL, D_INP)),
                _const_spec((1, HEAD_DIM)), _const_spec((1, HEAD_DIM)),
                _const_spec((3, 3 * DN_QK)), _const_spec((1, LANES)), _const_spec((1, LANES))]
    args = [x, x, x, mod3, lw["g_pre1"], lw["w_in"], lw["g_qn"], lw["g_kn"], lw["dn_conv_w"],
            lw["dn_ac"], lw["dn_bc"]]
    if rope_tabs is not None:
        in_specs += [pl.BlockSpec((tm, HEAD_DIM), lambda i: (i % tps, 0))] * 2
        args += list(rope_tabs)

    def rows(width):
        return pl.BlockSpec((tm, width), lambda i: (i, 0))

    outs = [(ATTN_Q, BF16), (ATTN_KV, F32 if want_v else BF16), None]
    if want_v:
        outs.append((ATTN_KV, F32))
    outs += [(DN_QK, BF16), (DN_QK, BF16), (DN_VW, BF16), (LANES, F32), (D_REST, BF16)]
    vt_spec = pl.BlockSpec((ATTN_KV, tm), lambda i: (0, i))
    res = pl.pallas_call(
        functools.partial(_inproj_kernel, rope=rope_tabs is not None, want_v=want_v, tm=tm, tps=tps, tn=512),
        grid=(n // tm,),
        in_specs=in_specs,
        out_specs=[rows(o[0]) if o else vt_spec for o in outs],
        out_shape=[jax.ShapeDtypeStruct((n, o[0]), o[1]) if o else jax.ShapeDtypeStruct((ATTN_KV, n), BF16)
                   for o in outs],
        compiler_params=_cparams(("arbitrary",)),
    )(*args)
    res = list(res)
    q_p, k_p, v_p = res[:3]
    v_f32 = res[3] if want_v else None
    qn, kn, vn, gb, rest = res[-5:]
    return q_p, k_p, v_p, v_f32, qn, kn, vn, gb, rest


def _attn_kernel(*refs, has_cache, t, tk, tq, nq):
    refs = list(refs)
    q_ref = refs.pop(0)
    qn_ref = refs.pop(0) if nq > 1 else None
    kc_ref, vc_ref = (refs.pop(0), refs.pop(0)) if has_cache else (None, None)
    k_ref, v_ref, o_ref = refs[:3]
    s_bufs, m_bufs = refs[3:3 + len(refs[3:]) // 2], refs[3 + len(refs[3:]) // 2:]
    qi = pl.program_id(2)

    segs = []
    if has_cache:
        past = kc_ref.shape[0]
        segs += [(kc_ref, vc_ref, r, min(tk, past - r)) for r in range(0, past, tk)]
    segs += [(k_ref, v_ref, r, tk) for r in range(0, t, tk)]
    offs = [sum(w for _, _, _, w in segs[:i]) for i in range(len(segs))]

    def stack(ref):
        q = ref[...]
        return jnp.concatenate([q[:, g * HEAD_DIM:(g + 1) * HEAD_DIM] for g in range(Q_PER_KV)], axis=0)

    def scores(qs, s_ref, i, m_run):
        kr, _, r, w = segs[i]
        s = _dot_nt(kr[r:r + w, :].astype(BF16), qs)
        s_ref[offs[i]:offs[i] + w, :] = s
        for r0 in range(0, w, SUBLANES):
            blk = s[r0:r0 + SUBLANES, :]
            m_run = blk if m_run is None else jnp.maximum(m_run, blk)
        return m_run

    def weighted(s_ref, i, m, acc):
        _, vr, r, w = segs[i]
        p = jnp.exp2(s_ref[offs[i]:offs[i] + w, :] - m).astype(BF16)
        vt_ext = jnp.concatenate([vr[:, r:r + w].astype(BF16), jnp.ones((BF16_SUBLANES, w), BF16)], axis=0)
        pv = _dot(vt_ext, p)
        return pv if acc is None else acc + pv

    def finish(acc):
        o = (acc[:HEAD_DIM, :] / acc[HEAD_DIM:HEAD_DIM + 1, :]).T
        for g in range(Q_PER_KV):
            o_ref[:, g * HEAD_DIM:(g + 1) * HEAD_DIM] = o[g * tq:(g + 1) * tq].astype(o_ref.dtype)

    if nq > 1:
        @pl.when(qi == 0)
        def _():
            qs0 = stack(q_ref)
            m_run = None
            for i in range(len(segs)):
                m_run = scores(qs0, s_bufs[0], i, m_run)
            m_bufs[0][...] = m_run

        def step(s_cur, m_cur, s_nxt, m_nxt):
            qs_next = stack(qn_ref)
            m = jnp.max(m_cur[...], axis=0, keepdims=True)
            acc, m_run = None, None
            for i in range(len(segs)):
                m_run = scores(qs_next, s_nxt, i, m_run)
                acc = weighted(s_cur, i, m, acc)
            m_nxt[...] = m_run
            finish(acc)

        pl.when(qi % 2 == 0)(functools.partial(step, s_bufs[0], m_bufs[0], s_bufs[1], m_bufs[1]))
        pl.when(qi % 2 == 1)(functools.partial(step, s_bufs[1], m_bufs[1], s_bufs[0], m_bufs[0]))
    else:
        qs = stack(q_ref)
        m_run = None
        for i in range(len(segs)):
            m_run = scores(qs, s_bufs[0], i, m_run)
        m = jnp.max(m_run, axis=0, keepdims=True)
        acc = None
        for i in range(len(segs)):
            acc = weighted(s_bufs[0], i, m, acc)
        finish(acc)


def _attention(q_p, k_p, vt_p, b, t, cache):
    n = q_p.shape[0]
    tq = 128 if t > 256 else 256
    tk = min(t, 512)
    nq = t // tq
    qw = Q_PER_KV * HEAD_DIM
    in_specs = [pl.BlockSpec((tq, qw), lambda bi, j, qi: (bi * nq + qi, j))]
    args = [q_p]
    if nq > 1:
        in_specs.append(pl.BlockSpec((tq, qw), lambda bi, j, qi: (bi * nq + jnp.minimum(qi + 1, nq - 1), j)))
        args.append(q_p)
    if cache is not None:
        cache_k, cache_vt, layer = cache
        past = cache_k.shape[2]
        in_specs += [pl.BlockSpec((None, None, past, HEAD_DIM), lambda bi, j, qi: (bi, layer, 0, j)),
                     pl.BlockSpec((None, None, HEAD_DIM, past), lambda bi, j, qi: (bi, layer, j, 0))]
        args += [cache_k, cache_vt]
    in_specs += [pl.BlockSpec((t, HEAD_DIM), lambda bi, j, qi: (bi, j)),
                 pl.BlockSpec((HEAD_DIM, t), lambda bi, j, qi: (j, bi))]
    args += [k_p, vt_p]
    n_keys = t + (cache[0].shape[2] if cache is not None else 0)
    slots = 2 if nq > 1 else 1
    return pl.pallas_call(
        functools.partial(_attn_kernel, has_cache=cache is not None, t=t, tk=tk, tq=tq, nq=nq),
        grid=(b, N_KV_HEADS, nq),
        in_specs=in_specs,
        out_specs=pl.BlockSpec((tq, qw), lambda bi, j, qi: (bi * nq + qi, j)),
        out_shape=jax.ShapeDtypeStruct((n, ATTN_Q), BF16),
        scratch_shapes=([pltpu.VMEM((n_keys, Q_PER_KV * tq), F32)] * slots
                        + [pltpu.VMEM((SUBLANES, Q_PER_KV * tq), F32)] * slots),
        compiler_params=_cparams(("arbitrary", "arbitrary", "arbitrary")),
    )(*args)


def _lane_pick(x, lane):
    idx = lax.broadcasted_iota(jnp.int32, x.shape, 1)
    return jnp.sum(jnp.where(idx == lane, x, 0.0), axis=-1, keepdims=True)


def _block_diag(x2):
    xb = x2.astype(BF16)
    z = jnp.zeros((DN_CHUNK, DN_CHUNK), BF16)
    return jnp.concatenate([jnp.concatenate([xb[:, :DN_CHUNK], z], axis=1),
                            jnp.concatenate([z, xb[:, DN_CHUNK:]], axis=1)], axis=0)


def _mm_pair(x2, y2):
    return _dot(x2.astype(BF16), _block_diag(y2))


def _unit_tri_inverses(lmats, ri, ci):
    def blk(s):
        return (ri ^ ci) < s

    eye = jnp.where(ri == ci, 1.0, 0.0)
    ps = [jnp.where(blk(DN_BASE), -lm, 0.0) for lm in lmats]
    xs = [eye + p for p in ps]
    s = 2
    while s < DN_BASE:
        ps = [_mm_pair(p, p) for p in ps]
        yield
        xs = [x + _mm_pair(x, p) for x, p in zip(xs, ps)]
        yield
        s *= 2
    s = DN_BASE
    while s < DN_CHUNK:
        sel = blk(2 * s) & jnp.logical_not(blk(s))
        ts = [_mm_pair(jnp.where(sel, lm, 0.0), x) for lm, x in zip(lmats, xs)]
        yield
        xs = [x - _mm_pair(x, t) for x, t in zip(xs, ts)]
        yield
        s *= 2
    return xs


def _dn_kernel(*refs, reverse, has_s0, want_state, nc, d, nt):
    refs = list(refs)
    q_ref, k_ref, v_ref, gb_ref, gr_ref = refs[:5]
    refs = refs[5:]
    s0_ref = refs.pop(0) if has_s0 else None
    o_ref = refs.pop(0)
    sfin_ref = refs.pop(0) if want_state else None
    s_sc = refs.pop(0)
    bufs = (refs[:5], refs[5:10])
    n_pairs = DN_HEADS // 2
    c2 = 2 * DN_CHUNK
    tt = nc * DN_CHUNK
    insts = [(c, p) for c in range(nc) for p in range(n_pairs)]

    g = pl.program_id(0)
    i_scan = (g - 1) % nt

    def pair_cols(x, lane_a, lane_b):
        shape = (x.shape[0], DN_CHUNK)
        return jnp.concatenate([jnp.broadcast_to(_lane_pick(x, lane_a), shape),
                                jnp.broadcast_to(_lane_pick(x, lane_b), shape)], axis=1)

    def prepare(buf):
        u_buf, wq_buf, attn_buf, kt_buf, gt_buf = buf
        gb = gb_ref[...]
        g_r = gr_ref[...]
        bi = lax.broadcasted_iota(jnp.int32, (tt, tt), 0)
        bj = lax.broadcasted_iota(jnp.int32, (tt, tt), 1)
        same = (bi ^ bj) < DN_CHUNK
        if reverse:
            tri_c = jnp.where(same & (bj >= bi), 1.0, 0.0).astype(BF16)
            tri_r = jnp.where(same & (bi >= bj), 1.0, 0.0).astype(BF16)
        else:
            tri_c = jnp.where(same & (bj <= bi), 1.0, 0.0).astype(BF16)
            tri_r = jnp.where(same & (bi <= bj), 1.0, 0.0).astype(BF16)
        gcum_c = sum(_dot(tri_c, part) for part in _split3(gb))
        gcum_r = sum(_dot(part, tri_r) for part in _split3(g_r))
        yield

        ri = lax.broadcasted_iota(jnp.int32, (DN_CHUNK, c2), 0)
        ci = lax.broadcasted_iota(jnp.int32, (DN_CHUNK, c2), 1) & (DN_CHUNK - 1)
        incl = (ci >= ri) if reverse else (ci <= ri)
        strict = (ci > ri) if reverse else (ci < ri)
        pre = []
        for c, p in insts:
            rs = slice(c * DN_CHUNK, (c + 1) * DN_CHUNK)
            cs = slice(p * c2, (p + 1) * c2)
            lane_b = d * DN_HEADS + 2 * p
            lane_g = 2 * DN_HEADS + d * DN_HEADS + 2 * p
            q2 = q_ref[rs, cs].astype(F32)
            k2 = k_ref[rs, cs].astype(F32)
            v2 = v_ref[rs, cs].astype(F32)
            g_i = pair_cols(gcum_c[rs, :], lane_g, lane_g + 1)
            b_i = pair_cols(gb[rs, :], lane_b, lane_b + 1)
            g_j = jnp.concatenate([gcum_r[lane_g:lane_g + 1, rs], gcum_r[lane_g + 1:lane_g + 2, rs]], axis=1)
            g_tot = g_i[0:1, :] if reverse else g_i[DN_CHUNK - 1:DN_CHUNK, :]
            decay = jnp.where(incl, jnp.exp(jnp.where(incl, g_i - g_j, 0.0)), 0.0)
            pre.append(dict(q2=q2, k2=k2, v2=v2, g_i=g_i, b_i=b_i, g_tot=g_tot, decay=decay, kb=k2 * b_i))
        a2s = [_dot_nt(jnp.concatenate([p["kb"], p["q2"]], axis=0).astype(BF16), _block_diag(p["k2"]))
               for p in pre]
        yield
        lmats = [jnp.where(strict, a2[:DN_CHUNK] * p["decay"], 0.0) for a2, p in zip(a2s, pre)]
        for n, (a2, p) in enumerate(zip(a2s, pre)):
            attn_buf[n] = (a2[DN_CHUNK:] * p["decay"]).astype(BF16)
        tinvs = yield from _unit_tri_inverses(lmats, ri, ci)
        gams = [jnp.exp(p["g_i"]) for p in pre]
        us = [_mm_pair(t, p["v2"] * p["b_i"]) for t, p in zip(tinvs, pre)]
        yield
        ws = [_mm_pair(t, p["kb"] * gam) for t, p, gam in zip(tinvs, pre, gams)]
        yield
        for n, (p, u, w, gam) in enumerate(zip(pre, us, ws, gams)):
            u_buf[n] = u
            wq_buf[n] = jnp.concatenate([w, p["q2"] * gam], axis=0).astype(BF16)
            kt_buf[n] = (p["k2"] * jnp.exp(p["g_tot"] - p["g_i"])).astype(BF16)
            gt_buf[n] = jnp.broadcast_to(jnp.exp(p["g_tot"]), (SUBLANES, c2))

    def scan(buf):
        u_buf, wq_buf, attn_buf, kt_buf, gt_buf = buf
        si = lax.broadcasted_iota(jnp.int32, (c2, c2), 0)
        sj = lax.broadcasted_iota(jnp.int32, (c2, c2), 1)
        on_diag = (si < DN_CHUNK) == (sj < DN_CHUNK)
        pairs = range(n_pairs)
        states = [s_sc[p] for p in pairs]
        for c in (range(nc - 1, -1, -1) if reverse else range(nc)):
            ns = [c * n_pairs + p for p in pairs]
            m1s = [_dot(wq_buf[n], s.astype(BF16)) for n, s in zip(ns, states)]
            yield
            v_news = [u_buf[n] - m1[:DN_CHUNK] for n, m1 in zip(ns, m1s)]
            outs = [m1[DN_CHUNK:] + _mm_pair(attn_buf[n], v) for n, m1, v in zip(ns, m1s, v_news)]
            yield
            states = [s * gt_buf[n][0:1, :] + jnp.where(on_diag, _dot_tn(kt_buf[n], v.astype(BF16)), 0.0)
                      for n, s, v in zip(ns, states, v_news)]
            for p in pairs:
                o_ref[c * DN_CHUNK:(c + 1) * DN_CHUNK, p * c2:(p + 1) * c2] = outs[p]
            yield
        for p in pairs:
            s_sc[p] = states[p]

    def step(par):
        @pl.when(i_scan == 0)
        def _():
            s_sc[...] = jnp.zeros_like(s_sc)
            if has_s0:
                for h in range(DN_HEADS):
                    lo = (h % 2) * DN_CHUNK
                    s_sc[h // 2, lo:lo + DN_CHUNK, lo:lo + DN_CHUNK] = s0_ref[h]

        _interleave(prepare(bufs[par]), scan(bufs[1 - par]))

        if want_state:
            @pl.when(i_scan == nt - 1)
            def _():
                for h in range(DN_HEADS):
                    lo = (h % 2) * DN_CHUNK
                    sfin_ref[h] = s_sc[h // 2, lo:lo + DN_CHUNK, lo:lo + DN_CHUNK]

    pl.when(g == 0)(lambda: _interleave(prepare(bufs[0])))
    pl.when((g > 0) & (g % 2 == 0))(functools.partial(step, 0))
    pl.when(g % 2 == 1)(functools.partial(step, 1))


def _deltanet_dir(qn, kn, vn, gb, gates_row, s0, b, t, d, want_state):
    n = qn.shape[0]
    nc = min(4, t // DN_CHUNK)
    tt = nc * DN_CHUNK
    nt = t // tt
    n_tiles = b * nt
    reverse = d == 1
    c2 = 2 * DN_CHUNK
    n_inst = nc * (DN_HEADS // 2)

    def seq(gt):
        bi, i = gt // nt, gt % nt
        return bi, ((nt - 1 - i) if reverse else i)

    def prep_tile(g):
        return seq(jnp.minimum(g, n_tiles - 1))

    def scan_tile(g):
        return seq(jnp.maximum(g - 1, 0))

    def row_block(bt):
        return bt[0] * nt + bt[1]

    qkv_spec = pl.BlockSpec((tt, DN_QK), lambda g: (row_block(prep_tile(g)), 0))
    in_specs = [
        qkv_spec, qkv_spec, qkv_spec,
        pl.BlockSpec((tt, LANES), lambda g: (row_block(prep_tile(g)), 0)),
        pl.BlockSpec((None, 4 * DN_HEADS, tt), lambda g: (prep_tile(g)[0], 0, prep_tile(g)[1])),
    ]
    args = [qn, kn, vn, gb, gates_row]
    if s0 is not None:
        in_specs.append(pl.BlockSpec((None, None, DN_HEADS, DN_DK, DN_DV),
                                     lambda g: (scan_tile(g)[0], d, 0, 0, 0)))
        args.append(s0)
    out_specs = [pl.BlockSpec((tt, DN_VW), lambda g: (row_block(scan_tile(g)), 0))]
    out_shape = [jax.ShapeDtypeStruct((n, DN_VW), F32)]
    if want_state:
        out_specs.append(pl.BlockSpec((None, DN_HEADS, DN_DK, DN_DV), lambda g: (scan_tile(g)[0], 0, 0, 0)))
        out_shape.append(jax.ShapeDtypeStruct((b, DN_HEADS, DN_DK, DN_DV), F32))
    prepared = [pltpu.VMEM((n_inst, DN_CHUNK, c2), F32), pltpu.VMEM((n_inst, c2, c2), BF16),
                pltpu.VMEM((n_inst, DN_CHUNK, c2), BF16), pltpu.VMEM((n_inst, DN_CHUNK, c2), BF16),
                pltpu.VMEM((n_inst, SUBLANES, c2), F32)]
    res = pl.pallas_call(
        functools.partial(_dn_kernel, reverse=reverse, has_s0=s0 is not None, want_state=want_state,
                          nc=nc, d=d, nt=nt),
        grid=(n_tiles + 1,),
        in_specs=in_specs,
        out_specs=out_specs,
        out_shape=out_shape,
        scratch_shapes=[pltpu.VMEM((DN_HEADS // 2, c2, c2), F32)] + prepared + prepared,
        compiler_params=_cparams(("arbitrary",)),
    )(*args)
    return res if want_state else (res[0], None)


def _post_kernel(x_ref, mod_ref, cb_ref, cc_ref, cx_ref, ccp_ref, cxp_ref, ccn_ref, cxn_ref, cw_ref,
                 att_ref, of_ref, ob_ref, z_ref, ng_ref, ga_ref, gb_ref, gc_ref, wpa_ref, wpb_ref, wpc_ref, wo_ref,
                 gpost_ref, o_ref, *, tm, tps):
    i = pl.program_id(0)
    first = (i % tps) == 0
    last = (i % tps) == tps - 1
    u = cc_ref[...].astype(F32) * cx_ref[...].astype(F32)
    hl = BF16_SUBLANES - 1
    prev_row = jnp.where(first, 0.0, ccp_ref[hl:hl + 1, :].astype(F32) * cxp_ref[hl:hl + 1, :].astype(F32))
    next_row = jnp.where(last, 0.0, ccn_ref[0:1, :].astype(F32) * cxn_ref[0:1, :].astype(F32))
    rows = lax.broadcasted_iota(jnp.int32, (tm, 1), 0)
    u_prev = jnp.where(rows == 0, prev_row, pltpu.roll(u, 1, axis=0))
    u_next = jnp.where(rows == tm - 1, next_row, pltpu.roll(u, tm - 1, axis=0))
    cw = cw_ref[...]
    conv = cw[0:1, :] * u_prev + cw[1:2, :] * u + cw[2:3, :] * u_next
    ya = _dot((cb_ref[...].astype(F32) * conv).astype(BF16), wpa_ref[...])
    yb = _dot(att_ref[...], wpb_ref[...])
    o = of_ref[...] + ob_ref[...]
    z = z_ref[...].astype(F32)
    parts = []
    for h in range(DN_HEADS):
        sl = slice(h * DN_DV, (h + 1) * DN_DV)
        parts.append((_rms(o[:, sl], ng_ref[...]) * _silu(z[:, sl])).astype(BF16))
    yc = _dot(jnp.concatenate(parts, axis=1), wpc_ref[...])
    mix_in = (_sigmoid(ga_ref[...].astype(F32)) * ya + _sigmoid(gb_ref[...].astype(F32)) * yb
              + _sigmoid(gc_ref[...].astype(F32)) * yc)
    mix = _dot(mix_in.astype(BF16), wo_ref[...])
    o_ref[...] = x_ref[...] + mod_ref[0, 2:3, :] * _rms(mix, gpost_ref[...])


def _post_mixer(x, mod3, mod_base, rows_per_mod, t, proj, att, o_f, o_b, conv_w, ng, wpa, wpb, wpc, wo, g_post1):
    n = x.shape[0]
    tm = 256
    tps = t // tm
    hb = tm // BF16_SUBLANES
    cwid = CONV_WIDTH
    c0 = REST_CONV // cwid

    def prev(i):
        return jnp.maximum(i * hb - 1, 0)

    def nxt(i):
        return jnp.minimum((i + 1) * hb, n // BF16_SUBLANES - 1)

    g0 = REST_GATE // D_MODEL
    in_specs = [
        pl.BlockSpec((tm, D_MODEL), lambda i: (i, 0)),
        pl.BlockSpec((1, 6, D_MODEL), lambda i: (mod_base + (i * tm) // rows_per_mod, 0, 0)),
        pl.BlockSpec((tm, cwid), lambda i: (i, c0)),
        pl.BlockSpec((tm, cwid), lambda i: (i, c0 + 1)),
        pl.BlockSpec((tm, cwid), lambda i: (i, c0 + 2)),
        pl.BlockSpec((BF16_SUBLANES, cwid), lambda i: (prev(i), c0 + 1)),
        pl.BlockSpec((BF16_SUBLANES, cwid), lambda i: (prev(i), c0 + 2)),
        pl.BlockSpec((BF16_SUBLANES, cwid), lambda i: (nxt(i), c0 + 1)),
        pl.BlockSpec((BF16_SUBLANES, cwid), lambda i: (nxt(i), c0 + 2)),
        _const_spec((3, cwid)),
        pl.BlockSpec((tm, ATTN_Q), lambda i: (i, 0)),
        pl.BlockSpec((tm, DN_VW), lambda i: (i, 0)),
        pl.BlockSpec((tm, DN_VW), lambda i: (i, 0)),
        pl.BlockSpec((tm, DN_VW), lambda i: (i, REST_Z // DN_VW)),
        _const_spec((1, DN_DV)),
        pl.BlockSpec((tm, D_MODEL), lambda i: (i, g0)),
        pl.BlockSpec((tm, D_MODEL), lambda i: (i, g0 + 1)),
        pl.BlockSpec((tm, D_MODEL), lambda i: (i, g0 + 2)),
        _const_spec((CONV_WIDTH, D_MODEL)), _const_spec((ATTN_Q, D_MODEL)),
        _const_spec((DN_VW, D_MODEL)), _const_spec((D_MODEL, D_MODEL)),
        _const_spec((1, D_MODEL)),
    ]
    return pl.pallas_call(
        functools.partial(_post_kernel, tm=tm, tps=tps),
        grid=(n // tm,),
        in_specs=in_specs,
        out_specs=pl.BlockSpec((tm, D_MODEL), lambda i: (i, 0)),
        out_shape=jax.ShapeDtypeStruct((n, D_MODEL), F32),
        compiler_params=_cparams(("arbitrary",)),
    )(x, mod3, proj, proj, proj, proj, proj, proj, proj, conv_w, att, o_f, o_b, proj, ng,
      proj, proj, proj, wpa, wpb, wpc, wo, g_post1)


def _ffn_kernel(x_ref, mod_ref, gpre_ref, wg_ref, wu_ref, wd_ref, gpost_ref, o_ref, act_sc, *, tf):
    x = x_ref[...]
    h2 = (_rms(x, gpre_ref[...]) * (1.0 + mod_ref[0, 4:5, :]) + mod_ref[0, 3:4, :]).astype(BF16)
    for j in range(0, D_FF, tf):
        gate = _dot(h2, wg_ref[:, j:j + tf])
        up = _dot(h2, wu_ref[:, j:j + tf])
        act_sc[:, j:j + tf] = (_silu(gate) * up).astype(BF16)
    ffn = _dot(act_sc[...], wd_ref[...])
    o_ref[...] = x + mod_ref[0, 5:6, :] * _rms(ffn, gpost_ref[...])


def _ffn(x, mod3, mod_base, rows_per_mod, g_pre2, wg, wu, wd, g_post2):
    n = x.shape[0]
    tm = 512
    return pl.pallas_call(
        functools.partial(_ffn_kernel, tf=256),
        grid=(n // tm,),
        in_specs=[pl.BlockSpec((tm, D_MODEL), lambda i: (i, 0)),
                  pl.BlockSpec((1, 6, D_MODEL), lambda i: (mod_base + (i * tm) // rows_per_mod, 0, 0)),
                  _const_spec((1, D_MODEL)),
                  _const_spec((D_MODEL, D_FF)), _const_spec((D_MODEL, D_FF)), _const_spec((D_FF, D_MODEL)),
                  _const_spec((1, D_MODEL))],
        out_specs=pl.BlockSpec((tm, D_MODEL), lambda i: (i, 0)),
        out_shape=jax.ShapeDtypeStruct((n, D_MODEL), F32),
        scratch_shapes=[pltpu.VMEM((tm, D_FF), BF16)],
        compiler_params=_cparams(("arbitrary",)),
    )(x, mod3, g_pre2, wg, wu, wd, g_post2)


def _rope_tables(t):
    rows = t // GRID_W
    row_id = jnp.repeat(jnp.arange(rows, dtype=F32), GRID_W)
    col_id = jnp.tile(jnp.arange(GRID_W, dtype=F32), rows)
    n_freq = HEAD_DIM // 4
    inv_freq = ROPE_THETA ** (-jnp.arange(n_freq, dtype=F32) / n_freq)
    ang = jnp.concatenate([row_id[:, None] * inv_freq, col_id[:, None] * inv_freq], axis=-1)
    cos = jnp.repeat(jnp.cos(ang), 2, axis=-1)
    sin = jnp.repeat(jnp.sin(ang), 2, axis=-1)
    sign = jnp.tile(jnp.array([-1.0, 1.0], F32), HEAD_DIM // 2)
    return cos, sin * sign


def _layer(x, b, t, mod3, mod_base, rows_per_mod, lw, rope_tabs, cache, state0, want_state):
    q_p, k_p, v_p, v_f32, qn, kn, vn, gb, proj = _inproj(x, t, mod3, mod_base, rows_per_mod, lw, rope_tabs,
                                                         want_state)
    att = _attention(q_p, k_p, v_p, b, t, cache)
    gates_row = jnp.swapaxes(gb[:, :4 * DN_HEADS].reshape(b, t, 4 * DN_HEADS), 1, 2)
    dn = [_deltanet_dir(qn, kn, vn, gb, gates_row, state0, b, t, d, want_state) for d in range(2)]
    x = _post_mixer(x, mod3, mod_base, rows_per_mod, t, proj, att, dn[0][0], dn[1][0], lw["conv_w"],
                    lw["dn_norm_g"], lw["w_pa"], lw["w_pb"], lw["w_pc"], lw["w_o"], lw["g_post1"])
    x = _ffn(x, mod3, mod_base, rows_per_mod, lw["g_pre2"], lw["w_gate"], lw["w_up"], lw["w_down"],
             lw["g_post2"])
    if not want_state:
        return x, None
    s_fin = jnp.stack([dn[0][1], dn[1][1]], axis=1)
    return x, (k_p, v_f32, s_fin)


def kernel(x_prompt, x_sample, cache_k, cache_v, state_dn, c, c_ctx, w_mod, b_mod, g_pre1, g_post1, g_pre2, g_post2, w_in, conv_w, g_qn, g_kn, dn_conv_w, dn_a_log, dn_dt_bias, dn_norm_g, w_pa, w_pb, w_pc, w_o, w_gate, w_up, w_down):
    bp, tp, d = x_prompt.shape
    bs, ts, _ = x_sample.shape
    depth = w_mod.shape[0]
    past = cache_k.shape[2]

    mod_rows = -(-(bs + 1) // SUBLANES) * SUBLANES
    cv = jnp.zeros((mod_rows, d), F32).at[:bs].set(c).at[bs].set(c_ctx)
    mod_all = _modulation(cv, w_mod, b_mod).reshape(depth, mod_rows, 6, d)

    w_in_r = jnp.concatenate(
        [w_in[:, :, ORIG_Q:ORIG_Z],
         w_in[:, :, ORIG_SMALL:ORIG_GATE], jnp.zeros((depth, d, LANES - 4 * DN_HEADS), w_in.dtype),
         w_in[:, :, ORIG_GATE:D_IN], w_in[:, :, ORIG_CONV:ORIG_Q], w_in[:, :, ORIG_Z:ORIG_SMALL]],
        axis=-1).astype(BF16)
    lane_pad = ((0, 0), (2 * DN_HEADS, LANES - 4 * DN_HEADS))
    a_flat = dn_a_log.reshape(depth, 2 * DN_HEADS)
    b_flat = dn_dt_bias.reshape(depth, 2 * DN_HEADS)
    rope_tabs = _rope_tables(ts)
    cache_k4 = cache_k.reshape(bs, depth, past, ATTN_KV)
    cache_v4 = jnp.swapaxes(cache_v.reshape(bs, depth, past, ATTN_KV), 2, 3)

    xp = x_prompt.reshape(bp * tp, d)
    xs = x_sample.reshape(bs * ts, d)
    ks, vs, ss = [], [], []
    for l in range(depth):
        lw = {
            "g_pre1": g_pre1[l][None], "g_post1": g_post1[l][None], "g_pre2": g_pre2[l][None],
            "g_post2": g_post2[l][None], "w_in": w_in_r[l], "conv_w": conv_w[l],
            "g_qn": g_qn[l][None], "g_kn": g_kn[l][None], "dn_conv_w": dn_conv_w[l],
            "dn_ac": jnp.pad(a_flat[l][None], lane_pad), "dn_bc": jnp.pad(b_flat[l][None], lane_pad),
            "dn_norm_g": dn_norm_g[l][None],
            "w_pa": w_pa[l].astype(BF16), "w_pb": w_pb[l].astype(BF16), "w_pc": w_pc[l].astype(BF16),
            "w_o": w_o[l].astype(BF16), "w_gate": w_gate[l].astype(BF16), "w_up": w_up[l].astype(BF16),
            "w_down": w_down[l].astype(BF16),
        }
        mod3 = mod_all[l]
        xp, (k_l, v_l, s_l) = _layer(xp, bp, tp, mod3, bs, bp * tp, lw, None, None, None, True)
        ks.append(k_l.reshape(bp, tp, N_KV_HEADS, HEAD_DIM))
        vs.append(v_l.reshape(bp, tp, N_KV_HEADS, HEAD_DIM))
        ss.append(s_l)
        xs, _ = _layer(xs, bs, ts, mod3, 0, ts, lw, rope_tabs, (cache_k4, cache_v4, l), state_dn[:, l], False)
    return (xp.reshape(bp, tp, d), xs.reshape(bs, ts, d), jnp.stack(ks, axis=1), jnp.stack(vs, axis=1),
            jnp.stack(ss, axis=1))
```

```python
import functools

import jax
import jax.numpy as jnp
import numpy as np
from jax import lax
from jax.experimental import pallas as pl
from jax.experimental.pallas import tpu as pltpu

F32 = jnp.float32
BF16 = jnp.bfloat16

D_MODEL = 1024
EPS = 1e-6
GRID_W = 64
N_HEADS = 8
N_KV_HEADS = 2
Q_PER_KV = N_HEADS // N_KV_HEADS
HEAD_DIM = 128
ATTN_Q = N_HEADS * HEAD_DIM
ATTN_KV = N_KV_HEADS * HEAD_DIM
ROPE_THETA = 10000.0
CONV_WIDTH = 512
DN_HEADS = 4
DN_DK = 128
DN_DV = 128
DN_QK = DN_HEADS * DN_DK
DN_VW = DN_HEADS * DN_DV
D_FF = 2816

SUBLANES = 8
BF16_SUBLANES = 16
LANES = 128

COL_Q = 0
COL_K = COL_Q + ATTN_Q
COL_V = COL_K + ATTN_KV
COL_DN = COL_V + ATTN_KV
COL_SMALL = COL_DN + 3 * DN_QK
COL_REST = COL_SMALL + LANES
REST_GATE = 0
REST_CONV = REST_GATE + 3 * D_MODEL
REST_Z = REST_CONV + 3 * CONV_WIDTH
D_REST = REST_Z + DN_VW
D_INP = COL_REST + D_REST
ORIG_CONV, ORIG_Q, ORIG_K = 0, 3 * CONV_WIDTH, 3 * CONV_WIDTH + ATTN_Q
ORIG_DN = ORIG_K + 2 * ATTN_KV
ORIG_Z = ORIG_DN + 3 * DN_QK
ORIG_SMALL = ORIG_Z + DN_VW
ORIG_GATE = ORIG_SMALL + 4 * DN_HEADS
D_IN = ORIG_GATE + 3 * D_MODEL

Q_SCALE = 1.4426950408889634 * HEAD_DIM ** -0.5
DN_CHUNK = 128
DN_BASE = 16
VMEM_LIMIT = 56 * 1024 * 1024


def _cparams(sem):
    return pltpu.CompilerParams(dimension_semantics=sem, vmem_limit_bytes=VMEM_LIMIT)


def _const_spec(shape):
    nd = len(shape)
    return pl.BlockSpec(shape, lambda *_: (0,) * nd, pipeline_mode=pl.Buffered(1))


def _dot(a, b):
    return jnp.dot(a, b, preferred_element_type=F32)


def _dot_nt(a, b):
    return lax.dot_general(a, b, (((1,), (1,)), ((), ())), preferred_element_type=F32)


def _dot_tn(a, b):
    return lax.dot_general(a, b, (((0,), (0,)), ((), ())), preferred_element_type=F32)


def _mm(a, b):
    return _dot(a.astype(BF16), b.astype(BF16))


def _rms(x, g):
    return x * lax.rsqrt(jnp.mean(x * x, axis=-1, keepdims=True) + EPS) * g


def _sigmoid(x):
    return 1.0 / (1.0 + jnp.exp(-x))


def _silu(x):
    return x * _sigmoid(x)


def _softplus(x):
    return jnp.maximum(x, 0.0) + jnp.log1p(jnp.exp(-jnp.abs(x)))


def _split3(x):
    hi = x.astype(BF16)
    r = x - hi.astype(F32)
    mid = r.astype(BF16)
    lo = (r - mid.astype(F32)).astype(BF16)
    return hi, mid, lo


def _mod_kernel(cv_ref, w_ref, b_ref, o_ref):
    cv = cv_ref[...]
    o_ref[0] = _dot(_silu(cv).astype(BF16), w_ref[0].astype(BF16)) + b_ref[0]


def _modulation(cv, w_mod, b_mod):
    depth, d, n6 = w_mod.shape
    rows = cv.shape[0]
    tn = 1536
    return pl.pallas_call(
        _mod_kernel,
        grid=(depth, n6 // tn),
        in_specs=[pl.BlockSpec((rows, d), lambda l, j: (0, 0)),
                  pl.BlockSpec((1, d, tn), lambda l, j: (l, 0, j)),
                  pl.BlockSpec((1, 1, tn), lambda l, j: (l, 0, j))],
        out_specs=pl.BlockSpec((1, rows, tn), lambda l, j: (l, 0, j)),
        out_shape=jax.ShapeDtypeStruct((depth, rows, n6), F32),
        compiler_params=_cparams(("arbitrary", "arbitrary")),
    )(cv, w_mod, b_mod.reshape(depth, 1, n6))


def _swap_pairs(y):
    lane = lax.broadcasted_iota(jnp.int32, y.shape, 1)
    return jnp.where(lane % 2 == 0, pltpu.roll(y, LANES - 1, axis=1), pltpu.roll(y, 1, axis=1))


def _interleave(*gens):
    gens = list(gens)
    while gens:
        for gen in list(gens):
            try:
                next(gen)
            except StopIteration:
                gens.remove(gen)


def _inproj_kernel(*refs, rope, want_v, tm, tps, tn):
    refs = list(refs)
    x_ref, xp_ref, xn_ref, mod_ref, g_ref, w_ref, gq_ref, gk_ref, cw_ref, ac_ref, bc_ref = refs[:11]
    refs = refs[11:]
    cos_ref, sin_ref = (refs.pop(0), refs.pop(0)) if rope else (None, None)
    q_out, k_out, v_out = refs[:3]
    refs = refs[3:]
    vf_out = refs.pop(0) if want_v else None
    qn_out, kn_out, vn_out, gb_out, gbt_out, rest_out = refs
    i = pl.program_id(0)

    def modnorm(x):
        return (_rms(x, g_ref[...]) * (1.0 + mod_ref[0, 1:2, :]) + mod_ref[0, 0:1, :]).astype(BF16)

    h = modnorm(x_ref[...])
    h_halo = modnorm(jnp.concatenate([xp_ref[...], xn_ref[...]], axis=0))

    def qk_head(x, g):
        y = _rms(x, g)
        if rope:
            y = y * cos_ref[...] + _swap_pairs(y) * sin_ref[...]
        return y

    def attn_epilogue():
        qkv = _dot(h, w_ref[:, COL_Q:COL_DN])
        yield
        for hh in range(N_HEADS):
            sl = slice(hh * HEAD_DIM, (hh + 1) * HEAD_DIM)
            q_out[:, sl] = (qk_head(qkv[:, sl], gq_ref[...]) * Q_SCALE).astype(q_out.dtype)
            if hh % 2 == 1:
                yield
        for hh in range(N_KV_HEADS):
            sl = slice(hh * HEAD_DIM, (hh + 1) * HEAD_DIM)
            k_out[:, sl] = qk_head(qkv[:, COL_K + hh * HEAD_DIM:COL_K + (hh + 1) * HEAD_DIM],
                                   gk_ref[...]).astype(k_out.dtype)
        v = qkv[:, COL_V:COL_DN]
        v_out[...] = v.T.astype(v_out.dtype)
        if want_v:
            vf_out[...] = v

    def dn_epilogue():
        dn = _dot(h, w_ref[:, COL_DN:COL_SMALL])
        dn_halo = _dot(h_halo, w_ref[:, COL_DN:COL_SMALL])
        gl = _dot(h, w_ref[:, COL_SMALL:COL_REST])
        yield
        rows = lax.broadcasted_iota(jnp.int32, (tm, 1), 0)
        prev_row = jnp.where((i % tps) == 0, 0.0, dn_halo[SUBLANES - 1:SUBLANES, :])
        next_row = jnp.where((i % tps) == tps - 1, 0.0, dn_halo[SUBLANES:SUBLANES + 1, :])
        dn_prev = jnp.where(rows == 0, prev_row, pltpu.roll(dn, 1, axis=0))
        dn_next = jnp.where(rows == tm - 1, next_row, pltpu.roll(dn, tm - 1, axis=0))
        cw = cw_ref[...]

        def conv_silu(sl):
            return _silu(cw[0:1, sl] * dn_prev[:, sl] + cw[1:2, sl] * dn[:, sl] + cw[2:3, sl] * dn_next[:, sl])

        for hh in range(DN_HEADS):
            sl = slice(hh * DN_DK, (hh + 1) * DN_DK)
            qh = conv_silu(sl)
            kh = conv_silu(slice(DN_QK + hh * DN_DK, DN_QK + (hh + 1) * DN_DK))
            qn_out[:, sl] = (qh * lax.rsqrt(jnp.sum(qh * qh, axis=-1, keepdims=True) + EPS)
                             * (DN_DK ** -0.5)).astype(qn_out.dtype)
            kn_out[:, sl] = (kh * lax.rsqrt(jnp.sum(kh * kh, axis=-1, keepdims=True) + EPS)).astype(kn_out.dtype)
            vn_out[:, sl] = conv_silu(slice(2 * DN_QK + hh * DN_DV, 2 * DN_QK + (hh + 1) * DN_DV)
                                      ).astype(vn_out.dtype)
            yield
        lane = lax.broadcasted_iota(jnp.int32, gl.shape, 1)
        gb = jnp.where(lane < 2 * DN_HEADS, _sigmoid(gl), -jnp.exp(ac_ref[...]) * _softplus(gl + bc_ref[...]))
        gb_out[...] = gb
        gbt_out[...] = gb.T

    def remaining():
        for j in range(0, D_REST, tn):
            rest_out[:, j:j + tn] = _dot(h, w_ref[:, COL_REST + j:COL_REST + j + tn]).astype(rest_out.dtype)
            yield

    _interleave(dn_epilogue(), attn_epilogue(), remaining())


def _inproj(x, t, mod3, mod_base, rows_per_mod, lw, rope_tabs, want_v):
    n = x.shape[0]
    tm = 256
    tps = t // tm
    halo = tm // SUBLANES
    in_specs = [pl.BlockSpec((tm, D_MODEL), lambda i: (i, 0)),
                pl.BlockSpec((SUBLANES, D_MODEL), lambda i: (jnp.maximum(i * halo - 1, 0), 0)),
                pl.BlockSpec((SUBLANES, D_MODEL), lambda i: (jnp.minimum((i + 1) * halo, n // SUBLANES - 1), 0)),
                pl.BlockSpec((1, 6, D_MODEL), lambda i: (mod_base + (i * tm) // rows_per_mod, 0, 0)),
                _const_spec((1, D_MODEL)),
                _const_spec((D_MODEL, D_INP)),
                _const_spec((1, HEAD_DIM)), _const_spec((1, HEAD_DIM)),
                _const_spec((3, 3 * DN_QK)), _const_spec((1, LANES)), _const_spec((1, LANES))]
    args = [x, x, x, mod3, lw["g_pre1"], lw["w_in"], lw["g_qn"], lw["g_kn"], lw["dn_conv_w"],
            lw["dn_ac"], lw["dn_bc"]]
    if rope_tabs is not None:
        in_specs += [pl.BlockSpec((tm, HEAD_DIM), lambda i: (i % tps, 0))] * 2
        args += list(rope_tabs)

    outs = [(ATTN_Q, BF16, False), (ATTN_KV, F32 if want_v else BF16, False), (ATTN_KV, BF16, True)]
    if want_v:
        outs.append((ATTN_KV, F32, False))
    outs += [(DN_QK, BF16, False), (DN_QK, BF16, False), (DN_VW, BF16, False), (LANES, F32, False),
             (LANES, F32, True), (D_REST, BF16, False)]
    res = pl.pallas_call(
        functools.partial(_inproj_kernel, rope=rope_tabs is not None, want_v=want_v, tm=tm, tps=tps, tn=512),
        grid=(n // tm,),
        in_specs=in_specs,
        out_specs=[pl.BlockSpec((w, tm), lambda i: (0, i)) if tr else pl.BlockSpec((tm, w), lambda i: (i, 0))
                   for w, _, tr in outs],
        out_shape=[jax.ShapeDtypeStruct((w, n) if tr else (n, w), dt) for w, dt, tr in outs],
        compiler_params=_cparams(("arbitrary",)),
    )(*args)
    res = list(res)
    q_p, k_p, vt_p = res[:3]
    v_f32 = res[3] if want_v else None
    qn, kn, vn, gb, gbt, rest = res[-6:]
    return q_p, k_p, vt_p, v_f32, qn, kn, vn, gb, gbt, rest


def _attn_kernel(*refs, has_cache, t, tk, tq, nq):
    refs = list(refs)
    q_ref = refs.pop(0)
    qn_ref = refs.pop(0) if nq > 1 else None
    kc_ref, vc_ref = (refs.pop(0), refs.pop(0)) if has_cache else (None, None)
    k_ref, v_ref, o_ref = refs[:3]
    refs = refs[3:]
    vct_sc = refs.pop() if has_cache else None
    s_bufs, m_bufs = refs[:len(refs) // 2], refs[len(refs) // 2:]
    qi = pl.program_id(2)

    if has_cache:
        @pl.when(qi == 0)
        def _():
            vct_sc[...] = vc_ref[...].T.astype(BF16)

    segs = []
    if has_cache:
        past = kc_ref.shape[0]
        segs += [(kc_ref, vct_sc, r, min(tk, past - r)) for r in range(0, past, tk)]
    segs += [(k_ref, v_ref, r, tk) for r in range(0, t, tk)]
    offs = [sum(w for _, _, _, w in segs[:i]) for i in range(len(segs))]

    def stack(ref):
        q = ref[...]
        return jnp.concatenate([q[:, g * HEAD_DIM:(g + 1) * HEAD_DIM] for g in range(Q_PER_KV)], axis=0)

    def scores(qs, s_ref, i, m_run):
        kr, _, r, w = segs[i]
        s = _dot_nt(kr[r:r + w, :].astype(BF16), qs)
        s_ref[offs[i]:offs[i] + w, :] = s
        for r0 in range(0, w, SUBLANES):
            blk = s[r0:r0 + SUBLANES, :]
            m_run = blk if m_run is None else jnp.maximum(m_run, blk)
        return m_run

    def weighted(s_ref, i, m, acc):
        _, vr, r, w = segs[i]
        p = jnp.exp2(s_ref[offs[i]:offs[i] + w, :] - m).astype(BF16)
        vt_ext = jnp.concatenate([vr[:, r:r + w].astype(BF16), jnp.ones((BF16_SUBLANES, w), BF16)], axis=0)
        pv = _dot(vt_ext, p)
        return pv if acc is None else acc + pv

    def finish(acc):
        o = (acc[:HEAD_DIM, :] / acc[HEAD_DIM:HEAD_DIM + 1, :]).T
        for g in range(Q_PER_KV):
            o_ref[:, g * HEAD_DIM:(g + 1) * HEAD_DIM] = o[g * tq:(g + 1) * tq].astype(o_ref.dtype)

    if nq > 1:
        @pl.when(qi == 0)
        def _():
            qs0 = stack(q_ref)
            m_run = None
            for i in range(len(segs)):
                m_run = scores(qs0, s_bufs[0], i, m_run)
            m_bufs[0][...] = m_run

        def step(s_cur, m_cur, s_nxt, m_nxt):
            qs_next = stack(qn_ref)
            m = jnp.max(m_cur[...], axis=0, keepdims=True)
            acc, m_run = None, None
            for i in range(len(segs)):
                m_run = scores(qs_next, s_nxt, i, m_run)
                acc = weighted(s_cur, i, m, acc)
            m_nxt[...] = m_run
            finish(acc)

        pl.when(qi % 2 == 0)(functools.partial(step, s_bufs[0], m_bufs[0], s_bufs[1], m_bufs[1]))
        pl.when(qi % 2 == 1)(functools.partial(step, s_bufs[1], m_bufs[1], s_bufs[0], m_bufs[0]))
    else:
        qs = stack(q_ref)
        m_run = None
        for i in range(len(segs)):
            m_run = scores(qs, s_bufs[0], i, m_run)
        m = jnp.max(m_run, axis=0, keepdims=True)
        acc = None
        for i in range(len(segs)):
            acc = weighted(s_bufs[0], i, m, acc)
        finish(acc)


def _attention(q_p, k_p, vt_p, b, t, cache):
    n = q_p.shape[0]
    tq = 256
    tk = min(t, 512)
    nq = t // tq
    qw = Q_PER_KV * HEAD_DIM
    in_specs = [pl.BlockSpec((tq, qw), lambda bi, j, qi: (bi * nq + qi, j))]
    args = [q_p]
    if nq > 1:
        in_specs.append(pl.BlockSpec((tq, qw), lambda bi, j, qi: (bi * nq + jnp.minimum(qi + 1, nq - 1), j)))
        args.append(q_p)
    if cache is not None:
        cache_k, cache_v, layer = cache
        past = cache_k.shape[2]
        cspec = pl.BlockSpec((None, None, past, HEAD_DIM), lambda bi, j, qi: (bi, layer, 0, j))
        in_specs += [cspec, cspec]
        args += [cache_k, cache_v]
    in_specs += [pl.BlockSpec((t, HEAD_DIM), lambda bi, j, qi: (bi, j)),
                 pl.BlockSpec((HEAD_DIM, t), lambda bi, j, qi: (j, bi))]
    args += [k_p, vt_p]
    n_keys = t + (cache[0].shape[2] if cache is not None else 0)
    slots = 2 if nq > 1 else 1
    return pl.pallas_call(
        functools.partial(_attn_kernel, has_cache=cache is not None, t=t, tk=tk, tq=tq, nq=nq),
        grid=(b, N_KV_HEADS, nq),
        in_specs=in_specs,
        out_specs=pl.BlockSpec((tq, qw), lambda bi, j, qi: (bi * nq + qi, j)),
        out_shape=jax.ShapeDtypeStruct((n, ATTN_Q), BF16),
        scratch_shapes=([pltpu.VMEM((n_keys, Q_PER_KV * tq), F32)] * slots
                        + [pltpu.VMEM((SUBLANES, Q_PER_KV * tq), F32)] * slots
                        + ([pltpu.VMEM((HEAD_DIM, cache[0].shape[2]), BF16)] if cache is not None else [])),
        compiler_params=_cparams(("arbitrary", "arbitrary", "arbitrary")),
    )(*args)


def _lane_pick(x, lane):
    idx = lax.broadcasted_iota(jnp.int32, x.shape, 1)
    return jnp.sum(jnp.where(idx == lane, x, 0.0), axis=-1, keepdims=True)


def _block_diag(x2):
    xb = x2.astype(BF16)
    z = jnp.zeros((DN_CHUNK, DN_CHUNK), BF16)
    return jnp.concatenate([jnp.concatenate([xb[:, :DN_CHUNK], z], axis=1),
                            jnp.concatenate([z, xb[:, DN_CHUNK:]], axis=1)], axis=0)


def _mm_pair(x2, y2):
    return _dot(x2.astype(BF16), _block_diag(y2))


def _unit_tri_inverses(lmats, ri, ci):
    def blk(s):
        return (ri ^ ci) < s

    eye = jnp.where(ri == ci, 1.0, 0.0)
    ps = [jnp.where(blk(DN_BASE), -lm, 0.0) for lm in lmats]
    xs = [eye + p for p in ps]
    s = 2
    while s < DN_BASE:
        ps = [_mm_pair(p, p) for p in ps]
        yield
        xs = [x + _mm_pair(x, p) for x, p in zip(xs, ps)]
        yield
        s *= 2
    s = DN_BASE
    while s < DN_CHUNK:
        sel = blk(2 * s) & jnp.logical_not(blk(s))
        ts = [_mm_pair(jnp.where(sel, lm, 0.0), x) for lm, x in zip(lmats, xs)]
        yield
        xs = [x - _mm_pair(x, t) for x, t in zip(xs, ts)]
        yield
        s *= 2
    return xs


def _dn_kernel(*refs, reverse, has_s0, want_state, nc, d, nt):
    refs = list(refs)
    q_ref, k_ref, v_ref, gb_ref, gr_ref = refs[:5]
    refs = refs[5:]
    s0_ref = refs.pop(0) if has_s0 else None
    o_ref = refs.pop(0)
    sfin_ref = refs.pop(0) if want_state else None
    s_sc = refs.pop(0)
    bufs = (refs[:5], refs[5:10])
    n_pairs = DN_HEADS // 2
    c2 = 2 * DN_CHUNK
    tt = nc * DN_CHUNK
    insts = [(c, p) for c in range(nc) for p in range(n_pairs)]

    g = pl.program_id(0)
    i_scan = (g - 1) % nt

    def pair_cols(x, lane_a, lane_b):
        shape = (x.shape[0], DN_CHUNK)
        return jnp.concatenate([jnp.broadcast_to(_lane_pick(x, lane_a), shape),
                                jnp.broadcast_to(_lane_pick(x, lane_b), shape)], axis=1)

    def prepare(buf):
        u_buf, wq_buf, attn_buf, kt_buf, gt_buf = buf
        gb = gb_ref[...]
        g_r = gr_ref[...]
        bi = lax.broadcasted_iota(jnp.int32, (tt, tt), 0)
        bj = lax.broadcasted_iota(jnp.int32, (tt, tt), 1)
        same = (bi ^ bj) < DN_CHUNK
        if reverse:
            tri_c = jnp.where(same & (bj >= bi), 1.0, 0.0).astype(BF16)
            tri_r = jnp.where(same & (bi >= bj), 1.0, 0.0).astype(BF16)
        else:
            tri_c = jnp.where(same & (bj <= bi), 1.0, 0.0).astype(BF16)
            tri_r = jnp.where(same & (bi <= bj), 1.0, 0.0).astype(BF16)
        gcum_c = sum(_dot(tri_c, part) for part in _split3(gb))
        gcum_r = sum(_dot(part, tri_r) for part in _split3(g_r))
        yield

        ri = lax.broadcasted_iota(jnp.int32, (DN_CHUNK, c2), 0)
        ci = lax.broadcasted_iota(jnp.int32, (DN_CHUNK, c2), 1) & (DN_CHUNK - 1)
        incl = (ci >= ri) if reverse else (ci <= ri)
        strict = (ci > ri) if reverse else (ci < ri)
        pre = []
        for c, p in insts:
            rs = slice(c * DN_CHUNK, (c + 1) * DN_CHUNK)
            cs = slice(p * c2, (p + 1) * c2)
            lane_b = d * DN_HEADS + 2 * p
            lane_g = 2 * DN_HEADS + d * DN_HEADS + 2 * p
            q2 = q_ref[rs, cs].astype(F32)
            k2 = k_ref[rs, cs].astype(F32)
            v2 = v_ref[rs, cs].astype(F32)
            g_i = pair_cols(gcum_c[rs, :], lane_g, lane_g + 1)
            b_i = pair_cols(gb[rs, :], lane_b, lane_b + 1)
            g_j = jnp.concatenate([gcum_r[lane_g:lane_g + 1, rs], gcum_r[lane_g + 1:lane_g + 2, rs]], axis=1)
            g_tot = g_i[0:1, :] if reverse else g_i[DN_CHUNK - 1:DN_CHUNK, :]
            decay = jnp.where(incl, jnp.exp(jnp.where(incl, g_i - g_j, 0.0)), 0.0)
            pre.append(dict(q2=q2, k2=k2, v2=v2, g_i=g_i, b_i=b_i, g_tot=g_tot, decay=decay, kb=k2 * b_i))
        a2s = [_dot_nt(jnp.concatenate([p["kb"], p["q2"]], axis=0).astype(BF16), _block_diag(p["k2"]))
               for p in pre]
        yield
        lmats = [jnp.where(strict, a2[:DN_CHUNK] * p["decay"], 0.0) for a2, p in zip(a2s, pre)]
        for n, (a2, p) in enumerate(zip(a2s, pre)):
            attn_buf[n] = (a2[DN_CHUNK:] * p["decay"]).astype(BF16)
        tinvs = yield from _unit_tri_inverses(lmats, ri, ci)
        gams = [jnp.exp(p["g_i"]) for p in pre]
        us = [_mm_pair(t, p["v2"] * p["b_i"]) for t, p in zip(tinvs, pre)]
        yield
        ws = [_mm_pair(t, p["kb"] * gam) for t, p, gam in zip(tinvs, pre, gams)]
        yield
        for n, (p, u, w, gam) in enumerate(zip(pre, us, ws, gams)):
            u_buf[n] = u
            wq_buf[n] = jnp.concatenate([w, p["q2"] * gam], axis=0).astype(BF16)
            kt_buf[n] = (p["k2"] * jnp.exp(p["g_tot"] - p["g_i"])).astype(BF16)
            gt_buf[n] = jnp.broadcast_to(jnp.exp(p["g_tot"]), (SUBLANES, c2))

    def scan(buf):
        u_buf, wq_buf, attn_buf, kt_buf, gt_buf = buf
        si = lax.broadcasted_iota(jnp.int32, (c2, c2), 0)
        sj = lax.broadcasted_iota(jnp.int32, (c2, c2), 1)
        on_diag = (si < DN_CHUNK) == (sj < DN_CHUNK)
        pairs = range(n_pairs)
        states = [s_sc[p] for p in pairs]
        for c in (range(nc - 1, -1, -1) if reverse else range(nc)):
            ns = [c * n_pairs + p for p in pairs]
            m1s = [_dot(wq_buf[n], s.astype(BF16)) for n, s in zip(ns, states)]
            yield
            v_news = [u_buf[n] - m1[:DN_CHUNK] for n, m1 in zip(ns, m1s)]
            outs = [m1[DN_CHUNK:] + _mm_pair(attn_buf[n], v) for n, m1, v in zip(ns, m1s, v_news)]
            yield
            states = [s * gt_buf[n][0:1, :] + jnp.where(on_diag, _dot_tn(kt_buf[n], v.astype(BF16)), 0.0)
                      for n, s, v in zip(ns, states, v_news)]
            for p in pairs:
                o_ref[c * DN_CHUNK:(c + 1) * DN_CHUNK, p * c2:(p + 1) * c2] = outs[p]
            yield
        for p in pairs:
            s_sc[p] = states[p]

    def step(par):
        @pl.when(i_scan == 0)
        def _():
            s_sc[...] = jnp.zeros_like(s_sc)
            if has_s0:
                for h in range(DN_HEADS):
                    lo = (h % 2) * DN_CHUNK
                    s_sc[h // 2, lo:lo + DN_CHUNK, lo:lo + DN_CHUNK] = s0_ref[h]

        _interleave(prepare(bufs[par]), scan(bufs[1 - par]))

        if want_state:
            @pl.when(i_scan == nt - 1)
            def _():
                for h in range(DN_HEADS):
                    lo = (h % 2) * DN_CHUNK
                    sfin_ref[h] = s_sc[h // 2, lo:lo + DN_CHUNK, lo:lo + DN_CHUNK]

    pl.when(g == 0)(lambda: _interleave(prepare(bufs[0])))
    pl.when((g > 0) & (g % 2 == 0))(functools.partial(step, 0))
    pl.when(g % 2 == 1)(functools.partial(step, 1))


def _deltanet_dir(qn, kn, vn, gb, gbt, s0, b, t, d, want_state):
    n = qn.shape[0]
    nc = min(4, t // DN_CHUNK)
    tt = nc * DN_CHUNK
    nt = t // tt
    n_tiles = b * nt
    reverse = d == 1
    c2 = 2 * DN_CHUNK
    n_inst = nc * (DN_HEADS // 2)

    def seq(gt):
        bi, i = gt // nt, gt % nt
        return bi, ((nt - 1 - i) if reverse else i)

    def prep_tile(g):
        return seq(jnp.minimum(g, n_tiles - 1))

    def scan_tile(g):
        return seq(jnp.maximum(g - 1, 0))

    def row_block(bt):
        return bt[0] * nt + bt[1]

    qkv_spec = pl.BlockSpec((tt, DN_QK), lambda g: (row_block(prep_tile(g)), 0))
    in_specs = [
        qkv_spec, qkv_spec, qkv_spec,
        pl.BlockSpec((tt, LANES), lambda g: (row_block(prep_tile(g)), 0)),
        pl.BlockSpec((4 * DN_HEADS, tt), lambda g: (0, row_block(prep_tile(g)))),
    ]
    args = [qn, kn, vn, gb, gbt]
    if s0 is not None:
        state, layer = s0
        in_specs.append(pl.BlockSpec((None, None, None, DN_HEADS, DN_DK, DN_DV),
                                     lambda g: (scan_tile(g)[0], layer, d, 0, 0, 0)))
        args.append(state)
    out_specs = [pl.BlockSpec((tt, DN_VW), lambda g: (row_block(scan_tile(g)), 0))]
    out_shape = [jax.ShapeDtypeStruct((n, DN_VW), F32)]
    if want_state:
        out_specs.append(pl.BlockSpec((None, DN_HEADS, DN_DK, DN_DV), lambda g: (scan_tile(g)[0], 0, 0, 0)))
        out_shape.append(jax.ShapeDtypeStruct((b, DN_HEADS, DN_DK, DN_DV), F32))
    prepared = [pltpu.VMEM((n_inst, DN_CHUNK, c2), F32), pltpu.VMEM((n_inst, c2, c2), BF16),
                pltpu.VMEM((n_inst, DN_CHUNK, c2), BF16), pltpu.VMEM((n_inst, DN_CHUNK, c2), BF16),
                pltpu.VMEM((n_inst, SUBLANES, c2), F32)]
    res = pl.pallas_call(
        functools.partial(_dn_kernel, reverse=reverse, has_s0=s0 is not None, want_state=want_state,
                          nc=nc, d=d, nt=nt),
        grid=(n_tiles + 1,),
        in_specs=in_specs,
        out_specs=out_specs,
        out_shape=out_shape,
        scratch_shapes=[pltpu.VMEM((DN_HEADS // 2, c2, c2), F32)] + prepared + prepared,
        compiler_params=_cparams(("arbitrary",)),
    )(*args)
    return res if want_state else (res[0], None)


def _post_kernel(x_ref, mod_ref, cb_ref, cc_ref, cx_ref, ccp_ref, cxp_ref, ccn_ref, cxn_ref, cw_ref,
                 att_ref, of_ref, ob_ref, z_ref, ng_ref, ga_ref, gb_ref, gc_ref, wpa_ref, wpb_ref, wpc_ref, wo_ref,
                 gpost_ref, o_ref, *, tm, tps):
    i = pl.program_id(0)
    first = (i % tps) == 0
    last = (i % tps) == tps - 1
    u = cc_ref[...].astype(F32) * cx_ref[...].astype(F32)
    hl = BF16_SUBLANES - 1
    prev_row = jnp.where(first, 0.0, ccp_ref[hl:hl + 1, :].astype(F32) * cxp_ref[hl:hl + 1, :].astype(F32))
    next_row = jnp.where(last, 0.0, ccn_ref[0:1, :].astype(F32) * cxn_ref[0:1, :].astype(F32))
    rows = lax.broadcasted_iota(jnp.int32, (tm, 1), 0)
    u_prev = jnp.where(rows == 0, prev_row, pltpu.roll(u, 1, axis=0))
    u_next = jnp.where(rows == tm - 1, next_row, pltpu.roll(u, tm - 1, axis=0))
    cw = cw_ref[...]
    conv = cw[0:1, :] * u_prev + cw[1:2, :] * u + cw[2:3, :] * u_next
    ya = _dot((cb_ref[...].astype(F32) * conv).astype(BF16), wpa_ref[...])
    yb = _dot(att_ref[...], wpb_ref[...])
    o = of_ref[...] + ob_ref[...]
    z = z_ref[...].astype(F32)
    parts = []
    for h in range(DN_HEADS):
        sl = slice(h * DN_DV, (h + 1) * DN_DV)
        parts.append((_rms(o[:, sl], ng_ref[...]) * _silu(z[:, sl])).astype(BF16))
    yc = _dot(jnp.concatenate(parts, axis=1), wpc_ref[...])
    mix_in = (_sigmoid(ga_ref[...].astype(F32)) * ya + _sigmoid(gb_ref[...].astype(F32)) * yb
              + _sigmoid(gc_ref[...].astype(F32)) * yc)
    mix = _dot(mix_in.astype(BF16), wo_ref[...])
    o_ref[...] = x_ref[...] + mod_ref[0, 2:3, :] * _rms(mix, gpost_ref[...])


def _post_mixer(x, mod3, mod_base, rows_per_mod, t, proj, att, o_f, o_b, conv_w, ng, wpa, wpb, wpc, wo, g_post1):
    n = x.shape[0]
    tm = 256
    tps = t // tm
    hb = tm // BF16_SUBLANES
    cwid = CONV_WIDTH
    c0 = REST_CONV // cwid

    def prev(i):
        return jnp.maximum(i * hb - 1, 0)

    def nxt(i):
        return jnp.minimum((i + 1) * hb, n // BF16_SUBLANES - 1)

    g0 = REST_GATE // D_MODEL
    in_specs = [
        pl.BlockSpec((tm, D_MODEL), lambda i: (i, 0)),
        pl.BlockSpec((1, 6, D_MODEL), lambda i: (mod_base + (i * tm) // rows_per_mod, 0, 0)),
        pl.BlockSpec((tm, cwid), lambda i: (i, c0)),
        pl.BlockSpec((tm, cwid), lambda i: (i, c0 + 1)),
        pl.BlockSpec((tm, cwid), lambda i: (i, c0 + 2)),
        pl.BlockSpec((BF16_SUBLANES, cwid), lambda i: (prev(i), c0 + 1)),
        pl.BlockSpec((BF16_SUBLANES, cwid), lambda i: (prev(i), c0 + 2)),
        pl.BlockSpec((BF16_SUBLANES, cwid), lambda i: (nxt(i), c0 + 1)),
        pl.BlockSpec((BF16_SUBLANES, cwid), lambda i: (nxt(i), c0 + 2)),
        _const_spec((3, cwid)),
        pl.BlockSpec((tm, ATTN_Q), lambda i: (i, 0)),
        pl.BlockSpec((tm, DN_VW), lambda i: (i, 0)),
        pl.BlockSpec((tm, DN_VW), lambda i: (i, 0)),
        pl.BlockSpec((tm, DN_VW), lambda i: (i, REST_Z // DN_VW)),
        _const_spec((1, DN_DV)),
        pl.BlockSpec((tm, D_MODEL), lambda i: (i, g0)),
        pl.BlockSpec((tm, D_MODEL), lambda i: (i, g0 + 1)),
        pl.BlockSpec((tm, D_MODEL), lambda i: (i, g0 + 2)),
        _const_spec((CONV_WIDTH, D_MODEL)), _const_spec((ATTN_Q, D_MODEL)),
        _const_spec((DN_VW, D_MODEL)), _const_spec((D_MODEL, D_MODEL)),
        _const_spec((1, D_MODEL)),
    ]
    return pl.pallas_call(
        functools.partial(_post_kernel, tm=tm, tps=tps),
        grid=(n // tm,),
        in_specs=in_specs,
        out_specs=pl.BlockSpec((tm, D_MODEL), lambda i: (i, 0)),
        out_shape=jax.ShapeDtypeStruct((n, D_MODEL), F32),
        compiler_params=_cparams(("arbitrary",)),
    )(x, mod3, proj, proj, proj, proj, proj, proj, proj, conv_w, att, o_f, o_b, proj, ng,
      proj, proj, proj, wpa, wpb, wpc, wo, g_post1)


def _ffn_kernel(x_ref, mod_ref, gpre_ref, wg_ref, wu_ref, wd_ref, gpost_ref, o_ref, act_sc, *, tf):
    x = x_ref[...]
    h2 = (_rms(x, gpre_ref[...]) * (1.0 + mod_ref[0, 4:5, :]) + mod_ref[0, 3:4, :]).astype(BF16)
    for j in range(0, D_FF, tf):
        gate = _dot(h2, wg_ref[:, j:j + tf])
        up = _dot(h2, wu_ref[:, j:j + tf])
        act_sc[:, j:j + tf] = (_silu(gate) * up).astype(BF16)
    ffn = _dot(act_sc[...], wd_ref[...])
    o_ref[...] = x + mod_ref[0, 5:6, :] * _rms(ffn, gpost_ref[...])


def _ffn(x, mod3, mod_base, rows_per_mod, g_pre2, wg, wu, wd, g_post2):
    n = x.shape[0]
    tm = 512
    return pl.pallas_call(
        functools.partial(_ffn_kernel, tf=256),
        grid=(n // tm,),
        in_specs=[pl.BlockSpec((tm, D_MODEL), lambda i: (i, 0)),
                  pl.BlockSpec((1, 6, D_MODEL), lambda i: (mod_base + (i * tm) // rows_per_mod, 0, 0)),
                  _const_spec((1, D_MODEL)),
                  _const_spec((D_MODEL, D_FF)), _const_spec((D_MODEL, D_FF)), _const_spec((D_FF, D_MODEL)),
                  _const_spec((1, D_MODEL))],
        out_specs=pl.BlockSpec((tm, D_MODEL), lambda i: (i, 0)),
        out_shape=jax.ShapeDtypeStruct((n, D_MODEL), F32),
        scratch_shapes=[pltpu.VMEM((tm, D_FF), BF16)],
        compiler_params=_cparams(("arbitrary",)),
    )(x, mod3, g_pre2, wg, wu, wd, g_post2)


def _rope_tables(t):
    rows = t // GRID_W
    row_id = np.repeat(np.arange(rows, dtype=np.float32), GRID_W)
    col_id = np.tile(np.arange(GRID_W, dtype=np.float32), rows)
    n_freq = HEAD_DIM // 4
    inv_freq = (np.float32(ROPE_THETA) ** (-np.arange(n_freq, dtype=np.float32) / np.float32(n_freq))).astype(np.float32)
    ang = np.concatenate([row_id[:, None] * inv_freq, col_id[:, None] * inv_freq], axis=-1).astype(np.float32)
    cos = np.repeat(np.cos(ang), 2, axis=-1).astype(np.float32)
    sin = np.repeat(np.sin(ang), 2, axis=-1).astype(np.float32)
    sign = np.tile(np.array([-1.0, 1.0], np.float32), HEAD_DIM // 2)
    return jnp.asarray(cos), jnp.asarray(sin * sign)


def _layer(x, b, t, mod3, mod_base, rows_per_mod, lw, rope_tabs, cache, state0, want_state):
    q_p, k_p, vt_p, v_f32, qn, kn, vn, gb, gbt, proj = _inproj(x, t, mod3, mod_base, rows_per_mod, lw,
                                                               rope_tabs, want_state)
    att = _attention(q_p, k_p, vt_p, b, t, cache)
    dn = [_deltanet_dir(qn, kn, vn, gb, gbt, state0, b, t, d, want_state) for d in range(2)]
    x = _post_mixer(x, mod3, mod_base, rows_per_mod, t, proj, att, dn[0][0], dn[1][0], lw["conv_w"],
                    lw["dn_norm_g"], lw["w_pa"], lw["w_pb"], lw["w_pc"], lw["w_o"], lw["g_post1"])
    x = _ffn(x, mod3, mod_base, rows_per_mod, lw["g_pre2"], lw["w_gate"], lw["w_up"], lw["w_down"],
             lw["g_post2"])
    if not want_state:
        return x, None
    s_fin = jnp.stack([dn[0][1], dn[1][1]], axis=1)
    return x, (k_p, v_f32, s_fin)


def kernel(x_prompt, x_sample, cache_k, cache_v, state_dn, c, c_ctx, w_mod, b_mod, g_pre1, g_post1, g_pre2, g_post2, w_in, conv_w, g_qn, g_kn, dn_conv_w, dn_a_log, dn_dt_bias, dn_norm_g, w_pa, w_pb, w_pc, w_o, w_gate, w_up, w_down):
    bp, tp, d = x_prompt.shape
    bs, ts, _ = x_sample.shape
    depth = w_mod.shape[0]
    past = cache_k.shape[2]

    mod_rows = -(-(bs + 1) // SUBLANES) * SUBLANES
    cv = jnp.zeros((mod_rows, d), F32).at[:bs].set(c).at[bs].set(c_ctx)
    mod_all = _modulation(cv, w_mod, b_mod).reshape(depth, mod_rows, 6, d)

    w_in_r = jnp.concatenate(
        [w_in[:, :, ORIG_Q:ORIG_Z],
         w_in[:, :, ORIG_SMALL:ORIG_GATE], jnp.zeros((depth, d, LANES - 4 * DN_HEADS), w_in.dtype),
         w_in[:, :, ORIG_GATE:D_IN], w_in[:, :, ORIG_CONV:ORIG_Q], w_in[:, :, ORIG_Z:ORIG_SMALL]],
        axis=-1).astype(BF16)
    lane_pad = ((0, 0), (2 * DN_HEADS, LANES - 4 * DN_HEADS))
    a_flat = dn_a_log.reshape(depth, 2 * DN_HEADS)
    b_flat = dn_dt_bias.reshape(depth, 2 * DN_HEADS)
    rope_tabs = _rope_tables(ts)
    cache_k4 = cache_k.reshape(bs, depth, past, ATTN_KV)
    cache_v4 = cache_v.reshape(bs, depth, past, ATTN_KV)

    xp = x_prompt.reshape(bp * tp, d)
    xs = x_sample.reshape(bs * ts, d)
    ks, vs, ss = [], [], []
    for l in range(depth):
        lw = {
            "g_pre1": g_pre1[l][None], "g_post1": g_post1[l][None], "g_pre2": g_pre2[l][None],
            "g_post2": g_post2[l][None], "w_in": w_in_r[l], "conv_w": conv_w[l],
            "g_qn": g_qn[l][None], "g_kn": g_kn[l][None], "dn_conv_w": dn_conv_w[l],
            "dn_ac": jnp.pad(a_flat[l][None], lane_pad), "dn_bc": jnp.pad(b_flat[l][None], lane_pad),
            "dn_norm_g": dn_norm_g[l][None],
            "w_pa": w_pa[l].astype(BF16), "w_pb": w_pb[l].astype(BF16), "w_pc": w_pc[l].astype(BF16),
            "w_o": w_o[l].astype(BF16), "w_gate": w_gate[l].astype(BF16), "w_up": w_up[l].astype(BF16),
            "w_down": w_down[l].astype(BF16),
        }
        mod3 = mod_all[l]
        xp, (k_l, v_l, s_l) = _layer(xp, bp, tp, mod3, bs, bp * tp, lw, None, None, None, True)
        ks.append(k_l.reshape(bp, tp, N_KV_HEADS, HEAD_DIM))
        vs.append(v_l.reshape(bp, tp, N_KV_HEADS, HEAD_DIM))
        ss.append(s_l)
        xs, _ = _layer(xs, bs, ts, mod3, 0, ts, lw, rope_tabs, (cache_k4, cache_v4, l), (state_dn, l), False)
    return (xp.reshape(bp, tp, d), xs.reshape(bs, ts, d), jnp.stack(ks, axis=1), jnp.stack(vs, axis=1),
            jnp.stack(ss, axis=1))
```

```python
import functools

import jax
import jax.numpy as jnp
import numpy as np
from jax import lax
from jax.experimental import pallas as pl
from jax.experimental.pallas import tpu as pltpu

F32 = jnp.float32
BF16 = jnp.bfloat16

D_MODEL = 1024
EPS = 1e-6
GRID_W = 64
N_HEADS = 8
N_KV_HEADS = 2
Q_PER_KV = N_HEADS // N_KV_HEADS
HEAD_DIM = 128
ATTN_Q = N_HEADS * HEAD_DIM
ATTN_KV = N_KV_HEADS * HEAD_DIM
ROPE_THETA = 10000.0
CONV_WIDTH = 512
DN_HEADS = 4
DN_DK = 128
DN_DV = 128
DN_QK = DN_HEADS * DN_DK
DN_VW = DN_HEADS * DN_DV
D_FF = 2816

SUBLANES = 8
BF16_SUBLANES = 16
LANES = 128

COL_CONV = 0
COL_Q = COL_CONV + 3 * CONV_WIDTH
COL_K = COL_Q + ATTN_Q
COL_V = COL_K + ATTN_KV
COL_DN = COL_V + ATTN_KV
COL_Z = COL_DN + 3 * DN_QK
COL_SMALL = COL_Z + DN_VW
COL_GATE = COL_SMALL + 4 * DN_HEADS
D_IN = COL_GATE + 3 * D_MODEL
TAIL_GATE = 0
TAIL_SMALL = TAIL_GATE + 3 * D_MODEL
D_TAIL = TAIL_SMALL + LANES
REST_GATE = 0
REST_CONV = REST_GATE + 3 * D_MODEL
REST_Z = REST_CONV + 3 * CONV_WIDTH
D_REST = REST_Z + DN_VW

Q_SCALE = 1.4426950408889634 * HEAD_DIM ** -0.5
DN_CHUNK = 128
DN_BASE = 16
VMEM_LIMIT = 56 * 1024 * 1024


def _cparams(sem):
    return pltpu.CompilerParams(dimension_semantics=sem, vmem_limit_bytes=VMEM_LIMIT)


def _const_spec(shape):
    nd = len(shape)
    return pl.BlockSpec(shape, lambda *_: (0,) * nd, pipeline_mode=pl.Buffered(1))


def _layer_spec(shape, layer):
    return pl.BlockSpec((None,) + tuple(shape), lambda *_: (layer,) + (0,) * len(shape),
                        pipeline_mode=pl.Buffered(1))


def _dot(a, b):
    return jnp.dot(a, b, preferred_element_type=F32)


def _dot_nt(a, b):
    return lax.dot_general(a, b, (((1,), (1,)), ((), ())), preferred_element_type=F32)


def _dot_tn(a, b):
    return lax.dot_general(a, b, (((0,), (0,)), ((), ())), preferred_element_type=F32)


def _mm(a, b):
    return _dot(a.astype(BF16), b.astype(BF16))


def _rms(x, g):
    return x * lax.rsqrt(jnp.mean(x * x, axis=-1, keepdims=True) + EPS) * g


def _sigmoid(x):
    return 1.0 / (1.0 + jnp.exp(-x))


def _silu(x):
    return x * _sigmoid(x)


def _softplus(x):
    return jnp.maximum(x, 0.0) + jnp.log1p(jnp.exp(-jnp.abs(x)))


def _split3(x):
    hi = x.astype(BF16)
    r = x - hi.astype(F32)
    mid = r.astype(BF16)
    lo = (r - mid.astype(F32)).astype(BF16)
    return hi, mid, lo


def _mod_kernel(cv_ref, w_ref, b_ref, o_ref):
    cv = cv_ref[...]
    o_ref[0] = _dot(_silu(cv).astype(BF16), w_ref[0].astype(BF16)) + b_ref[0]


def _modulation(cv, w_mod, b_mod):
    depth, d, n6 = w_mod.shape
    rows = cv.shape[0]
    tn = 1536
    return pl.pallas_call(
        _mod_kernel,
        grid=(depth, n6 // tn),
        in_specs=[pl.BlockSpec((rows, d), lambda l, j: (0, 0)),
                  pl.BlockSpec((1, d, tn), lambda l, j: (l, 0, j)),
                  pl.BlockSpec((1, 1, tn), lambda l, j: (l, 0, j))],
        out_specs=pl.BlockSpec((1, rows, tn), lambda l, j: (l, 0, j)),
        out_shape=jax.ShapeDtypeStruct((depth, rows, n6), F32),
        compiler_params=_cparams(("arbitrary", "arbitrary")),
    )(cv, w_mod, b_mod.reshape(depth, 1, n6))


def _swap_pairs(y):
    lane = lax.broadcasted_iota(jnp.int32, y.shape, 1)
    return jnp.where(lane % 2 == 0, pltpu.roll(y, LANES - 1, axis=1), pltpu.roll(y, 1, axis=1))


def _interleave(*gens):
    gens = list(gens)
    while gens:
        for gen in list(gens):
            try:
                next(gen)
            except StopIteration:
                gens.remove(gen)


def _inproj_kernel(*refs, rope, want_v, tm, tps, tn):
    refs = list(refs)
    x_ref, xp_ref, xn_ref, mod_ref, g_ref, w_ref, wt_ref, gq_ref, gk_ref, cw_ref, ac_ref, bc_ref = refs[:12]
    refs = refs[12:]
    cos_ref, sin_ref = (refs.pop(0), refs.pop(0)) if rope else (None, None)
    q_out, k_out, v_out = refs[:3]
    refs = refs[3:]
    vf_out = refs.pop(0) if want_v else None
    qn_out, kn_out, vn_out, gb_out, gbt_out, rest_out = refs
    i = pl.program_id(0)

    def modnorm(x):
        return (_rms(x, g_ref[...]) * (1.0 + mod_ref[0, 1:2, :]) + mod_ref[0, 0:1, :]).astype(BF16)

    h = modnorm(x_ref[...])
    h_halo = modnorm(jnp.concatenate([xp_ref[...], xn_ref[...]], axis=0))

    def qk_head(x, g):
        y = _rms(x, g)
        if rope:
            y = y * cos_ref[...] + _swap_pairs(y) * sin_ref[...]
        return y

    def attn_epilogue():
        qkv = _dot(h, w_ref[:, COL_Q:COL_DN])
        yield
        for hh in range(N_HEADS):
            sl = slice(hh * HEAD_DIM, (hh + 1) * HEAD_DIM)
            q_out[:, sl] = (qk_head(qkv[:, sl], gq_ref[...]) * Q_SCALE).astype(q_out.dtype)
            if hh % 2 == 1:
                yield
        for hh in range(N_KV_HEADS):
            sl = slice(hh * HEAD_DIM, (hh + 1) * HEAD_DIM)
            k_out[:, sl] = qk_head(qkv[:, ATTN_Q + hh * HEAD_DIM:ATTN_Q + (hh + 1) * HEAD_DIM],
                                   gk_ref[...]).astype(k_out.dtype)
        v = qkv[:, ATTN_Q + ATTN_KV:]
        v_out[...] = v.T.astype(v_out.dtype)
        if want_v:
            vf_out[...] = v

    def dn_epilogue():
        dn = _dot(h, w_ref[:, COL_DN:COL_Z])
        dn_halo = _dot(h_halo, w_ref[:, COL_DN:COL_Z])
        gl = _dot(h, wt_ref[:, TAIL_SMALL:D_TAIL])
        yield
        rows = lax.broadcasted_iota(jnp.int32, (tm, 1), 0)
        prev_row = jnp.where((i % tps) == 0, 0.0, dn_halo[SUBLANES - 1:SUBLANES, :])
        next_row = jnp.where((i % tps) == tps - 1, 0.0, dn_halo[SUBLANES:SUBLANES + 1, :])
        dn_prev = jnp.where(rows == 0, prev_row, pltpu.roll(dn, 1, axis=0))
        dn_next = jnp.where(rows == tm - 1, next_row, pltpu.roll(dn, tm - 1, axis=0))
        cw = cw_ref[...]

        def conv_silu(sl):
            return _silu(cw[0:1, sl] * dn_prev[:, sl] + cw[1:2, sl] * dn[:, sl] + cw[2:3, sl] * dn_next[:, sl])

        for hh in range(DN_HEADS):
            sl = slice(hh * DN_DK, (hh + 1) * DN_DK)
            qh = conv_silu(sl)
            kh = conv_silu(slice(DN_QK + hh * DN_DK, DN_QK + (hh + 1) * DN_DK))
            qn_out[:, sl] = (qh * lax.rsqrt(jnp.sum(qh * qh, axis=-1, keepdims=True) + EPS)
                             * (DN_DK ** -0.5)).astype(qn_out.dtype)
            kn_out[:, sl] = (kh * lax.rsqrt(jnp.sum(kh * kh, axis=-1, keepdims=True) + EPS)).astype(kn_out.dtype)
            vn_out[:, sl] = conv_silu(slice(2 * DN_QK + hh * DN_DV, 2 * DN_QK + (hh + 1) * DN_DV)
                                      ).astype(vn_out.dtype)
            yield
        lane = lax.broadcasted_iota(jnp.int32, gl.shape, 1)
        gb = jnp.where(lane < 2 * DN_HEADS, _sigmoid(gl), -jnp.exp(ac_ref[...]) * _softplus(gl + bc_ref[...]))
        gb_out[...] = gb
        gbt_out[...] = gb.T

    def remaining():
        groups = [(wt_ref, TAIL_GATE, REST_GATE, 3 * D_MODEL), (w_ref, COL_CONV, REST_CONV, 3 * CONV_WIDTH),
                  (w_ref, COL_Z, REST_Z, DN_VW)]
        for ref, src, dst, width in groups:
            for j in range(0, width, tn):
                rest_out[:, dst + j:dst + j + tn] = _dot(h, ref[:, src + j:src + j + tn]).astype(rest_out.dtype)
                yield

    _interleave(dn_epilogue(), attn_epilogue(), remaining())


def _inproj(x, t, mod3, mod_base, rows_per_mod, lw, rope_tabs, want_v):
    n = x.shape[0]
    tm = 256
    tps = t // tm
    halo = tm // SUBLANES
    w_main, w_tail, layer = lw["w_in"], lw["w_tail"], lw["layer"]
    in_specs = [pl.BlockSpec((tm, D_MODEL), lambda i: (i, 0)),
                pl.BlockSpec((SUBLANES, D_MODEL), lambda i: (jnp.maximum(i * halo - 1, 0), 0)),
                pl.BlockSpec((SUBLANES, D_MODEL), lambda i: (jnp.minimum((i + 1) * halo, n // SUBLANES - 1), 0)),
                pl.BlockSpec((1, 6, D_MODEL), lambda i: (mod_base + (i * tm) // rows_per_mod, 0, 0)),
                _const_spec((1, D_MODEL)),
                _layer_spec((D_MODEL, COL_SMALL), layer), _layer_spec((D_MODEL, D_TAIL), layer),
                _const_spec((1, HEAD_DIM)), _const_spec((1, HEAD_DIM)),
                _const_spec((3, 3 * DN_QK)), _const_spec((1, LANES)), _const_spec((1, LANES))]
    args = [x, x, x, mod3, lw["g_pre1"], w_main, w_tail, lw["g_qn"], lw["g_kn"], lw["dn_conv_w"],
            lw["dn_ac"], lw["dn_bc"]]
    if rope_tabs is not None:
        in_specs += [pl.BlockSpec((tm, HEAD_DIM), lambda i: (i % tps, 0))] * 2
        args += list(rope_tabs)

    outs = [(ATTN_Q, BF16, False), (ATTN_KV, F32 if want_v else BF16, False), (ATTN_KV, BF16, True)]
    if want_v:
        outs.append((ATTN_KV, F32, False))
    outs += [(DN_QK, BF16, False), (DN_QK, BF16, False), (DN_VW, BF16, False), (LANES, F32, False),
             (LANES, F32, True), (D_REST, BF16, False)]
    res = pl.pallas_call(
        functools.partial(_inproj_kernel, rope=rope_tabs is not None, want_v=want_v, tm=tm, tps=tps, tn=512),
        grid=(n // tm,),
        in_specs=in_specs,
        out_specs=[pl.BlockSpec((w, tm), lambda i: (0, i)) if tr else pl.BlockSpec((tm, w), lambda i: (i, 0))
                   for w, _, tr in outs],
        out_shape=[jax.ShapeDtypeStruct((w, n) if tr else (n, w), dt) for w, dt, tr in outs],
        compiler_params=_cparams(("arbitrary",)),
    )(*args)
    res = list(res)
    q_p, k_p, vt_p = res[:3]
    v_f32 = res[3] if want_v else None
    qn, kn, vn, gb, gbt, rest = res[-6:]
    return q_p, k_p, vt_p, v_f32, qn, kn, vn, gb, gbt, rest


def _attn_kernel(*refs, has_cache, t, tk, tq, nq):
    refs = list(refs)
    q_ref = refs.pop(0)
    qn_ref = refs.pop(0) if nq > 1 else None
    kc_ref, vc_ref = (refs.pop(0), refs.pop(0)) if has_cache else (None, None)
    k_ref, v_ref, o_ref = refs[:3]
    refs = refs[3:]
    vct_sc = refs.pop() if has_cache else None
    s_bufs, m_bufs = refs[:len(refs) // 2], refs[len(refs) // 2:]
    qi = pl.program_id(2)

    if has_cache:
        @pl.when(qi == 0)
        def _():
            vct_sc[...] = vc_ref[...].T.astype(BF16)

    segs = []
    if has_cache:
        past = kc_ref.shape[0]
        segs += [(kc_ref, vct_sc, r, min(tk, past - r)) for r in range(0, past, tk)]
    segs += [(k_ref, v_ref, r, tk) for r in range(0, t, tk)]
    offs = [sum(w for _, _, _, w in segs[:i]) for i in range(len(segs))]

    def stack(ref):
        q = ref[...]
        return jnp.concatenate([q[:, g * HEAD_DIM:(g + 1) * HEAD_DIM] for g in range(Q_PER_KV)], axis=0)

    def scores(qs, s_ref, i, m_run):
        kr, _, r, w = segs[i]
        s = _dot_nt(kr[r:r + w, :].astype(BF16), qs)
        s_ref[offs[i]:offs[i] + w, :] = s
        for r0 in range(0, w, SUBLANES):
            blk = s[r0:r0 + SUBLANES, :]
            m_run = blk if m_run is None else jnp.maximum(m_run, blk)
        return m_run

    def weighted(s_ref, i, m, acc):
        _, vr, r, w = segs[i]
        p = jnp.exp2(s_ref[offs[i]:offs[i] + w, :] - m).astype(BF16)
        vt_ext = jnp.concatenate([vr[:, r:r + w].astype(BF16), jnp.ones((BF16_SUBLANES, w), BF16)], axis=0)
        pv = _dot(vt_ext, p)
        return pv if acc is None else acc + pv

    def finish(acc):
        o = (acc[:HEAD_DIM, :] / acc[HEAD_DIM:HEAD_DIM + 1, :]).T
        for g in range(Q_PER_KV):
            o_ref[:, g * HEAD_DIM:(g + 1) * HEAD_DIM] = o[g * tq:(g + 1) * tq].astype(o_ref.dtype)

    if nq > 1:
        @pl.when(qi == 0)
        def _():
            qs0 = stack(q_ref)
            m_run = None
            for i in range(len(segs)):
                m_run = scores(qs0, s_bufs[0], i, m_run)
            m_bufs[0][...] = m_run

        def step(s_cur, m_cur, s_nxt, m_nxt):
            qs_next = stack(qn_ref)
            m = jnp.max(m_cur[...], axis=0, keepdims=True)
            acc, m_run = None, None
            for i in range(len(segs)):
                m_run = scores(qs_next, s_nxt, i, m_run)
                acc = weighted(s_cur, i, m, acc)
            m_nxt[...] = m_run
            finish(acc)

        pl.when(qi % 2 == 0)(functools.partial(step, s_bufs[0], m_bufs[0], s_bufs[1], m_bufs[1]))
        pl.when(qi % 2 == 1)(functools.partial(step, s_bufs[1], m_bufs[1], s_bufs[0], m_bufs[0]))
    else:
        qs = stack(q_ref)
        m_run = None
        for i in range(len(segs)):
            m_run = scores(qs, s_bufs[0], i, m_run)
        m = jnp.max(m_run, axis=0, keepdims=True)
        acc = None
        for i in range(len(segs)):
            acc = weighted(s_bufs[0], i, m, acc)
        finish(acc)


def _attention(q_p, k_p, vt_p, b, t, cache):
    n = q_p.shape[0]
    tq = 128 if t > 256 else 256
    tk = min(t, 512)
    nq = t // tq
    qw = Q_PER_KV * HEAD_DIM
    in_specs = [pl.BlockSpec((tq, qw), lambda bi, j, qi: (bi * nq + qi, j))]
    args = [q_p]
    if nq > 1:
        in_specs.append(pl.BlockSpec((tq, qw), lambda bi, j, qi: (bi * nq + jnp.minimum(qi + 1, nq - 1), j)))
        args.append(q_p)
    if cache is not None:
        cache_k, cache_v, layer = cache
        past = cache_k.shape[2]
        cspec = pl.BlockSpec((None, None, past, HEAD_DIM), lambda bi, j, qi: (bi, layer, 0, j))
        in_specs += [cspec, cspec]
        args += [cache_k, cache_v]
    in_specs += [pl.BlockSpec((t, HEAD_DIM), lambda bi, j, qi: (bi, j)),
                 pl.BlockSpec((HEAD_DIM, t), lambda bi, j, qi: (j, bi))]
    args += [k_p, vt_p]
    n_keys = t + (cache[0].shape[2] if cache is not None else 0)
    slots = 2 if nq > 1 else 1
    return pl.pallas_call(
        functools.partial(_attn_kernel, has_cache=cache is not None, t=t, tk=tk, tq=tq, nq=nq),
        grid=(b, N_KV_HEADS, nq),
        in_specs=in_specs,
        out_specs=pl.BlockSpec((tq, qw), lambda bi, j, qi: (bi * nq + qi, j)),
        out_shape=jax.ShapeDtypeStruct((n, ATTN_Q), BF16),
        scratch_shapes=([pltpu.VMEM((n_keys, Q_PER_KV * tq), F32)] * slots
                        + [pltpu.VMEM((SUBLANES, Q_PER_KV * tq), F32)] * slots
                        + ([pltpu.VMEM((HEAD_DIM, cache[0].shape[2]), BF16)] if cache is not None else [])),
        compiler_params=_cparams(("arbitrary", "arbitrary", "arbitrary")),
    )(*args)


def _lane_pick(x, lane):
    idx = lax.broadcasted_iota(jnp.int32, x.shape, 1)
    return jnp.sum(jnp.where(idx == lane, x, 0.0), axis=-1, keepdims=True)


def _block_diag(x2):
    xb = x2.astype(BF16)
    z = jnp.zeros((DN_CHUNK, DN_CHUNK), BF16)
    return jnp.concatenate([jnp.concatenate([xb[:, :DN_CHUNK], z], axis=1),
                            jnp.concatenate([z, xb[:, DN_CHUNK:]], axis=1)], axis=0)


def _mm_pair(x2, y2):
    return _dot(x2.astype(BF16), _block_diag(y2))


def _unit_tri_inverses(lmats, ri, ci):
    def blk(s):
        return (ri ^ ci) < s

    eye = jnp.where(ri == ci, 1.0, 0.0)
    ps = [jnp.where(blk(DN_BASE), -lm, 0.0) for lm in lmats]
    xs = [eye + p for p in ps]
    s = 2
    while s < DN_BASE:
        ps = [_mm_pair(p, p) for p in ps]
        yield
        xs = [x + _mm_pair(x, p) for x, p in zip(xs, ps)]
        yield
        s *= 2
    s = DN_BASE
    while s < DN_CHUNK:
        sel = blk(2 * s) & jnp.logical_not(blk(s))
        ts = [_mm_pair(jnp.where(sel, lm, 0.0), x) for lm, x in zip(lmats, xs)]
        yield
        xs = [x - _mm_pair(x, t) for x, t in zip(xs, ts)]
        yield
        s *= 2
    return xs


def _dn_kernel(*refs, reverse, has_s0, want_state, nc, d, nt):
    refs = list(refs)
    q_ref, k_ref, v_ref, gb_ref, gr_ref = refs[:5]
    refs = refs[5:]
    s0_ref = refs.pop(0) if has_s0 else None
    o_ref = refs.pop(0)
    sfin_ref = refs.pop(0) if want_state else None
    s_sc = refs.pop(0)
    bufs = (refs[:5], refs[5:10])
    n_pairs = DN_HEADS // 2
    c2 = 2 * DN_CHUNK
    tt = nc * DN_CHUNK
    insts = [(c, p) for c in range(nc) for p in range(n_pairs)]

    g = pl.program_id(0)
    i_scan = (g - 1) % nt

    def pair_cols(x, lane_a, lane_b):
        shape = (x.shape[0], DN_CHUNK)
        return jnp.concatenate([jnp.broadcast_to(_lane_pick(x, lane_a), shape),
                                jnp.broadcast_to(_lane_pick(x, lane_b), shape)], axis=1)

    def prepare(buf):
        u_buf, wq_buf, attn_buf, kt_buf, gt_buf = buf
        gb = gb_ref[...]
        g_r = gr_ref[...]
        bi = lax.broadcasted_iota(jnp.int32, (tt, tt), 0)
        bj = lax.broadcasted_iota(jnp.int32, (tt, tt), 1)
        same = (bi ^ bj) < DN_CHUNK
        if reverse:
            tri_c = jnp.where(same & (bj >= bi), 1.0, 0.0).astype(BF16)
            tri_r = jnp.where(same & (bi >= bj), 1.0, 0.0).astype(BF16)
        else:
            tri_c = jnp.where(same & (bj <= bi), 1.0, 0.0).astype(BF16)
            tri_r = jnp.where(same & (bi <= bj), 1.0, 0.0).astype(BF16)
        gcum_c = sum(_dot(tri_c, part) for part in _split3(gb))
        gcum_r = sum(_dot(part, tri_r) for part in _split3(g_r))
        yield

        ri = lax.broadcasted_iota(jnp.int32, (DN_CHUNK, c2), 0)
        ci = lax.broadcasted_iota(jnp.int32, (DN_CHUNK, c2), 1) & (DN_CHUNK - 1)
        incl = (ci >= ri) if reverse else (ci <= ri)
        strict = (ci > ri) if reverse else (ci < ri)
        pre = []
        for c, p in insts:
            rs = slice(c * DN_CHUNK, (c + 1) * DN_CHUNK)
            cs = slice(p * c2, (p + 1) * c2)
            lane_b = d * DN_HEADS + 2 * p
            lane_g = 2 * DN_HEADS + d * DN_HEADS + 2 * p
            q2 = q_ref[rs, cs].astype(F32)
            k2 = k_ref[rs, cs].astype(F32)
            v2 = v_ref[rs, cs].astype(F32)
            g_i = pair_cols(gcum_c[rs, :], lane_g, lane_g + 1)
            b_i = pair_cols(gb[rs, :], lane_b, lane_b + 1)
            g_j = jnp.concatenate([gcum_r[lane_g:lane_g + 1, rs], gcum_r[lane_g + 1:lane_g + 2, rs]], axis=1)
            g_tot = g_i[0:1, :] if reverse else g_i[DN_CHUNK - 1:DN_CHUNK, :]
            decay = jnp.where(incl, jnp.exp(jnp.where(incl, g_i - g_j, 0.0)), 0.0)
            pre.append(dict(q2=q2, k2=k2, v2=v2, g_i=g_i, b_i=b_i, g_tot=g_tot, decay=decay, kb=k2 * b_i))
        a2s = [_dot_nt(jnp.concatenate([p["kb"], p["q2"]], axis=0).astype(BF16), _block_diag(p["k2"]))
               for p in pre]
        yield
        lmats = [jnp.where(strict, a2[:DN_CHUNK] * p["decay"], 0.0) for a2, p in zip(a2s, pre)]
        for n, (a2, p) in enumerate(zip(a2s, pre)):
            attn_buf[n] = (a2[DN_CHUNK:] * p["decay"]).astype(BF16)
        tinvs = yield from _unit_tri_inverses(lmats, ri, ci)
        gams = [jnp.exp(p["g_i"]) for p in pre]
        us = [_mm_pair(t, p["v2"] * p["b_i"]) for t, p in zip(tinvs, pre)]
        yield
        ws = [_mm_pair(t, p["kb"] * gam) for t, p, gam in zip(tinvs, pre, gams)]
        yield
        for n, (p, u, w, gam) in enumerate(zip(pre, us, ws, gams)):
            u_buf[n] = u
            wq_buf[n] = jnp.concatenate([w, p["q2"] * gam], axis=0).astype(BF16)
            kt_buf[n] = (p["k2"] * jnp.exp(p["g_tot"] - p["g_i"])).astype(BF16)
            gt_buf[n] = jnp.broadcast_to(jnp.exp(p["g_tot"]), (SUBLANES, c2))

    def scan(buf):
        u_buf, wq_buf, attn_buf, kt_buf, gt_buf = buf
        si = lax.broadcasted_iota(jnp.int32, (c2, c2), 0)
        sj = lax.broadcasted_iota(jnp.int32, (c2, c2), 1)
        on_diag = (si < DN_CHUNK) == (sj < DN_CHUNK)
        pairs = range(n_pairs)
        states = [s_sc[p] for p in pairs]
        for c in (range(nc - 1, -1, -1) if reverse else range(nc)):
            ns = [c * n_pairs + p for p in pairs]
            m1s = [_dot(wq_buf[n], s.astype(BF16)) for n, s in zip(ns, states)]
            yield
            v_news = [u_buf[n] - m1[:DN_CHUNK] for n, m1 in zip(ns, m1s)]
            outs = [m1[DN_CHUNK:] + _mm_pair(attn_buf[n], v) for n, m1, v in zip(ns, m1s, v_news)]
            yield
            states = [s * gt_buf[n][0:1, :] + jnp.where(on_diag, _dot_tn(kt_buf[n], v.astype(BF16)), 0.0)
                      for n, s, v in zip(ns, states, v_news)]
            for p in pairs:
                o_ref[c * DN_CHUNK:(c + 1) * DN_CHUNK, p * c2:(p + 1) * c2] = outs[p]
            yield
        for p in pairs:
            s_sc[p] = states[p]

    def step(par):
        @pl.when(i_scan == 0)
        def _():
            s_sc[...] = jnp.zeros_like(s_sc)
            if has_s0:
                for h in range(DN_HEADS):
                    lo = (h % 2) * DN_CHUNK
                    s_sc[h // 2, lo:lo + DN_CHUNK, lo:lo + DN_CHUNK] = s0_ref[h]

        _interleave(prepare(bufs[par]), scan(bufs[1 - par]))

        if want_state:
            @pl.when(i_scan == nt - 1)
            def _():
                for h in range(DN_HEADS):
                    lo = (h % 2) * DN_CHUNK
                    sfin_ref[h] = s_sc[h // 2, lo:lo + DN_CHUNK, lo:lo + DN_CHUNK]

    pl.when(g == 0)(lambda: _interleave(prepare(bufs[0])))
    pl.when((g > 0) & (g % 2 == 0))(functools.partial(step, 0))
    pl.when(g % 2 == 1)(functools.partial(step, 1))


def _deltanet_dir(qn, kn, vn, gb, gbt, s0, b, t, d, want_state):
    n = qn.shape[0]
    nc = min(4, t // DN_CHUNK)
    tt = nc * DN_CHUNK
    nt = t // tt
    n_tiles = b * nt
    reverse = d == 1
    c2 = 2 * DN_CHUNK
    n_inst = nc * (DN_HEADS // 2)

    def seq(gt):
        bi, i = gt // nt, gt % nt
        return bi, ((nt - 1 - i) if reverse else i)

    def prep_tile(g):
        return seq(jnp.minimum(g, n_tiles - 1))

    def scan_tile(g):
        return seq(jnp.maximum(g - 1, 0))

    def row_block(bt):
        return bt[0] * nt + bt[1]

    qkv_spec = pl.BlockSpec((tt, DN_QK), lambda g: (row_block(prep_tile(g)), 0))
    in_specs = [
        qkv_spec, qkv_spec, qkv_spec,
        pl.BlockSpec((tt, LANES), lambda g: (row_block(prep_tile(g)), 0)),
        pl.BlockSpec((4 * DN_HEADS, tt), lambda g: (0, row_block(prep_tile(g)))),
    ]
    args = [qn, kn, vn, gb, gbt]
    if s0 is not None:
        state, layer = s0
        in_specs.append(pl.BlockSpec((None, None, None, DN_HEADS, DN_DK, DN_DV),
                                     lambda g: (scan_tile(g)[0], layer, d, 0, 0, 0)))
        args.append(state)
    out_specs = [pl.BlockSpec((tt, DN_VW), lambda g: (row_block(scan_tile(g)), 0))]
    out_shape = [jax.ShapeDtypeStruct((n, DN_VW), F32)]
    if want_state:
        out_specs.append(pl.BlockSpec((None, DN_HEADS, DN_DK, DN_DV), lambda g: (scan_tile(g)[0], 0, 0, 0)))
        out_shape.append(jax.ShapeDtypeStruct((b, DN_HEADS, DN_DK, DN_DV), F32))
    prepared = [pltpu.VMEM((n_inst, DN_CHUNK, c2), F32), pltpu.VMEM((n_inst, c2, c2), BF16),
                pltpu.VMEM((n_inst, DN_CHUNK, c2), BF16), pltpu.VMEM((n_inst, DN_CHUNK, c2), BF16),
                pltpu.VMEM((n_inst, SUBLANES, c2), F32)]
    res = pl.pallas_call(
        functools.partial(_dn_kernel, reverse=reverse, has_s0=s0 is not None, want_state=want_state,
                          nc=nc, d=d, nt=nt),
        grid=(n_tiles + 1,),
        in_specs=in_specs,
        out_specs=out_specs,
        out_shape=out_shape,
        scratch_shapes=[pltpu.VMEM((DN_HEADS // 2, c2, c2), F32)] + prepared + prepared,
        compiler_params=_cparams(("arbitrary",)),
    )(*args)
    return res if want_state else (res[0], None)


def _post_kernel(x_ref, mod_ref, cb_ref, cc_ref, cx_ref, ccp_ref, cxp_ref, ccn_ref, cxn_ref, cw_ref,
                 att_ref, of_ref, ob_ref, z_ref, ng_ref, ga_ref, gb_ref, gc_ref, wpa_ref, wpb_ref, wpc_ref, wo_ref,
                 gpost_ref, o_ref, *, tm, tps):
    i = pl.program_id(0)
    first = (i % tps) == 0
    last = (i % tps) == tps - 1
    u = cc_ref[...].astype(F32) * cx_ref[...].astype(F32)
    hl = BF16_SUBLANES - 1
    prev_row = jnp.where(first, 0.0, ccp_ref[hl:hl + 1, :].astype(F32) * cxp_ref[hl:hl + 1, :].astype(F32))
    next_row = jnp.where(last, 0.0, ccn_ref[0:1, :].astype(F32) * cxn_ref[0:1, :].astype(F32))
    rows = lax.broadcasted_iota(jnp.int32, (tm, 1), 0)
    u_prev = jnp.where(rows == 0, prev_row, pltpu.roll(u, 1, axis=0))
    u_next = jnp.where(rows == tm - 1, next_row, pltpu.roll(u, tm - 1, axis=0))
    cw = cw_ref[...]
    conv = cw[0:1, :] * u_prev + cw[1:2, :] * u + cw[2:3, :] * u_next
    ya = _dot((cb_ref[...].astype(F32) * conv).astype(BF16), wpa_ref[...])
    yb = _dot(att_ref[...], wpb_ref[...])
    o = of_ref[...] + ob_ref[...]
    z = z_ref[...].astype(F32)
    parts = []
    for h in range(DN_HEADS):
        sl = slice(h * DN_DV, (h + 1) * DN_DV)
        parts.append((_rms(o[:, sl], ng_ref[...]) * _silu(z[:, sl])).astype(BF16))
    yc = _dot(jnp.concatenate(parts, axis=1), wpc_ref[...])
    mix_in = (_sigmoid(ga_ref[...].astype(F32)) * ya + _sigmoid(gb_ref[...].astype(F32)) * yb
              + _sigmoid(gc_ref[...].astype(F32)) * yc)
    mix = _dot(mix_in.astype(BF16), wo_ref[...])
    o_ref[...] = x_ref[...] + mod_ref[0, 2:3, :] * _rms(mix, gpost_ref[...])


def _post_mixer(x, mod3, mod_base, rows_per_mod, t, proj, att, o_f, o_b, conv_w, ng, wpa, wpb, wpc, wo, layer,
                g_post1):
    n = x.shape[0]
    tm = 256
    tps = t // tm
    hb = tm // BF16_SUBLANES
    cwid = CONV_WIDTH
    c0 = REST_CONV // cwid

    def prev(i):
        return jnp.maximum(i * hb - 1, 0)

    def nxt(i):
        return jnp.minimum((i + 1) * hb, n // BF16_SUBLANES - 1)

    g0 = REST_GATE // D_MODEL
    in_specs = [
        pl.BlockSpec((tm, D_MODEL), lambda i: (i, 0)),
        pl.BlockSpec((1, 6, D_MODEL), lambda i: (mod_base + (i * tm) // rows_per_mod, 0, 0)),
        pl.BlockSpec((tm, cwid), lambda i: (i, c0)),
        pl.BlockSpec((tm, cwid), lambda i: (i, c0 + 1)),
        pl.BlockSpec((tm, cwid), lambda i: (i, c0 + 2)),
        pl.BlockSpec((BF16_SUBLANES, cwid), lambda i: (prev(i), c0 + 1)),
        pl.BlockSpec((BF16_SUBLANES, cwid), lambda i: (prev(i), c0 + 2)),
        pl.BlockSpec((BF16_SUBLANES, cwid), lambda i: (nxt(i), c0 + 1)),
        pl.BlockSpec((BF16_SUBLANES, cwid), lambda i: (nxt(i), c0 + 2)),
        _const_spec((3, cwid)),
        pl.BlockSpec((tm, ATTN_Q), lambda i: (i, 0)),
        pl.BlockSpec((tm, DN_VW), lambda i: (i, 0)),
        pl.BlockSpec((tm, DN_VW), lambda i: (i, 0)),
        pl.BlockSpec((tm, DN_VW), lambda i: (i, REST_Z // DN_VW)),
        _const_spec((1, DN_DV)),
        pl.BlockSpec((tm, D_MODEL), lambda i: (i, g0)),
        pl.BlockSpec((tm, D_MODEL), lambda i: (i, g0 + 1)),
        pl.BlockSpec((tm, D_MODEL), lambda i: (i, g0 + 2)),
        _layer_spec((CONV_WIDTH, D_MODEL), layer), _layer_spec((ATTN_Q, D_MODEL), layer),
        _layer_spec((DN_VW, D_MODEL), layer), _layer_spec((D_MODEL, D_MODEL), layer),
        _const_spec((1, D_MODEL)),
    ]
    return pl.pallas_call(
        functools.partial(_post_kernel, tm=tm, tps=tps),
        grid=(n // tm,),
        in_specs=in_specs,
        out_specs=pl.BlockSpec((tm, D_MODEL), lambda i: (i, 0)),
        out_shape=jax.ShapeDtypeStruct((n, D_MODEL), F32),
        compiler_params=_cparams(("arbitrary",)),
    )(x, mod3, proj, proj, proj, proj, proj, proj, proj, conv_w, att, o_f, o_b, proj, ng,
      proj, proj, proj, wpa, wpb, wpc, wo, g_post1)


def _ffn_kernel(x_ref, mod_ref, gpre_ref, wg_ref, wu_ref, wd_ref, gpost_ref, o_ref, act_sc, *, tf):
    x = x_ref[...]
    h2 = (_rms(x, gpre_ref[...]) * (1.0 + mod_ref[0, 4:5, :]) + mod_ref[0, 3:4, :]).astype(BF16)
    for j in range(0, D_FF, tf):
        gate = _dot(h2, wg_ref[:, j:j + tf])
        up = _dot(h2, wu_ref[:, j:j + tf])
        act_sc[:, j:j + tf] = (_silu(gate) * up).astype(BF16)
    ffn = _dot(act_sc[...], wd_ref[...])
    o_ref[...] = x + mod_ref[0, 5:6, :] * _rms(ffn, gpost_ref[...])


def _ffn(x, mod3, mod_base, rows_per_mod, g_pre2, wg, wu, wd, layer, g_post2):
    n = x.shape[0]
    tm = 512
    return pl.pallas_call(
        functools.partial(_ffn_kernel, tf=256),
        grid=(n // tm,),
        in_specs=[pl.BlockSpec((tm, D_MODEL), lambda i: (i, 0)),
                  pl.BlockSpec((1, 6, D_MODEL), lambda i: (mod_base + (i * tm) // rows_per_mod, 0, 0)),
                  _const_spec((1, D_MODEL)),
                  _layer_spec((D_MODEL, D_FF), layer), _layer_spec((D_MODEL, D_FF), layer),
                  _layer_spec((D_FF, D_MODEL), layer),
                  _const_spec((1, D_MODEL))],
        out_specs=pl.BlockSpec((tm, D_MODEL), lambda i: (i, 0)),
        out_shape=jax.ShapeDtypeStruct((n, D_MODEL), F32),
        scratch_shapes=[pltpu.VMEM((tm, D_FF), BF16)],
        compiler_params=_cparams(("arbitrary",)),
    )(x, mod3, g_pre2, wg, wu, wd, g_post2)


def _rope_tables(t):
    rows = t // GRID_W
    row_id = np.repeat(np.arange(rows, dtype=np.float32), GRID_W)
    col_id = np.tile(np.arange(GRID_W, dtype=np.float32), rows)
    n_freq = HEAD_DIM // 4
    inv_freq = (np.float32(ROPE_THETA) ** (-np.arange(n_freq, dtype=np.float32) / np.float32(n_freq))).astype(np.float32)
    ang = np.concatenate([row_id[:, None] * inv_freq, col_id[:, None] * inv_freq], axis=-1).astype(np.float32)
    cos = np.repeat(np.cos(ang), 2, axis=-1).astype(np.float32)
    sin = np.repeat(np.sin(ang), 2, axis=-1).astype(np.float32)
    sign = np.tile(np.array([-1.0, 1.0], np.float32), HEAD_DIM // 2)
    return jnp.asarray(cos), jnp.asarray(sin * sign)


def _layer(x, b, t, mod3, mod_base, rows_per_mod, lw, rope_tabs, cache, state0, want_state):
    q_p, k_p, vt_p, v_f32, qn, kn, vn, gb, gbt, proj = _inproj(x, t, mod3, mod_base, rows_per_mod, lw,
                                                               rope_tabs, want_state)
    att = _attention(q_p, k_p, vt_p, b, t, cache)
    dn = [_deltanet_dir(qn, kn, vn, gb, gbt, state0, b, t, d, want_state) for d in range(2)]
    x = _post_mixer(x, mod3, mod_base, rows_per_mod, t, proj, att, dn[0][0], dn[1][0], lw["conv_w"],
                    lw["dn_norm_g"], lw["w_pa"], lw["w_pb"], lw["w_pc"], lw["w_o"], lw["layer"], lw["g_post1"])
    x = _ffn(x, mod3, mod_base, rows_per_mod, lw["g_pre2"], lw["w_gate"], lw["w_up"], lw["w_down"],
             lw["layer"], lw["g_post2"])
    if not want_state:
        return x, None
    s_fin = jnp.stack([dn[0][1], dn[1][1]], axis=1)
    return x, (k_p, v_f32, s_fin)


def kernel(x_prompt, x_sample, cache_k, cache_v, state_dn, c, c_ctx, w_mod, b_mod, g_pre1, g_post1, g_pre2, g_post2, w_in, conv_w, g_qn, g_kn, dn_conv_w, dn_a_log, dn_dt_bias, dn_norm_g, w_pa, w_pb, w_pc, w_o, w_gate, w_up, w_down):
    bp, tp, d = x_prompt.shape
    bs, ts, _ = x_sample.shape
    depth = w_mod.shape[0]
    past = cache_k.shape[2]

    mod_rows = -(-(bs + 1) // SUBLANES) * SUBLANES
    cv = jnp.zeros((mod_rows, d), F32).at[:bs].set(c).at[bs].set(c_ctx)
    mod_all = _modulation(cv, w_mod, b_mod).reshape(depth, mod_rows, 6, d)

    stacked = {"w_pa": w_pa.astype(BF16), "w_pb": w_pb.astype(BF16), "w_pc": w_pc.astype(BF16),
               "w_o": w_o.astype(BF16), "w_gate": w_gate.astype(BF16), "w_up": w_up.astype(BF16),
               "w_down": w_down.astype(BF16)}
    w_main = w_in.astype(BF16)
    w_tail = jnp.concatenate(
        [w_in[:, :, COL_GATE:D_IN], w_in[:, :, COL_SMALL:COL_GATE],
         jnp.zeros((depth, d, LANES - 4 * DN_HEADS), w_in.dtype)], axis=-1).astype(BF16)
    lane_pad = ((0, 0), (2 * DN_HEADS, LANES - 4 * DN_HEADS))
    a_flat = dn_a_log.reshape(depth, 2 * DN_HEADS)
    b_flat = dn_dt_bias.reshape(depth, 2 * DN_HEADS)
    rope_tabs = _rope_tables(ts)
    cache_k4 = cache_k.reshape(bs, depth, past, ATTN_KV)
    cache_v4 = cache_v.reshape(bs, depth, past, ATTN_KV)

    xp = x_prompt.reshape(bp * tp, d)
    xs = x_sample.reshape(bs * ts, d)
    ks, vs, ss = [], [], []
    for l in range(depth):
        lw = {
            "g_pre1": g_pre1[l][None], "g_post1": g_post1[l][None], "g_pre2": g_pre2[l][None],
            "g_post2": g_post2[l][None], "layer": l, "w_in": w_main, "w_tail": w_tail, "conv_w": conv_w[l],
            "g_qn": g_qn[l][None], "g_kn": g_kn[l][None], "dn_conv_w": dn_conv_w[l],
            "dn_ac": jnp.pad(a_flat[l][None], lane_pad), "dn_bc": jnp.pad(b_flat[l][None], lane_pad),
            "dn_norm_g": dn_norm_g[l][None],
            **stacked,
        }
        mod3 = mod_all[l]
        xp, (k_l, v_l, s_l) = _layer(xp, bp, tp, mod3, bs, bp * tp, lw, None, None, None, True)
        ks.append(k_l.reshape(bp, tp, N_KV_HEADS, HEAD_DIM))
        vs.append(v_l.reshape(bp, tp, N_KV_HEADS, HEAD_DIM))
        ss.append(s_l)
        xs, _ = _layer(xs, bs, ts, mod3, 0, ts, lw, rope_tabs, (cache_k4, cache_v4, l), (state_dn, l), False)
    return (xp.reshape(bp, tp, d), xs.reshape(bs, ts, d), jnp.stack(ks, axis=1), jnp.stack(vs, axis=1),
            jnp.stack(ss, axis=1))
```

```python
import functools

import jax
import jax.numpy as jnp
import numpy as np
from jax import lax
from jax.experimental import pallas as pl
from jax.experimental.pallas import tpu as pltpu

F32 = jnp.float32
BF16 = jnp.bfloat16

D_MODEL = 1024
EPS = 1e-6
GRID_W = 64
N_HEADS = 8
N_KV_HEADS = 2
Q_PER_KV = N_HEADS // N_KV_HEADS
HEAD_DIM = 128
ATTN_Q = N_HEADS * HEAD_DIM
ATTN_KV = N_KV_HEADS * HEAD_DIM
ROPE_THETA = 10000.0
CONV_WIDTH = 512
DN_HEADS = 4
DN_DK = 128
DN_DV = 128
DN_QK = DN_HEADS * DN_DK
DN_VW = DN_HEADS * DN_DV
D_FF = 2816

SUBLANES = 8
BF16_SUBLANES = 16
LANES = 128

COL_CONV = 0
COL_Q = COL_CONV + 3 * CONV_WIDTH
COL_K = COL_Q + ATTN_Q
COL_V = COL_K + ATTN_KV
COL_DN = COL_V + ATTN_KV
COL_Z = COL_DN + 3 * DN_QK
COL_SMALL = COL_Z + DN_VW
COL_GATE = COL_SMALL + 4 * DN_HEADS
D_IN = COL_GATE + 3 * D_MODEL
TAIL_GATE = 0
TAIL_SMALL = TAIL_GATE + 3 * D_MODEL
D_TAIL = TAIL_SMALL + LANES
REST_GATE = 0
REST_CONV = REST_GATE + 3 * D_MODEL
REST_Z = REST_CONV + 3 * CONV_WIDTH
D_REST = REST_Z + DN_VW

Q_SCALE = 1.4426950408889634 * HEAD_DIM ** -0.5
DN_CHUNK = 128
DN_BASE = 16
VMEM_LIMIT = 56 * 1024 * 1024


def _cparams(sem):
    return pltpu.CompilerParams(dimension_semantics=sem, vmem_limit_bytes=VMEM_LIMIT)


def _const_spec(shape):
    nd = len(shape)
    return pl.BlockSpec(shape, lambda *_: (0,) * nd, pipeline_mode=pl.Buffered(1))


def _layer_spec(shape, layer):
    return pl.BlockSpec((None,) + tuple(shape), lambda *_: (layer,) + (0,) * len(shape),
                        pipeline_mode=pl.Buffered(1))


def _dot(a, b):
    return jnp.dot(a, b, preferred_element_type=F32)


def _dot_nt(a, b):
    return lax.dot_general(a, b, (((1,), (1,)), ((), ())), preferred_element_type=F32)


def _dot_tn(a, b):
    return lax.dot_general(a, b, (((0,), (0,)), ((), ())), preferred_element_type=F32)


def _mm(a, b):
    return _dot(a.astype(BF16), b.astype(BF16))


def _rms(x, g):
    return x * lax.rsqrt(jnp.mean(x * x, axis=-1, keepdims=True) + EPS) * g


def _sigmoid(x):
    return 1.0 / (1.0 + jnp.exp(-x))


def _silu(x):
    return x * _sigmoid(x)


def _softplus(x):
    return jnp.maximum(x, 0.0) + jnp.log1p(jnp.exp(-jnp.abs(x)))


def _split3(x):
    hi = x.astype(BF16)
    r = x - hi.astype(F32)
    mid = r.astype(BF16)
    lo = (r - mid.astype(F32)).astype(BF16)
    return hi, mid, lo


def _mod_kernel(cv_ref, w_ref, b_ref, o_ref):
    cv = cv_ref[...]
    o_ref[0] = _dot(_silu(cv).astype(BF16), w_ref[0].astype(BF16)) + b_ref[0]


def _modulation(cv, w_mod, b_mod):
    depth, d, n6 = w_mod.shape
    rows = cv.shape[0]
    tn = 1536
    return pl.pallas_call(
        _mod_kernel,
        grid=(depth, n6 // tn),
        in_specs=[pl.BlockSpec((rows, d), lambda l, j: (0, 0)),
                  pl.BlockSpec((1, d, tn), lambda l, j: (l, 0, j)),
                  pl.BlockSpec((1, 1, tn), lambda l, j: (l, 0, j))],
        out_specs=pl.BlockSpec((1, rows, tn), lambda l, j: (l, 0, j)),
        out_shape=jax.ShapeDtypeStruct((depth, rows, n6), F32),
        compiler_params=_cparams(("arbitrary", "arbitrary")),
    )(cv, w_mod, b_mod.reshape(depth, 1, n6))


def _swap_pairs(y):
    lane = lax.broadcasted_iota(jnp.int32, y.shape, 1)
    return jnp.where(lane % 2 == 0, pltpu.roll(y, LANES - 1, axis=1), pltpu.roll(y, 1, axis=1))


def _interleave(*gens):
    gens = list(gens)
    while gens:
        for gen in list(gens):
            try:
                next(gen)
            except StopIteration:
                gens.remove(gen)


def _inproj_kernel(*refs, rope, want_v, tm, tps, tn):
    refs = list(refs)
    x_ref, xp_ref, xn_ref, mod_ref, g_ref, w_ref, wt_ref, gq_ref, gk_ref, cw_ref, ac_ref, bc_ref = refs[:12]
    refs = refs[12:]
    cos_ref, sin_ref = (refs.pop(0), refs.pop(0)) if rope else (None, None)
    q_out, k_out, v_out = refs[:3]
    refs = refs[3:]
    vf_out = refs.pop(0) if want_v else None
    qn_out, kn_out, vn_out, gb_out, gbt_out, rest_out = refs
    i = pl.program_id(0)

    def modnorm(x):
        return (_rms(x, g_ref[...]) * (1.0 + mod_ref[0, 1:2, :]) + mod_ref[0, 0:1, :]).astype(BF16)

    h = modnorm(x_ref[...])
    h_halo = modnorm(jnp.concatenate([xp_ref[...], xn_ref[...]], axis=0))

    def qk_head(x, g):
        y = _rms(x, g)
        if rope:
            y = y * cos_ref[...] + _swap_pairs(y) * sin_ref[...]
        return y

    def attn_epilogue():
        qkv = _dot(h, w_ref[:, COL_Q:COL_DN])
        yield
        for hh in range(N_HEADS):
            sl = slice(hh * HEAD_DIM, (hh + 1) * HEAD_DIM)
            q_out[:, sl] = (qk_head(qkv[:, sl], gq_ref[...]) * Q_SCALE).astype(q_out.dtype)
            if hh % 2 == 1:
                yield
        for hh in range(N_KV_HEADS):
            sl = slice(hh * HEAD_DIM, (hh + 1) * HEAD_DIM)
            k_out[:, sl] = qk_head(qkv[:, ATTN_Q + hh * HEAD_DIM:ATTN_Q + (hh + 1) * HEAD_DIM],
                                   gk_ref[...]).astype(k_out.dtype)
        v = qkv[:, ATTN_Q + ATTN_KV:]
        v_out[...] = v.T.astype(v_out.dtype)
        if want_v:
            vf_out[...] = v

    def dn_epilogue():
        dn = _dot(h, w_ref[:, COL_DN:COL_Z])
        dn_halo = _dot(h_halo, w_ref[:, COL_DN:COL_Z])
        gl = _dot(h, wt_ref[:, TAIL_SMALL:D_TAIL])
        yield
        rows = lax.broadcasted_iota(jnp.int32, (tm, 1), 0)
        prev_row = jnp.where((i % tps) == 0, 0.0, dn_halo[SUBLANES - 1:SUBLANES, :])
        next_row = jnp.where((i % tps) == tps - 1, 0.0, dn_halo[SUBLANES:SUBLANES + 1, :])
        dn_prev = jnp.where(rows == 0, prev_row, pltpu.roll(dn, 1, axis=0))
        dn_next = jnp.where(rows == tm - 1, next_row, pltpu.roll(dn, tm - 1, axis=0))
        cw = cw_ref[...]

        def conv_silu(sl):
            return _silu(cw[0:1, sl] * dn_prev[:, sl] + cw[1:2, sl] * dn[:, sl] + cw[2:3, sl] * dn_next[:, sl])

        for hh in range(DN_HEADS):
            sl = slice(hh * DN_DK, (hh + 1) * DN_DK)
            qh = conv_silu(sl)
            kh = conv_silu(slice(DN_QK + hh * DN_DK, DN_QK + (hh + 1) * DN_DK))
            qn_out[:, sl] = (qh * lax.rsqrt(jnp.sum(qh * qh, axis=-1, keepdims=True) + EPS)
                             * (DN_DK ** -0.5)).astype(qn_out.dtype)
            kn_out[:, sl] = (kh * lax.rsqrt(jnp.sum(kh * kh, axis=-1, keepdims=True) + EPS)).astype(kn_out.dtype)
            vn_out[:, sl] = conv_silu(slice(2 * DN_QK + hh * DN_DV, 2 * DN_QK + (hh + 1) * DN_DV)
                                      ).astype(vn_out.dtype)
            yield
        lane = lax.broadcasted_iota(jnp.int32, gl.shape, 1)
        gb = jnp.where(lane < 2 * DN_HEADS, _sigmoid(gl), -jnp.exp(ac_ref[...]) * _softplus(gl + bc_ref[...]))
        gb_out[...] = gb
        gbt_out[...] = gb.T

    def remaining():
        groups = [(wt_ref, TAIL_GATE, REST_GATE, 3 * D_MODEL), (w_ref, COL_CONV, REST_CONV, 3 * CONV_WIDTH),
                  (w_ref, COL_Z, REST_Z, DN_VW)]
        for ref, src, dst, width in groups:
            for j in range(0, width, tn):
                rest_out[:, dst + j:dst + j + tn] = _dot(h, ref[:, src + j:src + j + tn]).astype(rest_out.dtype)
                yield

    _interleave(dn_epilogue(), attn_epilogue(), remaining())


def _inproj(x, t, mod3, mod_base, rows_per_mod, lw, rope_tabs, want_v):
    n = x.shape[0]
    tm = 256
    tps = t // tm
    halo = tm // SUBLANES
    w_main, w_tail, layer = lw["w_in"], lw["w_tail"], lw["layer"]
    in_specs = [pl.BlockSpec((tm, D_MODEL), lambda i: (i, 0)),
                pl.BlockSpec((SUBLANES, D_MODEL), lambda i: (jnp.maximum(i * halo - 1, 0), 0)),
                pl.BlockSpec((SUBLANES, D_MODEL), lambda i: (jnp.minimum((i + 1) * halo, n // SUBLANES - 1), 0)),
                pl.BlockSpec((1, 6, D_MODEL), lambda i: (mod_base + (i * tm) // rows_per_mod, 0, 0)),
                _const_spec((1, D_MODEL)),
                _layer_spec((D_MODEL, COL_SMALL), layer), _layer_spec((D_MODEL, D_TAIL), layer),
                _const_spec((1, HEAD_DIM)), _const_spec((1, HEAD_DIM)),
                _const_spec((3, 3 * DN_QK)), _const_spec((1, LANES)), _const_spec((1, LANES))]
    args = [x, x, x, mod3, lw["g_pre1"], w_main, w_tail, lw["g_qn"], lw["g_kn"], lw["dn_conv_w"],
            lw["dn_ac"], lw["dn_bc"]]
    if rope_tabs is not None:
        in_specs += [pl.BlockSpec((tm, HEAD_DIM), lambda i: (i % tps, 0))] * 2
        args += list(rope_tabs)

    outs = [(ATTN_Q, BF16, False), (ATTN_KV, F32 if want_v else BF16, False), (ATTN_KV, BF16, True)]
    if want_v:
        outs.append((ATTN_KV, F32, False))
    outs += [(DN_QK, BF16, False), (DN_QK, BF16, False), (DN_VW, BF16, False), (LANES, F32, False),
             (LANES, F32, True), (D_REST, BF16, False)]
    res = pl.pallas_call(
        functools.partial(_inproj_kernel, rope=rope_tabs is not None, want_v=want_v, tm=tm, tps=tps, tn=512),
        grid=(n // tm,),
        in_specs=in_specs,
        out_specs=[pl.BlockSpec((w, tm), lambda i: (0, i)) if tr else pl.BlockSpec((tm, w), lambda i: (i, 0))
                   for w, _, tr in outs],
        out_shape=[jax.ShapeDtypeStruct((w, n) if tr else (n, w), dt) for w, dt, tr in outs],
        compiler_params=_cparams(("arbitrary",)),
    )(*args)
    res = list(res)
    q_p, k_p, vt_p = res[:3]
    v_f32 = res[3] if want_v else None
    qn, kn, vn, gb, gbt, rest = res[-6:]
    return q_p, k_p, vt_p, v_f32, qn, kn, vn, gb, gbt, rest


def _attn_kernel(*refs, has_cache, t, tk, tq, nq):
    refs = list(refs)
    q_ref = refs.pop(0)
    qn_ref = refs.pop(0) if nq > 1 else None
    kc_ref, vc_ref = (refs.pop(0), refs.pop(0)) if has_cache else (None, None)
    k_ref, v_ref, o_ref = refs[:3]
    refs = refs[3:]
    vct_sc = refs.pop() if has_cache else None
    s_bufs, m_bufs = refs[:len(refs) // 2], refs[len(refs) // 2:]
    qi = pl.program_id(2)

    if has_cache:
        @pl.when(qi == 0)
        def _():
            vct_sc[...] = vc_ref[...].T.astype(BF16)

    segs = []
    if has_cache:
        past = kc_ref.shape[0]
        segs += [(kc_ref, vct_sc, r, min(tk, past - r)) for r in range(0, past, tk)]
    segs += [(k_ref, v_ref, r, tk) for r in range(0, t, tk)]
    offs = [sum(w for _, _, _, w in segs[:i]) for i in range(len(segs))]

    def stack(ref):
        q = ref[...]
        return jnp.concatenate([q[:, g * HEAD_DIM:(g + 1) * HEAD_DIM] for g in range(Q_PER_KV)], axis=0)

    def scores(qs, s_ref, i, m_run):
        kr, _, r, w = segs[i]
        s = _dot_nt(kr[r:r + w, :].astype(BF16), qs)
        s_ref[offs[i]:offs[i] + w, :] = s
        for r0 in range(0, w, SUBLANES):
            blk = s[r0:r0 + SUBLANES, :]
            m_run = blk if m_run is None else jnp.maximum(m_run, blk)
        return m_run

    def weighted(s_ref, i, m, acc):
        _, vr, r, w = segs[i]
        p = jnp.exp2(s_ref[offs[i]:offs[i] + w, :] - m)
        den = p[0:SUBLANES, :]
        for r0 in range(SUBLANES, w, SUBLANES):
            den = den + p[r0:r0 + SUBLANES, :]
        pv = _dot(vr[:, r:r + w].astype(BF16), p.astype(BF16))
        return (pv, den) if acc is None else (acc[0] + pv, acc[1] + den)

    def finish(acc):
        o = (acc[0] / jnp.sum(acc[1], axis=0, keepdims=True)).T
        for g in range(Q_PER_KV):
            o_ref[:, g * HEAD_DIM:(g + 1) * HEAD_DIM] = o[g * tq:(g + 1) * tq].astype(o_ref.dtype)

    if nq > 1:
        @pl.when(qi == 0)
        def _():
            qs0 = stack(q_ref)
            m_run = None
            for i in range(len(segs)):
                m_run = scores(qs0, s_bufs[0], i, m_run)
            m_bufs[0][...] = m_run

        def step(s_cur, m_cur, s_nxt, m_nxt):
            qs_next = stack(qn_ref)
            m = jnp.max(m_cur[...], axis=0, keepdims=True)
            acc, m_run = None, None
            for i in range(len(segs)):
                m_run = scores(qs_next, s_nxt, i, m_run)
                acc = weighted(s_cur, i, m, acc)
            m_nxt[...] = m_run
            finish(acc)

        pl.when(qi % 2 == 0)(functools.partial(step, s_bufs[0], m_bufs[0], s_bufs[1], m_bufs[1]))
        pl.when(qi % 2 == 1)(functools.partial(step, s_bufs[1], m_bufs[1], s_bufs[0], m_bufs[0]))
    else:
        qs = stack(q_ref)
        m_run = None
        for i in range(len(segs)):
            m_run = scores(qs, s_bufs[0], i, m_run)
        m = jnp.max(m_run, axis=0, keepdims=True)
        acc = None
        for i in range(len(segs)):
            acc = weighted(s_bufs[0], i, m, acc)
        finish(acc)


def _attention(q_p, k_p, vt_p, b, t, cache):
    n = q_p.shape[0]
    tq = 128 if t > 256 else 256
    tk = min(t, 512)
    nq = t // tq
    qw = Q_PER_KV * HEAD_DIM
    in_specs = [pl.BlockSpec((tq, qw), lambda bi, j, qi: (bi * nq + qi, j))]
    args = [q_p]
    if nq > 1:
        in_specs.append(pl.BlockSpec((tq, qw), lambda bi, j, qi: (bi * nq + jnp.minimum(qi + 1, nq - 1), j)))
        args.append(q_p)
    if cache is not None:
        cache_k, cache_v, layer = cache
        past = cache_k.shape[2]
        cspec = pl.BlockSpec((None, None, past, HEAD_DIM), lambda bi, j, qi: (bi, layer, 0, j))
        in_specs += [cspec, cspec]
        args += [cache_k, cache_v]
    in_specs += [pl.BlockSpec((t, HEAD_DIM), lambda bi, j, qi: (bi, j)),
                 pl.BlockSpec((HEAD_DIM, t), lambda bi, j, qi: (j, bi))]
    args += [k_p, vt_p]
    n_keys = t + (cache[0].shape[2] if cache is not None else 0)
    slots = 2 if nq > 1 else 1
    return pl.pallas_call(
        functools.partial(_attn_kernel, has_cache=cache is not None, t=t, tk=tk, tq=tq, nq=nq),
        grid=(b, N_KV_HEADS, nq),
        in_specs=in_specs,
        out_specs=pl.BlockSpec((tq, qw), lambda bi, j, qi: (bi * nq + qi, j)),
        out_shape=jax.ShapeDtypeStruct((n, ATTN_Q), BF16),
        scratch_shapes=([pltpu.VMEM((n_keys, Q_PER_KV * tq), F32)] * slots
                        + [pltpu.VMEM((SUBLANES, Q_PER_KV * tq), F32)] * slots
                        + ([pltpu.VMEM((HEAD_DIM, cache[0].shape[2]), BF16)] if cache is not None else [])),
        compiler_params=_cparams(("arbitrary", "arbitrary", "arbitrary")),
    )(*args)


def _lane_pick(x, lane):
    idx = lax.broadcasted_iota(jnp.int32, x.shape, 1)
    return jnp.sum(jnp.where(idx == lane, x, 0.0), axis=-1, keepdims=True)


def _block_diag(x2):
    xb = x2.astype(BF16)
    z = jnp.zeros((DN_CHUNK, DN_CHUNK), BF16)
    return jnp.concatenate([jnp.concatenate([xb[:, :DN_CHUNK], z], axis=1),
                            jnp.concatenate([z, xb[:, DN_CHUNK:]], axis=1)], axis=0)


def _mm_pair(x2, y2):
    return _dot(x2.astype(BF16), _block_diag(y2))


def _unit_tri_inverses(lmats, ri, ci):
    def blk(s):
        return (ri ^ ci) < s

    eye = jnp.where(ri == ci, 1.0, 0.0)
    ps = [jnp.where(blk(DN_BASE), -lm, 0.0) for lm in lmats]
    xs = [eye + p for p in ps]
    s = 2
    while s < DN_BASE:
        ps = [_mm_pair(p, p) for p in ps]
        yield
        xs = [x + _mm_pair(x, p) for x, p in zip(xs, ps)]
        yield
        s *= 2
    s = DN_BASE
    while s < DN_CHUNK:
        sel = blk(2 * s) & jnp.logical_not(blk(s))
        ts = [_mm_pair(jnp.where(sel, lm, 0.0), x) for lm, x in zip(lmats, xs)]
        yield
        xs = [x - _mm_pair(x, t) for x, t in zip(xs, ts)]
        yield
        s *= 2
    return xs


def _dn_kernel(*refs, reverse, has_s0, want_state, nc, d, nt):
    refs = list(refs)
    q_ref, k_ref, v_ref, gb_ref, gr_ref = refs[:5]
    refs = refs[5:]
    s0_ref = refs.pop(0) if has_s0 else None
    o_ref = refs.pop(0)
    sfin_ref = refs.pop(0) if want_state else None
    s_sc = refs.pop(0)
    bufs = (refs[:5], refs[5:10])
    n_pairs = DN_HEADS // 2
    c2 = 2 * DN_CHUNK
    tt = nc * DN_CHUNK
    insts = [(c, p) for c in range(nc) for p in range(n_pairs)]

    g = pl.program_id(0)
    i_scan = (g - 1) % nt

    def pair_cols(x, lane_a, lane_b):
        shape = (x.shape[0], DN_CHUNK)
        return jnp.concatenate([jnp.broadcast_to(_lane_pick(x, lane_a), shape),
                                jnp.broadcast_to(_lane_pick(x, lane_b), shape)], axis=1)

    def prepare(buf):
        u_buf, wq_buf, attn_buf, kt_buf, gt_buf = buf
        gb = gb_ref[...]
        g_r = gr_ref[...]
        bi = lax.broadcasted_iota(jnp.int32, (tt, tt), 0)
        bj = lax.broadcasted_iota(jnp.int32, (tt, tt), 1)
        same = (bi ^ bj) < DN_CHUNK
        if reverse:
            tri_c = jnp.where(same & (bj >= bi), 1.0, 0.0).astype(BF16)
            tri_r = jnp.where(same & (bi >= bj), 1.0, 0.0).astype(BF16)
        else:
            tri_c = jnp.where(same & (bj <= bi), 1.0, 0.0).astype(BF16)
            tri_r = jnp.where(same & (bi <= bj), 1.0, 0.0).astype(BF16)
        gcum_c = sum(_dot(tri_c, part) for part in _split3(gb))
        gcum_r = sum(_dot(part, tri_r) for part in _split3(g_r))
        yield

        ri = lax.broadcasted_iota(jnp.int32, (DN_CHUNK, c2), 0)
        ci = lax.broadcasted_iota(jnp.int32, (DN_CHUNK, c2), 1) & (DN_CHUNK - 1)
        incl = (ci >= ri) if reverse else (ci <= ri)
        strict = (ci > ri) if reverse else (ci < ri)
        pre = []
        for c, p in insts:
            rs = slice(c * DN_CHUNK, (c + 1) * DN_CHUNK)
            cs = slice(p * c2, (p + 1) * c2)
            lane_b = d * DN_HEADS + 2 * p
            lane_g = 2 * DN_HEADS + d * DN_HEADS + 2 * p
            q2 = q_ref[rs, cs].astype(F32)
            k2 = k_ref[rs, cs].astype(F32)
            v2 = v_ref[rs, cs].astype(F32)
            g_i = pair_cols(gcum_c[rs, :], lane_g, lane_g + 1)
            b_i = pair_cols(gb[rs, :], lane_b, lane_b + 1)
            g_j = jnp.concatenate([gcum_r[lane_g:lane_g + 1, rs], gcum_r[lane_g + 1:lane_g + 2, rs]], axis=1)
            g_tot = g_i[0:1, :] if reverse else g_i[DN_CHUNK - 1:DN_CHUNK, :]
            decay = jnp.where(incl, jnp.exp(jnp.where(incl, g_i - g_j, 0.0)), 0.0)
            pre.append(dict(q2=q2, k2=k2, v2=v2, g_i=g_i, b_i=b_i, g_tot=g_tot, decay=decay, kb=k2 * b_i))
        a2s = [_dot_nt(jnp.concatenate([p["kb"], p["q2"]], axis=0).astype(BF16), _block_diag(p["k2"]))
               for p in pre]
        yield
        lmats = [jnp.where(strict, a2[:DN_CHUNK] * p["decay"], 0.0) for a2, p in zip(a2s, pre)]
        for n, (a2, p) in enumerate(zip(a2s, pre)):
            attn_buf[n] = (a2[DN_CHUNK:] * p["decay"]).astype(BF16)
        tinvs = yield from _unit_tri_inverses(lmats, ri, ci)
        gams = [jnp.exp(p["g_i"]) for p in pre]
        us = [_mm_pair(t, p["v2"] * p["b_i"]) for t, p in zip(tinvs, pre)]
        yield
        ws = [_mm_pair(t, p["kb"] * gam) for t, p, gam in zip(tinvs, pre, gams)]
        yield
        for n, (p, u, w, gam) in enumerate(zip(pre, us, ws, gams)):
            u_buf[n] = u
            wq_buf[n] = jnp.concatenate([w, p["q2"] * gam], axis=0).astype(BF16)
            kt_buf[n] = (p["k2"] * jnp.exp(p["g_tot"] - p["g_i"])).astype(BF16)
            gt_buf[n] = jnp.broadcast_to(jnp.exp(p["g_tot"]), (SUBLANES, c2))

    def scan(buf):
        u_buf, wq_buf, attn_buf, kt_buf, gt_buf = buf
        si = lax.broadcasted_iota(jnp.int32, (c2, c2), 0)
        sj = lax.broadcasted_iota(jnp.int32, (c2, c2), 1)
        on_diag = (si < DN_CHUNK) == (sj < DN_CHUNK)
        pairs = range(n_pairs)
        states = [s_sc[p] for p in pairs]
        for c in (range(nc - 1, -1, -1) if reverse else range(nc)):
            ns = [c * n_pairs + p for p in pairs]
            m1s = [_dot(wq_buf[n], s.astype(BF16)) for n, s in zip(ns, states)]
            yield
            v_news = [u_buf[n] - m1[:DN_CHUNK] for n, m1 in zip(ns, m1s)]
            outs = [m1[DN_CHUNK:] + _mm_pair(attn_buf[n], v) for n, m1, v in zip(ns, m1s, v_news)]
            yield
            states = [s * gt_buf[n][0:1, :] + jnp.where(on_diag, _dot_tn(kt_buf[n], v.astype(BF16)), 0.0)
                      for n, s, v in zip(ns, states, v_news)]
            for p in pairs:
                o_ref[c * DN_CHUNK:(c + 1) * DN_CHUNK, p * c2:(p + 1) * c2] = outs[p]
            yield
        for p in pairs:
            s_sc[p] = states[p]

    def step(par):
        @pl.when(i_scan == 0)
        def _():
            s_sc[...] = jnp.zeros_like(s_sc)
            if has_s0:
                for h in range(DN_HEADS):
                    lo = (h % 2) * DN_CHUNK
                    s_sc[h // 2, lo:lo + DN_CHUNK, lo:lo + DN_CHUNK] = s0_ref[h]

        _interleave(prepare(bufs[par]), scan(bufs[1 - par]))

        if want_state:
            @pl.when(i_scan == nt - 1)
            def _():
                for h in range(DN_HEADS):
                    lo = (h % 2) * DN_CHUNK
                    sfin_ref[h] = s_sc[h // 2, lo:lo + DN_CHUNK, lo:lo + DN_CHUNK]

    pl.when(g == 0)(lambda: _interleave(prepare(bufs[0])))
    pl.when((g > 0) & (g % 2 == 0))(functools.partial(step, 0))
    pl.when(g % 2 == 1)(functools.partial(step, 1))


def _deltanet_dir(qn, kn, vn, gb, gbt, s0, b, t, d, want_state):
    n = qn.shape[0]
    nc = min(4, t // DN_CHUNK)
    tt = nc * DN_CHUNK
    nt = t // tt
    n_tiles = b * nt
    reverse = d == 1
    c2 = 2 * DN_CHUNK
    n_inst = nc * (DN_HEADS // 2)

    def seq(gt):
        bi, i = gt // nt, gt % nt
        return bi, ((nt - 1 - i) if reverse else i)

    def prep_tile(g):
        return seq(jnp.minimum(g, n_tiles - 1))

    def scan_tile(g):
        return seq(jnp.maximum(g - 1, 0))

    def row_block(bt):
        return bt[0] * nt + bt[1]

    qkv_spec = pl.BlockSpec((tt, DN_QK), lambda g: (row_block(prep_tile(g)), 0))
    in_specs = [
        qkv_spec, qkv_spec, qkv_spec,
        pl.BlockSpec((tt, LANES), lambda g: (row_block(prep_tile(g)), 0)),
        pl.BlockSpec((4 * DN_HEADS, tt), lambda g: (0, row_block(prep_tile(g)))),
    ]
    args = [qn, kn, vn, gb, gbt]
    if s0 is not None:
        state, layer = s0
        in_specs.append(pl.BlockSpec((None, None, None, DN_HEADS, DN_DK, DN_DV),
                                     lambda g: (scan_tile(g)[0], layer, d, 0, 0, 0)))
        args.append(state)
    out_specs = [pl.BlockSpec((tt, DN_VW), lambda g: (row_block(scan_tile(g)), 0))]
    out_shape = [jax.ShapeDtypeStruct((n, DN_VW), F32)]
    if want_state:
        out_specs.append(pl.BlockSpec((None, DN_HEADS, DN_DK, DN_DV), lambda g: (scan_tile(g)[0], 0, 0, 0)))
        out_shape.append(jax.ShapeDtypeStruct((b, DN_HEADS, DN_DK, DN_DV), F32))
    prepared = [pltpu.VMEM((n_inst, DN_CHUNK, c2), F32), pltpu.VMEM((n_inst, c2, c2), BF16),
                pltpu.VMEM((n_inst, DN_CHUNK, c2), BF16), pltpu.VMEM((n_inst, DN_CHUNK, c2), BF16),
                pltpu.VMEM((n_inst, SUBLANES, c2), F32)]
    res = pl.pallas_call(
        functools.partial(_dn_kernel, reverse=reverse, has_s0=s0 is not None, want_state=want_state,
                          nc=nc, d=d, nt=nt),
        grid=(n_tiles + 1,),
        in_specs=in_specs,
        out_specs=out_specs,
        out_shape=out_shape,
        scratch_shapes=[pltpu.VMEM((DN_HEADS // 2, c2, c2), F32)] + prepared + prepared,
        compiler_params=_cparams(("arbitrary",)),
    )(*args)
    return res if want_state else (res[0], None)


def _post_kernel(x_ref, mod_ref, cb_ref, cc_ref, cx_ref, ccp_ref, cxp_ref, ccn_ref, cxn_ref, cw_ref,
                 att_ref, of_ref, ob_ref, z_ref, ng_ref, ga_ref, gb_ref, gc_ref, wpa_ref, wpb_ref, wpc_ref, wo_ref,
                 gpost_ref, o_ref, *, tm, tps):
    i = pl.program_id(0)
    first = (i % tps) == 0
    last = (i % tps) == tps - 1
    u = cc_ref[...].astype(F32) * cx_ref[...].astype(F32)
    hl = BF16_SUBLANES - 1
    prev_row = jnp.where(first, 0.0, ccp_ref[hl:hl + 1, :].astype(F32) * cxp_ref[hl:hl + 1, :].astype(F32))
    next_row = jnp.where(last, 0.0, ccn_ref[0:1, :].astype(F32) * cxn_ref[0:1, :].astype(F32))
    rows = lax.broadcasted_iota(jnp.int32, (tm, 1), 0)
    u_prev = jnp.where(rows == 0, prev_row, pltpu.roll(u, 1, axis=0))
    u_next = jnp.where(rows == tm - 1, next_row, pltpu.roll(u, tm - 1, axis=0))
    cw = cw_ref[...]
    conv = cw[0:1, :] * u_prev + cw[1:2, :] * u + cw[2:3, :] * u_next
    ya = _dot((cb_ref[...].astype(F32) * conv).astype(BF16), wpa_ref[...])
    yb = _dot(att_ref[...], wpb_ref[...])
    o = of_ref[...] + ob_ref[...]
    z = z_ref[...].astype(F32)
    parts = []
    for h in range(DN_HEADS):
        sl = slice(h * DN_DV, (h + 1) * DN_DV)
        parts.append((_rms(o[:, sl], ng_ref[...]) * _silu(z[:, sl])).astype(BF16))
    yc = _dot(jnp.concatenate(parts, axis=1), wpc_ref[...])
    mix_in = (_sigmoid(ga_ref[...].astype(F32)) * ya + _sigmoid(gb_ref[...].astype(F32)) * yb
              + _sigmoid(gc_ref[...].astype(F32)) * yc)
    mix = _dot(mix_in.astype(BF16), wo_ref[...])
    o_ref[...] = x_ref[...] + mod_ref[0, 2:3, :] * _rms(mix, gpost_ref[...])


def _post_mixer(x, mod3, mod_base, rows_per_mod, t, proj, att, o_f, o_b, conv_w, ng, wpa, wpb, wpc, wo, layer,
                g_post1):
    n = x.shape[0]
    tm = 256
    tps = t // tm
    hb = tm // BF16_SUBLANES
    cwid = CONV_WIDTH
    c0 = REST_CONV // cwid

    def prev(i):
        return jnp.maximum(i * hb - 1, 0)

    def nxt(i):
        return jnp.minimum((i + 1) * hb, n // BF16_SUBLANES - 1)

    g0 = REST_GATE // D_MODEL
    in_specs = [
        pl.BlockSpec((tm, D_MODEL), lambda i: (i, 0)),
        pl.BlockSpec((1, 6, D_MODEL), lambda i: (mod_base + (i * tm) // rows_per_mod, 0, 0)),
        pl.BlockSpec((tm, cwid), lambda i: (i, c0)),
        pl.BlockSpec((tm, cwid), lambda i: (i, c0 + 1)),
        pl.BlockSpec((tm, cwid), lambda i: (i, c0 + 2)),
        pl.BlockSpec((BF16_SUBLANES, cwid), lambda i: (prev(i), c0 + 1)),
        pl.BlockSpec((BF16_SUBLANES, cwid), lambda i: (prev(i), c0 + 2)),
        pl.BlockSpec((BF16_SUBLANES, cwid), lambda i: (nxt(i), c0 + 1)),
        pl.BlockSpec((BF16_SUBLANES, cwid), lambda i: (nxt(i), c0 + 2)),
        _const_spec((3, cwid)),
        pl.BlockSpec((tm, ATTN_Q), lambda i: (i, 0)),
        pl.BlockSpec((tm, DN_VW), lambda i: (i, 0)),
        pl.BlockSpec((tm, DN_VW), lambda i: (i, 0)),
        pl.BlockSpec((tm, DN_VW), lambda i: (i, REST_Z // DN_VW)),
        _const_spec((1, DN_DV)),
        pl.BlockSpec((tm, D_MODEL), lambda i: (i, g0)),
        pl.BlockSpec((tm, D_MODEL), lambda i: (i, g0 + 1)),
        pl.BlockSpec((tm, D_MODEL), lambda i: (i, g0 + 2)),
        _layer_spec((CONV_WIDTH, D_MODEL), layer), _layer_spec((ATTN_Q, D_MODEL), layer),
        _layer_spec((DN_VW, D_MODEL), layer), _layer_spec((D_MODEL, D_MODEL), layer),
        _const_spec((1, D_MODEL)),
    ]
    return pl.pallas_call(
        functools.partial(_post_kernel, tm=tm, tps=tps),
        grid=(n // tm,),
        in_specs=in_specs,
        out_specs=pl.BlockSpec((tm, D_MODEL), lambda i: (i, 0)),
        out_shape=jax.ShapeDtypeStruct((n, D_MODEL), F32),
        compiler_params=_cparams(("arbitrary",)),
    )(x, mod3, proj, proj, proj, proj, proj, proj, proj, conv_w, att, o_f, o_b, proj, ng,
      proj, proj, proj, wpa, wpb, wpc, wo, g_post1)


def _ffn_kernel(x_ref, mod_ref, gpre_ref, wg_ref, wu_ref, wd_ref, gpost_ref, o_ref, act_sc, *, tf):
    x = x_ref[...]
    h2 = (_rms(x, gpre_ref[...]) * (1.0 + mod_ref[0, 4:5, :]) + mod_ref[0, 3:4, :]).astype(BF16)
    for j in range(0, D_FF, tf):
        gate = _dot(h2, wg_ref[:, j:j + tf])
        up = _dot(h2, wu_ref[:, j:j + tf])
        act_sc[:, j:j + tf] = (_silu(gate) * up).astype(BF16)
    ffn = _dot(act_sc[...], wd_ref[...])
    o_ref[...] = x + mod_ref[0, 5:6, :] * _rms(ffn, gpost_ref[...])


def _ffn(x, mod3, mod_base, rows_per_mod, g_pre2, wg, wu, wd, layer, g_post2):
    n = x.shape[0]
    tm = 512
    return pl.pallas_call(
        functools.partial(_ffn_kernel, tf=256),
        grid=(n // tm,),
        in_specs=[pl.BlockSpec((tm, D_MODEL), lambda i: (i, 0)),
                  pl.BlockSpec((1, 6, D_MODEL), lambda i: (mod_base + (i * tm) // rows_per_mod, 0, 0)),
                  _const_spec((1, D_MODEL)),
                  _layer_spec((D_MODEL, D_FF), layer), _layer_spec((D_MODEL, D_FF), layer),
                  _layer_spec((D_FF, D_MODEL), layer),
                  _const_spec((1, D_MODEL))],
        out_specs=pl.BlockSpec((tm, D_MODEL), lambda i: (i, 0)),
        out_shape=jax.ShapeDtypeStruct((n, D_MODEL), F32),
        scratch_shapes=[pltpu.VMEM((tm, D_FF), BF16)],
        compiler_params=_cparams(("arbitrary",)),
    )(x, mod3, g_pre2, wg, wu, wd, g_post2)


def _rope_tables(t):
    rows = t // GRID_W
    row_id = np.repeat(np.arange(rows, dtype=np.float32), GRID_W)
    col_id = np.tile(np.arange(GRID_W, dtype=np.float32), rows)
    n_freq = HEAD_DIM // 4
    inv_freq = (np.float32(ROPE_THETA) ** (-np.arange(n_freq, dtype=np.float32) / np.float32(n_freq))).astype(np.float32)
    ang = np.concatenate([row_id[:, None] * inv_freq, col_id[:, None] * inv_freq], axis=-1).astype(np.float32)
    cos = np.repeat(np.cos(ang), 2, axis=-1).astype(np.float32)
    sin = np.repeat(np.sin(ang), 2, axis=-1).astype(np.float32)
    sign = np.tile(np.array([-1.0, 1.0], np.float32), HEAD_DIM // 2)
    return jnp.asarray(cos), jnp.asarray(sin * sign)


def _layer(x, b, t, mod3, mod_base, rows_per_mod, lw, rope_tabs, cache, state0, want_state):
    q_p, k_p, vt_p, v_f32, qn, kn, vn, gb, gbt, proj = _inproj(x, t, mod3, mod_base, rows_per_mod, lw,
                                                               rope_tabs, want_state)
    att = _attention(q_p, k_p, vt_p, b, t, cache)
    dn = [_deltanet_dir(qn, kn, vn, gb, gbt, state0, b, t, d, want_state) for d in range(2)]
    x = _post_mixer(x, mod3, mod_base, rows_per_mod, t, proj, att, dn[0][0], dn[1][0], lw["conv_w"],
                    lw["dn_norm_g"], lw["w_pa"], lw["w_pb"], lw["w_pc"], lw["w_o"], lw["layer"], lw["g_post1"])
    x = _ffn(x, mod3, mod_base, rows_per_mod, lw["g_pre2"], lw["w_gate"], lw["w_up"], lw["w_down"],
             lw["layer"], lw["g_post2"])
    if not want_state:
        return x, None
    s_fin = jnp.stack([dn[0][1], dn[1][1]], axis=1)
    return x, (k_p, v_f32, s_fin)


def kernel(x_prompt, x_sample, cache_k, cache_v, state_dn, c, c_ctx, w_mod, b_mod, g_pre1, g_post1, g_pre2, g_post2, w_in, conv_w, g_qn, g_kn, dn_conv_w, dn_a_log, dn_dt_bias, dn_norm_g, w_pa, w_pb, w_pc, w_o, w_gate, w_up, w_down):
    bp, tp, d = x_prompt.shape
    bs, ts, _ = x_sample.shape
    depth = w_mod.shape[0]
    past = cache_k.shape[2]

    mod_rows = -(-(bs + 1) // SUBLANES) * SUBLANES
    cv = jnp.zeros((mod_rows, d), F32).at[:bs].set(c).at[bs].set(c_ctx)
    mod_all = _modulation(cv, w_mod, b_mod).reshape(depth, mod_rows, 6, d)

    stacked = {"w_pa": w_pa.astype(BF16), "w_pb": w_pb.astype(BF16), "w_pc": w_pc.astype(BF16),
               "w_o": w_o.astype(BF16), "w_gate": w_gate.astype(BF16), "w_up": w_up.astype(BF16),
               "w_down": w_down.astype(BF16)}
    w_main = w_in.astype(BF16)
    w_tail = jnp.concatenate(
        [w_in[:, :, COL_GATE:D_IN], w_in[:, :, COL_SMALL:COL_GATE],
         jnp.zeros((depth, d, LANES - 4 * DN_HEADS), w_in.dtype)], axis=-1).astype(BF16)
    lane_pad = ((0, 0), (2 * DN_HEADS, LANES - 4 * DN_HEADS))
    a_flat = dn_a_log.reshape(depth, 2 * DN_HEADS)
    b_flat = dn_dt_bias.reshape(depth, 2 * DN_HEADS)
    rope_tabs = _rope_tables(ts)
    cache_k4 = cache_k.reshape(bs, depth, past, ATTN_KV)
    cache_v4 = cache_v.reshape(bs, depth, past, ATTN_KV)

    xp = x_prompt.reshape(bp * tp, d)
    xs = x_sample.reshape(bs * ts, d)
    ks, vs, ss = [], [], []
    for l in range(depth):
        lw = {
            "g_pre1": g_pre1[l][None], "g_post1": g_post1[l][None], "g_pre2": g_pre2[l][None],
            "g_post2": g_post2[l][None], "layer": l, "w_in": w_main, "w_tail": w_tail, "conv_w": conv_w[l],
            "g_qn": g_qn[l][None], "g_kn": g_kn[l][None], "dn_conv_w": dn_conv_w[l],
            "dn_ac": jnp.pad(a_flat[l][None], lane_pad), "dn_bc": jnp.pad(b_flat[l][None], lane_pad),
            "dn_norm_g": dn_norm_g[l][None],
            **stacked,
        }
        mod3 = mod_all[l]
        xp, (k_l, v_l, s_l) = _layer(xp, bp, tp, mod3, bs, bp * tp, lw, None, None, None, True)
        ks.append(k_l.reshape(bp, tp, N_KV_HEADS, HEAD_DIM))
        vs.append(v_l.reshape(bp, tp, N_KV_HEADS, HEAD_DIM))
        ss.append(s_l)
        xs, _ = _layer(xs, bs, ts, mod3, 0, ts, lw, rope_tabs, (cache_k4, cache_v4, l), (state_dn, l), False)
    return (xp.reshape(bp, tp, d), xs.reshape(bs, ts, d), jnp.stack(ks, axis=1), jnp.stack(vs, axis=1),
            jnp.stack(ss, axis=1))
```

```python
import functools

import jax
import jax.numpy as jnp
import numpy as np
from jax import lax
from jax.experimental import pallas as pl
from jax.experimental.pallas import tpu as pltpu

F32 = jnp.float32
BF16 = jnp.bfloat16

D_MODEL = 1024
EPS = 1e-6
GRID_W = 64
N_HEADS = 8
N_KV_HEADS = 2
Q_PER_KV = N_HEADS // N_KV_HEADS
HEAD_DIM = 128
ATTN_Q = N_HEADS * HEAD_DIM
ATTN_KV = N_KV_HEADS * HEAD_DIM
ROPE_THETA = 10000.0
CONV_WIDTH = 512
DN_HEADS = 4
DN_DK = 128
DN_DV = 128
DN_QK = DN_HEADS * DN_DK
DN_VW = DN_HEADS * DN_DV
D_FF = 2816

SUBLANES = 8
BF16_SUBLANES = 16
LANES = 128

COL_CONV = 0
COL_Q = COL_CONV + 3 * CONV_WIDTH
COL_K = COL_Q + ATTN_Q
COL_V = COL_K + ATTN_KV
COL_DN = COL_V + ATTN_KV
COL_Z = COL_DN + 3 * DN_QK
COL_SMALL = COL_Z + DN_VW
COL_GATE = COL_SMALL + 4 * DN_HEADS
D_IN = COL_GATE + 3 * D_MODEL
TAIL_GATE = 0
TAIL_SMALL = TAIL_GATE + 3 * D_MODEL
D_TAIL = TAIL_SMALL + LANES
REST_GATE = 0
REST_CONV = REST_GATE + 3 * D_MODEL
REST_Z = REST_CONV + 3 * CONV_WIDTH
D_REST = REST_Z + DN_VW

Q_SCALE = 1.4426950408889634 * HEAD_DIM ** -0.5
DN_CHUNK = 128
DN_BASE = 16
VMEM_LIMIT = 56 * 1024 * 1024


def _cparams(sem):
    return pltpu.CompilerParams(dimension_semantics=sem, vmem_limit_bytes=VMEM_LIMIT)


def _const_spec(shape):
    nd = len(shape)
    return pl.BlockSpec(shape, lambda *_: (0,) * nd, pipeline_mode=pl.Buffered(1))


def _layer_spec(shape, layer):
    return pl.BlockSpec((None,) + tuple(shape), lambda *_: (layer,) + (0,) * len(shape),
                        pipeline_mode=pl.Buffered(1))


def _dot(a, b):
    return jnp.dot(a, b, preferred_element_type=F32)


def _dot_nt(a, b):
    return lax.dot_general(a, b, (((1,), (1,)), ((), ())), preferred_element_type=F32)


def _dot_tn(a, b):
    return lax.dot_general(a, b, (((0,), (0,)), ((), ())), preferred_element_type=F32)


def _rms(x, g):
    return x * lax.rsqrt(jnp.mean(x * x, axis=-1, keepdims=True) + EPS) * g


def _sigmoid(x):
    return 1.0 / (1.0 + jnp.exp(-x))


def _silu(x):
    return x * _sigmoid(x)


def _softplus(x):
    return jnp.maximum(x, 0.0) + jnp.log1p(jnp.exp(-jnp.abs(x)))


def _split3(x):
    hi = x.astype(BF16)
    r = x - hi.astype(F32)
    mid = r.astype(BF16)
    lo = (r - mid.astype(F32)).astype(BF16)
    return hi, mid, lo


def _mod_kernel(cv_ref, w_ref, b_ref, o_ref):
    cv = cv_ref[...]
    o_ref[0] = _dot(_silu(cv).astype(BF16), w_ref[0].astype(BF16)) + b_ref[0]


def _modulation(cv, w_mod, b_mod):
    depth, d, n6 = w_mod.shape
    rows = cv.shape[0]
    tn = 1536
    return pl.pallas_call(
        _mod_kernel,
        grid=(depth, n6 // tn),
        in_specs=[pl.BlockSpec((rows, d), lambda l, j: (0, 0)),
                  pl.BlockSpec((1, d, tn), lambda l, j: (l, 0, j)),
                  pl.BlockSpec((1, 1, tn), lambda l, j: (l, 0, j))],
        out_specs=pl.BlockSpec((1, rows, tn), lambda l, j: (l, 0, j)),
        out_shape=jax.ShapeDtypeStruct((depth, rows, n6), F32),
        compiler_params=_cparams(("arbitrary", "arbitrary")),
    )(cv, w_mod, b_mod.reshape(depth, 1, n6))


def _swap_pairs(y):
    lane = lax.broadcasted_iota(jnp.int32, y.shape, 1)
    return jnp.where(lane % 2 == 0, pltpu.roll(y, LANES - 1, axis=1), pltpu.roll(y, 1, axis=1))


def _interleave(*gens):
    gens = list(gens)
    while gens:
        for gen in list(gens):
            try:
                next(gen)
            except StopIteration:
                gens.remove(gen)


def _inproj_kernel(*refs, rope, want_v, tm, tps, tn):
    refs = list(refs)
    x_ref, xp_ref, xn_ref, mod_ref, g_ref, w_ref, wt_ref, gq_ref, gk_ref, cw_ref, ac_ref, bc_ref = refs[:12]
    refs = refs[12:]
    cos_ref, sin_ref = (refs.pop(0), refs.pop(0)) if rope else (None, None)
    q_out, k_out, v_out = refs[:3]
    refs = refs[3:]
    vf_out = refs.pop(0) if want_v else None
    qn_out, kn_out, vn_out, gb_out, gbt_out, rest_out = refs
    i = pl.program_id(0)

    def modnorm(x):
        return (_rms(x, g_ref[...]) * (1.0 + mod_ref[0, 1:2, :]) + mod_ref[0, 0:1, :]).astype(BF16)

    h = modnorm(x_ref[...])
    h_halo = modnorm(jnp.concatenate([xp_ref[...], xn_ref[...]], axis=0))

    def qk_head(x, g):
        y = _rms(x, g)
        if rope:
            y = y * cos_ref[...] + _swap_pairs(y) * sin_ref[...]
        return y

    def attn_epilogue():
        qkv = _dot(h, w_ref[:, COL_Q:COL_DN])
        yield
        for hh in range(N_HEADS):
            sl = slice(hh * HEAD_DIM, (hh + 1) * HEAD_DIM)
            q_out[:, sl] = (qk_head(qkv[:, sl], gq_ref[...]) * Q_SCALE).astype(q_out.dtype)
            if hh % 2 == 1:
                yield
        for hh in range(N_KV_HEADS):
            sl = slice(hh * HEAD_DIM, (hh + 1) * HEAD_DIM)
            k_out[:, sl] = qk_head(qkv[:, ATTN_Q + hh * HEAD_DIM:ATTN_Q + (hh + 1) * HEAD_DIM],
                                   gk_ref[...]).astype(k_out.dtype)
        v = qkv[:, ATTN_Q + ATTN_KV:]
        v_out[...] = v.T.astype(v_out.dtype)
        if want_v:
            vf_out[...] = v

    def dn_epilogue():
        dn = _dot(h, w_ref[:, COL_DN:COL_Z])
        dn_halo = _dot(h_halo, w_ref[:, COL_DN:COL_Z])
        gl = _dot(h, wt_ref[:, TAIL_SMALL:D_TAIL])
        yield
        rows = lax.broadcasted_iota(jnp.int32, (tm, 1), 0)
        prev_row = jnp.where((i % tps) == 0, 0.0, dn_halo[SUBLANES - 1:SUBLANES, :])
        next_row = jnp.where((i % tps) == tps - 1, 0.0, dn_halo[SUBLANES:SUBLANES + 1, :])
        dn_prev = jnp.where(rows == 0, prev_row, pltpu.roll(dn, 1, axis=0))
        dn_next = jnp.where(rows == tm - 1, next_row, pltpu.roll(dn, tm - 1, axis=0))
        cw = cw_ref[...]

        def conv_silu(sl):
            return _silu(cw[0:1, sl] * dn_prev[:, sl] + cw[1:2, sl] * dn[:, sl] + cw[2:3, sl] * dn_next[:, sl])

        for hh in range(DN_HEADS):
            sl = slice(hh * DN_DK, (hh + 1) * DN_DK)
            qh = conv_silu(sl)
            kh = conv_silu(slice(DN_QK + hh * DN_DK, DN_QK + (hh + 1) * DN_DK))
            qn_out[:, sl] = (qh * lax.rsqrt(jnp.sum(qh * qh, axis=-1, keepdims=True) + EPS)
                             * (DN_DK ** -0.5)).astype(qn_out.dtype)
            kn_out[:, sl] = (kh * lax.rsqrt(jnp.sum(kh * kh, axis=-1, keepdims=True) + EPS)).astype(kn_out.dtype)
            vn_out[:, sl] = conv_silu(slice(2 * DN_QK + hh * DN_DV, 2 * DN_QK + (hh + 1) * DN_DV)
                                      ).astype(vn_out.dtype)
            yield
        lane = lax.broadcasted_iota(jnp.int32, gl.shape, 1)
        gb = jnp.where(lane < 2 * DN_HEADS, _sigmoid(gl), -jnp.exp(ac_ref[...]) * _softplus(gl + bc_ref[...]))
        gb_out[...] = gb
        gbt_out[...] = gb.T

    def remaining():
        groups = [(wt_ref, TAIL_GATE, REST_GATE, 3 * D_MODEL), (w_ref, COL_CONV, REST_CONV, 3 * CONV_WIDTH),
                  (w_ref, COL_Z, REST_Z, DN_VW)]
        for ref, src, dst, width in groups:
            for j in range(0, width, tn):
                rest_out[:, dst + j:dst + j + tn] = _dot(h, ref[:, src + j:src + j + tn]).astype(rest_out.dtype)
                yield

    _interleave(dn_epilogue(), attn_epilogue(), remaining())


def _inproj(x, t, mod3, mod_base, rows_per_mod, lw, rope_tabs, want_v):
    n = x.shape[0]
    tm = 256
    tps = t // tm
    halo = tm // SUBLANES
    w_main, w_tail, layer = lw["w_in"], lw["w_tail"], lw["layer"]
    in_specs = [pl.BlockSpec((tm, D_MODEL), lambda i: (i, 0)),
                pl.BlockSpec((SUBLANES, D_MODEL), lambda i: (jnp.maximum(i * halo - 1, 0), 0)),
                pl.BlockSpec((SUBLANES, D_MODEL), lambda i: (jnp.minimum((i + 1) * halo, n // SUBLANES - 1), 0)),
                pl.BlockSpec((1, 6, D_MODEL), lambda i: (mod_base + (i * tm) // rows_per_mod, 0, 0)),
                _const_spec((1, D_MODEL)),
                _layer_spec((D_MODEL, COL_SMALL), layer), _layer_spec((D_MODEL, D_TAIL), layer),
                _const_spec((1, HEAD_DIM)), _const_spec((1, HEAD_DIM)),
                _const_spec((3, 3 * DN_QK)), _const_spec((1, LANES)), _const_spec((1, LANES))]
    args = [x, x, x, mod3, lw["g_pre1"], w_main, w_tail, lw["g_qn"], lw["g_kn"], lw["dn_conv_w"],
            lw["dn_ac"], lw["dn_bc"]]
    if rope_tabs is not None:
        in_specs += [pl.BlockSpec((tm, HEAD_DIM), lambda i: (i % tps, 0))] * 2
        args += list(rope_tabs)

    outs = [(ATTN_Q, BF16, False), (ATTN_KV, F32 if want_v else BF16, False), (ATTN_KV, BF16, True)]
    if want_v:
        outs.append((ATTN_KV, F32, False))
    outs += [(DN_QK, BF16, False), (DN_QK, BF16, False), (DN_VW, BF16, False), (LANES, F32, False),
             (LANES, F32, True), (D_REST, BF16, False)]
    res = pl.pallas_call(
        functools.partial(_inproj_kernel, rope=rope_tabs is not None, want_v=want_v, tm=tm, tps=tps, tn=512),
        grid=(n // tm,),
        in_specs=in_specs,
        out_specs=[pl.BlockSpec((w, tm), lambda i: (0, i)) if tr else pl.BlockSpec((tm, w), lambda i: (i, 0))
                   for w, _, tr in outs],
        out_shape=[jax.ShapeDtypeStruct((w, n) if tr else (n, w), dt) for w, dt, tr in outs],
        compiler_params=_cparams(("arbitrary",)),
    )(*args)
    res = list(res)
    q_p, k_p, vt_p = res[:3]
    v_f32 = res[3] if want_v else None
    qn, kn, vn, gb, gbt, rest = res[-6:]
    return q_p, k_p, vt_p, v_f32, qn, kn, vn, gb, gbt, rest


def _attn_kernel(*refs, has_cache, t, tk, tq, nq):
    refs = list(refs)
    q_ref = refs.pop(0)
    qn_ref = refs.pop(0) if nq > 1 else None
    kc_ref, vc_ref = (refs.pop(0), refs.pop(0)) if has_cache else (None, None)
    k_ref, v_ref, o_ref = refs[:3]
    refs = refs[3:]
    vct_sc = refs.pop() if has_cache else None
    s_bufs, m_bufs = refs[:len(refs) // 2], refs[len(refs) // 2:]
    qi = pl.program_id(2)

    if has_cache:
        @pl.when(qi == 0)
        def _():
            vct_sc[...] = vc_ref[...].T.astype(BF16)

    segs = []
    if has_cache:
        past = kc_ref.shape[0]
        segs += [(kc_ref, vct_sc, r, min(tk, past - r)) for r in range(0, past, tk)]
    segs += [(k_ref, v_ref, r, tk) for r in range(0, t, tk)]
    offs = [sum(w for _, _, _, w in segs[:i]) for i in range(len(segs))]

    def stack(ref):
        q = ref[...]
        return jnp.concatenate([q[:, g * HEAD_DIM:(g + 1) * HEAD_DIM] for g in range(Q_PER_KV)], axis=0)

    def scores(qs, s_ref, i, m_run):
        kr, _, r, w = segs[i]
        s = _dot_nt(kr[r:r + w, :].astype(BF16), qs)
        s_ref[offs[i]:offs[i] + w, :] = s
        for r0 in range(0, w, SUBLANES):
            blk = s[r0:r0 + SUBLANES, :]
            m_run = blk if m_run is None else jnp.maximum(m_run, blk)
        return m_run

    def weighted(s_ref, i, m, acc):
        _, vr, r, w = segs[i]
        p = jnp.exp2(s_ref[offs[i]:offs[i] + w, :] - m)
        den = p[0:SUBLANES, :]
        for r0 in range(SUBLANES, w, SUBLANES):
            den = den + p[r0:r0 + SUBLANES, :]
        pv = _dot(vr[:, r:r + w].astype(BF16), p.astype(BF16))
        return (pv, den) if acc is None else (acc[0] + pv, acc[1] + den)

    def finish(acc):
        o = (acc[0] / jnp.sum(acc[1], axis=0, keepdims=True)).T
        for g in range(Q_PER_KV):
            o_ref[:, g * HEAD_DIM:(g + 1) * HEAD_DIM] = o[g * tq:(g + 1) * tq].astype(o_ref.dtype)

    if nq > 1:
        @pl.when(qi == 0)
        def _():
            qs0 = stack(q_ref)
            m_run = None
            for i in range(len(segs)):
                m_run = scores(qs0, s_bufs[0], i, m_run)
            m_bufs[0][...] = m_run

        def step(s_cur, m_cur, s_nxt, m_nxt):
            qs_next = stack(qn_ref)
            m = jnp.max(m_cur[...], axis=0, keepdims=True)
            acc, m_run = None, None
            for i in range(len(segs)):
                m_run = scores(qs_next, s_nxt, i, m_run)
                acc = weighted(s_cur, i, m, acc)
            m_nxt[...] = m_run
            finish(acc)

        pl.when(qi % 2 == 0)(functools.partial(step, s_bufs[0], m_bufs[0], s_bufs[1], m_bufs[1]))
        pl.when(qi % 2 == 1)(functools.partial(step, s_bufs[1], m_bufs[1], s_bufs[0], m_bufs[0]))
    else:
        qs = stack(q_ref)
        m_run = None
        for i in range(len(segs)):
            m_run = scores(qs, s_bufs[0], i, m_run)
        m = jnp.max(m_run, axis=0, keepdims=True)
        acc = None
        for i in range(len(segs)):
            acc = weighted(s_bufs[0], i, m, acc)
        finish(acc)


def _attention(q_p, k_p, vt_p, b, t, cache):
    n = q_p.shape[0]
    tq = 128 if t > 256 else 256
    tk = min(t, 512)
    nq = t // tq
    qw = Q_PER_KV * HEAD_DIM
    in_specs = [pl.BlockSpec((tq, qw), lambda bi, j, qi: (bi * nq + qi, j))]
    args = [q_p]
    if nq > 1:
        in_specs.append(pl.BlockSpec((tq, qw), lambda bi, j, qi: (bi * nq + jnp.minimum(qi + 1, nq - 1), j)))
        args.append(q_p)
    if cache is not None:
        cache_k, cache_v, layer = cache
        past = cache_k.shape[2]
        cspec = pl.BlockSpec((None, None, past, HEAD_DIM), lambda bi, j, qi: (bi, layer, 0, j))
        in_specs += [cspec, cspec]
        args += [cache_k, cache_v]
    in_specs += [pl.BlockSpec((t, HEAD_DIM), lambda bi, j, qi: (bi, j)),
                 pl.BlockSpec((HEAD_DIM, t), lambda bi, j, qi: (j, bi))]
    args += [k_p, vt_p]
    n_keys = t + (cache[0].shape[2] if cache is not None else 0)
    slots = 2 if nq > 1 else 1
    return pl.pallas_call(
        functools.partial(_attn_kernel, has_cache=cache is not None, t=t, tk=tk, tq=tq, nq=nq),
        grid=(b, N_KV_HEADS, nq),
        in_specs=in_specs,
        out_specs=pl.BlockSpec((tq, qw), lambda bi, j, qi: (bi * nq + qi, j)),
        out_shape=jax.ShapeDtypeStruct((n, ATTN_Q), BF16),
        scratch_shapes=([pltpu.VMEM((n_keys, Q_PER_KV * tq), F32)] * slots
                        + [pltpu.VMEM((SUBLANES, Q_PER_KV * tq), F32)] * slots
                        + ([pltpu.VMEM((HEAD_DIM, cache[0].shape[2]), BF16)] if cache is not None else [])),
        compiler_params=_cparams(("arbitrary", "arbitrary", "arbitrary")),
    )(*args)


def _lane_pick(x, lane):
    idx = lax.broadcasted_iota(jnp.int32, x.shape, 1)
    return jnp.sum(jnp.where(idx == lane, x, 0.0), axis=-1, keepdims=True)


def _block_diag(x2):
    xb = x2.astype(BF16)
    z = jnp.zeros((DN_CHUNK, DN_CHUNK), BF16)
    return jnp.concatenate([jnp.concatenate([xb[:, :DN_CHUNK], z], axis=1),
                            jnp.concatenate([z, xb[:, DN_CHUNK:]], axis=1)], axis=0)


def _mm_pair(x2, y2):
    return _dot(x2.astype(BF16), _block_diag(y2))


def _unit_tri_inverses(lmats, ri, ci):
    def blk(s):
        return (ri ^ ci) < s

    eye = jnp.where(ri == ci, 1.0, 0.0)
    ps = [jnp.where(blk(DN_BASE), -lm, 0.0) for lm in lmats]
    xs = [eye + p for p in ps]
    s = 2
    while s < DN_BASE:
        ps = [_mm_pair(p, p) for p in ps]
        yield
        xs = [x + _mm_pair(x, p) for x, p in zip(xs, ps)]
        yield
        s *= 2
    s = DN_BASE
    while s < DN_CHUNK:
        sel = blk(2 * s) & jnp.logical_not(blk(s))
        ts = [_mm_pair(jnp.where(sel, lm, 0.0), x) for lm, x in zip(lmats, xs)]
        yield
        xs = [x - _mm_pair(x, t) for x, t in zip(xs, ts)]
        yield
        s *= 2
    return xs


def _dn_kernel(*refs, reverse, has_s0, want_state, n_seq, cps, d, nt):
    nc = n_seq * cps
    refs = list(refs)
    q_ref, k_ref, v_ref, gb_ref, gr_ref = refs[:5]
    refs = refs[5:]
    s0_ref = refs.pop(0) if has_s0 else None
    o_ref = refs.pop(0)
    sfin_ref = refs.pop(0) if want_state else None
    s_sc = refs.pop(0)
    bufs = (refs[:5], refs[5:10])
    n_pairs = DN_HEADS // 2
    c2 = 2 * DN_CHUNK
    tt = nc * DN_CHUNK
    insts = [(c, p) for c in range(nc) for p in range(n_pairs)]

    g = pl.program_id(0)
    i_scan = (g - 1) % nt

    def pair_cols(x, lane_a, lane_b):
        shape = (x.shape[0], DN_CHUNK)
        return jnp.concatenate([jnp.broadcast_to(_lane_pick(x, lane_a), shape),
                                jnp.broadcast_to(_lane_pick(x, lane_b), shape)], axis=1)

    def prepare(buf):
        u_buf, wq_buf, attn_buf, kt_buf, gt_buf = buf
        gb = gb_ref[...]
        g_r = gr_ref[...]
        bi = lax.broadcasted_iota(jnp.int32, (tt, tt), 0)
        bj = lax.broadcasted_iota(jnp.int32, (tt, tt), 1)
        same = (bi ^ bj) < DN_CHUNK
        if reverse:
            tri_c = jnp.where(same & (bj >= bi), 1.0, 0.0).astype(BF16)
            tri_r = jnp.where(same & (bi >= bj), 1.0, 0.0).astype(BF16)
        else:
            tri_c = jnp.where(same & (bj <= bi), 1.0, 0.0).astype(BF16)
            tri_r = jnp.where(same & (bi <= bj), 1.0, 0.0).astype(BF16)
        gcum_c = sum(_dot(tri_c, part) for part in _split3(gb))
        gcum_r = sum(_dot(part, tri_r) for part in _split3(g_r))
        yield

        ri = lax.broadcasted_iota(jnp.int32, (DN_CHUNK, c2), 0)
        ci = lax.broadcasted_iota(jnp.int32, (DN_CHUNK, c2), 1) & (DN_CHUNK - 1)
        incl = (ci >= ri) if reverse else (ci <= ri)
        strict = (ci > ri) if reverse else (ci < ri)
        pre = []
        for c, p in insts:
            rs = slice(c * DN_CHUNK, (c + 1) * DN_CHUNK)
            cs = slice(p * c2, (p + 1) * c2)
            lane_b = d * DN_HEADS + 2 * p
            lane_g = 2 * DN_HEADS + d * DN_HEADS + 2 * p
            q2 = q_ref[rs, cs].astype(F32)
            k2 = k_ref[rs, cs].astype(F32)
            v2 = v_ref[rs, cs].astype(F32)
            g_i = pair_cols(gcum_c[rs, :], lane_g, lane_g + 1)
            b_i = pair_cols(gb[rs, :], lane_b, lane_b + 1)
            g_j = jnp.concatenate([gcum_r[lane_g:lane_g + 1, rs], gcum_r[lane_g + 1:lane_g + 2, rs]], axis=1)
            g_tot = g_i[0:1, :] if reverse else g_i[DN_CHUNK - 1:DN_CHUNK, :]
            decay = jnp.where(incl, jnp.exp(jnp.where(incl, g_i - g_j, 0.0)), 0.0)
            pre.append(dict(q2=q2, k2=k2, v2=v2, g_i=g_i, b_i=b_i, g_tot=g_tot, decay=decay, kb=k2 * b_i))
        a2s = [_dot_nt(jnp.concatenate([p["kb"], p["q2"]], axis=0).astype(BF16), _block_diag(p["k2"]))
               for p in pre]
        yield
        lmats = [jnp.where(strict, a2[:DN_CHUNK] * p["decay"], 0.0) for a2, p in zip(a2s, pre)]
        for n, (a2, p) in enumerate(zip(a2s, pre)):
            attn_buf[n] = (a2[DN_CHUNK:] * p["decay"]).astype(BF16)
        tinvs = yield from _unit_tri_inverses(lmats, ri, ci)
        gams = [jnp.exp(p["g_i"]) for p in pre]
        us = [_mm_pair(t, p["v2"] * p["b_i"]) for t, p in zip(tinvs, pre)]
        yield
        ws = [_mm_pair(t, p["kb"] * gam) for t, p, gam in zip(tinvs, pre, gams)]
        yield
        for n, (p, u, w, gam) in enumerate(zip(pre, us, ws, gams)):
            u_buf[n] = u
            wq_buf[n] = jnp.concatenate([w, p["q2"] * gam], axis=0).astype(BF16)
            kt_buf[n] = (p["k2"] * jnp.exp(p["g_tot"] - p["g_i"])).astype(BF16)
            gt_buf[n] = jnp.broadcast_to(jnp.exp(p["g_tot"]), (SUBLANES, c2))

    def scan(buf):
        u_buf, wq_buf, attn_buf, kt_buf, gt_buf = buf
        si = lax.broadcasted_iota(jnp.int32, (c2, c2), 0)
        sj = lax.broadcasted_iota(jnp.int32, (c2, c2), 1)
        on_diag = (si < DN_CHUNK) == (sj < DN_CHUNK)
        chains = [(sq, p) for sq in range(n_seq) for p in range(n_pairs)]
        states = [s_sc[sq * n_pairs + p] for sq, p in chains]
        for k in (range(cps - 1, -1, -1) if reverse else range(cps)):
            cs = [sq * cps + k for sq, _ in chains]
            ns = [c * n_pairs + p for c, (_, p) in zip(cs, chains)]
            m1s = [_dot(wq_buf[n], s.astype(BF16)) for n, s in zip(ns, states)]
            yield
            v_news = [u_buf[n] - m1[:DN_CHUNK] for n, m1 in zip(ns, m1s)]
            outs = [m1[DN_CHUNK:] + _mm_pair(attn_buf[n], v) for n, m1, v in zip(ns, m1s, v_news)]
            yield
            states = [s * gt_buf[n][0:1, :] + jnp.where(on_diag, _dot_tn(kt_buf[n], v.astype(BF16)), 0.0)
                      for n, s, v in zip(ns, states, v_news)]
            for c, (_, p), out in zip(cs, chains, outs):
                o_ref[c * DN_CHUNK:(c + 1) * DN_CHUNK, p * c2:(p + 1) * c2] = out
            yield
        for (sq, p), s in zip(chains, states):
            s_sc[sq * n_pairs + p] = s

    def step(par):
        @pl.when(i_scan == 0)
        def _():
            s_sc[...] = jnp.zeros_like(s_sc)
            if has_s0:
                for h in range(DN_HEADS):
                    lo = (h % 2) * DN_CHUNK
                    s_sc[h // 2, lo:lo + DN_CHUNK, lo:lo + DN_CHUNK] = s0_ref[h]

        _interleave(prepare(bufs[par]), scan(bufs[1 - par]))

        if want_state:
            @pl.when(i_scan == nt - 1)
            def _():
                for sq in range(n_seq):
                    for h in range(DN_HEADS):
                        lo = (h % 2) * DN_CHUNK
                        sfin_ref[sq, h] = s_sc[sq * n_pairs + h // 2, lo:lo + DN_CHUNK, lo:lo + DN_CHUNK]

    pl.when(g == 0)(lambda: _interleave(prepare(bufs[0])))
    pl.when((g > 0) & (g % 2 == 0))(functools.partial(step, 0))
    pl.when(g % 2 == 1)(functools.partial(step, 1))


def _deltanet_dir(qn, kn, vn, gb, gbt, s0, b, t, d, want_state):
    n = qn.shape[0]
    chunks_per_tile = 4
    cps = min(chunks_per_tile, t // DN_CHUNK)
    n_seq = chunks_per_tile // cps
    assert b % n_seq == 0 and (s0 is None or n_seq == 1)
    nc = n_seq * cps
    tt = nc * DN_CHUNK
    nt = t // (cps * DN_CHUNK)
    n_tiles = (b // n_seq) * nt
    reverse = d == 1
    c2 = 2 * DN_CHUNK
    n_inst = nc * (DN_HEADS // 2)

    def seq(gt):
        bi, i = gt // nt, gt % nt
        return bi, ((nt - 1 - i) if reverse else i)

    def prep_tile(g):
        return seq(jnp.minimum(g, n_tiles - 1))

    def scan_tile(g):
        return seq(jnp.maximum(g - 1, 0))

    def row_block(bt):
        return bt[0] * nt + bt[1]

    qkv_spec = pl.BlockSpec((tt, DN_QK), lambda g: (row_block(prep_tile(g)), 0))
    in_specs = [
        qkv_spec, qkv_spec, qkv_spec,
        pl.BlockSpec((tt, LANES), lambda g: (row_block(prep_tile(g)), 0)),
        pl.BlockSpec((4 * DN_HEADS, tt), lambda g: (0, row_block(prep_tile(g)))),
    ]
    args = [qn, kn, vn, gb, gbt]
    if s0 is not None:
        state, layer = s0
        in_specs.append(pl.BlockSpec((None, None, None, DN_HEADS, DN_DK, DN_DV),
                                     lambda g: (scan_tile(g)[0], layer, d, 0, 0, 0)))
        args.append(state)
    out_specs = [pl.BlockSpec((tt, DN_VW), lambda g: (row_block(scan_tile(g)), 0))]
    out_shape = [jax.ShapeDtypeStruct((n, DN_VW), F32)]
    if want_state:
        out_specs.append(pl.BlockSpec((n_seq, DN_HEADS, DN_DK, DN_DV), lambda g: (scan_tile(g)[0], 0, 0, 0)))
        out_shape.append(jax.ShapeDtypeStruct((b, DN_HEADS, DN_DK, DN_DV), F32))
    prepared = [pltpu.VMEM((n_inst, DN_CHUNK, c2), F32), pltpu.VMEM((n_inst, c2, c2), BF16),
                pltpu.VMEM((n_inst, DN_CHUNK, c2), BF16), pltpu.VMEM((n_inst, DN_CHUNK, c2), BF16),
                pltpu.VMEM((n_inst, SUBLANES, c2), F32)]
    res = pl.pallas_call(
        functools.partial(_dn_kernel, reverse=reverse, has_s0=s0 is not None, want_state=want_state,
                          n_seq=n_seq, cps=cps, d=d, nt=nt),
        grid=(n_tiles + 1,),
        in_specs=in_specs,
        out_specs=out_specs,
        out_shape=out_shape,
        scratch_shapes=[pltpu.VMEM((n_seq * (DN_HEADS // 2), c2, c2), F32)] + prepared + prepared,
        compiler_params=_cparams(("arbitrary",)),
    )(*args)
    return res if want_state else (res[0], None)


def _post_kernel(x_ref, mod_ref, cb_ref, cc_ref, cx_ref, ccp_ref, cxp_ref, ccn_ref, cxn_ref, cw_ref,
                 att_ref, of_ref, ob_ref, z_ref, ng_ref, ga_ref, gb_ref, gc_ref, wpa_ref, wpb_ref, wpc_ref, wo_ref,
                 gpost_ref, o_ref, *, tm, tps):
    i = pl.program_id(0)
    first = (i % tps) == 0
    last = (i % tps) == tps - 1
    u = cc_ref[...].astype(F32) * cx_ref[...].astype(F32)
    hl = BF16_SUBLANES - 1
    prev_row = jnp.where(first, 0.0, ccp_ref[hl:hl + 1, :].astype(F32) * cxp_ref[hl:hl + 1, :].astype(F32))
    next_row = jnp.where(last, 0.0, ccn_ref[0:1, :].astype(F32) * cxn_ref[0:1, :].astype(F32))
    rows = lax.broadcasted_iota(jnp.int32, (tm, 1), 0)
    u_prev = jnp.where(rows == 0, prev_row, pltpu.roll(u, 1, axis=0))
    u_next = jnp.where(rows == tm - 1, next_row, pltpu.roll(u, tm - 1, axis=0))
    cw = cw_ref[...]
    conv = cw[0:1, :] * u_prev + cw[1:2, :] * u + cw[2:3, :] * u_next
    ya = _dot((cb_ref[...].astype(F32) * conv).astype(BF16), wpa_ref[...])
    yb = _dot(att_ref[...], wpb_ref[...])
    o = of_ref[...] + ob_ref[...]
    z = z_ref[...].astype(F32)
    parts = []
    for h in range(DN_HEADS):
        sl = slice(h * DN_DV, (h + 1) * DN_DV)
        parts.append((_rms(o[:, sl], ng_ref[...]) * _silu(z[:, sl])).astype(BF16))
    yc = _dot(jnp.concatenate(parts, axis=1), wpc_ref[...])
    mix_in = (_sigmoid(ga_ref[...].astype(F32)) * ya + _sigmoid(gb_ref[...].astype(F32)) * yb
              + _sigmoid(gc_ref[...].astype(F32)) * yc)
    mix = _dot(mix_in.astype(BF16), wo_ref[...])
    o_ref[...] = x_ref[...] + mod_ref[0, 2:3, :] * _rms(mix, gpost_ref[...])


def _post_mixer(x, mod3, mod_base, rows_per_mod, t, proj, att, o_f, o_b, conv_w, ng, wpa, wpb, wpc, wo, layer,
                g_post1):
    n = x.shape[0]
    tm = 256
    tps = t // tm
    hb = tm // BF16_SUBLANES
    cwid = CONV_WIDTH
    c0 = REST_CONV // cwid

    def prev(i):
        return jnp.maximum(i * hb - 1, 0)

    def nxt(i):
        return jnp.minimum((i + 1) * hb, n // BF16_SUBLANES - 1)

    g0 = REST_GATE // D_MODEL
    in_specs = [
        pl.BlockSpec((tm, D_MODEL), lambda i: (i, 0)),
        pl.BlockSpec((1, 6, D_MODEL), lambda i: (mod_base + (i * tm) // rows_per_mod, 0, 0)),
        pl.BlockSpec((tm, cwid), lambda i: (i, c0)),
        pl.BlockSpec((tm, cwid), lambda i: (i, c0 + 1)),
        pl.BlockSpec((tm, cwid), lambda i: (i, c0 + 2)),
        pl.BlockSpec((BF16_SUBLANES, cwid), lambda i: (prev(i), c0 + 1)),
        pl.BlockSpec((BF16_SUBLANES, cwid), lambda i: (prev(i), c0 + 2)),
        pl.BlockSpec((BF16_SUBLANES, cwid), lambda i: (nxt(i), c0 + 1)),
        pl.BlockSpec((BF16_SUBLANES, cwid), lambda i: (nxt(i), c0 + 2)),
        _const_spec((3, cwid)),
        pl.BlockSpec((tm, ATTN_Q), lambda i: (i, 0)),
        pl.BlockSpec((tm, DN_VW), lambda i: (i, 0)),
        pl.BlockSpec((tm, DN_VW), lambda i: (i, 0)),
        pl.BlockSpec((tm, DN_VW), lambda i: (i, REST_Z // DN_VW)),
        _const_spec((1, DN_DV)),
        pl.BlockSpec((tm, D_MODEL), lambda i: (i, g0)),
        pl.BlockSpec((tm, D_MODEL), lambda i: (i, g0 + 1)),
        pl.BlockSpec((tm, D_MODEL), lambda i: (i, g0 + 2)),
        _layer_spec((CONV_WIDTH, D_MODEL), layer), _layer_spec((ATTN_Q, D_MODEL), layer),
        _layer_spec((DN_VW, D_MODEL), layer), _layer_spec((D_MODEL, D_MODEL), layer),
        _const_spec((1, D_MODEL)),
    ]
    return pl.pallas_call(
        functools.partial(_post_kernel, tm=tm, tps=tps),
        grid=(n // tm,),
        in_specs=in_specs,
        out_specs=pl.BlockSpec((tm, D_MODEL), lambda i: (i, 0)),
        out_shape=jax.ShapeDtypeStruct((n, D_MODEL), F32),
        compiler_params=_cparams(("arbitrary",)),
    )(x, mod3, proj, proj, proj, proj, proj, proj, proj, conv_w, att, o_f, o_b, proj, ng,
      proj, proj, proj, wpa, wpb, wpc, wo, g_post1)


def _ffn_kernel(x_ref, mod_ref, gpre_ref, wg_ref, wu_ref, wd_ref, gpost_ref, o_ref, act_sc, *, tf):
    x = x_ref[...]
    h2 = (_rms(x, gpre_ref[...]) * (1.0 + mod_ref[0, 4:5, :]) + mod_ref[0, 3:4, :]).astype(BF16)
    for j in range(0, D_FF, tf):
        gate = _dot(h2, wg_ref[:, j:j + tf])
        up = _dot(h2, wu_ref[:, j:j + tf])
        act_sc[:, j:j + tf] = (_silu(gate) * up).astype(BF16)
    ffn = _dot(act_sc[...], wd_ref[...])
    o_ref[...] = x + mod_ref[0, 5:6, :] * _rms(ffn, gpost_ref[...])


def _ffn(x, mod3, mod_base, rows_per_mod, g_pre2, wg, wu, wd, layer, g_post2):
    n = x.shape[0]
    tm = 512
    return pl.pallas_call(
        functools.partial(_ffn_kernel, tf=256),
        grid=(n // tm,),
        in_specs=[pl.BlockSpec((tm, D_MODEL), lambda i: (i, 0)),
                  pl.BlockSpec((1, 6, D_MODEL), lambda i: (mod_base + (i * tm) // rows_per_mod, 0, 0)),
                  _const_spec((1, D_MODEL)),
                  _layer_spec((D_MODEL, D_FF), layer), _layer_spec((D_MODEL, D_FF), layer),
                  _layer_spec((D_FF, D_MODEL), layer),
                  _const_spec((1, D_MODEL))],
        out_specs=pl.BlockSpec((tm, D_MODEL), lambda i: (i, 0)),
        out_shape=jax.ShapeDtypeStruct((n, D_MODEL), F32),
        scratch_shapes=[pltpu.VMEM((tm, D_FF), BF16)],
        compiler_params=_cparams(("arbitrary",)),
    )(x, mod3, g_pre2, wg, wu, wd, g_post2)


def _rope_tables(t):
    rows = t // GRID_W
    row_id = np.repeat(np.arange(rows, dtype=np.float32), GRID_W)
    col_id = np.tile(np.arange(GRID_W, dtype=np.float32), rows)
    n_freq = HEAD_DIM // 4
    inv_freq = (np.float32(ROPE_THETA) ** (-np.arange(n_freq, dtype=np.float32) / np.float32(n_freq))).astype(np.float32)
    ang = np.concatenate([row_id[:, None] * inv_freq, col_id[:, None] * inv_freq], axis=-1).astype(np.float32)
    cos = np.repeat(np.cos(ang), 2, axis=-1).astype(np.float32)
    sin = np.repeat(np.sin(ang), 2, axis=-1).astype(np.float32)
    sign = np.tile(np.array([-1.0, 1.0], np.float32), HEAD_DIM // 2)
    return jnp.asarray(cos), jnp.asarray(sin * sign)


def _layer(x, b, t, mod3, mod_base, rows_per_mod, lw, rope_tabs, cache, state0, want_state):
    q_p, k_p, vt_p, v_f32, qn, kn, vn, gb, gbt, proj = _inproj(x, t, mod3, mod_base, rows_per_mod, lw,
                                                               rope_tabs, want_state)
    att = _attention(q_p, k_p, vt_p, b, t, cache)
    dn = [_deltanet_dir(qn, kn, vn, gb, gbt, state0, b, t, d, want_state) for d in range(2)]
    x = _post_mixer(x, mod3, mod_base, rows_per_mod, t, proj, att, dn[0][0], dn[1][0], lw["conv_w"],
                    lw["dn_norm_g"], lw["w_pa"], lw["w_pb"], lw["w_pc"], lw["w_o"], lw["layer"], lw["g_post1"])
    x = _ffn(x, mod3, mod_base, rows_per_mod, lw["g_pre2"], lw["w_gate"], lw["w_up"], lw["w_down"],
             lw["layer"], lw["g_post2"])
    if not want_state:
        return x, None
    s_fin = jnp.stack([dn[0][1], dn[1][1]], axis=1)
    return x, (k_p, v_f32, s_fin)


def kernel(x_prompt, x_sample, cache_k, cache_v, state_dn, c, c_ctx, w_mod, b_mod, g_pre1, g_post1, g_pre2, g_post2, w_in, conv_w, g_qn, g_kn, dn_conv_w, dn_a_log, dn_dt_bias, dn_norm_g, w_pa, w_pb, w_pc, w_o, w_gate, w_up, w_down):
    bp, tp, d = x_prompt.shape
    bs, ts, _ = x_sample.shape
    depth = w_mod.shape[0]
    past = cache_k.shape[2]

    mod_rows = -(-(bs + 1) // SUBLANES) * SUBLANES
    cv = jnp.zeros((mod_rows, d), F32).at[:bs].set(c).at[bs].set(c_ctx)
    mod_all = _modulation(cv, w_mod, b_mod).reshape(depth, mod_rows, 6, d)

    stacked = {"w_pa": w_pa.astype(BF16), "w_pb": w_pb.astype(BF16), "w_pc": w_pc.astype(BF16),
               "w_o": w_o.astype(BF16), "w_gate": w_gate.astype(BF16), "w_up": w_up.astype(BF16),
               "w_down": w_down.astype(BF16)}
    w_main = w_in.astype(BF16)
    w_tail = jnp.concatenate(
        [w_in[:, :, COL_GATE:D_IN], w_in[:, :, COL_SMALL:COL_GATE],
         jnp.zeros((depth, d, LANES - 4 * DN_HEADS), w_in.dtype)], axis=-1).astype(BF16)
    lane_pad = ((0, 0), (2 * DN_HEADS, LANES - 4 * DN_HEADS))
    a_flat = dn_a_log.reshape(depth, 2 * DN_HEADS)
    b_flat = dn_dt_bias.reshape(depth, 2 * DN_HEADS)
    rope_tabs = _rope_tables(ts)
    cache_k4 = cache_k.reshape(bs, depth, past, ATTN_KV)
    cache_v4 = cache_v.reshape(bs, depth, past, ATTN_KV)

    xp = x_prompt.reshape(bp * tp, d)
    xs = x_sample.reshape(bs * ts, d)
    ks, vs, ss = [], [], []
    for l in range(depth):
        lw = {
            "g_pre1": g_pre1[l][None], "g_post1": g_post1[l][None], "g_pre2": g_pre2[l][None],
            "g_post2": g_post2[l][None], "layer": l, "w_in": w_main, "w_tail": w_tail, "conv_w": conv_w[l],
            "g_qn": g_qn[l][None], "g_kn": g_kn[l][None], "dn_conv_w": dn_conv_w[l],
            "dn_ac": jnp.pad(a_flat[l][None], lane_pad), "dn_bc": jnp.pad(b_flat[l][None], lane_pad),
            "dn_norm_g": dn_norm_g[l][None],
            **stacked,
        }
        mod3 = mod_all[l]
        xp, (k_l, v_l, s_l) = _layer(xp, bp, tp, mod3, bs, bp * tp, lw, None, None, None, True)
        ks.append(k_l.reshape(bp, tp, N_KV_HEADS, HEAD_DIM))
        vs.append(v_l.reshape(bp, tp, N_KV_HEADS, HEAD_DIM))
        ss.append(s_l)
        xs, _ = _layer(xs, bs, ts, mod3, 0, ts, lw, rope_tabs, (cache_k4, cache_v4, l), (state_dn, l), False)
    return (xp.reshape(bp, tp, d), xs.reshape(bs, ts, d), jnp.stack(ks, axis=1), jnp.stack(vs, axis=1),
            jnp.stack(ss, axis=1))
```

```python
import functools

import jax
import jax.numpy as jnp
import numpy as np
from jax import lax
from jax.experimental import pallas as pl
from jax.experimental.pallas import tpu as pltpu

F32 = jnp.float32
BF16 = jnp.bfloat16

D_MODEL = 1024
EPS = 1e-6
GRID_W = 64
N_HEADS = 8
N_KV_HEADS = 2
Q_PER_KV = N_HEADS // N_KV_HEADS
HEAD_DIM = 128
ATTN_Q = N_HEADS * HEAD_DIM
ATTN_KV = N_KV_HEADS * HEAD_DIM
ROPE_THETA = 10000.0
CONV_WIDTH = 512
DN_HEADS = 4
DN_DK = 128
DN_DV = 128
DN_QK = DN_HEADS * DN_DK
DN_VW = DN_HEADS * DN_DV
D_FF = 2816

SUBLANES = 8
BF16_SUBLANES = 16
LANES = 128

COL_CONV = 0
COL_Q = COL_CONV + 3 * CONV_WIDTH
COL_K = COL_Q + ATTN_Q
COL_V = COL_K + ATTN_KV
COL_DN = COL_V + ATTN_KV
COL_Z = COL_DN + 3 * DN_QK
COL_SMALL = COL_Z + DN_VW
COL_GATE = COL_SMALL + 4 * DN_HEADS
D_IN = COL_GATE + 3 * D_MODEL
TAIL_GATE = 0
TAIL_SMALL = TAIL_GATE + 3 * D_MODEL
D_TAIL = TAIL_SMALL + LANES
REST_GATE = 0
REST_CONV = REST_GATE + 3 * D_MODEL
REST_Z = REST_CONV + 3 * CONV_WIDTH
D_REST = REST_Z + DN_VW

Q_SCALE = 1.4426950408889634 * HEAD_DIM ** -0.5
DN_CHUNK = 128
DN_BASE = 16
VMEM_LIMIT = 56 * 1024 * 1024


def _cparams(sem):
    return pltpu.CompilerParams(dimension_semantics=sem, vmem_limit_bytes=VMEM_LIMIT)


def _const_spec(shape):
    nd = len(shape)
    return pl.BlockSpec(shape, lambda *_: (0,) * nd, pipeline_mode=pl.Buffered(1))


def _layer_spec(shape, layer):
    return pl.BlockSpec((None,) + tuple(shape), lambda *_: (layer,) + (0,) * len(shape),
                        pipeline_mode=pl.Buffered(1))


def _dot(a, b):
    return jnp.dot(a, b, preferred_element_type=F32)


def _dot_nt(a, b):
    return lax.dot_general(a, b, (((1,), (1,)), ((), ())), preferred_element_type=F32)


def _dot_tn(a, b):
    return lax.dot_general(a, b, (((0,), (0,)), ((), ())), preferred_element_type=F32)


def _rms(x, g):
    return x * lax.rsqrt(jnp.mean(x * x, axis=-1, keepdims=True) + EPS) * g


def _sigmoid(x):
    return 1.0 / (1.0 + jnp.exp(-x))


def _silu(x):
    return x * _sigmoid(x)


def _softplus(x):
    return jnp.maximum(x, 0.0) + jnp.log1p(jnp.exp(-jnp.abs(x)))


def _split3(x):
    hi = x.astype(BF16)
    r = x - hi.astype(F32)
    mid = r.astype(BF16)
    lo = (r - mid.astype(F32)).astype(BF16)
    return hi, mid, lo


def _mod_kernel(cv_ref, w_ref, b_ref, o_ref):
    cv = cv_ref[...]
    o_ref[0] = _dot(_silu(cv).astype(BF16), w_ref[0].astype(BF16)) + b_ref[0]


def _modulation(cv, w_mod, b_mod):
    depth, d, n6 = w_mod.shape
    rows = cv.shape[0]
    tn = 1536
    return pl.pallas_call(
        _mod_kernel,
        grid=(depth, n6 // tn),
        in_specs=[pl.BlockSpec((rows, d), lambda l, j: (0, 0)),
                  pl.BlockSpec((1, d, tn), lambda l, j: (l, 0, j)),
                  pl.BlockSpec((1, 1, tn), lambda l, j: (l, 0, j))],
        out_specs=pl.BlockSpec((1, rows, tn), lambda l, j: (l, 0, j)),
        out_shape=jax.ShapeDtypeStruct((depth, rows, n6), F32),
        compiler_params=_cparams(("arbitrary", "arbitrary")),
    )(cv, w_mod, b_mod.reshape(depth, 1, n6))


def _swap_pairs(y):
    lane = lax.broadcasted_iota(jnp.int32, y.shape, 1)
    return jnp.where(lane % 2 == 0, pltpu.roll(y, LANES - 1, axis=1), pltpu.roll(y, 1, axis=1))


def _interleave(*gens):
    gens = list(gens)
    while gens:
        for gen in list(gens):
            try:
                next(gen)
            except StopIteration:
                gens.remove(gen)


def _inproj_kernel(*refs, rope, want_v, tm, tps, tn):
    refs = list(refs)
    x_ref, xp_ref, xn_ref, mod_ref, g_ref, w_ref, wt_ref, gq_ref, gk_ref, cw_ref, ac_ref, bc_ref = refs[:12]
    refs = refs[12:]
    cos_ref, sin_ref = (refs.pop(0), refs.pop(0)) if rope else (None, None)
    q_out, k_out, v_out = refs[:3]
    refs = refs[3:]
    vf_out = refs.pop(0) if want_v else None
    qn_out, kn_out, vn_out, gb_out, gbt_out, rest_out = refs
    i = pl.program_id(0)

    def modnorm(x):
        return (_rms(x, g_ref[...]) * (1.0 + mod_ref[0, 1:2, :]) + mod_ref[0, 0:1, :]).astype(BF16)

    h = modnorm(x_ref[...])
    h_halo = modnorm(jnp.concatenate([xp_ref[...], xn_ref[...]], axis=0))

    def qk_head(x, g):
        y = _rms(x, g)
        if rope:
            y = y * cos_ref[...] + _swap_pairs(y) * sin_ref[...]
        return y

    def attn_epilogue():
        qkv = _dot(h, w_ref[:, COL_Q:COL_DN])
        yield
        for hh in range(N_HEADS):
            sl = slice(hh * HEAD_DIM, (hh + 1) * HEAD_DIM)
            q_out[:, sl] = (qk_head(qkv[:, sl], gq_ref[...]) * Q_SCALE).astype(q_out.dtype)
            if hh % 2 == 1:
                yield
        for hh in range(N_KV_HEADS):
            sl = slice(hh * HEAD_DIM, (hh + 1) * HEAD_DIM)
            k_out[:, sl] = qk_head(qkv[:, ATTN_Q + hh * HEAD_DIM:ATTN_Q + (hh + 1) * HEAD_DIM],
                                   gk_ref[...]).astype(k_out.dtype)
        v = qkv[:, ATTN_Q + ATTN_KV:]
        v_out[...] = v.T.astype(v_out.dtype)
        if want_v:
            vf_out[...] = v

    def dn_epilogue():
        dn = _dot(h, w_ref[:, COL_DN:COL_Z])
        dn_halo = _dot(h_halo, w_ref[:, COL_DN:COL_Z])
        gl = _dot(h, wt_ref[:, TAIL_SMALL:D_TAIL])
        yield
        rows = lax.broadcasted_iota(jnp.int32, (tm, 1), 0)
        prev_row = jnp.where((i % tps) == 0, 0.0, dn_halo[SUBLANES - 1:SUBLANES, :])
        next_row = jnp.where((i % tps) == tps - 1, 0.0, dn_halo[SUBLANES:SUBLANES + 1, :])
        dn_prev = jnp.where(rows == 0, prev_row, pltpu.roll(dn, 1, axis=0))
        dn_next = jnp.where(rows == tm - 1, next_row, pltpu.roll(dn, tm - 1, axis=0))
        cw = cw_ref[...]

        def conv_silu(sl):
            return _silu(cw[0:1, sl] * dn_prev[:, sl] + cw[1:2, sl] * dn[:, sl] + cw[2:3, sl] * dn_next[:, sl])

        for hh in range(DN_HEADS):
            sl = slice(hh * DN_DK, (hh + 1) * DN_DK)
            qh = conv_silu(sl)
            kh = conv_silu(slice(DN_QK + hh * DN_DK, DN_QK + (hh + 1) * DN_DK))
            qn_out[:, sl] = (qh * lax.rsqrt(jnp.sum(qh * qh, axis=-1, keepdims=True) + EPS)
                             * (DN_DK ** -0.5)).astype(qn_out.dtype)
            kn_out[:, sl] = (kh * lax.rsqrt(jnp.sum(kh * kh, axis=-1, keepdims=True) + EPS)).astype(kn_out.dtype)
            vn_out[:, sl] = conv_silu(slice(2 * DN_QK + hh * DN_DV, 2 * DN_QK + (hh + 1) * DN_DV)
                                      ).astype(vn_out.dtype)
            yield
        lane = lax.broadcasted_iota(jnp.int32, gl.shape, 1)
        gb = jnp.where(lane < 2 * DN_HEADS, _sigmoid(gl), -jnp.exp(ac_ref[...]) * _softplus(gl + bc_ref[...]))
        gb_out[...] = gb
        gbt_out[...] = gb.T

    def remaining():
        groups = [(wt_ref, TAIL_GATE, REST_GATE, 3 * D_MODEL), (w_ref, COL_CONV, REST_CONV, 3 * CONV_WIDTH),
                  (w_ref, COL_Z, REST_Z, DN_VW)]
        for ref, src, dst, width in groups:
            for j in range(0, width, tn):
                rest_out[:, dst + j:dst + j + tn] = _dot(h, ref[:, src + j:src + j + tn]).astype(rest_out.dtype)
                yield

    _interleave(dn_epilogue(), attn_epilogue(), remaining())


def _inproj(x, t, mod3, mod_base, rows_per_mod, lw, rope_tabs, want_v):
    n = x.shape[0]
    tm = min(512, t)
    tps = t // tm
    halo = tm // SUBLANES
    w_main, w_tail, layer = lw["w_in"], lw["w_tail"], lw["layer"]
    in_specs = [pl.BlockSpec((tm, D_MODEL), lambda i: (i, 0)),
                pl.BlockSpec((SUBLANES, D_MODEL), lambda i: (jnp.maximum(i * halo - 1, 0), 0)),
                pl.BlockSpec((SUBLANES, D_MODEL), lambda i: (jnp.minimum((i + 1) * halo, n // SUBLANES - 1), 0)),
                pl.BlockSpec((1, 6, D_MODEL), lambda i: (mod_base + (i * tm) // rows_per_mod, 0, 0)),
                _const_spec((1, D_MODEL)),
                _layer_spec((D_MODEL, COL_SMALL), layer), _layer_spec((D_MODEL, D_TAIL), layer),
                _const_spec((1, HEAD_DIM)), _const_spec((1, HEAD_DIM)),
                _const_spec((3, 3 * DN_QK)), _const_spec((1, LANES)), _const_spec((1, LANES))]
    args = [x, x, x, mod3, lw["g_pre1"], w_main, w_tail, lw["g_qn"], lw["g_kn"], lw["dn_conv_w"],
            lw["dn_ac"], lw["dn_bc"]]
    if rope_tabs is not None:
        in_specs += [pl.BlockSpec((tm, HEAD_DIM), lambda i: (i % tps, 0))] * 2
        args += list(rope_tabs)

    outs = [(ATTN_Q, BF16, False), (ATTN_KV, F32 if want_v else BF16, False), (ATTN_KV, BF16, True)]
    if want_v:
        outs.append((ATTN_KV, F32, False))
    outs += [(DN_QK, BF16, False), (DN_QK, BF16, False), (DN_VW, BF16, False), (LANES, F32, False),
             (LANES, F32, True), (D_REST, BF16, False)]
    res = pl.pallas_call(
        functools.partial(_inproj_kernel, rope=rope_tabs is not None, want_v=want_v, tm=tm, tps=tps, tn=512),
        grid=(n // tm,),
        in_specs=in_specs,
        out_specs=[pl.BlockSpec((w, tm), lambda i: (0, i)) if tr else pl.BlockSpec((tm, w), lambda i: (i, 0))
                   for w, _, tr in outs],
        out_shape=[jax.ShapeDtypeStruct((w, n) if tr else (n, w), dt) for w, dt, tr in outs],
        compiler_params=_cparams(("arbitrary",)),
    )(*args)
    res = list(res)
    q_p, k_p, vt_p = res[:3]
    v_f32 = res[3] if want_v else None
    qn, kn, vn, gb, gbt, rest = res[-6:]
    return q_p, k_p, vt_p, v_f32, qn, kn, vn, gb, gbt, rest


def _attn_kernel(*refs, has_cache, t, tk, tq, nq):
    refs = list(refs)
    q_ref = refs.pop(0)
    qn_ref = refs.pop(0) if nq > 1 else None
    kc_ref, vc_ref = (refs.pop(0), refs.pop(0)) if has_cache else (None, None)
    k_ref, v_ref, o_ref = refs[:3]
    refs = refs[3:]
    vct_sc = refs.pop() if has_cache else None
    s_bufs, m_bufs = refs[:len(refs) // 2], refs[len(refs) // 2:]
    qi = pl.program_id(2)

    if has_cache:
        @pl.when(qi == 0)
        def _():
            vct_sc[...] = vc_ref[...].T.astype(BF16)

    segs = []
    if has_cache:
        past = kc_ref.shape[0]
        segs += [(kc_ref, vct_sc, r, min(tk, past - r)) for r in range(0, past, tk)]
    segs += [(k_ref, v_ref, r, tk) for r in range(0, t, tk)]
    offs = [sum(w for _, _, _, w in segs[:i]) for i in range(len(segs))]

    def stack(ref):
        q = ref[...]
        return jnp.concatenate([q[:, g * HEAD_DIM:(g + 1) * HEAD_DIM] for g in range(Q_PER_KV)], axis=0)

    def scores(qs, s_ref, i, m_run):
        kr, _, r, w = segs[i]
        s = _dot_nt(kr[r:r + w, :].astype(BF16), qs)
        s_ref[offs[i]:offs[i] + w, :] = s
        for r0 in range(0, w, SUBLANES):
            blk = s[r0:r0 + SUBLANES, :]
            m_run = blk if m_run is None else jnp.maximum(m_run, blk)
        return m_run

    def weighted(s_ref, i, m, acc):
        _, vr, r, w = segs[i]
        p = jnp.exp2(s_ref[offs[i]:offs[i] + w, :] - m)
        den = p[0:SUBLANES, :]
        for r0 in range(SUBLANES, w, SUBLANES):
            den = den + p[r0:r0 + SUBLANES, :]
        pv = _dot(vr[:, r:r + w].astype(BF16), p.astype(BF16))
        return (pv, den) if acc is None else (acc[0] + pv, acc[1] + den)

    def finish(acc):
        o = (acc[0] / jnp.sum(acc[1], axis=0, keepdims=True)).T
        for g in range(Q_PER_KV):
            o_ref[:, g * HEAD_DIM:(g + 1) * HEAD_DIM] = o[g * tq:(g + 1) * tq].astype(o_ref.dtype)

    if nq > 1:
        @pl.when(qi == 0)
        def _():
            qs0 = stack(q_ref)
            m_run = None
            for i in range(len(segs)):
                m_run = scores(qs0, s_bufs[0], i, m_run)
            m_bufs[0][...] = m_run

        def step(s_cur, m_cur, s_nxt, m_nxt):
            qs_next = stack(qn_ref)
            m = jnp.max(m_cur[...], axis=0, keepdims=True)
            acc, m_run = None, None
            for i in range(len(segs)):
                m_run = scores(qs_next, s_nxt, i, m_run)
                acc = weighted(s_cur, i, m, acc)
            m_nxt[...] = m_run
            finish(acc)

        pl.when(qi % 2 == 0)(functools.partial(step, s_bufs[0], m_bufs[0], s_bufs[1], m_bufs[1]))
        pl.when(qi % 2 == 1)(functools.partial(step, s_bufs[1], m_bufs[1], s_bufs[0], m_bufs[0]))
    else:
        qs = stack(q_ref)
        m_run = None
        for i in range(len(segs)):
            m_run = scores(qs, s_bufs[0], i, m_run)
        m = jnp.max(m_run, axis=0, keepdims=True)
        acc = None
        for i in range(len(segs)):
            acc = weighted(s_bufs[0], i, m, acc)
        finish(acc)


def _attention(q_p, k_p, vt_p, b, t, cache):
    n = q_p.shape[0]
    tq = 128 if t > 256 else 256
    tk = min(t, 512)
    nq = t // tq
    qw = Q_PER_KV * HEAD_DIM
    in_specs = [pl.BlockSpec((tq, qw), lambda bi, j, qi: (bi * nq + qi, j))]
    args = [q_p]
    if nq > 1:
        in_specs.append(pl.BlockSpec((tq, qw), lambda bi, j, qi: (bi * nq + jnp.minimum(qi + 1, nq - 1), j)))
        args.append(q_p)
    if cache is not None:
        cache_k, cache_v, layer = cache
        past = cache_k.shape[2]
        cspec = pl.BlockSpec((None, None, past, HEAD_DIM), lambda bi, j, qi: (bi, layer, 0, j))
        in_specs += [cspec, cspec]
        args += [cache_k, cache_v]
    in_specs += [pl.BlockSpec((t, HEAD_DIM), lambda bi, j, qi: (bi, j)),
                 pl.BlockSpec((HEAD_DIM, t), lambda bi, j, qi: (j, bi))]
    args += [k_p, vt_p]
    n_keys = t + (cache[0].shape[2] if cache is not None else 0)
    slots = 2 if nq > 1 else 1
    return pl.pallas_call(
        functools.partial(_attn_kernel, has_cache=cache is not None, t=t, tk=tk, tq=tq, nq=nq),
        grid=(b, N_KV_HEADS, nq),
        in_specs=in_specs,
        out_specs=pl.BlockSpec((tq, qw), lambda bi, j, qi: (bi * nq + qi, j)),
        out_shape=jax.ShapeDtypeStruct((n, ATTN_Q), BF16),
        scratch_shapes=([pltpu.VMEM((n_keys, Q_PER_KV * tq), F32)] * slots
                        + [pltpu.VMEM((SUBLANES, Q_PER_KV * tq), F32)] * slots
                        + ([pltpu.VMEM((HEAD_DIM, cache[0].shape[2]), BF16)] if cache is not None else [])),
        compiler_params=_cparams(("arbitrary", "arbitrary", "arbitrary")),
    )(*args)


def _lane_pick(x, lane):
    idx = lax.broadcasted_iota(jnp.int32, x.shape, 1)
    return jnp.sum(jnp.where(idx == lane, x, 0.0), axis=-1, keepdims=True)


def _block_diag(x2):
    xb = x2.astype(BF16)
    z = jnp.zeros((DN_CHUNK, DN_CHUNK), BF16)
    return jnp.concatenate([jnp.concatenate([xb[:, :DN_CHUNK], z], axis=1),
                            jnp.concatenate([z, xb[:, DN_CHUNK:]], axis=1)], axis=0)


def _mm_pair(x2, y2):
    return _dot(x2.astype(BF16), _block_diag(y2))


def _unit_tri_inverses(lmats, ri, ci):
    def blk(s):
        return (ri ^ ci) < s

    eye = jnp.where(ri == ci, 1.0, 0.0)
    ps = [jnp.where(blk(DN_BASE), -lm, 0.0) for lm in lmats]
    xs = [eye + p for p in ps]
    s = 2
    while s < DN_BASE:
        ps = [_mm_pair(p, p) for p in ps]
        yield
        xs = [x + _mm_pair(x, p) for x, p in zip(xs, ps)]
        yield
        s *= 2
    s = DN_BASE
    while s < DN_CHUNK:
        sel = blk(2 * s) & jnp.logical_not(blk(s))
        ts = [_mm_pair(jnp.where(sel, lm, 0.0), x) for lm, x in zip(lmats, xs)]
        yield
        xs = [x - _mm_pair(x, t) for x, t in zip(xs, ts)]
        yield
        s *= 2
    return xs


def _dn_kernel(*refs, reverse, has_s0, want_state, n_seq, cps, d, nt):
    nc = n_seq * cps
    refs = list(refs)
    q_ref, k_ref, v_ref, gb_ref, gr_ref = refs[:5]
    refs = refs[5:]
    s0_ref = refs.pop(0) if has_s0 else None
    o_ref = refs.pop(0)
    sfin_ref = refs.pop(0) if want_state else None
    s_sc = refs.pop(0)
    bufs = (refs[:5], refs[5:10])
    n_pairs = DN_HEADS // 2
    c2 = 2 * DN_CHUNK
    tt = nc * DN_CHUNK
    insts = [(c, p) for c in range(nc) for p in range(n_pairs)]

    g = pl.program_id(0)
    i_scan = (g - 1) % nt

    def pair_cols(x, lane_a, lane_b):
        shape = (x.shape[0], DN_CHUNK)
        return jnp.concatenate([jnp.broadcast_to(_lane_pick(x, lane_a), shape),
                                jnp.broadcast_to(_lane_pick(x, lane_b), shape)], axis=1)

    def prepare(buf):
        u_buf, wq_buf, attn_buf, kt_buf, gt_buf = buf
        gb = gb_ref[...]
        g_r = gr_ref[...]
        bi = lax.broadcasted_iota(jnp.int32, (tt, tt), 0)
        bj = lax.broadcasted_iota(jnp.int32, (tt, tt), 1)
        same = (bi ^ bj) < DN_CHUNK
        if reverse:
            tri_c = jnp.where(same & (bj >= bi), 1.0, 0.0).astype(BF16)
            tri_r = jnp.where(same & (bi >= bj), 1.0, 0.0).astype(BF16)
        else:
            tri_c = jnp.where(same & (bj <= bi), 1.0, 0.0).astype(BF16)
            tri_r = jnp.where(same & (bi <= bj), 1.0, 0.0).astype(BF16)
        gcum_c = sum(_dot(tri_c, part) for part in _split3(gb))
        gcum_r = sum(_dot(part, tri_r) for part in _split3(g_r))
        yield

        ri = lax.broadcasted_iota(jnp.int32, (DN_CHUNK, c2), 0)
        ci = lax.broadcasted_iota(jnp.int32, (DN_CHUNK, c2), 1) & (DN_CHUNK - 1)
        incl = (ci >= ri) if reverse else (ci <= ri)
        strict = (ci > ri) if reverse else (ci < ri)
        pre = []
        for c, p in insts:
            rs = slice(c * DN_CHUNK, (c + 1) * DN_CHUNK)
            cs = slice(p * c2, (p + 1) * c2)
            lane_b = d * DN_HEADS + 2 * p
            lane_g = 2 * DN_HEADS + d * DN_HEADS + 2 * p
            q2 = q_ref[rs, cs].astype(F32)
            k2 = k_ref[rs, cs].astype(F32)
            v2 = v_ref[rs, cs].astype(F32)
            g_i = pair_cols(gcum_c[rs, :], lane_g, lane_g + 1)
            b_i = pair_cols(gb[rs, :], lane_b, lane_b + 1)
            g_j = jnp.concatenate([gcum_r[lane_g:lane_g + 1, rs], gcum_r[lane_g + 1:lane_g + 2, rs]], axis=1)
            g_tot = g_i[0:1, :] if reverse else g_i[DN_CHUNK - 1:DN_CHUNK, :]
            decay = jnp.where(incl, jnp.exp(jnp.where(incl, g_i - g_j, 0.0)), 0.0)
            pre.append(dict(q2=q2, k2=k2, v2=v2, g_i=g_i, b_i=b_i, g_tot=g_tot, decay=decay, kb=k2 * b_i))
        a2s = [_dot_nt(jnp.concatenate([p["kb"], p["q2"]], axis=0).astype(BF16), _block_diag(p["k2"]))
               for p in pre]
        yield
        lmats = [jnp.where(strict, a2[:DN_CHUNK] * p["decay"], 0.0) for a2, p in zip(a2s, pre)]
        for n, (a2, p) in enumerate(zip(a2s, pre)):
            attn_buf[n] = (a2[DN_CHUNK:] * p["decay"]).astype(BF16)
        tinvs = yield from _unit_tri_inverses(lmats, ri, ci)
        gams = [jnp.exp(p["g_i"]) for p in pre]
        us = [_mm_pair(t, p["v2"] * p["b_i"]) for t, p in zip(tinvs, pre)]
        yield
        ws = [_mm_pair(t, p["kb"] * gam) for t, p, gam in zip(tinvs, pre, gams)]
        yield
        for n, (p, u, w, gam) in enumerate(zip(pre, us, ws, gams)):
            u_buf[n] = u
            wq_buf[n] = jnp.concatenate([w, p["q2"] * gam], axis=0).astype(BF16)
            kt_buf[n] = (p["k2"] * jnp.exp(p["g_tot"] - p["g_i"])).astype(BF16)
            gt_buf[n] = jnp.broadcast_to(jnp.exp(p["g_tot"]), (SUBLANES, c2))

    def scan(buf):
        u_buf, wq_buf, attn_buf, kt_buf, gt_buf = buf
        si = lax.broadcasted_iota(jnp.int32, (c2, c2), 0)
        sj = lax.broadcasted_iota(jnp.int32, (c2, c2), 1)
        on_diag = (si < DN_CHUNK) == (sj < DN_CHUNK)
        chains = [(sq, p) for sq in range(n_seq) for p in range(n_pairs)]
        states = [s_sc[sq * n_pairs + p] for sq, p in chains]
        for k in (range(cps - 1, -1, -1) if reverse else range(cps)):
            cs = [sq * cps + k for sq, _ in chains]
            ns = [c * n_pairs + p for c, (_, p) in zip(cs, chains)]
            m1s = [_dot(wq_buf[n], s.astype(BF16)) for n, s in zip(ns, states)]
            yield
            v_news = [u_buf[n] - m1[:DN_CHUNK] for n, m1 in zip(ns, m1s)]
            outs = [m1[DN_CHUNK:] + _mm_pair(attn_buf[n], v) for n, m1, v in zip(ns, m1s, v_news)]
            yield
            states = [s * gt_buf[n][0:1, :] + jnp.where(on_diag, _dot_tn(kt_buf[n], v.astype(BF16)), 0.0)
                      for n, s, v in zip(ns, states, v_news)]
            for c, (_, p), out in zip(cs, chains, outs):
                o_ref[c * DN_CHUNK:(c + 1) * DN_CHUNK, p * c2:(p + 1) * c2] = out
            yield
        for (sq, p), s in zip(chains, states):
            s_sc[sq * n_pairs + p] = s

    def step(par):
        @pl.when(i_scan == 0)
        def _():
            s_sc[...] = jnp.zeros_like(s_sc)
            if has_s0:
                for h in range(DN_HEADS):
                    lo = (h % 2) * DN_CHUNK
                    s_sc[h // 2, lo:lo + DN_CHUNK, lo:lo + DN_CHUNK] = s0_ref[h]

        _interleave(prepare(bufs[par]), scan(bufs[1 - par]))

        if want_state:
            @pl.when(i_scan == nt - 1)
            def _():
                for sq in range(n_seq):
                    for h in range(DN_HEADS):
                        lo = (h % 2) * DN_CHUNK
                        sfin_ref[sq, h] = s_sc[sq * n_pairs + h // 2, lo:lo + DN_CHUNK, lo:lo + DN_CHUNK]

    pl.when(g == 0)(lambda: _interleave(prepare(bufs[0])))
    pl.when((g > 0) & (g % 2 == 0))(functools.partial(step, 0))
    pl.when(g % 2 == 1)(functools.partial(step, 1))


def _deltanet_dir(qn, kn, vn, gb, gbt, s0, b, t, d, want_state):
    n = qn.shape[0]
    chunks_per_tile = 4
    cps = min(chunks_per_tile, t // DN_CHUNK)
    n_seq = chunks_per_tile // cps
    assert b % n_seq == 0 and (s0 is None or n_seq == 1)
    nc = n_seq * cps
    tt = nc * DN_CHUNK
    nt = t // (cps * DN_CHUNK)
    n_tiles = (b // n_seq) * nt
    reverse = d == 1
    c2 = 2 * DN_CHUNK
    n_inst = nc * (DN_HEADS // 2)

    def seq(gt):
        bi, i = gt // nt, gt % nt
        return bi, ((nt - 1 - i) if reverse else i)

    def prep_tile(g):
        return seq(jnp.minimum(g, n_tiles - 1))

    def scan_tile(g):
        return seq(jnp.maximum(g - 1, 0))

    def row_block(bt):
        return bt[0] * nt + bt[1]

    qkv_spec = pl.BlockSpec((tt, DN_QK), lambda g: (row_block(prep_tile(g)), 0))
    in_specs = [
        qkv_spec, qkv_spec, qkv_spec,
        pl.BlockSpec((tt, LANES), lambda g: (row_block(prep_tile(g)), 0)),
        pl.BlockSpec((4 * DN_HEADS, tt), lambda g: (0, row_block(prep_tile(g)))),
    ]
    args = [qn, kn, vn, gb, gbt]
    if s0 is not None:
        state, layer = s0
        in_specs.append(pl.BlockSpec((None, None, None, DN_HEADS, DN_DK, DN_DV),
                                     lambda g: (scan_tile(g)[0], layer, d, 0, 0, 0)))
        args.append(state)
    out_specs = [pl.BlockSpec((tt, DN_VW), lambda g: (row_block(scan_tile(g)), 0))]
    out_shape = [jax.ShapeDtypeStruct((n, DN_VW), F32)]
    if want_state:
        out_specs.append(pl.BlockSpec((n_seq, DN_HEADS, DN_DK, DN_DV), lambda g: (scan_tile(g)[0], 0, 0, 0)))
        out_shape.append(jax.ShapeDtypeStruct((b, DN_HEADS, DN_DK, DN_DV), F32))
    prepared = [pltpu.VMEM((n_inst, DN_CHUNK, c2), F32), pltpu.VMEM((n_inst, c2, c2), BF16),
                pltpu.VMEM((n_inst, DN_CHUNK, c2), BF16), pltpu.VMEM((n_inst, DN_CHUNK, c2), BF16),
                pltpu.VMEM((n_inst, SUBLANES, c2), F32)]
    res = pl.pallas_call(
        functools.partial(_dn_kernel, reverse=reverse, has_s0=s0 is not None, want_state=want_state,
                          n_seq=n_seq, cps=cps, d=d, nt=nt),
        grid=(n_tiles + 1,),
        in_specs=in_specs,
        out_specs=out_specs,
        out_shape=out_shape,
        scratch_shapes=[pltpu.VMEM((n_seq * (DN_HEADS // 2), c2, c2), F32)] + prepared + prepared,
        compiler_params=_cparams(("arbitrary",)),
    )(*args)
    return res if want_state else (res[0], None)


def _post_kernel(x_ref, mod_ref, cb_ref, cc_ref, cx_ref, ccp_ref, cxp_ref, ccn_ref, cxn_ref, cw_ref,
                 att_ref, of_ref, ob_ref, z_ref, ng_ref, ga_ref, gb_ref, gc_ref, wpa_ref, wpb_ref, wpc_ref, wo_ref,
                 gpost_ref, o_ref, *, tm, tps):
    i = pl.program_id(0)
    first = (i % tps) == 0
    last = (i % tps) == tps - 1
    u = cc_ref[...].astype(F32) * cx_ref[...].astype(F32)
    hl = BF16_SUBLANES - 1
    prev_row = jnp.where(first, 0.0, ccp_ref[hl:hl + 1, :].astype(F32) * cxp_ref[hl:hl + 1, :].astype(F32))
    next_row = jnp.where(last, 0.0, ccn_ref[0:1, :].astype(F32) * cxn_ref[0:1, :].astype(F32))
    rows = lax.broadcasted_iota(jnp.int32, (tm, 1), 0)
    u_prev = jnp.where(rows == 0, prev_row, pltpu.roll(u, 1, axis=0))
    u_next = jnp.where(rows == tm - 1, next_row, pltpu.roll(u, tm - 1, axis=0))
    cw = cw_ref[...]
    conv = cw[0:1, :] * u_prev + cw[1:2, :] * u + cw[2:3, :] * u_next
    ya = _dot((cb_ref[...].astype(F32) * conv).astype(BF16), wpa_ref[...])
    yb = _dot(att_ref[...], wpb_ref[...])
    o = of_ref[...] + ob_ref[...]
    z = z_ref[...].astype(F32)
    parts = []
    for h in range(DN_HEADS):
        sl = slice(h * DN_DV, (h + 1) * DN_DV)
        parts.append((_rms(o[:, sl], ng_ref[...]) * _silu(z[:, sl])).astype(BF16))
    yc = _dot(jnp.concatenate(parts, axis=1), wpc_ref[...])
    mix_in = (_sigmoid(ga_ref[...].astype(F32)) * ya + _sigmoid(gb_ref[...].astype(F32)) * yb
              + _sigmoid(gc_ref[...].astype(F32)) * yc)
    mix = _dot(mix_in.astype(BF16), wo_ref[...])
    o_ref[...] = x_ref[...] + mod_ref[0, 2:3, :] * _rms(mix, gpost_ref[...])


def _post_mixer(x, mod3, mod_base, rows_per_mod, t, proj, att, o_f, o_b, conv_w, ng, wpa, wpb, wpc, wo, layer,
                g_post1):
    n = x.shape[0]
    tm = min(512, t)
    tps = t // tm
    hb = tm // BF16_SUBLANES
    cwid = CONV_WIDTH
    c0 = REST_CONV // cwid

    def prev(i):
        return jnp.maximum(i * hb - 1, 0)

    def nxt(i):
        return jnp.minimum((i + 1) * hb, n // BF16_SUBLANES - 1)

    g0 = REST_GATE // D_MODEL
    in_specs = [
        pl.BlockSpec((tm, D_MODEL), lambda i: (i, 0)),
        pl.BlockSpec((1, 6, D_MODEL), lambda i: (mod_base + (i * tm) // rows_per_mod, 0, 0)),
        pl.BlockSpec((tm, cwid), lambda i: (i, c0)),
        pl.BlockSpec((tm, cwid), lambda i: (i, c0 + 1)),
        pl.BlockSpec((tm, cwid), lambda i: (i, c0 + 2)),
        pl.BlockSpec((BF16_SUBLANES, cwid), lambda i: (prev(i), c0 + 1)),
        pl.BlockSpec((BF16_SUBLANES, cwid), lambda i: (prev(i), c0 + 2)),
        pl.BlockSpec((BF16_SUBLANES, cwid), lambda i: (nxt(i), c0 + 1)),
        pl.BlockSpec((BF16_SUBLANES, cwid), lambda i: (nxt(i), c0 + 2)),
        _const_spec((3, cwid)),
        pl.BlockSpec((tm, ATTN_Q), lambda i: (i, 0)),
        pl.BlockSpec((tm, DN_VW), lambda i: (i, 0)),
        pl.BlockSpec((tm, DN_VW), lambda i: (i, 0)),
        pl.BlockSpec((tm, DN_VW), lambda i: (i, REST_Z // DN_VW)),
        _const_spec((1, DN_DV)),
        pl.BlockSpec((tm, D_MODEL), lambda i: (i, g0)),
        pl.BlockSpec((tm, D_MODEL), lambda i: (i, g0 + 1)),
        pl.BlockSpec((tm, D_MODEL), lambda i: (i, g0 + 2)),
        _layer_spec((CONV_WIDTH, D_MODEL), layer), _layer_spec((ATTN_Q, D_MODEL), layer),
        _layer_spec((DN_VW, D_MODEL), layer), _layer_spec((D_MODEL, D_MODEL), layer),
        _const_spec((1, D_MODEL)),
    ]
    return pl.pallas_call(
        functools.partial(_post_kernel, tm=tm, tps=tps),
        grid=(n // tm,),
        in_specs=in_specs,
        out_specs=pl.BlockSpec((tm, D_MODEL), lambda i: (i, 0)),
        out_shape=jax.ShapeDtypeStruct((n, D_MODEL), F32),
        compiler_params=_cparams(("arbitrary",)),
    )(x, mod3, proj, proj, proj, proj, proj, proj, proj, conv_w, att, o_f, o_b, proj, ng,
      proj, proj, proj, wpa, wpb, wpc, wo, g_post1)


def _ffn_kernel(x_ref, mod_ref, gpre_ref, wg_ref, wu_ref, wd_ref, gpost_ref, o_ref, act_sc, *, tf):
    x = x_ref[...]
    h2 = (_rms(x, gpre_ref[...]) * (1.0 + mod_ref[0, 4:5, :]) + mod_ref[0, 3:4, :]).astype(BF16)
    for j in range(0, D_FF, tf):
        gate = _dot(h2, wg_ref[:, j:j + tf])
        up = _dot(h2, wu_ref[:, j:j + tf])
        act_sc[:, j:j + tf] = (_silu(gate) * up).astype(BF16)
    ffn = _dot(act_sc[...], wd_ref[...])
    o_ref[...] = x + mod_ref[0, 5:6, :] * _rms(ffn, gpost_ref[...])


def _ffn(x, mod3, mod_base, rows_per_mod, g_pre2, wg, wu, wd, layer, g_post2):
    n = x.shape[0]
    tm = 512
    return pl.pallas_call(
        functools.partial(_ffn_kernel, tf=256),
        grid=(n // tm,),
        in_specs=[pl.BlockSpec((tm, D_MODEL), lambda i: (i, 0)),
                  pl.BlockSpec((1, 6, D_MODEL), lambda i: (mod_base + (i * tm) // rows_per_mod, 0, 0)),
                  _const_spec((1, D_MODEL)),
                  _layer_spec((D_MODEL, D_FF), layer), _layer_spec((D_MODEL, D_FF), layer),
                  _layer_spec((D_FF, D_MODEL), layer),
                  _const_spec((1, D_MODEL))],
        out_specs=pl.BlockSpec((tm, D_MODEL), lambda i: (i, 0)),
        out_shape=jax.ShapeDtypeStruct((n, D_MODEL), F32),
        scratch_shapes=[pltpu.VMEM((tm, D_FF), BF16)],
        compiler_params=_cparams(("arbitrary",)),
    )(x, mod3, g_pre2, wg, wu, wd, g_post2)


def _rope_tables(t):
    rows = t // GRID_W
    row_id = np.repeat(np.arange(rows, dtype=np.float32), GRID_W)
    col_id = np.tile(np.arange(GRID_W, dtype=np.float32), rows)
    n_freq = HEAD_DIM // 4
    inv_freq = (np.float32(ROPE_THETA) ** (-np.arange(n_freq, dtype=np.float32) / np.float32(n_freq))).astype(np.float32)
    ang = np.concatenate([row_id[:, None] * inv_freq, col_id[:, None] * inv_freq], axis=-1).astype(np.float32)
    cos = np.repeat(np.cos(ang), 2, axis=-1).astype(np.float32)
    sin = np.repeat(np.sin(ang), 2, axis=-1).astype(np.float32)
    sign = np.tile(np.array([-1.0, 1.0], np.float32), HEAD_DIM // 2)
    return jnp.asarray(cos), jnp.asarray(sin * sign)


def _layer(x, b, t, mod3, mod_base, rows_per_mod, lw, rope_tabs, cache, state0, want_state):
    q_p, k_p, vt_p, v_f32, qn, kn, vn, gb, gbt, proj = _inproj(x, t, mod3, mod_base, rows_per_mod, lw,
                                                               rope_tabs, want_state)
    att = _attention(q_p, k_p, vt_p, b, t, cache)
    dn = [_deltanet_dir(qn, kn, vn, gb, gbt, state0, b, t, d, want_state) for d in range(2)]
    x = _post_mixer(x, mod3, mod_base, rows_per_mod, t, proj, att, dn[0][0], dn[1][0], lw["conv_w"],
                    lw["dn_norm_g"], lw["w_pa"], lw["w_pb"], lw["w_pc"], lw["w_o"], lw["layer"], lw["g_post1"])
    x = _ffn(x, mod3, mod_base, rows_per_mod, lw["g_pre2"], lw["w_gate"], lw["w_up"], lw["w_down"],
             lw["layer"], lw["g_post2"])
    if not want_state:
        return x, None
    s_fin = jnp.stack([dn[0][1], dn[1][1]], axis=1)
    return x, (k_p, v_f32, s_fin)


def kernel(x_prompt, x_sample, cache_k, cache_v, state_dn, c, c_ctx, w_mod, b_mod, g_pre1, g_post1, g_pre2, g_post2, w_in, conv_w, g_qn, g_kn, dn_conv_w, dn_a_log, dn_dt_bias, dn_norm_g, w_pa, w_pb, w_pc, w_o, w_gate, w_up, w_down):
    bp, tp, d = x_prompt.shape
    bs, ts, _ = x_sample.shape
    depth = w_mod.shape[0]
    past = cache_k.shape[2]

    mod_rows = -(-(bs + 1) // SUBLANES) * SUBLANES
    cv = jnp.zeros((mod_rows, d), F32).at[:bs].set(c).at[bs].set(c_ctx)
    mod_all = _modulation(cv, w_mod, b_mod).reshape(depth, mod_rows, 6, d)

    stacked = {"w_pa": w_pa.astype(BF16), "w_pb": w_pb.astype(BF16), "w_pc": w_pc.astype(BF16),
               "w_o": w_o.astype(BF16), "w_gate": w_gate.astype(BF16), "w_up": w_up.astype(BF16),
               "w_down": w_down.astype(BF16)}
    w_main = w_in.astype(BF16)
    w_tail = jnp.concatenate(
        [w_in[:, :, COL_GATE:D_IN], w_in[:, :, COL_SMALL:COL_GATE],
         jnp.zeros((depth, d, LANES - 4 * DN_HEADS), w_in.dtype)], axis=-1).astype(BF16)
    lane_pad = ((0, 0), (2 * DN_HEADS, LANES - 4 * DN_HEADS))
    a_flat = dn_a_log.reshape(depth, 2 * DN_HEADS)
    b_flat = dn_dt_bias.reshape(depth, 2 * DN_HEADS)
    rope_tabs = _rope_tables(ts)
    cache_k4 = cache_k.reshape(bs, depth, past, ATTN_KV)
    cache_v4 = cache_v.reshape(bs, depth, past, ATTN_KV)

    xp = x_prompt.reshape(bp * tp, d)
    xs = x_sample.reshape(bs * ts, d)
    ks, vs, ss = [], [], []
    for l in range(depth):
        lw = {
            "g_pre1": g_pre1[l][None], "g_post1": g_post1[l][None], "g_pre2": g_pre2[l][None],
            "g_post2": g_post2[l][None], "layer": l, "w_in": w_main, "w_tail": w_tail, "conv_w": conv_w[l],
            "g_qn": g_qn[l][None], "g_kn": g_kn[l][None], "dn_conv_w": dn_conv_w[l],
            "dn_ac": jnp.pad(a_flat[l][None], lane_pad), "dn_bc": jnp.pad(b_flat[l][None], lane_pad),
            "dn_norm_g": dn_norm_g[l][None],
            **stacked,
        }
        mod3 = mod_all[l]
        xp, (k_l, v_l, s_l) = _layer(xp, bp, tp, mod3, bs, bp * tp, lw, None, None, None, True)
        ks.append(k_l.reshape(bp, tp, N_KV_HEADS, HEAD_DIM))
        vs.append(v_l.reshape(bp, tp, N_KV_HEADS, HEAD_DIM))
        ss.append(s_l)
        xs, _ = _layer(xs, bs, ts, mod3, 0, ts, lw, rope_tabs, (cache_k4, cache_v4, l), (state_dn, l), False)
    return (xp.reshape(bp, tp, d), xs.reshape(bs, ts, d), jnp.stack(ks, axis=1), jnp.stack(vs, axis=1),
            jnp.stack(ss, axis=1))
```

```python
import functools

import jax
import jax.numpy as jnp
import numpy as np
from jax import lax
from jax.experimental import pallas as pl
from jax.experimental.pallas import tpu as pltpu

F32 = jnp.float32
BF16 = jnp.bfloat16

D_MODEL = 1024
EPS = 1e-6
GRID_W = 64
N_HEADS = 8
N_KV_HEADS = 2
Q_PER_KV = N_HEADS // N_KV_HEADS
HEAD_DIM = 128
ATTN_Q = N_HEADS * HEAD_DIM
ATTN_KV = N_KV_HEADS * HEAD_DIM
ROPE_THETA = 10000.0
CONV_WIDTH = 512
DN_HEADS = 4
DN_DK = 128
DN_DV = 128
DN_QK = DN_HEADS * DN_DK
DN_VW = DN_HEADS * DN_DV
D_FF = 2816

SUBLANES = 8
BF16_SUBLANES = 16
LANES = 128

COL_CONV = 0
COL_Q = COL_CONV + 3 * CONV_WIDTH
COL_K = COL_Q + ATTN_Q
COL_V = COL_K + ATTN_KV
COL_DN = COL_V + ATTN_KV
COL_Z = COL_DN + 3 * DN_QK
COL_SMALL = COL_Z + DN_VW
COL_GATE = COL_SMALL + 4 * DN_HEADS
D_IN = COL_GATE + 3 * D_MODEL
TAIL_GATE = 0
TAIL_SMALL = TAIL_GATE + 3 * D_MODEL
D_TAIL = TAIL_SMALL + LANES
REST_GATE = 0
REST_CONV = REST_GATE + 3 * D_MODEL
REST_Z = REST_CONV + 3 * CONV_WIDTH
D_REST = REST_Z + DN_VW

Q_SCALE = 1.4426950408889634 * HEAD_DIM ** -0.5
DN_CHUNK = 128
DN_BASE = 16
VMEM_LIMIT = 56 * 1024 * 1024


def _cparams(sem):
    return pltpu.CompilerParams(dimension_semantics=sem, vmem_limit_bytes=VMEM_LIMIT)


def _const_spec(shape):
    nd = len(shape)
    return pl.BlockSpec(shape, lambda *_: (0,) * nd, pipeline_mode=pl.Buffered(1))


def _layer_spec(shape, layer):
    return pl.BlockSpec((None,) + tuple(shape), lambda *_: (layer,) + (0,) * len(shape),
                        pipeline_mode=pl.Buffered(1))


def _dot(a, b):
    return jnp.dot(a, b, preferred_element_type=F32)


def _dot_nt(a, b):
    return lax.dot_general(a, b, (((1,), (1,)), ((), ())), preferred_element_type=F32)


def _dot_tn(a, b):
    return lax.dot_general(a, b, (((0,), (0,)), ((), ())), preferred_element_type=F32)


def _rms(x, g):
    return x * lax.rsqrt(jnp.mean(x * x, axis=-1, keepdims=True) + EPS) * g


def _sigmoid(x):
    return 1.0 / (1.0 + jnp.exp(-x))


def _silu(x):
    return x * _sigmoid(x)


def _softplus(x):
    return jnp.maximum(x, 0.0) + jnp.log1p(jnp.exp(-jnp.abs(x)))


def _split3(x):
    hi = x.astype(BF16)
    r = x - hi.astype(F32)
    mid = r.astype(BF16)
    lo = (r - mid.astype(F32)).astype(BF16)
    return hi, mid, lo


def _mod_kernel(cv_ref, w_ref, b_ref, o_ref):
    cv = cv_ref[...]
    o_ref[0] = _dot(_silu(cv).astype(BF16), w_ref[0].astype(BF16)) + b_ref[0]


def _modulation(cv, w_mod, b_mod):
    depth, d, n6 = w_mod.shape
    rows = cv.shape[0]
    tn = 1536
    return pl.pallas_call(
        _mod_kernel,
        grid=(depth, n6 // tn),
        in_specs=[pl.BlockSpec((rows, d), lambda l, j: (0, 0)),
                  pl.BlockSpec((1, d, tn), lambda l, j: (l, 0, j)),
                  pl.BlockSpec((1, 1, tn), lambda l, j: (l, 0, j))],
        out_specs=pl.BlockSpec((1, rows, tn), lambda l, j: (l, 0, j)),
        out_shape=jax.ShapeDtypeStruct((depth, rows, n6), F32),
        compiler_params=_cparams(("arbitrary", "arbitrary")),
    )(cv, w_mod, b_mod.reshape(depth, 1, n6))


def _swap_pairs(y):
    lane = lax.broadcasted_iota(jnp.int32, y.shape, 1)
    return jnp.where(lane % 2 == 0, pltpu.roll(y, LANES - 1, axis=1), pltpu.roll(y, 1, axis=1))


def _interleave(*gens):
    gens = list(gens)
    while gens:
        for gen in list(gens):
            try:
                next(gen)
            except StopIteration:
                gens.remove(gen)


def _inproj_kernel(*refs, rope, want_v, tm, tps, tn):
    refs = list(refs)
    x_ref, xp_ref, xn_ref, mod_ref, g_ref, w_ref, wt_ref, gq_ref, gk_ref, cw_ref, ac_ref, bc_ref = refs[:12]
    refs = refs[12:]
    cos_ref, sin_ref = (refs.pop(0), refs.pop(0)) if rope else (None, None)
    q_out, k_out, v_out = refs[:3]
    refs = refs[3:]
    vf_out = refs.pop(0) if want_v else None
    qn_out, kn_out, vn_out, gb_out, gbt_out, rest_out = refs
    i = pl.program_id(0)

    def modnorm(x):
        return (_rms(x, g_ref[...]) * (1.0 + mod_ref[0, 1:2, :]) + mod_ref[0, 0:1, :]).astype(BF16)

    h = modnorm(x_ref[...])
    h_halo = modnorm(jnp.concatenate([xp_ref[...], xn_ref[...]], axis=0))

    def qk_head(x, g):
        y = _rms(x, g)
        if rope:
            y = y * cos_ref[...] + _swap_pairs(y) * sin_ref[...]
        return y

    def attn_epilogue():
        qkv = _dot(h, w_ref[:, COL_Q:COL_DN])
        yield
        for hh in range(N_HEADS):
            sl = slice(hh * HEAD_DIM, (hh + 1) * HEAD_DIM)
            q_out[:, sl] = (qk_head(qkv[:, sl], gq_ref[...]) * Q_SCALE).astype(q_out.dtype)
            if hh % 2 == 1:
                yield
        for hh in range(N_KV_HEADS):
            sl = slice(hh * HEAD_DIM, (hh + 1) * HEAD_DIM)
            k_out[:, sl] = qk_head(qkv[:, ATTN_Q + hh * HEAD_DIM:ATTN_Q + (hh + 1) * HEAD_DIM],
                                   gk_ref[...]).astype(k_out.dtype)
        v = qkv[:, ATTN_Q + ATTN_KV:]
        v_out[...] = v.T.astype(v_out.dtype)
        if want_v:
            vf_out[...] = v

    def dn_epilogue():
        dn = _dot(h, w_ref[:, COL_DN:COL_Z])
        dn_halo = _dot(h_halo, w_ref[:, COL_DN:COL_Z])
        gl = _dot(h, wt_ref[:, TAIL_SMALL:D_TAIL])
        yield
        rows = lax.broadcasted_iota(jnp.int32, (tm, 1), 0)
        prev_row = jnp.where((i % tps) == 0, 0.0, dn_halo[SUBLANES - 1:SUBLANES, :])
        next_row = jnp.where((i % tps) == tps - 1, 0.0, dn_halo[SUBLANES:SUBLANES + 1, :])
        dn_prev = jnp.where(rows == 0, prev_row, pltpu.roll(dn, 1, axis=0))
        dn_next = jnp.where(rows == tm - 1, next_row, pltpu.roll(dn, tm - 1, axis=0))
        cw = cw_ref[...]

        def conv_silu(sl):
            return _silu(cw[0:1, sl] * dn_prev[:, sl] + cw[1:2, sl] * dn[:, sl] + cw[2:3, sl] * dn_next[:, sl])

        for hh in range(DN_HEADS):
            sl = slice(hh * DN_DK, (hh + 1) * DN_DK)
            qh = conv_silu(sl)
            kh = conv_silu(slice(DN_QK + hh * DN_DK, DN_QK + (hh + 1) * DN_DK))
            qn_out[:, sl] = (qh * lax.rsqrt(jnp.sum(qh * qh, axis=-1, keepdims=True) + EPS)
                             * (DN_DK ** -0.5)).astype(qn_out.dtype)
            kn_out[:, sl] = (kh * lax.rsqrt(jnp.sum(kh * kh, axis=-1, keepdims=True) + EPS)).astype(kn_out.dtype)
            vn_out[:, sl] = conv_silu(slice(2 * DN_QK + hh * DN_DV, 2 * DN_QK + (hh + 1) * DN_DV)
                                      ).astype(vn_out.dtype)
            yield
        lane = lax.broadcasted_iota(jnp.int32, gl.shape, 1)
        gb = jnp.where(lane < 2 * DN_HEADS, _sigmoid(gl), -jnp.exp(ac_ref[...]) * _softplus(gl + bc_ref[...]))
        gb_out[...] = gb
        gbt_out[...] = gb.T

    def remaining():
        groups = [(wt_ref, TAIL_GATE, REST_GATE, 3 * D_MODEL), (w_ref, COL_CONV, REST_CONV, 3 * CONV_WIDTH),
                  (w_ref, COL_Z, REST_Z, DN_VW)]
        for ref, src, dst, width in groups:
            for j in range(0, width, tn):
                rest_out[:, dst + j:dst + j + tn] = _dot(h, ref[:, src + j:src + j + tn]).astype(rest_out.dtype)
                yield

    _interleave(dn_epilogue(), attn_epilogue(), remaining())


def _inproj(x, t, mod3, mod_base, rows_per_mod, lw, rope_tabs, want_v):
    n = x.shape[0]
    tm = min(512, t)
    tps = t // tm
    halo = tm // SUBLANES
    w_main, w_tail, layer = lw["w_in"], lw["w_tail"], lw["layer"]
    in_specs = [pl.BlockSpec((tm, D_MODEL), lambda i: (i, 0)),
                pl.BlockSpec((SUBLANES, D_MODEL), lambda i: (jnp.maximum(i * halo - 1, 0), 0)),
                pl.BlockSpec((SUBLANES, D_MODEL), lambda i: (jnp.minimum((i + 1) * halo, n // SUBLANES - 1), 0)),
                pl.BlockSpec((1, 6, D_MODEL), lambda i: (mod_base + (i * tm) // rows_per_mod, 0, 0)),
                _const_spec((1, D_MODEL)),
                _layer_spec((D_MODEL, COL_SMALL), layer), _layer_spec((D_MODEL, D_TAIL), layer),
                _const_spec((1, HEAD_DIM)), _const_spec((1, HEAD_DIM)),
                _const_spec((3, 3 * DN_QK)), _const_spec((1, LANES)), _const_spec((1, LANES))]
    args = [x, x, x, mod3, lw["g_pre1"], w_main, w_tail, lw["g_qn"], lw["g_kn"], lw["dn_conv_w"],
            lw["dn_ac"], lw["dn_bc"]]
    if rope_tabs is not None:
        in_specs += [pl.BlockSpec((tm, HEAD_DIM), lambda i: (i % tps, 0))] * 2
        args += list(rope_tabs)

    outs = [(ATTN_Q, BF16, False), (ATTN_KV, F32 if want_v else BF16, False), (ATTN_KV, BF16, True)]
    if want_v:
        outs.append((ATTN_KV, F32, False))
    outs += [(DN_QK, BF16, False), (DN_QK, BF16, False), (DN_VW, BF16, False), (LANES, F32, False),
             (LANES, F32, True), (D_REST, BF16, False)]
    res = pl.pallas_call(
        functools.partial(_inproj_kernel, rope=rope_tabs is not None, want_v=want_v, tm=tm, tps=tps, tn=512),
        grid=(n // tm,),
        in_specs=in_specs,
        out_specs=[pl.BlockSpec((w, tm), lambda i: (0, i)) if tr else pl.BlockSpec((tm, w), lambda i: (i, 0))
                   for w, _, tr in outs],
        out_shape=[jax.ShapeDtypeStruct((w, n) if tr else (n, w), dt) for w, dt, tr in outs],
        compiler_params=_cparams(("arbitrary",)),
    )(*args)
    res = list(res)
    q_p, k_p, vt_p = res[:3]
    v_f32 = res[3] if want_v else None
    qn, kn, vn, gb, gbt, rest = res[-6:]
    return q_p, k_p, vt_p, v_f32, qn, kn, vn, gb, gbt, rest


def _attn_kernel(*refs, has_cache, t, tk, tq, nq):
    refs = list(refs)
    q_ref = refs.pop(0)
    qn_ref = refs.pop(0) if nq > 1 else None
    kc_ref, vc_ref = (refs.pop(0), refs.pop(0)) if has_cache else (None, None)
    k_ref, v_ref, o_ref = refs[:3]
    refs = refs[3:]
    vct_sc = refs.pop() if has_cache else None
    s_bufs, m_bufs = refs[:len(refs) // 2], refs[len(refs) // 2:]
    qi = pl.program_id(2)

    if has_cache:
        @pl.when(qi == 0)
        def _():
            vct_sc[...] = vc_ref[...].T.astype(BF16)

    segs = []
    if has_cache:
        past = kc_ref.shape[0]
        segs += [(kc_ref, vct_sc, r, min(tk, past - r)) for r in range(0, past, tk)]
    segs += [(k_ref, v_ref, r, tk) for r in range(0, t, tk)]
    offs = [sum(w for _, _, _, w in segs[:i]) for i in range(len(segs))]

    def stack(ref, row0=0):
        q = ref[row0:row0 + tq, :]
        return jnp.concatenate([q[:, g * HEAD_DIM:(g + 1) * HEAD_DIM] for g in range(Q_PER_KV)], axis=0)

    def scores(qs, s_ref, i, m_run):
        kr, _, r, w = segs[i]
        s = _dot_nt(kr[r:r + w, :].astype(BF16), qs)
        s_ref[offs[i]:offs[i] + w, :] = s
        for r0 in range(0, w, SUBLANES):
            blk = s[r0:r0 + SUBLANES, :]
            m_run = blk if m_run is None else jnp.maximum(m_run, blk)
        return m_run

    def weighted(s_ref, i, m, acc):
        _, vr, r, w = segs[i]
        p = jnp.exp2(s_ref[offs[i]:offs[i] + w, :] - m)
        den = p[0:SUBLANES, :]
        for r0 in range(SUBLANES, w, SUBLANES):
            den = den + p[r0:r0 + SUBLANES, :]
        pv = _dot(vr[:, r:r + w].astype(BF16), p.astype(BF16))
        return (pv, den) if acc is None else (acc[0] + pv, acc[1] + den)

    def finish(acc, row0=0):
        o = (acc[0] / jnp.sum(acc[1], axis=0, keepdims=True)).T
        for g in range(Q_PER_KV):
            o_ref[row0:row0 + tq, g * HEAD_DIM:(g + 1) * HEAD_DIM] = o[g * tq:(g + 1) * tq].astype(o_ref.dtype)

    if nq > 1:
        @pl.when(qi == 0)
        def _():
            qs0 = stack(q_ref)
            m_run = None
            for i in range(len(segs)):
                m_run = scores(qs0, s_bufs[0], i, m_run)
            m_bufs[0][...] = m_run

        def step(s_cur, m_cur, s_nxt, m_nxt, qs_next, row0):
            m = jnp.max(m_cur[...], axis=0, keepdims=True)
            acc, m_run = None, None
            for i in range(len(segs)):
                m_run = scores(qs_next, s_nxt, i, m_run)
                acc = weighted(s_cur, i, m, acc)
            m_nxt[...] = m_run
            finish(acc, row0)

        step(s_bufs[0], m_bufs[0], s_bufs[1], m_bufs[1], stack(q_ref, tq), 0)
        step(s_bufs[1], m_bufs[1], s_bufs[0], m_bufs[0], stack(qn_ref), tq)
    else:
        qs = stack(q_ref)
        m_run = None
        for i in range(len(segs)):
            m_run = scores(qs, s_bufs[0], i, m_run)
        m = jnp.max(m_run, axis=0, keepdims=True)
        acc = None
        for i in range(len(segs)):
            acc = weighted(s_bufs[0], i, m, acc)
        finish(acc)


def _attention(q_p, k_p, vt_p, b, t, cache):
    n = q_p.shape[0]
    tq = 128 if t > 256 else 256
    tk = min(t, 512)
    nq = t // tq
    assert nq == 1 or nq % 2 == 0
    per_step = 2 if nq > 1 else 1
    steps = nq // per_step
    qw = Q_PER_KV * HEAD_DIM
    in_specs = [pl.BlockSpec((per_step * tq, qw), lambda bi, j, qi: (bi * steps + qi, j))]
    args = [q_p]
    if nq > 1:
        in_specs.append(pl.BlockSpec((tq, qw),
                                     lambda bi, j, qi: (bi * nq + jnp.minimum(2 * qi + 2, nq - 1), j)))
        args.append(q_p)
    if cache is not None:
        cache_k, cache_v, layer = cache
        past = cache_k.shape[2]
        cspec = pl.BlockSpec((None, None, past, HEAD_DIM), lambda bi, j, qi: (bi, layer, 0, j))
        in_specs += [cspec, cspec]
        args += [cache_k, cache_v]
    in_specs += [pl.BlockSpec((t, HEAD_DIM), lambda bi, j, qi: (bi, j)),
                 pl.BlockSpec((HEAD_DIM, t), lambda bi, j, qi: (j, bi))]
    args += [k_p, vt_p]
    n_keys = t + (cache[0].shape[2] if cache is not None else 0)
    slots = 2 if nq > 1 else 1
    return pl.pallas_call(
        functools.partial(_attn_kernel, has_cache=cache is not None, t=t, tk=tk, tq=tq, nq=nq),
        grid=(b, N_KV_HEADS, steps),
        in_specs=in_specs,
        out_specs=pl.BlockSpec((per_step * tq, qw), lambda bi, j, qi: (bi * steps + qi, j)),
        out_shape=jax.ShapeDtypeStruct((n, ATTN_Q), BF16),
        scratch_shapes=([pltpu.VMEM((n_keys, Q_PER_KV * tq), F32)] * slots
                        + [pltpu.VMEM((SUBLANES, Q_PER_KV * tq), F32)] * slots
                        + ([pltpu.VMEM((HEAD_DIM, cache[0].shape[2]), BF16)] if cache is not None else [])),
        compiler_params=_cparams(("arbitrary", "arbitrary", "arbitrary")),
    )(*args)


def _lane_pick(x, lane):
    idx = lax.broadcasted_iota(jnp.int32, x.shape, 1)
    return jnp.sum(jnp.where(idx == lane, x, 0.0), axis=-1, keepdims=True)


def _block_diag(x2):
    xb = x2.astype(BF16)
    z = jnp.zeros((DN_CHUNK, DN_CHUNK), BF16)
    return jnp.concatenate([jnp.concatenate([xb[:, :DN_CHUNK], z], axis=1),
                            jnp.concatenate([z, xb[:, DN_CHUNK:]], axis=1)], axis=0)


def _mm_pair(x2, y2):
    return _dot(x2.astype(BF16), _block_diag(y2))


def _unit_tri_inverses(lmats, ri, ci):
    def blk(s):
        return (ri ^ ci) < s

    eye = jnp.where(ri == ci, 1.0, 0.0)
    ps = [jnp.where(blk(DN_BASE), -lm, 0.0) for lm in lmats]
    xs = [eye + p for p in ps]
    s = 2
    while s < DN_BASE:
        ps = [_mm_pair(p, p) for p in ps]
        yield
        xs = [x + _mm_pair(x, p) for x, p in zip(xs, ps)]
        yield
        s *= 2
    s = DN_BASE
    while s < DN_CHUNK:
        sel = blk(2 * s) & jnp.logical_not(blk(s))
        ts = [_mm_pair(jnp.where(sel, lm, 0.0), x) for lm, x in zip(lmats, xs)]
        yield
        xs = [x - _mm_pair(x, t) for x, t in zip(xs, ts)]
        yield
        s *= 2
    return xs


def _dn_kernel(*refs, reverse, has_s0, want_state, n_seq, cps, d, nt):
    nc = n_seq * cps
    refs = list(refs)
    q_ref, k_ref, v_ref, gb_ref, gr_ref = refs[:5]
    refs = refs[5:]
    s0_ref = refs.pop(0) if has_s0 else None
    o_ref = refs.pop(0)
    sfin_ref = refs.pop(0) if want_state else None
    s_sc = refs.pop(0)
    bufs = (refs[:5], refs[5:10])
    n_pairs = DN_HEADS // 2
    c2 = 2 * DN_CHUNK
    tt = nc * DN_CHUNK
    insts = [(c, p) for c in range(nc) for p in range(n_pairs)]

    g = pl.program_id(0)
    i_scan = (g - 1) % nt

    def pair_cols(x, lane_a, lane_b):
        shape = (x.shape[0], DN_CHUNK)
        return jnp.concatenate([jnp.broadcast_to(_lane_pick(x, lane_a), shape),
                                jnp.broadcast_to(_lane_pick(x, lane_b), shape)], axis=1)

    def prepare(buf):
        u_buf, wq_buf, attn_buf, kt_buf, gt_buf = buf
        gb = gb_ref[...]
        g_r = gr_ref[...]
        bi = lax.broadcasted_iota(jnp.int32, (tt, tt), 0)
        bj = lax.broadcasted_iota(jnp.int32, (tt, tt), 1)
        same = (bi ^ bj) < DN_CHUNK
        if reverse:
            tri_c = jnp.where(same & (bj >= bi), 1.0, 0.0).astype(BF16)
            tri_r = jnp.where(same & (bi >= bj), 1.0, 0.0).astype(BF16)
        else:
            tri_c = jnp.where(same & (bj <= bi), 1.0, 0.0).astype(BF16)
            tri_r = jnp.where(same & (bi <= bj), 1.0, 0.0).astype(BF16)
        gcum_c = sum(_dot(tri_c, part) for part in _split3(gb))
        gcum_r = sum(_dot(part, tri_r) for part in _split3(g_r))
        yield

        ri = lax.broadcasted_iota(jnp.int32, (DN_CHUNK, c2), 0)
        ci = lax.broadcasted_iota(jnp.int32, (DN_CHUNK, c2), 1) & (DN_CHUNK - 1)
        incl = (ci >= ri) if reverse else (ci <= ri)
        strict = (ci > ri) if reverse else (ci < ri)
        pre = []
        for c, p in insts:
            rs = slice(c * DN_CHUNK, (c + 1) * DN_CHUNK)
            cs = slice(p * c2, (p + 1) * c2)
            lane_b = d * DN_HEADS + 2 * p
            lane_g = 2 * DN_HEADS + d * DN_HEADS + 2 * p
            q2 = q_ref[rs, cs].astype(F32)
            k2 = k_ref[rs, cs].astype(F32)
            v2 = v_ref[rs, cs].astype(F32)
            g_i = pair_cols(gcum_c[rs, :], lane_g, lane_g + 1)
            b_i = pair_cols(gb[rs, :], lane_b, lane_b + 1)
            g_j = jnp.concatenate([gcum_r[lane_g:lane_g + 1, rs], gcum_r[lane_g + 1:lane_g + 2, rs]], axis=1)
            g_tot = g_i[0:1, :] if reverse else g_i[DN_CHUNK - 1:DN_CHUNK, :]
            decay = jnp.where(incl, jnp.exp(jnp.where(incl, g_i - g_j, 0.0)), 0.0)
            pre.append(dict(q2=q2, k2=k2, v2=v2, g_i=g_i, b_i=b_i, g_tot=g_tot, decay=decay, kb=k2 * b_i))
        a2s = [_dot_nt(jnp.concatenate([p["kb"], p["q2"]], axis=0).astype(BF16), _block_diag(p["k2"]))
               for p in pre]
        yield
        lmats = [jnp.where(strict, a2[:DN_CHUNK] * p["decay"], 0.0) for a2, p in zip(a2s, pre)]
        for n, (a2, p) in enumerate(zip(a2s, pre)):
            attn_buf[n] = (a2[DN_CHUNK:] * p["decay"]).astype(BF16)
        tinvs = yield from _unit_tri_inverses(lmats, ri, ci)
        gams = [jnp.exp(p["g_i"]) for p in pre]
        us = [_mm_pair(t, p["v2"] * p["b_i"]) for t, p in zip(tinvs, pre)]
        yield
        ws = [_mm_pair(t, p["kb"] * gam) for t, p, gam in zip(tinvs, pre, gams)]
        yield
        for n, (p, u, w, gam) in enumerate(zip(pre, us, ws, gams)):
            u_buf[n] = u
            wq_buf[n] = jnp.concatenate([w, p["q2"] * gam], axis=0).astype(BF16)
            kt_buf[n] = (p["k2"] * jnp.exp(p["g_tot"] - p["g_i"])).astype(BF16)
            gt_buf[n] = jnp.broadcast_to(jnp.exp(p["g_tot"]), (SUBLANES, c2))

    def scan(buf):
        u_buf, wq_buf, attn_buf, kt_buf, gt_buf = buf
        si = lax.broadcasted_iota(jnp.int32, (c2, c2), 0)
        sj = lax.broadcasted_iota(jnp.int32, (c2, c2), 1)
        on_diag = (si < DN_CHUNK) == (sj < DN_CHUNK)
        chains = [(sq, p) for sq in range(n_seq) for p in range(n_pairs)]
        states = [s_sc[sq * n_pairs + p] for sq, p in chains]
        for k in (range(cps - 1, -1, -1) if reverse else range(cps)):
            cs = [sq * cps + k for sq, _ in chains]
            ns = [c * n_pairs + p for c, (_, p) in zip(cs, chains)]
            m1s = [_dot(wq_buf[n], s.astype(BF16)) for n, s in zip(ns, states)]
            yield
            v_news = [u_buf[n] - m1[:DN_CHUNK] for n, m1 in zip(ns, m1s)]
            outs = [m1[DN_CHUNK:] + _mm_pair(attn_buf[n], v) for n, m1, v in zip(ns, m1s, v_news)]
            yield
            states = [s * gt_buf[n][0:1, :] + jnp.where(on_diag, _dot_tn(kt_buf[n], v.astype(BF16)), 0.0)
                      for n, s, v in zip(ns, states, v_news)]
            for c, (_, p), out in zip(cs, chains, outs):
                o_ref[c * DN_CHUNK:(c + 1) * DN_CHUNK, p * c2:(p + 1) * c2] = out
            yield
        for (sq, p), s in zip(chains, states):
            s_sc[sq * n_pairs + p] = s

    def step(par):
        @pl.when(i_scan == 0)
        def _():
            s_sc[...] = jnp.zeros_like(s_sc)
            if has_s0:
                for h in range(DN_HEADS):
                    lo = (h % 2) * DN_CHUNK
                    s_sc[h // 2, lo:lo + DN_CHUNK, lo:lo + DN_CHUNK] = s0_ref[h]

        _interleave(prepare(bufs[par]), scan(bufs[1 - par]))

        if want_state:
            @pl.when(i_scan == nt - 1)
            def _():
                for sq in range(n_seq):
                    for h in range(DN_HEADS):
                        lo = (h % 2) * DN_CHUNK
                        sfin_ref[sq, h] = s_sc[sq * n_pairs + h // 2, lo:lo + DN_CHUNK, lo:lo + DN_CHUNK]

    pl.when(g == 0)(lambda: _interleave(prepare(bufs[0])))
    pl.when((g > 0) & (g % 2 == 0))(functools.partial(step, 0))
    pl.when(g % 2 == 1)(functools.partial(step, 1))


def _deltanet_dir(qn, kn, vn, gb, gbt, s0, b, t, d, want_state):
    n = qn.shape[0]
    chunks_per_tile = 4
    cps = min(chunks_per_tile, t // DN_CHUNK)
    n_seq = chunks_per_tile // cps
    assert b % n_seq == 0 and (s0 is None or n_seq == 1)
    nc = n_seq * cps
    tt = nc * DN_CHUNK
    nt = t // (cps * DN_CHUNK)
    n_tiles = (b // n_seq) * nt
    reverse = d == 1
    c2 = 2 * DN_CHUNK
    n_inst = nc * (DN_HEADS // 2)

    def seq(gt):
        bi, i = gt // nt, gt % nt
        return bi, ((nt - 1 - i) if reverse else i)

    def prep_tile(g):
        return seq(jnp.minimum(g, n_tiles - 1))

    def scan_tile(g):
        return seq(jnp.maximum(g - 1, 0))

    def row_block(bt):
        return bt[0] * nt + bt[1]

    qkv_spec = pl.BlockSpec((tt, DN_QK), lambda g: (row_block(prep_tile(g)), 0))
    in_specs = [
        qkv_spec, qkv_spec, qkv_spec,
        pl.BlockSpec((tt, LANES), lambda g: (row_block(prep_tile(g)), 0)),
        pl.BlockSpec((4 * DN_HEADS, tt), lambda g: (0, row_block(prep_tile(g)))),
    ]
    args = [qn, kn, vn, gb, gbt]
    if s0 is not None:
        state, layer = s0
        in_specs.append(pl.BlockSpec((None, None, None, DN_HEADS, DN_DK, DN_DV),
                                     lambda g: (scan_tile(g)[0], layer, d, 0, 0, 0)))
        args.append(state)
    out_specs = [pl.BlockSpec((tt, DN_VW), lambda g: (row_block(scan_tile(g)), 0))]
    out_shape = [jax.ShapeDtypeStruct((n, DN_VW), F32)]
    if want_state:
        out_specs.append(pl.BlockSpec((n_seq, DN_HEADS, DN_DK, DN_DV), lambda g: (scan_tile(g)[0], 0, 0, 0)))
        out_shape.append(jax.ShapeDtypeStruct((b, DN_HEADS, DN_DK, DN_DV), F32))
    prepared = [pltpu.VMEM((n_inst, DN_CHUNK, c2), F32), pltpu.VMEM((n_inst, c2, c2), BF16),
                pltpu.VMEM((n_inst, DN_CHUNK, c2), BF16), pltpu.VMEM((n_inst, DN_CHUNK, c2), BF16),
                pltpu.VMEM((n_inst, SUBLANES, c2), F32)]
    res = pl.pallas_call(
        functools.partial(_dn_kernel, reverse=reverse, has_s0=s0 is not None, want_state=want_state,
                          n_seq=n_seq, cps=cps, d=d, nt=nt),
        grid=(n_tiles + 1,),
        in_specs=in_specs,
        out_specs=out_specs,
        out_shape=out_shape,
        scratch_shapes=[pltpu.VMEM((n_seq * (DN_HEADS // 2), c2, c2), F32)] + prepared + prepared,
        compiler_params=_cparams(("arbitrary",)),
    )(*args)
    return res if want_state else (res[0], None)


def _post_kernel(x_ref, mod_ref, cb_ref, cc_ref, cx_ref, ccp_ref, cxp_ref, ccn_ref, cxn_ref, cw_ref,
                 att_ref, of_ref, ob_ref, z_ref, ng_ref, ga_ref, gb_ref, gc_ref, wpa_ref, wpb_ref, wpc_ref, wo_ref,
                 gpost_ref, o_ref, *, tm, tps):
    i = pl.program_id(0)
    first = (i % tps) == 0
    last = (i % tps) == tps - 1
    u = cc_ref[...].astype(F32) * cx_ref[...].astype(F32)
    hl = BF16_SUBLANES - 1
    prev_row = jnp.where(first, 0.0, ccp_ref[hl:hl + 1, :].astype(F32) * cxp_ref[hl:hl + 1, :].astype(F32))
    next_row = jnp.where(last, 0.0, ccn_ref[0:1, :].astype(F32) * cxn_ref[0:1, :].astype(F32))
    rows = lax.broadcasted_iota(jnp.int32, (tm, 1), 0)
    u_prev = jnp.where(rows == 0, prev_row, pltpu.roll(u, 1, axis=0))
    u_next = jnp.where(rows == tm - 1, next_row, pltpu.roll(u, tm - 1, axis=0))
    cw = cw_ref[...]
    conv = cw[0:1, :] * u_prev + cw[1:2, :] * u + cw[2:3, :] * u_next
    ya = _dot((cb_ref[...].astype(F32) * conv).astype(BF16), wpa_ref[...])
    yb = _dot(att_ref[...], wpb_ref[...])
    o = of_ref[...] + ob_ref[...]
    z = z_ref[...].astype(F32)
    parts = []
    for h in range(DN_HEADS):
        sl = slice(h * DN_DV, (h + 1) * DN_DV)
        parts.append((_rms(o[:, sl], ng_ref[...]) * _silu(z[:, sl])).astype(BF16))
    yc = _dot(jnp.concatenate(parts, axis=1), wpc_ref[...])
    mix_in = (_sigmoid(ga_ref[...].astype(F32)) * ya + _sigmoid(gb_ref[...].astype(F32)) * yb
              + _sigmoid(gc_ref[...].astype(F32)) * yc)
    mix = _dot(mix_in.astype(BF16), wo_ref[...])
    o_ref[...] = x_ref[...] + mod_ref[0, 2:3, :] * _rms(mix, gpost_ref[...])


def _post_mixer(x, mod3, mod_base, rows_per_mod, t, proj, att, o_f, o_b, conv_w, ng, wpa, wpb, wpc, wo, layer,
                g_post1):
    n = x.shape[0]
    tm = min(512, t)
    tps = t // tm
    hb = tm // BF16_SUBLANES
    cwid = CONV_WIDTH
    c0 = REST_CONV // cwid

    def prev(i):
        return jnp.maximum(i * hb - 1, 0)

    def nxt(i):
        return jnp.minimum((i + 1) * hb, n // BF16_SUBLANES - 1)

    g0 = REST_GATE // D_MODEL
    in_specs = [
        pl.BlockSpec((tm, D_MODEL), lambda i: (i, 0)),
        pl.BlockSpec((1, 6, D_MODEL), lambda i: (mod_base + (i * tm) // rows_per_mod, 0, 0)),
        pl.BlockSpec((tm, cwid), lambda i: (i, c0)),
        pl.BlockSpec((tm, cwid), lambda i: (i, c0 + 1)),
        pl.BlockSpec((tm, cwid), lambda i: (i, c0 + 2)),
        pl.BlockSpec((BF16_SUBLANES, cwid), lambda i: (prev(i), c0 + 1)),
        pl.BlockSpec((BF16_SUBLANES, cwid), lambda i: (prev(i), c0 + 2)),
        pl.BlockSpec((BF16_SUBLANES, cwid), lambda i: (nxt(i), c0 + 1)),
        pl.BlockSpec((BF16_SUBLANES, cwid), lambda i: (nxt(i), c0 + 2)),
        _const_spec((3, cwid)),
        pl.BlockSpec((tm, ATTN_Q), lambda i: (i, 0)),
        pl.BlockSpec((tm, DN_VW), lambda i: (i, 0)),
        pl.BlockSpec((tm, DN_VW), lambda i: (i, 0)),
        pl.BlockSpec((tm, DN_VW), lambda i: (i, REST_Z // DN_VW)),
        _const_spec((1, DN_DV)),
        pl.BlockSpec((tm, D_MODEL), lambda i: (i, g0)),
        pl.BlockSpec((tm, D_MODEL), lambda i: (i, g0 + 1)),
        pl.BlockSpec((tm, D_MODEL), lambda i: (i, g0 + 2)),
        _layer_spec((CONV_WIDTH, D_MODEL), layer), _layer_spec((ATTN_Q, D_MODEL), layer),
        _layer_spec((DN_VW, D_MODEL), layer), _layer_spec((D_MODEL, D_MODEL), layer),
        _const_spec((1, D_MODEL)),
    ]
    return pl.pallas_call(
        functools.partial(_post_kernel, tm=tm, tps=tps),
        grid=(n // tm,),
        in_specs=in_specs,
        out_specs=pl.BlockSpec((tm, D_MODEL), lambda i: (i, 0)),
        out_shape=jax.ShapeDtypeStruct((n, D_MODEL), F32),
        compiler_params=_cparams(("arbitrary",)),
    )(x, mod3, proj, proj, proj, proj, proj, proj, proj, conv_w, att, o_f, o_b, proj, ng,
      proj, proj, proj, wpa, wpb, wpc, wo, g_post1)


def _ffn_kernel(x_ref, mod_ref, gpre_ref, wg_ref, wu_ref, wd_ref, gpost_ref, o_ref, act_sc, *, tf):
    x = x_ref[...]
    h2 = (_rms(x, gpre_ref[...]) * (1.0 + mod_ref[0, 4:5, :]) + mod_ref[0, 3:4, :]).astype(BF16)
    for j in range(0, D_FF, tf):
        gate = _dot(h2, wg_ref[:, j:j + tf])
        up = _dot(h2, wu_ref[:, j:j + tf])
        act_sc[:, j:j + tf] = (_silu(gate) * up).astype(BF16)
    ffn = _dot(act_sc[...], wd_ref[...])
    o_ref[...] = x + mod_ref[0, 5:6, :] * _rms(ffn, gpost_ref[...])


def _ffn(x, mod3, mod_base, rows_per_mod, g_pre2, wg, wu, wd, layer, g_post2):
    n = x.shape[0]
    tm = min(1024, n)
    assert n % tm == 0 and rows_per_mod % tm == 0
    return pl.pallas_call(
        functools.partial(_ffn_kernel, tf=256),
        grid=(n // tm,),
        in_specs=[pl.BlockSpec((tm, D_MODEL), lambda i: (i, 0)),
                  pl.BlockSpec((1, 6, D_MODEL), lambda i: (mod_base + (i * tm) // rows_per_mod, 0, 0)),
                  _const_spec((1, D_MODEL)),
                  _layer_spec((D_MODEL, D_FF), layer), _layer_spec((D_MODEL, D_FF), layer),
                  _layer_spec((D_FF, D_MODEL), layer),
                  _const_spec((1, D_MODEL))],
        out_specs=pl.BlockSpec((tm, D_MODEL), lambda i: (i, 0)),
        out_shape=jax.ShapeDtypeStruct((n, D_MODEL), F32),
        scratch_shapes=[pltpu.VMEM((tm, D_FF), BF16)],
        compiler_params=_cparams(("arbitrary",)),
    )(x, mod3, g_pre2, wg, wu, wd, g_post2)


def _rope_tables(t):
    rows = t // GRID_W
    row_id = np.repeat(np.arange(rows, dtype=np.float32), GRID_W)
    col_id = np.tile(np.arange(GRID_W, dtype=np.float32), rows)
    n_freq = HEAD_DIM // 4
    inv_freq = (np.float32(ROPE_THETA) ** (-np.arange(n_freq, dtype=np.float32) / np.float32(n_freq))).astype(np.float32)
    ang = np.concatenate([row_id[:, None] * inv_freq, col_id[:, None] * inv_freq], axis=-1).astype(np.float32)
    cos = np.repeat(np.cos(ang), 2, axis=-1).astype(np.float32)
    sin = np.repeat(np.sin(ang), 2, axis=-1).astype(np.float32)
    sign = np.tile(np.array([-1.0, 1.0], np.float32), HEAD_DIM // 2)
    return jnp.asarray(cos), jnp.asarray(sin * sign)


def _layer(x, b, t, mod3, mod_base, rows_per_mod, lw, rope_tabs, cache, state0, want_state):
    q_p, k_p, vt_p, v_f32, qn, kn, vn, gb, gbt, proj = _inproj(x, t, mod3, mod_base, rows_per_mod, lw,
                                                               rope_tabs, want_state)
    att = _attention(q_p, k_p, vt_p, b, t, cache)
    dn = [_deltanet_dir(qn, kn, vn, gb, gbt, state0, b, t, d, want_state) for d in range(2)]
    x = _post_mixer(x, mod3, mod_base, rows_per_mod, t, proj, att, dn[0][0], dn[1][0], lw["conv_w"],
                    lw["dn_norm_g"], lw["w_pa"], lw["w_pb"], lw["w_pc"], lw["w_o"], lw["layer"], lw["g_post1"])
    x = _ffn(x, mod3, mod_base, rows_per_mod, lw["g_pre2"], lw["w_gate"], lw["w_up"], lw["w_down"],
             lw["layer"], lw["g_post2"])
    if not want_state:
        return x, None
    s_fin = jnp.stack([dn[0][1], dn[1][1]], axis=1)
    return x, (k_p, v_f32, s_fin)


def kernel(x_prompt, x_sample, cache_k, cache_v, state_dn, c, c_ctx, w_mod, b_mod, g_pre1, g_post1, g_pre2, g_post2, w_in, conv_w, g_qn, g_kn, dn_conv_w, dn_a_log, dn_dt_bias, dn_norm_g, w_pa, w_pb, w_pc, w_o, w_gate, w_up, w_down):
    bp, tp, d = x_prompt.shape
    bs, ts, _ = x_sample.shape
    depth = w_mod.shape[0]
    past = cache_k.shape[2]

    mod_rows = -(-(bs + 1) // SUBLANES) * SUBLANES
    cv = jnp.zeros((mod_rows, d), F32).at[:bs].set(c).at[bs].set(c_ctx)
    mod_all = _modulation(cv, w_mod, b_mod).reshape(depth, mod_rows, 6, d)

    stacked = {"w_pa": w_pa.astype(BF16), "w_pb": w_pb.astype(BF16), "w_pc": w_pc.astype(BF16),
               "w_o": w_o.astype(BF16), "w_gate": w_gate.astype(BF16), "w_up": w_up.astype(BF16),
               "w_down": w_down.astype(BF16)}
    w_main = w_in.astype(BF16)
    w_tail = jnp.concatenate(
        [w_in[:, :, COL_GATE:D_IN], w_in[:, :, COL_SMALL:COL_GATE],
         jnp.zeros((depth, d, LANES - 4 * DN_HEADS), w_in.dtype)], axis=-1).astype(BF16)
    lane_pad = ((0, 0), (2 * DN_HEADS, LANES - 4 * DN_HEADS))
    a_flat = dn_a_log.reshape(depth, 2 * DN_HEADS)
    b_flat = dn_dt_bias.reshape(depth, 2 * DN_HEADS)
    rope_tabs = _rope_tables(ts)
    cache_k4 = cache_k.reshape(bs, depth, past, ATTN_KV)
    cache_v4 = cache_v.reshape(bs, depth, past, ATTN_KV)

    xp = x_prompt.reshape(bp * tp, d)
    xs = x_sample.reshape(bs * ts, d)
    ks, vs, ss = [], [], []
    for l in range(depth):
        lw = {
            "g_pre1": g_pre1[l][None], "g_post1": g_post1[l][None], "g_pre2": g_pre2[l][None],
            "g_post2": g_post2[l][None], "layer": l, "w_in": w_main, "w_tail": w_tail, "conv_w": conv_w[l],
            "g_qn": g_qn[l][None], "g_kn": g_kn[l][None], "dn_conv_w": dn_conv_w[l],
            "dn_ac": jnp.pad(a_flat[l][None], lane_pad), "dn_bc": jnp.pad(b_flat[l][None], lane_pad),
            "dn_norm_g": dn_norm_g[l][None],
            **stacked,
        }
        mod3 = mod_all[l]
        xp, (k_l, v_l, s_l) = _layer(xp, bp, tp, mod3, bs, bp * tp, lw, None, None, None, True)
        ks.append(k_l.reshape(bp, tp, N_KV_HEADS, HEAD_DIM))
        vs.append(v_l.reshape(bp, tp, N_KV_HEADS, HEAD_DIM))
        ss.append(s_l)
        xs, _ = _layer(xs, bs, ts, mod3, 0, ts, lw, rope_tabs, (cache_k4, cache_v4, l), (state_dn, l), False)
    return (xp.reshape(bp, tp, d), xs.reshape(bs, ts, d), jnp.stack(ks, axis=1), jnp.stack(vs, axis=1),
            jnp.stack(ss, axis=1))
```

```python
import functools

import jax
import jax.numpy as jnp
import numpy as np
from jax import lax
from jax.experimental import pallas as pl
from jax.experimental.pallas import tpu as pltpu

F32 = jnp.float32
BF16 = jnp.bfloat16

D_MODEL = 1024
EPS = 1e-6
GRID_W = 64
N_HEADS = 8
N_KV_HEADS = 2
Q_PER_KV = N_HEADS // N_KV_HEADS
HEAD_DIM = 128
ATTN_Q = N_HEADS * HEAD_DIM
ATTN_KV = N_KV_HEADS * HEAD_DIM
ROPE_THETA = 10000.0
CONV_WIDTH = 512
DN_HEADS = 4
DN_DK = 128
DN_DV = 128
DN_QK = DN_HEADS * DN_DK
DN_VW = DN_HEADS * DN_DV
D_FF = 2816

SUBLANES = 8
BF16_SUBLANES = 16
LANES = 128

COL_CONV = 0
COL_Q = COL_CONV + 3 * CONV_WIDTH
COL_K = COL_Q + ATTN_Q
COL_V = COL_K + ATTN_KV
COL_DN = COL_V + ATTN_KV
COL_Z = COL_DN + 3 * DN_QK
COL_SMALL = COL_Z + DN_VW
COL_GATE = COL_SMALL + 4 * DN_HEADS
D_IN = COL_GATE + 3 * D_MODEL
TAIL_GATE = 0
TAIL_SMALL = TAIL_GATE + 3 * D_MODEL
D_TAIL = TAIL_SMALL + LANES
REST_GATE = 0
REST_CONV = REST_GATE + 3 * D_MODEL
REST_Z = REST_CONV + 3 * CONV_WIDTH
D_REST = REST_Z + DN_VW

Q_SCALE = 1.4426950408889634 * HEAD_DIM ** -0.5
DN_CHUNK = 128
DN_BASE = 16
VMEM_LIMIT = 56 * 1024 * 1024


def _cparams(sem):
    return pltpu.CompilerParams(dimension_semantics=sem, vmem_limit_bytes=VMEM_LIMIT)


def _const_spec(shape):
    nd = len(shape)
    return pl.BlockSpec(shape, lambda *_: (0,) * nd, pipeline_mode=pl.Buffered(1))


def _layer_spec(shape, layer):
    return pl.BlockSpec((None,) + tuple(shape), lambda *_: (layer,) + (0,) * len(shape),
                        pipeline_mode=pl.Buffered(1))


def _dot(a, b):
    return jnp.dot(a, b, preferred_element_type=F32)


def _dot_nt(a, b):
    return lax.dot_general(a, b, (((1,), (1,)), ((), ())), preferred_element_type=F32)


def _dot_tn(a, b):
    return lax.dot_general(a, b, (((0,), (0,)), ((), ())), preferred_element_type=F32)


def _rms(x, g):
    return x * lax.rsqrt(jnp.mean(x * x, axis=-1, keepdims=True) + EPS) * g


def _sigmoid(x):
    return 1.0 / (1.0 + jnp.exp(-x))


def _silu(x):
    return x * _sigmoid(x)


def _softplus(x):
    return jnp.maximum(x, 0.0) + jnp.log1p(jnp.exp(-jnp.abs(x)))


def _split3(x):
    hi = x.astype(BF16)
    r = x - hi.astype(F32)
    mid = r.astype(BF16)
    lo = (r - mid.astype(F32)).astype(BF16)
    return hi, mid, lo


def _mod_kernel(cv_ref, w_ref, b_ref, o_ref):
    cv = cv_ref[...]
    o_ref[0] = _dot(_silu(cv).astype(BF16), w_ref[0].astype(BF16)) + b_ref[0]


def _modulation(cv, w_mod, b_mod):
    depth, d, n6 = w_mod.shape
    rows = cv.shape[0]
    tn = 1536
    return pl.pallas_call(
        _mod_kernel,
        grid=(depth, n6 // tn),
        in_specs=[pl.BlockSpec((rows, d), lambda l, j: (0, 0)),
                  pl.BlockSpec((1, d, tn), lambda l, j: (l, 0, j)),
                  pl.BlockSpec((1, 1, tn), lambda l, j: (l, 0, j))],
        out_specs=pl.BlockSpec((1, rows, tn), lambda l, j: (l, 0, j)),
        out_shape=jax.ShapeDtypeStruct((depth, rows, n6), F32),
        compiler_params=_cparams(("arbitrary", "arbitrary")),
    )(cv, w_mod, b_mod.reshape(depth, 1, n6))


def _swap_pairs(y):
    lane = lax.broadcasted_iota(jnp.int32, y.shape, 1)
    return jnp.where(lane % 2 == 0, pltpu.roll(y, LANES - 1, axis=1), pltpu.roll(y, 1, axis=1))


def _interleave(*gens):
    gens = list(gens)
    while gens:
        for gen in list(gens):
            try:
                next(gen)
            except StopIteration:
                gens.remove(gen)


def _inproj_kernel(*refs, rope, want_v, tm, tps, tn):
    refs = list(refs)
    x_ref, xp_ref, xn_ref, mod_ref, g_ref, w_ref, wt_ref, gq_ref, gk_ref, cw_ref, ac_ref, bc_ref = refs[:12]
    refs = refs[12:]
    cos_ref, sin_ref = (refs.pop(0), refs.pop(0)) if rope else (None, None)
    q_out, k_out, v_out = refs[:3]
    refs = refs[3:]
    vf_out = refs.pop(0) if want_v else None
    qn_out, kn_out, vn_out, gb_out, gbt_out, rest_out = refs
    i = pl.program_id(0)

    def modnorm(x):
        return (_rms(x, g_ref[...]) * (1.0 + mod_ref[0, 1:2, :]) + mod_ref[0, 0:1, :]).astype(BF16)

    h = modnorm(x_ref[...])
    h_halo = modnorm(jnp.concatenate([xp_ref[...], xn_ref[...]], axis=0))

    def qk_head(x, g):
        y = _rms(x, g)
        if rope:
            y = y * cos_ref[...] + _swap_pairs(y) * sin_ref[...]
        return y

    def attn_epilogue():
        qkv = _dot(h, w_ref[:, COL_Q:COL_DN])
        yield
        for hh in range(N_HEADS):
            sl = slice(hh * HEAD_DIM, (hh + 1) * HEAD_DIM)
            q_out[:, sl] = (qk_head(qkv[:, sl], gq_ref[...]) * Q_SCALE).astype(q_out.dtype)
            if hh % 2 == 1:
                yield
        for hh in range(N_KV_HEADS):
            sl = slice(hh * HEAD_DIM, (hh + 1) * HEAD_DIM)
            k_out[:, sl] = qk_head(qkv[:, ATTN_Q + hh * HEAD_DIM:ATTN_Q + (hh + 1) * HEAD_DIM],
                                   gk_ref[...]).astype(k_out.dtype)
        v = qkv[:, ATTN_Q + ATTN_KV:]
        v_out[...] = v.T.astype(v_out.dtype)
        if want_v:
            vf_out[...] = v

    def dn_epilogue():
        dn = _dot(h, w_ref[:, COL_DN:COL_Z])
        dn_halo = _dot(h_halo, w_ref[:, COL_DN:COL_Z])
        gl = _dot(h, wt_ref[:, TAIL_SMALL:D_TAIL])
        yield
        rows = lax.broadcasted_iota(jnp.int32, (tm, 1), 0)
        prev_row = jnp.where((i % tps) == 0, 0.0, dn_halo[SUBLANES - 1:SUBLANES, :])
        next_row = jnp.where((i % tps) == tps - 1, 0.0, dn_halo[SUBLANES:SUBLANES + 1, :])
        dn_prev = jnp.where(rows == 0, prev_row, pltpu.roll(dn, 1, axis=0))
        dn_next = jnp.where(rows == tm - 1, next_row, pltpu.roll(dn, tm - 1, axis=0))
        cw = cw_ref[...]

        def conv_silu(sl):
            return _silu(cw[0:1, sl] * dn_prev[:, sl] + cw[1:2, sl] * dn[:, sl] + cw[2:3, sl] * dn_next[:, sl])

        for hh in range(DN_HEADS):
            sl = slice(hh * DN_DK, (hh + 1) * DN_DK)
            qh = conv_silu(sl)
            kh = conv_silu(slice(DN_QK + hh * DN_DK, DN_QK + (hh + 1) * DN_DK))
            qn_out[:, sl] = (qh * lax.rsqrt(jnp.sum(qh * qh, axis=-1, keepdims=True) + EPS)
                             * (DN_DK ** -0.5)).astype(qn_out.dtype)
            kn_out[:, sl] = (kh * lax.rsqrt(jnp.sum(kh * kh, axis=-1, keepdims=True) + EPS)).astype(kn_out.dtype)
            vn_out[:, sl] = conv_silu(slice(2 * DN_QK + hh * DN_DV, 2 * DN_QK + (hh + 1) * DN_DV)
                                      ).astype(vn_out.dtype)
            yield
        lane = lax.broadcasted_iota(jnp.int32, gl.shape, 1)
        gb = jnp.where(lane < 2 * DN_HEADS, _sigmoid(gl), -jnp.exp(ac_ref[...]) * _softplus(gl + bc_ref[...]))
        gb_out[...] = gb
        gbt_out[...] = gb.T

    def remaining():
        groups = [(wt_ref, TAIL_GATE, REST_GATE, 3 * D_MODEL), (w_ref, COL_CONV, REST_CONV, 3 * CONV_WIDTH),
                  (w_ref, COL_Z, REST_Z, DN_VW)]
        for ref, src, dst, width in groups:
            for j in range(0, width, tn):
                rest_out[:, dst + j:dst + j + tn] = _dot(h, ref[:, src + j:src + j + tn]).astype(rest_out.dtype)
                yield

    _interleave(dn_epilogue(), attn_epilogue(), remaining())


def _inproj(x, t, mod3, mod_base, rows_per_mod, lw, rope_tabs, want_v):
    n = x.shape[0]
    tm = min(512, t)
    tps = t // tm
    halo = tm // SUBLANES
    w_main, w_tail, layer = lw["w_in"], lw["w_tail"], lw["layer"]
    in_specs = [pl.BlockSpec((tm, D_MODEL), lambda i: (i, 0)),
                pl.BlockSpec((SUBLANES, D_MODEL), lambda i: (jnp.maximum(i * halo - 1, 0), 0)),
                pl.BlockSpec((SUBLANES, D_MODEL), lambda i: (jnp.minimum((i + 1) * halo, n // SUBLANES - 1), 0)),
                pl.BlockSpec((1, 6, D_MODEL), lambda i: (mod_base + (i * tm) // rows_per_mod, 0, 0)),
                _const_spec((1, D_MODEL)),
                _layer_spec((D_MODEL, COL_SMALL), layer), _layer_spec((D_MODEL, D_TAIL), layer),
                _const_spec((1, HEAD_DIM)), _const_spec((1, HEAD_DIM)),
                _const_spec((3, 3 * DN_QK)), _const_spec((1, LANES)), _const_spec((1, LANES))]
    args = [x, x, x, mod3, lw["g_pre1"], w_main, w_tail, lw["g_qn"], lw["g_kn"], lw["dn_conv_w"],
            lw["dn_ac"], lw["dn_bc"]]
    if rope_tabs is not None:
        in_specs += [pl.BlockSpec((tm, HEAD_DIM), lambda i: (i % tps, 0))] * 2
        args += list(rope_tabs)

    outs = [(ATTN_Q, BF16, False), (ATTN_KV, F32 if want_v else BF16, False), (ATTN_KV, BF16, True)]
    if want_v:
        outs.append((ATTN_KV, F32, False))
    outs += [(DN_QK, BF16, False), (DN_QK, BF16, False), (DN_VW, BF16, False), (LANES, F32, False),
             (LANES, F32, True), (D_REST, BF16, False)]
    res = pl.pallas_call(
        functools.partial(_inproj_kernel, rope=rope_tabs is not None, want_v=want_v, tm=tm, tps=tps, tn=512),
        grid=(n // tm,),
        in_specs=in_specs,
        out_specs=[pl.BlockSpec((w, tm), lambda i: (0, i)) if tr else pl.BlockSpec((tm, w), lambda i: (i, 0))
                   for w, _, tr in outs],
        out_shape=[jax.ShapeDtypeStruct((w, n) if tr else (n, w), dt) for w, dt, tr in outs],
        compiler_params=_cparams(("arbitrary",)),
    )(*args)
    res = list(res)
    q_p, k_p, vt_p = res[:3]
    v_f32 = res[3] if want_v else None
    qn, kn, vn, gb, gbt, rest = res[-6:]
    return q_p, k_p, vt_p, v_f32, qn, kn, vn, gb, gbt, rest


def _attn_kernel(*refs, has_cache, t, tk, tq, nq, per_step):
    refs = list(refs)
    q_ref = refs.pop(0)
    qn_ref = refs.pop(0) if nq > 1 else None
    kc_ref, vc_ref = (refs.pop(0), refs.pop(0)) if has_cache else (None, None)
    k_ref, v_ref, o_ref = refs[:3]
    refs = refs[3:]
    vct_sc = refs.pop() if has_cache else None
    s_bufs, m_bufs = refs[:len(refs) // 2], refs[len(refs) // 2:]
    qi = pl.program_id(2)

    if has_cache:
        @pl.when(qi == 0)
        def _():
            vct_sc[...] = vc_ref[...].T.astype(BF16)

    segs = []
    if has_cache:
        past = kc_ref.shape[0]
        segs += [(kc_ref, vct_sc, r, min(tk, past - r)) for r in range(0, past, tk)]
    segs += [(k_ref, v_ref, r, tk) for r in range(0, t, tk)]
    offs = [sum(w for _, _, _, w in segs[:i]) for i in range(len(segs))]

    def stack(ref, row0=0):
        q = ref[row0:row0 + tq, :]
        return jnp.concatenate([q[:, g * HEAD_DIM:(g + 1) * HEAD_DIM] for g in range(Q_PER_KV)], axis=0)

    def scores(qs, s_ref, i, m_run):
        kr, _, r, w = segs[i]
        s = _dot_nt(kr[r:r + w, :].astype(BF16), qs)
        s_ref[offs[i]:offs[i] + w, :] = s
        for r0 in range(0, w, SUBLANES):
            blk = s[r0:r0 + SUBLANES, :]
            m_run = blk if m_run is None else jnp.maximum(m_run, blk)
        return m_run

    def weighted(s_ref, i, m, acc):
        _, vr, r, w = segs[i]
        p = jnp.exp2(s_ref[offs[i]:offs[i] + w, :] - m)
        den = p[0:SUBLANES, :]
        for r0 in range(SUBLANES, w, SUBLANES):
            den = den + p[r0:r0 + SUBLANES, :]
        pv = _dot(vr[:, r:r + w].astype(BF16), p.astype(BF16))
        return (pv, den) if acc is None else (acc[0] + pv, acc[1] + den)

    def finish(acc, row0=0):
        o = (acc[0] / jnp.sum(acc[1], axis=0, keepdims=True)).T
        for g in range(Q_PER_KV):
            o_ref[row0:row0 + tq, g * HEAD_DIM:(g + 1) * HEAD_DIM] = o[g * tq:(g + 1) * tq].astype(o_ref.dtype)

    if nq > 1:
        @pl.when(qi == 0)
        def _():
            qs0 = stack(q_ref)
            m_run = None
            for i in range(len(segs)):
                m_run = scores(qs0, s_bufs[0], i, m_run)
            m_bufs[0][...] = m_run

        def step(s_cur, m_cur, s_nxt, m_nxt, qs_next, row0):
            m = jnp.max(m_cur[...], axis=0, keepdims=True)
            acc, m_run = None, None
            for i in range(len(segs)):
                m_run = scores(qs_next, s_nxt, i, m_run)
                acc = weighted(s_cur, i, m, acc)
            m_nxt[...] = m_run
            finish(acc, row0)

        for u in range(per_step):
            qs_next = stack(q_ref, (u + 1) * tq) if u + 1 < per_step else stack(qn_ref)
            step(s_bufs[u % 2], m_bufs[u % 2], s_bufs[1 - u % 2], m_bufs[1 - u % 2], qs_next, u * tq)
    else:
        qs = stack(q_ref)
        m_run = None
        for i in range(len(segs)):
            m_run = scores(qs, s_bufs[0], i, m_run)
        m = jnp.max(m_run, axis=0, keepdims=True)
        acc = None
        for i in range(len(segs)):
            acc = weighted(s_bufs[0], i, m, acc)
        finish(acc)


def _attention(q_p, k_p, vt_p, b, t, cache):
    n = q_p.shape[0]
    tq = 128 if t > 256 else 256
    tk = min(t, 512)
    nq = t // tq
    per_step = 4 if nq % 4 == 0 else (2 if nq % 2 == 0 else 1)
    assert nq == 1 or per_step > 1
    steps = nq // per_step
    qw = Q_PER_KV * HEAD_DIM
    in_specs = [pl.BlockSpec((per_step * tq, qw), lambda bi, j, qi: (bi * steps + qi, j))]
    args = [q_p]
    if nq > 1:
        in_specs.append(pl.BlockSpec((tq, qw),
                                     lambda bi, j, qi: (bi * nq + jnp.minimum(per_step * (qi + 1), nq - 1), j)))
        args.append(q_p)
    if cache is not None:
        cache_k, cache_v, layer = cache
        past = cache_k.shape[2]
        cspec = pl.BlockSpec((None, None, past, HEAD_DIM), lambda bi, j, qi: (bi, layer, 0, j))
        in_specs += [cspec, cspec]
        args += [cache_k, cache_v]
    in_specs += [pl.BlockSpec((t, HEAD_DIM), lambda bi, j, qi: (bi, j)),
                 pl.BlockSpec((HEAD_DIM, t), lambda bi, j, qi: (j, bi))]
    args += [k_p, vt_p]
    n_keys = t + (cache[0].shape[2] if cache is not None else 0)
    slots = 2 if nq > 1 else 1
    return pl.pallas_call(
        functools.partial(_attn_kernel, has_cache=cache is not None, t=t, tk=tk, tq=tq, nq=nq,
                          per_step=per_step),
        grid=(b, N_KV_HEADS, steps),
        in_specs=in_specs,
        out_specs=pl.BlockSpec((per_step * tq, qw), lambda bi, j, qi: (bi * steps + qi, j)),
        out_shape=jax.ShapeDtypeStruct((n, ATTN_Q), BF16),
        scratch_shapes=([pltpu.VMEM((n_keys, Q_PER_KV * tq), F32)] * slots
                        + [pltpu.VMEM((SUBLANES, Q_PER_KV * tq), F32)] * slots
                        + ([pltpu.VMEM((HEAD_DIM, cache[0].shape[2]), BF16)] if cache is not None else [])),
        compiler_params=_cparams(("arbitrary", "arbitrary", "arbitrary")),
    )(*args)


def _lane_pick(x, lane):
    idx = lax.broadcasted_iota(jnp.int32, x.shape, 1)
    return jnp.sum(jnp.where(idx == lane, x, 0.0), axis=-1, keepdims=True)


def _block_diag(x2):
    xb = x2.astype(BF16)
    z = jnp.zeros((DN_CHUNK, DN_CHUNK), BF16)
    return jnp.concatenate([jnp.concatenate([xb[:, :DN_CHUNK], z], axis=1),
                            jnp.concatenate([z, xb[:, DN_CHUNK:]], axis=1)], axis=0)


def _mm_pair(x2, y2):
    return _dot(x2.astype(BF16), _block_diag(y2))


def _unit_tri_inverses(lmats, ri, ci):
    def blk(s):
        return (ri ^ ci) < s

    eye = jnp.where(ri == ci, 1.0, 0.0)
    ps = [jnp.where(blk(DN_BASE), -lm, 0.0) for lm in lmats]
    xs = [eye + p for p in ps]
    s = 2
    while s < DN_BASE:
        ps = [_mm_pair(p, p) for p in ps]
        yield
        xs = [x + _mm_pair(x, p) for x, p in zip(xs, ps)]
        yield
        s *= 2
    s = DN_BASE
    while s < DN_CHUNK:
        sel = blk(2 * s) & jnp.logical_not(blk(s))
        ts = [_mm_pair(jnp.where(sel, lm, 0.0), x) for lm, x in zip(lmats, xs)]
        yield
        xs = [x - _mm_pair(x, t) for x, t in zip(xs, ts)]
        yield
        s *= 2
    return xs


def _dn_kernel(*refs, reverse, has_s0, want_state, n_seq, cps, d, nt):
    nc = n_seq * cps
    refs = list(refs)
    q_ref, k_ref, v_ref, gb_ref, gr_ref = refs[:5]
    refs = refs[5:]
    s0_ref = refs.pop(0) if has_s0 else None
    o_ref = refs.pop(0)
    sfin_ref = refs.pop(0) if want_state else None
    s_sc = refs.pop(0)
    bufs = (refs[:5], refs[5:10])
    n_pairs = DN_HEADS // 2
    c2 = 2 * DN_CHUNK
    tt = nc * DN_CHUNK
    insts = [(c, p) for c in range(nc) for p in range(n_pairs)]

    g = pl.program_id(0)
    i_scan = (g - 1) % nt

    def pair_cols(x, lane_a, lane_b):
        shape = (x.shape[0], DN_CHUNK)
        return jnp.concatenate([jnp.broadcast_to(_lane_pick(x, lane_a), shape),
                                jnp.broadcast_to(_lane_pick(x, lane_b), shape)], axis=1)

    def prepare(buf):
        u_buf, wq_buf, attn_buf, kt_buf, gt_buf = buf
        gb = gb_ref[...]
        g_r = gr_ref[...]
        bi = lax.broadcasted_iota(jnp.int32, (tt, tt), 0)
        bj = lax.broadcasted_iota(jnp.int32, (tt, tt), 1)
        same = (bi ^ bj) < DN_CHUNK
        if reverse:
            tri_c = jnp.where(same & (bj >= bi), 1.0, 0.0).astype(BF16)
            tri_r = jnp.where(same & (bi >= bj), 1.0, 0.0).astype(BF16)
        else:
            tri_c = jnp.where(same & (bj <= bi), 1.0, 0.0).astype(BF16)
            tri_r = jnp.where(same & (bi <= bj), 1.0, 0.0).astype(BF16)
        gcum_c = sum(_dot(tri_c, part) for part in _split3(gb))
        gcum_r = sum(_dot(part, tri_r) for part in _split3(g_r))
        yield

        ri = lax.broadcasted_iota(jnp.int32, (DN_CHUNK, c2), 0)
        ci = lax.broadcasted_iota(jnp.int32, (DN_CHUNK, c2), 1) & (DN_CHUNK - 1)
        incl = (ci >= ri) if reverse else (ci <= ri)
        strict = (ci > ri) if reverse else (ci < ri)
        pre = []
        for c, p in insts:
            rs = slice(c * DN_CHUNK, (c + 1) * DN_CHUNK)
            cs = slice(p * c2, (p + 1) * c2)
            lane_b = d * DN_HEADS + 2 * p
            lane_g = 2 * DN_HEADS + d * DN_HEADS + 2 * p
            q2 = q_ref[rs, cs].astype(F32)
            k2 = k_ref[rs, cs].astype(F32)
            v2 = v_ref[rs, cs].astype(F32)
            g_i = pair_cols(gcum_c[rs, :], lane_g, lane_g + 1)
            b_i = pair_cols(gb[rs, :], lane_b, lane_b + 1)
            g_j = jnp.concatenate([gcum_r[lane_g:lane_g + 1, rs], gcum_r[lane_g + 1:lane_g + 2, rs]], axis=1)
            g_tot = g_i[0:1, :] if reverse else g_i[DN_CHUNK - 1:DN_CHUNK, :]
            decay = jnp.where(incl, jnp.exp(jnp.where(incl, g_i - g_j, 0.0)), 0.0)
            pre.append(dict(q2=q2, k2=k2, v2=v2, g_i=g_i, b_i=b_i, g_tot=g_tot, decay=decay, kb=k2 * b_i))
        a2s = [_dot_nt(jnp.concatenate([p["kb"], p["q2"]], axis=0).astype(BF16), _block_diag(p["k2"]))
               for p in pre]
        yield
        lmats = [jnp.where(strict, a2[:DN_CHUNK] * p["decay"], 0.0) for a2, p in zip(a2s, pre)]
        for n, (a2, p) in enumerate(zip(a2s, pre)):
            attn_buf[n] = (a2[DN_CHUNK:] * p["decay"]).astype(BF16)
        tinvs = yield from _unit_tri_inverses(lmats, ri, ci)
        gams = [jnp.exp(p["g_i"]) for p in pre]
        us = [_mm_pair(t, p["v2"] * p["b_i"]) for t, p in zip(tinvs, pre)]
        yield
        ws = [_mm_pair(t, p["kb"] * gam) for t, p, gam in zip(tinvs, pre, gams)]
        yield
        for n, (p, u, w, gam) in enumerate(zip(pre, us, ws, gams)):
            u_buf[n] = u
            wq_buf[n] = jnp.concatenate([w, p["q2"] * gam], axis=0).astype(BF16)
            kt_buf[n] = (p["k2"] * jnp.exp(p["g_tot"] - p["g_i"])).astype(BF16)
            gt_buf[n] = jnp.broadcast_to(jnp.exp(p["g_tot"]), (SUBLANES, c2))

    def scan(buf):
        u_buf, wq_buf, attn_buf, kt_buf, gt_buf = buf
        si = lax.broadcasted_iota(jnp.int32, (c2, c2), 0)
        sj = lax.broadcasted_iota(jnp.int32, (c2, c2), 1)
        on_diag = (si < DN_CHUNK) == (sj < DN_CHUNK)
        chains = [(sq, p) for sq in range(n_seq) for p in range(n_pairs)]
        states = [s_sc[sq * n_pairs + p] for sq, p in chains]
        for k in (range(cps - 1, -1, -1) if reverse else range(cps)):
            cs = [sq * cps + k for sq, _ in chains]
            ns = [c * n_pairs + p for c, (_, p) in zip(cs, chains)]
            m1s = [_dot(wq_buf[n], s.astype(BF16)) for n, s in zip(ns, states)]
            yield
            v_news = [u_buf[n] - m1[:DN_CHUNK] for n, m1 in zip(ns, m1s)]
            outs = [m1[DN_CHUNK:] + _mm_pair(attn_buf[n], v) for n, m1, v in zip(ns, m1s, v_news)]
            yield
            states = [s * gt_buf[n][0:1, :] + jnp.where(on_diag, _dot_tn(kt_buf[n], v.astype(BF16)), 0.0)
                      for n, s, v in zip(ns, states, v_news)]
            for c, (_, p), out in zip(cs, chains, outs):
                o_ref[c * DN_CHUNK:(c + 1) * DN_CHUNK, p * c2:(p + 1) * c2] = out
            yield
        for (sq, p), s in zip(chains, states):
            s_sc[sq * n_pairs + p] = s

    def step(par):
        @pl.when(i_scan == 0)
        def _():
            s_sc[...] = jnp.zeros_like(s_sc)
            if has_s0:
                for h in range(DN_HEADS):
                    lo = (h % 2) * DN_CHUNK
                    s_sc[h // 2, lo:lo + DN_CHUNK, lo:lo + DN_CHUNK] = s0_ref[h]

        _interleave(prepare(bufs[par]), scan(bufs[1 - par]))

        if want_state:
            @pl.when(i_scan == nt - 1)
            def _():
                for sq in range(n_seq):
                    for h in range(DN_HEADS):
                        lo = (h % 2) * DN_CHUNK
                        sfin_ref[sq, h] = s_sc[sq * n_pairs + h // 2, lo:lo + DN_CHUNK, lo:lo + DN_CHUNK]

    pl.when(g == 0)(lambda: _interleave(prepare(bufs[0])))
    pl.when((g > 0) & (g % 2 == 0))(functools.partial(step, 0))
    pl.when(g % 2 == 1)(functools.partial(step, 1))


def _deltanet_dir(qn, kn, vn, gb, gbt, s0, b, t, d, want_state):
    n = qn.shape[0]
    chunks_per_tile = 4
    cps = min(chunks_per_tile, t // DN_CHUNK)
    n_seq = chunks_per_tile // cps
    assert b % n_seq == 0 and (s0 is None or n_seq == 1)
    nc = n_seq * cps
    tt = nc * DN_CHUNK
    nt = t // (cps * DN_CHUNK)
    n_tiles = (b // n_seq) * nt
    reverse = d == 1
    c2 = 2 * DN_CHUNK
    n_inst = nc * (DN_HEADS // 2)

    def seq(gt):
        bi, i = gt // nt, gt % nt
        return bi, ((nt - 1 - i) if reverse else i)

    def prep_tile(g):
        return seq(jnp.minimum(g, n_tiles - 1))

    def scan_tile(g):
        return seq(jnp.maximum(g - 1, 0))

    def row_block(bt):
        return bt[0] * nt + bt[1]

    qkv_spec = pl.BlockSpec((tt, DN_QK), lambda g: (row_block(prep_tile(g)), 0))
    in_specs = [
        qkv_spec, qkv_spec, qkv_spec,
        pl.BlockSpec((tt, LANES), lambda g: (row_block(prep_tile(g)), 0)),
        pl.BlockSpec((4 * DN_HEADS, tt), lambda g: (0, row_block(prep_tile(g)))),
    ]
    args = [qn, kn, vn, gb, gbt]
    if s0 is not None:
        state, layer = s0
        in_specs.append(pl.BlockSpec((None, None, None, DN_HEADS, DN_DK, DN_DV),
                                     lambda g: (scan_tile(g)[0], layer, d, 0, 0, 0)))
        args.append(state)
    out_specs = [pl.BlockSpec((tt, DN_VW), lambda g: (row_block(scan_tile(g)), 0))]
    out_shape = [jax.ShapeDtypeStruct((n, DN_VW), F32)]
    if want_state:
        out_specs.append(pl.BlockSpec((n_seq, DN_HEADS, DN_DK, DN_DV), lambda g: (scan_tile(g)[0], 0, 0, 0)))
        out_shape.append(jax.ShapeDtypeStruct((b, DN_HEADS, DN_DK, DN_DV), F32))
    prepared = [pltpu.VMEM((n_inst, DN_CHUNK, c2), F32), pltpu.VMEM((n_inst, c2, c2), BF16),
                pltpu.VMEM((n_inst, DN_CHUNK, c2), BF16), pltpu.VMEM((n_inst, DN_CHUNK, c2), BF16),
                pltpu.VMEM((n_inst, SUBLANES, c2), F32)]
    res = pl.pallas_call(
        functools.partial(_dn_kernel, reverse=reverse, has_s0=s0 is not None, want_state=want_state,
                          n_seq=n_seq, cps=cps, d=d, nt=nt),
        grid=(n_tiles + 1,),
        in_specs=in_specs,
        out_specs=out_specs,
        out_shape=out_shape,
        scratch_shapes=[pltpu.VMEM((n_seq * (DN_HEADS // 2), c2, c2), F32)] + prepared + prepared,
        compiler_params=_cparams(("arbitrary",)),
    )(*args)
    return res if want_state else (res[0], None)


def _post_kernel(x_ref, mod_ref, cb_ref, cc_ref, cx_ref, ccp_ref, cxp_ref, ccn_ref, cxn_ref, cw_ref,
                 att_ref, of_ref, ob_ref, z_ref, ng_ref, ga_ref, gb_ref, gc_ref, wpa_ref, wpb_ref, wpc_ref, wo_ref,
                 gpost_ref, o_ref, *, tm, tps):
    i = pl.program_id(0)
    first = (i % tps) == 0
    last = (i % tps) == tps - 1
    u = cc_ref[...].astype(F32) * cx_ref[...].astype(F32)
    hl = BF16_SUBLANES - 1
    prev_row = jnp.where(first, 0.0, ccp_ref[hl:hl + 1, :].astype(F32) * cxp_ref[hl:hl + 1, :].astype(F32))
    next_row = jnp.where(last, 0.0, ccn_ref[0:1, :].astype(F32) * cxn_ref[0:1, :].astype(F32))
    rows = lax.broadcasted_iota(jnp.int32, (tm, 1), 0)
    u_prev = jnp.where(rows == 0, prev_row, pltpu.roll(u, 1, axis=0))
    u_next = jnp.where(rows == tm - 1, next_row, pltpu.roll(u, tm - 1, axis=0))
    cw = cw_ref[...]
    conv = cw[0:1, :] * u_prev + cw[1:2, :] * u + cw[2:3, :] * u_next
    ya = _dot((cb_ref[...].astype(F32) * conv).astype(BF16), wpa_ref[...])
    yb = _dot(att_ref[...], wpb_ref[...])
    o = of_ref[...] + ob_ref[...]
    z = z_ref[...].astype(F32)
    parts = []
    for h in range(DN_HEADS):
        sl = slice(h * DN_DV, (h + 1) * DN_DV)
        parts.append((_rms(o[:, sl], ng_ref[...]) * _silu(z[:, sl])).astype(BF16))
    yc = _dot(jnp.concatenate(parts, axis=1), wpc_ref[...])
    mix_in = (_sigmoid(ga_ref[...].astype(F32)) * ya + _sigmoid(gb_ref[...].astype(F32)) * yb
              + _sigmoid(gc_ref[...].astype(F32)) * yc)
    mix = _dot(mix_in.astype(BF16), wo_ref[...])
    o_ref[...] = x_ref[...] + mod_ref[0, 2:3, :] * _rms(mix, gpost_ref[...])


def _post_mixer(x, mod3, mod_base, rows_per_mod, t, proj, att, o_f, o_b, conv_w, ng, wpa, wpb, wpc, wo, layer,
                g_post1):
    n = x.shape[0]
    tm = min(512, t)
    tps = t // tm
    hb = tm // BF16_SUBLANES
    cwid = CONV_WIDTH
    c0 = REST_CONV // cwid

    def prev(i):
        return jnp.maximum(i * hb - 1, 0)

    def nxt(i):
        return jnp.minimum((i + 1) * hb, n // BF16_SUBLANES - 1)

    g0 = REST_GATE // D_MODEL
    in_specs = [
        pl.BlockSpec((tm, D_MODEL), lambda i: (i, 0)),
        pl.BlockSpec((1, 6, D_MODEL), lambda i: (mod_base + (i * tm) // rows_per_mod, 0, 0)),
        pl.BlockSpec((tm, cwid), lambda i: (i, c0)),
        pl.BlockSpec((tm, cwid), lambda i: (i, c0 + 1)),
        pl.BlockSpec((tm, cwid), lambda i: (i, c0 + 2)),
        pl.BlockSpec((BF16_SUBLANES, cwid), lambda i: (prev(i), c0 + 1)),
        pl.BlockSpec((BF16_SUBLANES, cwid), lambda i: (prev(i), c0 + 2)),
        pl.BlockSpec((BF16_SUBLANES, cwid), lambda i: (nxt(i), c0 + 1)),
        pl.BlockSpec((BF16_SUBLANES, cwid), lambda i: (nxt(i), c0 + 2)),
        _const_spec((3, cwid)),
        pl.BlockSpec((tm, ATTN_Q), lambda i: (i, 0)),
        pl.BlockSpec((tm, DN_VW), lambda i: (i, 0)),
        pl.BlockSpec((tm, DN_VW), lambda i: (i, 0)),
        pl.BlockSpec((tm, DN_VW), lambda i: (i, REST_Z // DN_VW)),
        _const_spec((1, DN_DV)),
        pl.BlockSpec((tm, D_MODEL), lambda i: (i, g0)),
        pl.BlockSpec((tm, D_MODEL), lambda i: (i, g0 + 1)),
        pl.BlockSpec((tm, D_MODEL), lambda i: (i, g0 + 2)),
        _layer_spec((CONV_WIDTH, D_MODEL), layer), _layer_spec((ATTN_Q, D_MODEL), layer),
        _layer_spec((DN_VW, D_MODEL), layer), _layer_spec((D_MODEL, D_MODEL), layer),
        _const_spec((1, D_MODEL)),
    ]
    return pl.pallas_call(
        functools.partial(_post_kernel, tm=tm, tps=tps),
        grid=(n // tm,),
        in_specs=in_specs,
        out_specs=pl.BlockSpec((tm, D_MODEL), lambda i: (i, 0)),
        out_shape=jax.ShapeDtypeStruct((n, D_MODEL), F32),
        compiler_params=_cparams(("arbitrary",)),
    )(x, mod3, proj, proj, proj, proj, proj, proj, proj, conv_w, att, o_f, o_b, proj, ng,
      proj, proj, proj, wpa, wpb, wpc, wo, g_post1)


def _ffn_kernel(x_ref, mod_ref, gpre_ref, wg_ref, wu_ref, wd_ref, gpost_ref, o_ref, act_sc, *, tf):
    x = x_ref[...]
    h2 = (_rms(x, gpre_ref[...]) * (1.0 + mod_ref[0, 4:5, :]) + mod_ref[0, 3:4, :]).astype(BF16)
    for j in range(0, D_FF, tf):
        gate = _dot(h2, wg_ref[:, j:j + tf])
        up = _dot(h2, wu_ref[:, j:j + tf])
        act_sc[:, j:j + tf] = (_silu(gate) * up).astype(BF16)
    ffn = _dot(act_sc[...], wd_ref[...])
    o_ref[...] = x + mod_ref[0, 5:6, :] * _rms(ffn, gpost_ref[...])


def _ffn(x, mod3, mod_base, rows_per_mod, g_pre2, wg, wu, wd, layer, g_post2):
    n = x.shape[0]
    tm = min(1024, n)
    assert n % tm == 0 and rows_per_mod % tm == 0
    return pl.pallas_call(
        functools.partial(_ffn_kernel, tf=256),
        grid=(n // tm,),
        in_specs=[pl.BlockSpec((tm, D_MODEL), lambda i: (i, 0)),
                  pl.BlockSpec((1, 6, D_MODEL), lambda i: (mod_base + (i * tm) // rows_per_mod, 0, 0)),
                  _const_spec((1, D_MODEL)),
                  _layer_spec((D_MODEL, D_FF), layer), _layer_spec((D_MODEL, D_FF), layer),
                  _layer_spec((D_FF, D_MODEL), layer),
                  _const_spec((1, D_MODEL))],
        out_specs=pl.BlockSpec((tm, D_MODEL), lambda i: (i, 0)),
        out_shape=jax.ShapeDtypeStruct((n, D_MODEL), F32),
        scratch_shapes=[pltpu.VMEM((tm, D_FF), BF16)],
        compiler_params=_cparams(("arbitrary",)),
    )(x, mod3, g_pre2, wg, wu, wd, g_post2)


def _rope_tables(t):
    rows = t // GRID_W
    row_id = np.repeat(np.arange(rows, dtype=np.float32), GRID_W)
    col_id = np.tile(np.arange(GRID_W, dtype=np.float32), rows)
    n_freq = HEAD_DIM // 4
    inv_freq = (np.float32(ROPE_THETA) ** (-np.arange(n_freq, dtype=np.float32) / np.float32(n_freq))).astype(np.float32)
    ang = np.concatenate([row_id[:, None] * inv_freq, col_id[:, None] * inv_freq], axis=-1).astype(np.float32)
    cos = np.repeat(np.cos(ang), 2, axis=-1).astype(np.float32)
    sin = np.repeat(np.sin(ang), 2, axis=-1).astype(np.float32)
    sign = np.tile(np.array([-1.0, 1.0], np.float32), HEAD_DIM // 2)
    return jnp.asarray(cos), jnp.asarray(sin * sign)


def _layer(x, b, t, mod3, mod_base, rows_per_mod, lw, rope_tabs, cache, state0, want_state):
    q_p, k_p, vt_p, v_f32, qn, kn, vn, gb, gbt, proj = _inproj(x, t, mod3, mod_base, rows_per_mod, lw,
                                                               rope_tabs, want_state)
    att = _attention(q_p, k_p, vt_p, b, t, cache)
    dn = [_deltanet_dir(qn, kn, vn, gb, gbt, state0, b, t, d, want_state) for d in range(2)]
    x = _post_mixer(x, mod3, mod_base, rows_per_mod, t, proj, att, dn[0][0], dn[1][0], lw["conv_w"],
                    lw["dn_norm_g"], lw["w_pa"], lw["w_pb"], lw["w_pc"], lw["w_o"], lw["layer"], lw["g_post1"])
    x = _ffn(x, mod3, mod_base, rows_per_mod, lw["g_pre2"], lw["w_gate"], lw["w_up"], lw["w_down"],
             lw["layer"], lw["g_post2"])
    if not want_state:
        return x, None
    s_fin = jnp.stack([dn[0][1], dn[1][1]], axis=1)
    return x, (k_p, v_f32, s_fin)


def kernel(x_prompt, x_sample, cache_k, cache_v, state_dn, c, c_ctx, w_mod, b_mod, g_pre1, g_post1, g_pre2, g_post2, w_in, conv_w, g_qn, g_kn, dn_conv_w, dn_a_log, dn_dt_bias, dn_norm_g, w_pa, w_pb, w_pc, w_o, w_gate, w_up, w_down):
    bp, tp, d = x_prompt.shape
    bs, ts, _ = x_sample.shape
    depth = w_mod.shape[0]
    past = cache_k.shape[2]

    mod_rows = -(-(bs + 1) // SUBLANES) * SUBLANES
    cv = jnp.zeros((mod_rows, d), F32).at[:bs].set(c).at[bs].set(c_ctx)
    mod_all = _modulation(cv, w_mod, b_mod).reshape(depth, mod_rows, 6, d)

    stacked = {"w_pa": w_pa.astype(BF16), "w_pb": w_pb.astype(BF16), "w_pc": w_pc.astype(BF16),
               "w_o": w_o.astype(BF16), "w_gate": w_gate.astype(BF16), "w_up": w_up.astype(BF16),
               "w_down": w_down.astype(BF16)}
    w_main = w_in.astype(BF16)
    w_tail = jnp.concatenate(
        [w_in[:, :, COL_GATE:D_IN], w_in[:, :, COL_SMALL:COL_GATE],
         jnp.zeros((depth, d, LANES - 4 * DN_HEADS), w_in.dtype)], axis=-1).astype(BF16)
    lane_pad = ((0, 0), (2 * DN_HEADS, LANES - 4 * DN_HEADS))
    a_flat = dn_a_log.reshape(depth, 2 * DN_HEADS)
    b_flat = dn_dt_bias.reshape(depth, 2 * DN_HEADS)
    rope_tabs = _rope_tables(ts)
    cache_k4 = cache_k.reshape(bs, depth, past, ATTN_KV)
    cache_v4 = cache_v.reshape(bs, depth, past, ATTN_KV)

    xp = x_prompt.reshape(bp * tp, d)
    xs = x_sample.reshape(bs * ts, d)
    ks, vs, ss = [], [], []
    for l in range(depth):
        lw = {
            "g_pre1": g_pre1[l][None], "g_post1": g_post1[l][None], "g_pre2": g_pre2[l][None],
            "g_post2": g_post2[l][None], "layer": l, "w_in": w_main, "w_tail": w_tail, "conv_w": conv_w[l],
            "g_qn": g_qn[l][None], "g_kn": g_kn[l][None], "dn_conv_w": dn_conv_w[l],
            "dn_ac": jnp.pad(a_flat[l][None], lane_pad), "dn_bc": jnp.pad(b_flat[l][None], lane_pad),
            "dn_norm_g": dn_norm_g[l][None],
            **stacked,
        }
        mod3 = mod_all[l]
        xp, (k_l, v_l, s_l) = _layer(xp, bp, tp, mod3, bs, bp * tp, lw, None, None, None, True)
        ks.append(k_l.reshape(bp, tp, N_KV_HEADS, HEAD_DIM))
        vs.append(v_l.reshape(bp, tp, N_KV_HEADS, HEAD_DIM))
        ss.append(s_l)
        xs, _ = _layer(xs, bs, ts, mod3, 0, ts, lw, rope_tabs, (cache_k4, cache_v4, l), (state_dn, l), False)
    return (xp.reshape(bp, tp, d), xs.reshape(bs, ts, d), jnp.stack(ks, axis=1), jnp.stack(vs, axis=1),
            jnp.stack(ss, axis=1))
```

```python
import functools

import jax
import jax.numpy as jnp
import numpy as np
from jax import lax
from jax.experimental import pallas as pl
from jax.experimental.pallas import tpu as pltpu

F32 = jnp.float32
BF16 = jnp.bfloat16

D_MODEL = 1024
EPS = 1e-6
GRID_W = 64
N_HEADS = 8
N_KV_HEADS = 2
Q_PER_KV = N_HEADS // N_KV_HEADS
HEAD_DIM = 128
ATTN_Q = N_HEADS * HEAD_DIM
ATTN_KV = N_KV_HEADS * HEAD_DIM
ROPE_THETA = 10000.0
CONV_WIDTH = 512
DN_HEADS = 4
DN_DK = 128
DN_DV = 128
DN_QK = DN_HEADS * DN_DK
DN_VW = DN_HEADS * DN_DV
D_FF = 2816

SUBLANES = 8
BF16_SUBLANES = 16
LANES = 128

COL_CONV = 0
COL_Q = COL_CONV + 3 * CONV_WIDTH
COL_K = COL_Q + ATTN_Q
COL_V = COL_K + ATTN_KV
COL_DN = COL_V + ATTN_KV
COL_Z = COL_DN + 3 * DN_QK
COL_SMALL = COL_Z + DN_VW
COL_GATE = COL_SMALL + 4 * DN_HEADS
D_IN = COL_GATE + 3 * D_MODEL
TAIL_GATE = 0
TAIL_SMALL = TAIL_GATE + 3 * D_MODEL
D_TAIL = TAIL_SMALL + LANES
REST_GATE = 0
REST_CONV = REST_GATE + 3 * D_MODEL
REST_Z = REST_CONV + 3 * CONV_WIDTH
D_REST = REST_Z + DN_VW

Q_SCALE = 1.4426950408889634 * HEAD_DIM ** -0.5
DN_CHUNK = 128
DN_BASE = 16
VMEM_LIMIT = 56 * 1024 * 1024


def _cparams(sem):
    return pltpu.CompilerParams(dimension_semantics=sem, vmem_limit_bytes=VMEM_LIMIT)


def _const_spec(shape):
    nd = len(shape)
    return pl.BlockSpec(shape, lambda *_: (0,) * nd, pipeline_mode=pl.Buffered(1))


def _layer_spec(shape, layer):
    return pl.BlockSpec((None,) + tuple(shape), lambda *_: (layer,) + (0,) * len(shape),
                        pipeline_mode=pl.Buffered(1))


def _dot(a, b):
    return jnp.dot(a, b, preferred_element_type=F32)


def _dot_nt(a, b):
    return lax.dot_general(a, b, (((1,), (1,)), ((), ())), preferred_element_type=F32)


def _dot_tn(a, b):
    return lax.dot_general(a, b, (((0,), (0,)), ((), ())), preferred_element_type=F32)


def _rms(x, g):
    return x * lax.rsqrt(jnp.mean(x * x, axis=-1, keepdims=True) + EPS) * g


def _sigmoid(x):
    return 1.0 / (1.0 + jnp.exp(-x))


def _silu(x):
    return x * _sigmoid(x)


def _softplus(x):
    return jnp.maximum(x, 0.0) + jnp.log1p(jnp.exp(-jnp.abs(x)))


def _split3(x):
    hi = x.astype(BF16)
    r = x - hi.astype(F32)
    mid = r.astype(BF16)
    lo = (r - mid.astype(F32)).astype(BF16)
    return hi, mid, lo


def _mod_kernel(cv_ref, w_ref, b_ref, o_ref):
    cv = cv_ref[...]
    o_ref[0] = _dot(_silu(cv).astype(BF16), w_ref[0].astype(BF16)) + b_ref[0]


def _modulation(cv, w_mod, b_mod):
    depth, d, n6 = w_mod.shape
    rows = cv.shape[0]
    tn = 1536
    return pl.pallas_call(
        _mod_kernel,
        grid=(depth, n6 // tn),
        in_specs=[pl.BlockSpec((rows, d), lambda l, j: (0, 0)),
                  pl.BlockSpec((1, d, tn), lambda l, j: (l, 0, j)),
                  pl.BlockSpec((1, 1, tn), lambda l, j: (l, 0, j))],
        out_specs=pl.BlockSpec((1, rows, tn), lambda l, j: (l, 0, j)),
        out_shape=jax.ShapeDtypeStruct((depth, rows, n6), F32),
        compiler_params=_cparams(("arbitrary", "arbitrary")),
    )(cv, w_mod, b_mod.reshape(depth, 1, n6))


def _swap_pairs(y):
    lane = lax.broadcasted_iota(jnp.int32, y.shape, 1)
    return jnp.where(lane % 2 == 0, pltpu.roll(y, LANES - 1, axis=1), pltpu.roll(y, 1, axis=1))


def _shift_rows(x, prev_row, next_row, seq_rows):
    tm = x.shape[0]
    rows = lax.broadcasted_iota(jnp.int32, (tm, 1), 0)
    below, above = pltpu.roll(x, 1, axis=0), pltpu.roll(x, tm - 1, axis=0)
    if seq_rows >= tm:
        return jnp.where(rows == 0, prev_row, below), jnp.where(rows == tm - 1, next_row, above)
    pos = rows % seq_rows
    return jnp.where(pos == 0, 0.0, below), jnp.where(pos == seq_rows - 1, 0.0, above)


def _interleave(*gens):
    gens = list(gens)
    while gens:
        for gen in list(gens):
            try:
                next(gen)
            except StopIteration:
                gens.remove(gen)


def _inproj_kernel(*refs, rope, want_v, tm, tps, seq_rows, tn):
    refs = list(refs)
    x_ref, xp_ref, xn_ref, mod_ref, g_ref, w_ref, wt_ref, gq_ref, gk_ref, cw_ref, ac_ref, bc_ref = refs[:12]
    refs = refs[12:]
    cos_ref, sin_ref = (refs.pop(0), refs.pop(0)) if rope else (None, None)
    q_out, k_out, v_out = refs[:3]
    refs = refs[3:]
    vf_out = refs.pop(0) if want_v else None
    qn_out, kn_out, vn_out, gb_out, gbt_out, rest_out = refs
    i = pl.program_id(0)

    def modnorm(x):
        return (_rms(x, g_ref[...]) * (1.0 + mod_ref[0, 1:2, :]) + mod_ref[0, 0:1, :]).astype(BF16)

    h = modnorm(x_ref[...])
    h_halo = modnorm(jnp.concatenate([xp_ref[...], xn_ref[...]], axis=0))

    def qk_head(x, g):
        y = _rms(x, g)
        if rope:
            y = y * cos_ref[...] + _swap_pairs(y) * sin_ref[...]
        return y

    def attn_epilogue():
        qkv = _dot(h, w_ref[:, COL_Q:COL_DN])
        yield
        for hh in range(N_HEADS):
            sl = slice(hh * HEAD_DIM, (hh + 1) * HEAD_DIM)
            q_out[:, sl] = (qk_head(qkv[:, sl], gq_ref[...]) * Q_SCALE).astype(q_out.dtype)
            if hh % 2 == 1:
                yield
        for hh in range(N_KV_HEADS):
            sl = slice(hh * HEAD_DIM, (hh + 1) * HEAD_DIM)
            k_out[:, sl] = qk_head(qkv[:, ATTN_Q + hh * HEAD_DIM:ATTN_Q + (hh + 1) * HEAD_DIM],
                                   gk_ref[...]).astype(k_out.dtype)
        v = qkv[:, ATTN_Q + ATTN_KV:]
        v_out[...] = v.T.astype(v_out.dtype)
        if want_v:
            vf_out[...] = v

    def dn_epilogue():
        dn = _dot(h, w_ref[:, COL_DN:COL_Z])
        dn_halo = _dot(h_halo, w_ref[:, COL_DN:COL_Z])
        gl = _dot(h, wt_ref[:, TAIL_SMALL:D_TAIL])
        yield
        prev_row = jnp.where((i % tps) == 0, 0.0, dn_halo[SUBLANES - 1:SUBLANES, :])
        next_row = jnp.where((i % tps) == tps - 1, 0.0, dn_halo[SUBLANES:SUBLANES + 1, :])
        dn_prev, dn_next = _shift_rows(dn, prev_row, next_row, seq_rows)
        cw = cw_ref[...]

        def conv_silu(sl):
            return _silu(cw[0:1, sl] * dn_prev[:, sl] + cw[1:2, sl] * dn[:, sl] + cw[2:3, sl] * dn_next[:, sl])

        for hh in range(DN_HEADS):
            sl = slice(hh * DN_DK, (hh + 1) * DN_DK)
            qh = conv_silu(sl)
            kh = conv_silu(slice(DN_QK + hh * DN_DK, DN_QK + (hh + 1) * DN_DK))
            qn_out[:, sl] = (qh * lax.rsqrt(jnp.sum(qh * qh, axis=-1, keepdims=True) + EPS)
                             * (DN_DK ** -0.5)).astype(qn_out.dtype)
            kn_out[:, sl] = (kh * lax.rsqrt(jnp.sum(kh * kh, axis=-1, keepdims=True) + EPS)).astype(kn_out.dtype)
            vn_out[:, sl] = conv_silu(slice(2 * DN_QK + hh * DN_DV, 2 * DN_QK + (hh + 1) * DN_DV)
                                      ).astype(vn_out.dtype)
            yield
        lane = lax.broadcasted_iota(jnp.int32, gl.shape, 1)
        gb = jnp.where(lane < 2 * DN_HEADS, _sigmoid(gl), -jnp.exp(ac_ref[...]) * _softplus(gl + bc_ref[...]))
        gb_out[...] = gb
        gbt_out[...] = gb.T

    def remaining():
        groups = [(wt_ref, TAIL_GATE, REST_GATE, 3 * D_MODEL), (w_ref, COL_CONV, REST_CONV, 3 * CONV_WIDTH),
                  (w_ref, COL_Z, REST_Z, DN_VW)]
        for ref, src, dst, width in groups:
            for j in range(0, width, tn):
                rest_out[:, dst + j:dst + j + tn] = _dot(h, ref[:, src + j:src + j + tn]).astype(rest_out.dtype)
                yield

    _interleave(dn_epilogue(), attn_epilogue(), remaining())


def _inproj(x, t, mod3, mod_base, rows_per_mod, lw, rope_tabs, want_v):
    n = x.shape[0]
    tm = 512 if rope_tabs is None else min(512, t)
    assert n % tm == 0 and (t % tm == 0 or tm % t == 0) and rows_per_mod % tm == 0
    tps = max(t // tm, 1)
    halo = tm // SUBLANES
    w_main, w_tail, layer = lw["w_in"], lw["w_tail"], lw["layer"]
    in_specs = [pl.BlockSpec((tm, D_MODEL), lambda i: (i, 0)),
                pl.BlockSpec((SUBLANES, D_MODEL), lambda i: (jnp.maximum(i * halo - 1, 0), 0)),
                pl.BlockSpec((SUBLANES, D_MODEL), lambda i: (jnp.minimum((i + 1) * halo, n // SUBLANES - 1), 0)),
                pl.BlockSpec((1, 6, D_MODEL), lambda i: (mod_base + (i * tm) // rows_per_mod, 0, 0)),
                _const_spec((1, D_MODEL)),
                _layer_spec((D_MODEL, COL_SMALL), layer), _layer_spec((D_MODEL, D_TAIL), layer),
                _const_spec((1, HEAD_DIM)), _const_spec((1, HEAD_DIM)),
                _const_spec((3, 3 * DN_QK)), _const_spec((1, LANES)), _const_spec((1, LANES))]
    args = [x, x, x, mod3, lw["g_pre1"], w_main, w_tail, lw["g_qn"], lw["g_kn"], lw["dn_conv_w"],
            lw["dn_ac"], lw["dn_bc"]]
    if rope_tabs is not None:
        in_specs += [pl.BlockSpec((tm, HEAD_DIM), lambda i: (i % tps, 0))] * 2
        args += list(rope_tabs)

    outs = [(ATTN_Q, BF16, False), (ATTN_KV, F32 if want_v else BF16, False), (ATTN_KV, BF16, True)]
    if want_v:
        outs.append((ATTN_KV, F32, False))
    outs += [(DN_QK, BF16, False), (DN_QK, BF16, False), (DN_VW, BF16, False), (LANES, F32, False),
             (LANES, F32, True), (D_REST, BF16, False)]
    res = pl.pallas_call(
        functools.partial(_inproj_kernel, rope=rope_tabs is not None, want_v=want_v, tm=tm, tps=tps, seq_rows=t,
                          tn=512),
        grid=(n // tm,),
        in_specs=in_specs,
        out_specs=[pl.BlockSpec((w, tm), lambda i: (0, i)) if tr else pl.BlockSpec((tm, w), lambda i: (i, 0))
                   for w, _, tr in outs],
        out_shape=[jax.ShapeDtypeStruct((w, n) if tr else (n, w), dt) for w, dt, tr in outs],
        compiler_params=_cparams(("arbitrary",)),
    )(*args)
    res = list(res)
    q_p, k_p, vt_p = res[:3]
    v_f32 = res[3] if want_v else None
    qn, kn, vn, gb, gbt, rest = res[-6:]
    return q_p, k_p, vt_p, v_f32, qn, kn, vn, gb, gbt, rest


def _attn_kernel(*refs, has_cache, t, tk, tq, nq, per_step):
    refs = list(refs)
    q_ref = refs.pop(0)
    qn_ref = refs.pop(0) if nq > 1 else None
    kc_ref, vc_ref = (refs.pop(0), refs.pop(0)) if has_cache else (None, None)
    k_ref, v_ref, o_ref = refs[:3]
    refs = refs[3:]
    vct_sc = refs.pop() if has_cache else None
    s_bufs, m_bufs = refs[:len(refs) // 2], refs[len(refs) // 2:]
    qi = pl.program_id(2)

    if has_cache:
        @pl.when(qi == 0)
        def _():
            vct_sc[...] = vc_ref[...].T.astype(BF16)

    segs = []
    if has_cache:
        past = kc_ref.shape[0]
        segs += [(kc_ref, vct_sc, r, min(tk, past - r)) for r in range(0, past, tk)]
    segs += [(k_ref, v_ref, r, tk) for r in range(0, t, tk)]
    offs = [sum(w for _, _, _, w in segs[:i]) for i in range(len(segs))]

    def stack(ref, row0=0):
        q = ref[row0:row0 + tq, :]
        return jnp.concatenate([q[:, g * HEAD_DIM:(g + 1) * HEAD_DIM] for g in range(Q_PER_KV)], axis=0)

    def scores(qs, s_ref, i, m_run):
        kr, _, r, w = segs[i]
        s = _dot_nt(kr[r:r + w, :].astype(BF16), qs)
        s_ref[offs[i]:offs[i] + w, :] = s
        for r0 in range(0, w, SUBLANES):
            blk = s[r0:r0 + SUBLANES, :]
            m_run = blk if m_run is None else jnp.maximum(m_run, blk)
        return m_run

    def weighted(s_ref, i, m, acc):
        _, vr, r, w = segs[i]
        p = jnp.exp2(s_ref[offs[i]:offs[i] + w, :] - m)
        den = p[0:SUBLANES, :]
        for r0 in range(SUBLANES, w, SUBLANES):
            den = den + p[r0:r0 + SUBLANES, :]
        pv = _dot(vr[:, r:r + w].astype(BF16), p.astype(BF16))
        return (pv, den) if acc is None else (acc[0] + pv, acc[1] + den)

    def finish(acc, row0=0):
        o = (acc[0] / jnp.sum(acc[1], axis=0, keepdims=True)).T
        for g in range(Q_PER_KV):
            o_ref[row0:row0 + tq, g * HEAD_DIM:(g + 1) * HEAD_DIM] = o[g * tq:(g + 1) * tq].astype(o_ref.dtype)

    if nq > 1:
        @pl.when(qi == 0)
        def _():
            qs0 = stack(q_ref)
            m_run = None
            for i in range(len(segs)):
                m_run = scores(qs0, s_bufs[0], i, m_run)
            m_bufs[0][...] = m_run

        def step(s_cur, m_cur, s_nxt, m_nxt, qs_next, row0):
            m = jnp.max(m_cur[...], axis=0, keepdims=True)
            acc, m_run = None, None
            for i in range(len(segs)):
                m_run = scores(qs_next, s_nxt, i, m_run)
                acc = weighted(s_cur, i, m, acc)
            m_nxt[...] = m_run
            finish(acc, row0)

        for u in range(per_step):
            qs_next = stack(q_ref, (u + 1) * tq) if u + 1 < per_step else stack(qn_ref)
            step(s_bufs[u % 2], m_bufs[u % 2], s_bufs[1 - u % 2], m_bufs[1 - u % 2], qs_next, u * tq)
    else:
        qs = stack(q_ref)
        m_run = None
        for i in range(len(segs)):
            m_run = scores(qs, s_bufs[0], i, m_run)
        m = jnp.max(m_run, axis=0, keepdims=True)
        acc = None
        for i in range(len(segs)):
            acc = weighted(s_bufs[0], i, m, acc)
        finish(acc)


def _attention(q_p, k_p, vt_p, b, t, cache):
    n = q_p.shape[0]
    tq = 128 if t > 256 else 256
    tk = min(t, 512)
    nq = t // tq
    per_step = 4 if nq % 4 == 0 else (2 if nq % 2 == 0 else 1)
    assert nq == 1 or per_step > 1
    steps = nq // per_step
    qw = Q_PER_KV * HEAD_DIM
    in_specs = [pl.BlockSpec((per_step * tq, qw), lambda bi, j, qi: (bi * steps + qi, j))]
    args = [q_p]
    if nq > 1:
        in_specs.append(pl.BlockSpec((tq, qw),
                                     lambda bi, j, qi: (bi * nq + jnp.minimum(per_step * (qi + 1), nq - 1), j)))
        args.append(q_p)
    if cache is not None:
        cache_k, cache_v, layer = cache
        past = cache_k.shape[2]
        cspec = pl.BlockSpec((None, None, past, HEAD_DIM), lambda bi, j, qi: (bi, layer, 0, j))
        in_specs += [cspec, cspec]
        args += [cache_k, cache_v]
    in_specs += [pl.BlockSpec((t, HEAD_DIM), lambda bi, j, qi: (bi, j)),
                 pl.BlockSpec((HEAD_DIM, t), lambda bi, j, qi: (j, bi))]
    args += [k_p, vt_p]
    n_keys = t + (cache[0].shape[2] if cache is not None else 0)
    slots = 2 if nq > 1 else 1
    return pl.pallas_call(
        functools.partial(_attn_kernel, has_cache=cache is not None, t=t, tk=tk, tq=tq, nq=nq,
                          per_step=per_step),
        grid=(b, N_KV_HEADS, steps),
        in_specs=in_specs,
        out_specs=pl.BlockSpec((per_step * tq, qw), lambda bi, j, qi: (bi * steps + qi, j)),
        out_shape=jax.ShapeDtypeStruct((n, ATTN_Q), BF16),
        scratch_shapes=([pltpu.VMEM((n_keys, Q_PER_KV * tq), F32)] * slots
                        + [pltpu.VMEM((SUBLANES, Q_PER_KV * tq), F32)] * slots
                        + ([pltpu.VMEM((HEAD_DIM, cache[0].shape[2]), BF16)] if cache is not None else [])),
        compiler_params=_cparams(("arbitrary", "arbitrary", "arbitrary")),
    )(*args)


def _lane_pick(x, lane):
    idx = lax.broadcasted_iota(jnp.int32, x.shape, 1)
    return jnp.sum(jnp.where(idx == lane, x, 0.0), axis=-1, keepdims=True)


def _block_diag(x2):
    xb = x2.astype(BF16)
    z = jnp.zeros((DN_CHUNK, DN_CHUNK), BF16)
    return jnp.concatenate([jnp.concatenate([xb[:, :DN_CHUNK], z], axis=1),
                            jnp.concatenate([z, xb[:, DN_CHUNK:]], axis=1)], axis=0)


def _mm_pair(x2, y2):
    return _dot(x2.astype(BF16), _block_diag(y2))


def _unit_tri_inverses(lmats, ri, ci):
    def blk(s):
        return (ri ^ ci) < s

    eye = jnp.where(ri == ci, 1.0, 0.0)
    ps = [jnp.where(blk(DN_BASE), -lm, 0.0) for lm in lmats]
    xs = [eye + p for p in ps]
    s = 2
    while s < DN_BASE:
        ps = [_mm_pair(p, p) for p in ps]
        yield
        xs = [x + _mm_pair(x, p) for x, p in zip(xs, ps)]
        yield
        s *= 2
    s = DN_BASE
    while s < DN_CHUNK:
        sel = blk(2 * s) & jnp.logical_not(blk(s))
        ts = [_mm_pair(jnp.where(sel, lm, 0.0), x) for lm, x in zip(lmats, xs)]
        yield
        xs = [x - _mm_pair(x, t) for x, t in zip(xs, ts)]
        yield
        s *= 2
    return xs


def _dn_kernel(*refs, reverse, has_s0, want_state, n_seq, cps, d, nt):
    nc = n_seq * cps
    refs = list(refs)
    q_ref, k_ref, v_ref, gb_ref, gr_ref = refs[:5]
    refs = refs[5:]
    s0_ref = refs.pop(0) if has_s0 else None
    o_ref = refs.pop(0)
    sfin_ref = refs.pop(0) if want_state else None
    s_sc = refs.pop(0)
    bufs = (refs[:5], refs[5:10])
    n_pairs = DN_HEADS // 2
    c2 = 2 * DN_CHUNK
    tt = nc * DN_CHUNK
    insts = [(c, p) for c in range(nc) for p in range(n_pairs)]

    g = pl.program_id(0)
    i_scan = (g - 1) % nt

    def pair_cols(x, lane_a, lane_b):
        shape = (x.shape[0], DN_CHUNK)
        return jnp.concatenate([jnp.broadcast_to(_lane_pick(x, lane_a), shape),
                                jnp.broadcast_to(_lane_pick(x, lane_b), shape)], axis=1)

    def prepare(buf):
        u_buf, wq_buf, attn_buf, kt_buf, gt_buf = buf
        gb = gb_ref[...]
        g_r = gr_ref[...]
        bi = lax.broadcasted_iota(jnp.int32, (tt, tt), 0)
        bj = lax.broadcasted_iota(jnp.int32, (tt, tt), 1)
        same = (bi ^ bj) < DN_CHUNK
        if reverse:
            tri_c = jnp.where(same & (bj >= bi), 1.0, 0.0).astype(BF16)
            tri_r = jnp.where(same & (bi >= bj), 1.0, 0.0).astype(BF16)
        else:
            tri_c = jnp.where(same & (bj <= bi), 1.0, 0.0).astype(BF16)
            tri_r = jnp.where(same & (bi <= bj), 1.0, 0.0).astype(BF16)
        gcum_c = sum(_dot(tri_c, part) for part in _split3(gb))
        gcum_r = sum(_dot(part, tri_r) for part in _split3(g_r))
        yield

        ri = lax.broadcasted_iota(jnp.int32, (DN_CHUNK, c2), 0)
        ci = lax.broadcasted_iota(jnp.int32, (DN_CHUNK, c2), 1) & (DN_CHUNK - 1)
        incl = (ci >= ri) if reverse else (ci <= ri)
        strict = (ci > ri) if reverse else (ci < ri)
        pre = []
        for c, p in insts:
            rs = slice(c * DN_CHUNK, (c + 1) * DN_CHUNK)
            cs = slice(p * c2, (p + 1) * c2)
            lane_b = d * DN_HEADS + 2 * p
            lane_g = 2 * DN_HEADS + d * DN_HEADS + 2 * p
            q2 = q_ref[rs, cs].astype(F32)
            k2 = k_ref[rs, cs].astype(F32)
            v2 = v_ref[rs, cs].astype(F32)
            g_i = pair_cols(gcum_c[rs, :], lane_g, lane_g + 1)
            b_i = pair_cols(gb[rs, :], lane_b, lane_b + 1)
            g_j = jnp.concatenate([gcum_r[lane_g:lane_g + 1, rs], gcum_r[lane_g + 1:lane_g + 2, rs]], axis=1)
            g_tot = g_i[0:1, :] if reverse else g_i[DN_CHUNK - 1:DN_CHUNK, :]
            decay = jnp.where(incl, jnp.exp(jnp.where(incl, g_i - g_j, 0.0)), 0.0)
            pre.append(dict(q2=q2, k2=k2, v2=v2, g_i=g_i, b_i=b_i, g_tot=g_tot, decay=decay, kb=k2 * b_i))
        a2s = [_dot_nt(jnp.concatenate([p["kb"], p["q2"]], axis=0).astype(BF16), _block_diag(p["k2"]))
               for p in pre]
        yield
        lmats = [jnp.where(strict, a2[:DN_CHUNK] * p["decay"], 0.0) for a2, p in zip(a2s, pre)]
        for n, (a2, p) in enumerate(zip(a2s, pre)):
            attn_buf[n] = (a2[DN_CHUNK:] * p["decay"]).astype(BF16)
        tinvs = yield from _unit_tri_inverses(lmats, ri, ci)
        gams = [jnp.exp(p["g_i"]) for p in pre]
        us = [_mm_pair(t, p["v2"] * p["b_i"]) for t, p in zip(tinvs, pre)]
        yield
        ws = [_mm_pair(t, p["kb"] * gam) for t, p, gam in zip(tinvs, pre, gams)]
        yield
        for n, (p, u, w, gam) in enumerate(zip(pre, us, ws, gams)):
            u_buf[n] = u
            wq_buf[n] = jnp.concatenate([w, p["q2"] * gam], axis=0).astype(BF16)
            kt_buf[n] = (p["k2"] * jnp.exp(p["g_tot"] - p["g_i"])).astype(BF16)
            gt_buf[n] = jnp.broadcast_to(jnp.exp(p["g_tot"]), (SUBLANES, c2))

    def scan(buf):
        u_buf, wq_buf, attn_buf, kt_buf, gt_buf = buf
        si = lax.broadcasted_iota(jnp.int32, (c2, c2), 0)
        sj = lax.broadcasted_iota(jnp.int32, (c2, c2), 1)
        on_diag = (si < DN_CHUNK) == (sj < DN_CHUNK)
        chains = [(sq, p) for sq in range(n_seq) for p in range(n_pairs)]
        states = [s_sc[sq * n_pairs + p] for sq, p in chains]
        for k in (range(cps - 1, -1, -1) if reverse else range(cps)):
            cs = [sq * cps + k for sq, _ in chains]
            ns = [c * n_pairs + p for c, (_, p) in zip(cs, chains)]
            m1s = [_dot(wq_buf[n], s.astype(BF16)) for n, s in zip(ns, states)]
            yield
            v_news = [u_buf[n] - m1[:DN_CHUNK] for n, m1 in zip(ns, m1s)]
            outs = [m1[DN_CHUNK:] + _mm_pair(attn_buf[n], v) for n, m1, v in zip(ns, m1s, v_news)]
            yield
            states = [s * gt_buf[n][0:1, :] + jnp.where(on_diag, _dot_tn(kt_buf[n], v.astype(BF16)), 0.0)
                      for n, s, v in zip(ns, states, v_news)]
            for c, (_, p), out in zip(cs, chains, outs):
                o_ref[c * DN_CHUNK:(c + 1) * DN_CHUNK, p * c2:(p + 1) * c2] = out
            yield
        for (sq, p), s in zip(chains, states):
            s_sc[sq * n_pairs + p] = s

    def step(par):
        @pl.when(i_scan == 0)
        def _():
            s_sc[...] = jnp.zeros_like(s_sc)
            if has_s0:
                for h in range(DN_HEADS):
                    lo = (h % 2) * DN_CHUNK
                    s_sc[h // 2, lo:lo + DN_CHUNK, lo:lo + DN_CHUNK] = s0_ref[h]

        _interleave(prepare(bufs[par]), scan(bufs[1 - par]))

        if want_state:
            @pl.when(i_scan == nt - 1)
            def _():
                for sq in range(n_seq):
                    for h in range(DN_HEADS):
                        lo = (h % 2) * DN_CHUNK
                        sfin_ref[sq, h] = s_sc[sq * n_pairs + h // 2, lo:lo + DN_CHUNK, lo:lo + DN_CHUNK]

    pl.when(g == 0)(lambda: _interleave(prepare(bufs[0])))
    pl.when((g > 0) & (g % 2 == 0))(functools.partial(step, 0))
    pl.when(g % 2 == 1)(functools.partial(step, 1))


def _deltanet_dir(qn, kn, vn, gb, gbt, s0, b, t, d, want_state):
    n = qn.shape[0]
    chunks_per_tile = 4
    cps = min(chunks_per_tile, t // DN_CHUNK)
    n_seq = chunks_per_tile // cps
    assert b % n_seq == 0 and (s0 is None or n_seq == 1)
    nc = n_seq * cps
    tt = nc * DN_CHUNK
    nt = t // (cps * DN_CHUNK)
    n_tiles = (b // n_seq) * nt
    reverse = d == 1
    c2 = 2 * DN_CHUNK
    n_inst = nc * (DN_HEADS // 2)

    def seq(gt):
        bi, i = gt // nt, gt % nt
        return bi, ((nt - 1 - i) if reverse else i)

    def prep_tile(g):
        return seq(jnp.minimum(g, n_tiles - 1))

    def scan_tile(g):
        return seq(jnp.maximum(g - 1, 0))

    def row_block(bt):
        return bt[0] * nt + bt[1]

    qkv_spec = pl.BlockSpec((tt, DN_QK), lambda g: (row_block(prep_tile(g)), 0))
    in_specs = [
        qkv_spec, qkv_spec, qkv_spec,
        pl.BlockSpec((tt, LANES), lambda g: (row_block(prep_tile(g)), 0)),
        pl.BlockSpec((4 * DN_HEADS, tt), lambda g: (0, row_block(prep_tile(g)))),
    ]
    args = [qn, kn, vn, gb, gbt]
    if s0 is not None:
        state, layer = s0
        in_specs.append(pl.BlockSpec((None, None, None, DN_HEADS, DN_DK, DN_DV),
                                     lambda g: (scan_tile(g)[0], layer, d, 0, 0, 0)))
        args.append(state)
    out_specs = [pl.BlockSpec((tt, DN_VW), lambda g: (row_block(scan_tile(g)), 0))]
    out_shape = [jax.ShapeDtypeStruct((n, DN_VW), F32)]
    if want_state:
        out_specs.append(pl.BlockSpec((n_seq, DN_HEADS, DN_DK, DN_DV), lambda g: (scan_tile(g)[0], 0, 0, 0)))
        out_shape.append(jax.ShapeDtypeStruct((b, DN_HEADS, DN_DK, DN_DV), F32))
    prepared = [pltpu.VMEM((n_inst, DN_CHUNK, c2), F32), pltpu.VMEM((n_inst, c2, c2), BF16),
                pltpu.VMEM((n_inst, DN_CHUNK, c2), BF16), pltpu.VMEM((n_inst, DN_CHUNK, c2), BF16),
                pltpu.VMEM((n_inst, SUBLANES, c2), F32)]
    res = pl.pallas_call(
        functools.partial(_dn_kernel, reverse=reverse, has_s0=s0 is not None, want_state=want_state,
                          n_seq=n_seq, cps=cps, d=d, nt=nt),
        grid=(n_tiles + 1,),
        in_specs=in_specs,
        out_specs=out_specs,
        out_shape=out_shape,
        scratch_shapes=[pltpu.VMEM((n_seq * (DN_HEADS // 2), c2, c2), F32)] + prepared + prepared,
        compiler_params=_cparams(("arbitrary",)),
    )(*args)
    return res if want_state else (res[0], None)


def _post_kernel(x_ref, mod_ref, cb_ref, cc_ref, cx_ref, ccp_ref, cxp_ref, ccn_ref, cxn_ref, cw_ref,
                 att_ref, of_ref, ob_ref, z_ref, ng_ref, ga_ref, gb_ref, gc_ref, wpa_ref, wpb_ref, wpc_ref, wo_ref,
                 gpost_ref, o_ref, *, tm, tps, seq_rows):
    i = pl.program_id(0)
    first = (i % tps) == 0
    last = (i % tps) == tps - 1
    u = cc_ref[...].astype(F32) * cx_ref[...].astype(F32)
    hl = BF16_SUBLANES - 1
    prev_row = jnp.where(first, 0.0, ccp_ref[hl:hl + 1, :].astype(F32) * cxp_ref[hl:hl + 1, :].astype(F32))
    next_row = jnp.where(last, 0.0, ccn_ref[0:1, :].astype(F32) * cxn_ref[0:1, :].astype(F32))
    u_prev, u_next = _shift_rows(u, prev_row, next_row, seq_rows)
    cw = cw_ref[...]
    conv =cw[0:1, :] * u_prev + cw[1:2, :] * u + cw[2:3, :] * u_next
    ya = _dot((cb_ref[...].astype(F32) * conv).astype(BF16), wpa_ref[...])
    yb = _dot(att_ref[...], wpb_ref[...])
    o = of_ref[...] + ob_ref[...]
    z = z_ref[...].astype(F32)
    parts = []
    for h in range(DN_HEADS):
        sl = slice(h * DN_DV, (h + 1) * DN_DV)
        parts.append((_rms(o[:, sl], ng_ref[...]) * _silu(z[:, sl])).astype(BF16))
    yc = _dot(jnp.concatenate(parts, axis=1), wpc_ref[...])
    mix_in = (_sigmoid(ga_ref[...].astype(F32)) * ya + _sigmoid(gb_ref[...].astype(F32)) * yb
              + _sigmoid(gc_ref[...].astype(F32)) * yc)
    mix = _dot(mix_in.astype(BF16), wo_ref[...])
    o_ref[...] = x_ref[...] + mod_ref[0, 2:3, :] * _rms(mix, gpost_ref[...])


def _post_mixer(x, mod3, mod_base, rows_per_mod, t, proj, att, o_f, o_b, conv_w, ng, wpa, wpb, wpc, wo, layer,
                g_post1):
    n = x.shape[0]
    tm = 512
    assert n % tm == 0 and (t % tm == 0 or tm % t == 0) and rows_per_mod % tm == 0
    tps = max(t // tm, 1)
    hb = tm // BF16_SUBLANES
    cwid = CONV_WIDTH
    c0 = REST_CONV // cwid

    def prev(i):
        return jnp.maximum(i * hb - 1, 0)

    def nxt(i):
        return jnp.minimum((i + 1) * hb, n // BF16_SUBLANES - 1)

    g0 = REST_GATE // D_MODEL
    in_specs = [
        pl.BlockSpec((tm, D_MODEL), lambda i: (i, 0)),
        pl.BlockSpec((1, 6, D_MODEL), lambda i: (mod_base + (i * tm) // rows_per_mod, 0, 0)),
        pl.BlockSpec((tm, cwid), lambda i: (i, c0)),
        pl.BlockSpec((tm, cwid), lambda i: (i, c0 + 1)),
        pl.BlockSpec((tm, cwid), lambda i: (i, c0 + 2)),
        pl.BlockSpec((BF16_SUBLANES, cwid), lambda i: (prev(i), c0 + 1)),
        pl.BlockSpec((BF16_SUBLANES, cwid), lambda i: (prev(i), c0 + 2)),
        pl.BlockSpec((BF16_SUBLANES, cwid), lambda i: (nxt(i), c0 + 1)),
        pl.BlockSpec((BF16_SUBLANES, cwid), lambda i: (nxt(i), c0 + 2)),
        _const_spec((3, cwid)),
        pl.BlockSpec((tm, ATTN_Q), lambda i: (i, 0)),
        pl.BlockSpec((tm, DN_VW), lambda i: (i, 0)),
        pl.BlockSpec((tm, DN_VW), lambda i: (i, 0)),
        pl.BlockSpec((tm, DN_VW), lambda i: (i, REST_Z // DN_VW)),
        _const_spec((1, DN_DV)),
        pl.BlockSpec((tm, D_MODEL), lambda i: (i, g0)),
        pl.BlockSpec((tm, D_MODEL), lambda i: (i, g0 + 1)),
        pl.BlockSpec((tm, D_MODEL), lambda i: (i, g0 + 2)),
        _layer_spec((CONV_WIDTH, D_MODEL), layer), _layer_spec((ATTN_Q, D_MODEL), layer),
        _layer_spec((DN_VW, D_MODEL), layer), _layer_spec((D_MODEL, D_MODEL), layer),
        _const_spec((1, D_MODEL)),
    ]
    return pl.pallas_call(
        functools.partial(_post_kernel, tm=tm, tps=tps, seq_rows=t),
        grid=(n // tm,),
        in_specs=in_specs,
        out_specs=pl.BlockSpec((tm, D_MODEL), lambda i: (i, 0)),
        out_shape=jax.ShapeDtypeStruct((n, D_MODEL), F32),
        compiler_params=_cparams(("arbitrary",)),
    )(x, mod3, proj, proj, proj, proj, proj, proj, proj, conv_w, att, o_f, o_b, proj, ng,
      proj, proj, proj, wpa, wpb, wpc, wo, g_post1)


def _ffn_kernel(x_ref, mod_ref, gpre_ref, wg_ref, wu_ref, wd_ref, gpost_ref, o_ref, act_sc, *, tf):
    x = x_ref[...]
    h2 = (_rms(x, gpre_ref[...]) * (1.0 + mod_ref[0, 4:5, :]) + mod_ref[0, 3:4, :]).astype(BF16)
    for j in range(0, D_FF, tf):
        gate = _dot(h2, wg_ref[:, j:j + tf])
        up = _dot(h2, wu_ref[:, j:j + tf])
        act_sc[:, j:j + tf] = (_silu(gate) * up).astype(BF16)
    ffn = _dot(act_sc[...], wd_ref[...])
    o_ref[...] = x + mod_ref[0, 5:6, :] * _rms(ffn, gpost_ref[...])


def _ffn(x, mod3, mod_base, rows_per_mod, g_pre2, wg, wu, wd, layer, g_post2):
    n = x.shape[0]
    tm = min(1024, n)
    assert n % tm == 0 and rows_per_mod % tm == 0
    return pl.pallas_call(
        functools.partial(_ffn_kernel, tf=256),
        grid=(n // tm,),
        in_specs=[pl.BlockSpec((tm, D_MODEL), lambda i: (i, 0)),
                  pl.BlockSpec((1, 6, D_MODEL), lambda i: (mod_base + (i * tm) // rows_per_mod, 0, 0)),
                  _const_spec((1, D_MODEL)),
                  _layer_spec((D_MODEL, D_FF), layer), _layer_spec((D_MODEL, D_FF), layer),
                  _layer_spec((D_FF, D_MODEL), layer),
                  _const_spec((1, D_MODEL))],
        out_specs=pl.BlockSpec((tm, D_MODEL), lambda i: (i, 0)),
        out_shape=jax.ShapeDtypeStruct((n, D_MODEL), F32),
        scratch_shapes=[pltpu.VMEM((tm, D_FF), BF16)],
        compiler_params=_cparams(("arbitrary",)),
    )(x, mod3, g_pre2, wg, wu, wd, g_post2)


def _rope_tables(t):
    rows = t // GRID_W
    row_id = np.repeat(np.arange(rows, dtype=np.float32), GRID_W)
    col_id = np.tile(np.arange(GRID_W, dtype=np.float32), rows)
    n_freq = HEAD_DIM // 4
    inv_freq = (np.float32(ROPE_THETA) ** (-np.arange(n_freq, dtype=np.float32) / np.float32(n_freq))).astype(np.float32)
    ang = np.concatenate([row_id[:, None] * inv_freq, col_id[:, None] * inv_freq], axis=-1).astype(np.float32)
    cos = np.repeat(np.cos(ang), 2, axis=-1).astype(np.float32)
    sin = np.repeat(np.sin(ang), 2, axis=-1).astype(np.float32)
    sign = np.tile(np.array([-1.0, 1.0], np.float32), HEAD_DIM // 2)
    return jnp.asarray(cos), jnp.asarray(sin * sign)


def _layer(x, b, t, mod3, mod_base, rows_per_mod, lw, rope_tabs, cache, state0, want_state):
    q_p, k_p, vt_p, v_f32, qn, kn, vn, gb, gbt, proj = _inproj(x, t, mod3, mod_base, rows_per_mod, lw,
                                                               rope_tabs, want_state)
    att = _attention(q_p, k_p, vt_p, b, t, cache)
    dn = [_deltanet_dir(qn, kn, vn, gb, gbt, state0, b, t, d, want_state) for d in range(2)]
    x = _post_mixer(x, mod3, mod_base, rows_per_mod, t, proj, att, dn[0][0], dn[1][0], lw["conv_w"],
                    lw["dn_norm_g"], lw["w_pa"], lw["w_pb"], lw["w_pc"], lw["w_o"], lw["layer"], lw["g_post1"])
    x = _ffn(x, mod3, mod_base, rows_per_mod, lw["g_pre2"], lw["w_gate"], lw["w_up"], lw["w_down"],
             lw["layer"], lw["g_post2"])
    if not want_state:
        return x, None
    s_fin = jnp.stack([dn[0][1], dn[1][1]], axis=1)
    return x, (k_p, v_f32, s_fin)


def kernel(x_prompt, x_sample, cache_k, cache_v, state_dn, c, c_ctx, w_mod, b_mod, g_pre1, g_post1, g_pre2, g_post2, w_in, conv_w, g_qn, g_kn, dn_conv_w, dn_a_log, dn_dt_bias, dn_norm_g, w_pa, w_pb, w_pc, w_o, w_gate, w_up, w_down):
    bp, tp, d = x_prompt.shape
    bs, ts, _ = x_sample.shape
    depth = w_mod.shape[0]
    past = cache_k.shape[2]

    mod_rows = -(-(bs + 1) // SUBLANES) * SUBLANES
    cv = jnp.zeros((mod_rows, d), F32).at[:bs].set(c).at[bs].set(c_ctx)
    mod_all = _modulation(cv, w_mod, b_mod).reshape(depth, mod_rows, 6, d)

    stacked = {"w_pa": w_pa.astype(BF16), "w_pb": w_pb.astype(BF16), "w_pc": w_pc.astype(BF16),
               "w_o": w_o.astype(BF16), "w_gate": w_gate.astype(BF16), "w_up": w_up.astype(BF16),
               "w_down": w_down.astype(BF16)}
    w_main = w_in.astype(BF16)
    w_tail = jnp.concatenate(
        [w_in[:, :, COL_GATE:D_IN], w_in[:, :, COL_SMALL:COL_GATE],
         jnp.zeros((depth, d, LANES - 4 * DN_HEADS), w_in.dtype)], axis=-1).astype(BF16)
    lane_pad = ((0, 0), (2 * DN_HEADS, LANES - 4 * DN_HEADS))
    a_flat = dn_a_log.reshape(depth, 2 * DN_HEADS)
    b_flat = dn_dt_bias.reshape(depth, 2 * DN_HEADS)
    rope_tabs = _rope_tables(ts)
    cache_k4 = cache_k.reshape(bs, depth, past, ATTN_KV)
    cache_v4 = cache_v.reshape(bs, depth, past, ATTN_KV)

    xp = x_prompt.reshape(bp * tp, d)
    xs = x_sample.reshape(bs * ts, d)
    ks, vs, ss = [], [], []
    for l in range(depth):
        lw = {
            "g_pre1": g_pre1[l][None], "g_post1": g_post1[l][None], "g_pre2": g_pre2[l][None],
            "g_post2": g_post2[l][None], "layer": l, "w_in": w_main, "w_tail": w_tail, "conv_w": conv_w[l],
            "g_qn": g_qn[l][None], "g_kn": g_kn[l][None], "dn_conv_w": dn_conv_w[l],
            "dn_ac": jnp.pad(a_flat[l][None], lane_pad), "dn_bc": jnp.pad(b_flat[l][None], lane_pad),
            "dn_norm_g": dn_norm_g[l][None],
            **stacked,
        }
        mod3 = mod_all[l]
        xp, (k_l, v_l, s_l) = _layer(xp, bp, tp, mod3, bs, bp * tp, lw, None, None, None, True)
        ks.append(k_l.reshape(bp, tp, N_KV_HEADS, HEAD_DIM))
        vs.append(v_l.reshape(bp, tp, N_KV_HEADS, HEAD_DIM))
        ss.append(s_l)
        xs, _ = _layer(xs, bs, ts, mod3, 0, ts, lw, rope_tabs, (cache_k4, cache_v4, l), (state_dn, l), False)
    return (xp.reshape(bp, tp, d), xs.reshape(bs, ts, d), jnp.stack(ks, axis=1), jnp.stack(vs, axis=1),
            jnp.stack(ss, axis=1))
```

```python
import functools

import jax
import jax.numpy as jnp
import numpy as np
from jax import lax
from jax.experimental import pallas as pl
from jax.experimental.pallas import tpu as pltpu

F32 = jnp.float32
BF16 = jnp.bfloat16

D_MODEL = 1024
EPS = 1e-6
GRID_W = 64
N_HEADS = 8
N_KV_HEADS = 2
Q_PER_KV = N_HEADS // N_KV_HEADS
HEAD_DIM = 128
ATTN_Q = N_HEADS * HEAD_DIM
ATTN_KV = N_KV_HEADS * HEAD_DIM
ROPE_THETA = 10000.0
CONV_WIDTH = 512
DN_HEADS = 4
DN_DK = 128
DN_DV = 128
DN_QK = DN_HEADS * DN_DK
DN_VW = DN_HEADS * DN_DV
D_FF = 2816

SUBLANES = 8
BF16_SUBLANES = 16
LANES = 128

COL_CONV = 0
COL_Q = COL_CONV + 3 * CONV_WIDTH
COL_K = COL_Q + ATTN_Q
COL_V = COL_K + ATTN_KV
COL_DN = COL_V + ATTN_KV
COL_Z = COL_DN + 3 * DN_QK
COL_SMALL = COL_Z + DN_VW
COL_GATE = COL_SMALL + 4 * DN_HEADS
D_IN = COL_GATE + 3 * D_MODEL
TAIL_GATE = 0
TAIL_SMALL = TAIL_GATE + 3 * D_MODEL
D_TAIL = TAIL_SMALL + LANES
REST_GATE = 0
REST_CONV = REST_GATE + 3 * D_MODEL
REST_Z = REST_CONV + 3 * CONV_WIDTH
D_REST = REST_Z + DN_VW

Q_SCALE = 1.4426950408889634 * HEAD_DIM ** -0.5
DN_CHUNK = 128
DN_BASE = 16
VMEM_LIMIT = 56 * 1024 * 1024


def _cparams(sem):
    return pltpu.CompilerParams(dimension_semantics=sem, vmem_limit_bytes=VMEM_LIMIT)


def _const_spec(shape):
    nd = len(shape)
    return pl.BlockSpec(shape, lambda *_: (0,) * nd, pipeline_mode=pl.Buffered(1))


def _layer_spec(shape, layer):
    return pl.BlockSpec((None,) + tuple(shape), lambda *_: (layer,) + (0,) * len(shape),
                        pipeline_mode=pl.Buffered(1))


def _dot(a, b):
    return jnp.dot(a, b, preferred_element_type=F32)


def _dot_nt(a, b):
    return lax.dot_general(a, b, (((1,), (1,)), ((), ())), preferred_element_type=F32)


def _dot_tn(a, b):
    return lax.dot_general(a, b, (((0,), (0,)), ((), ())), preferred_element_type=F32)


def _rms(x, g):
    return x * lax.rsqrt(jnp.mean(x * x, axis=-1, keepdims=True) + EPS) * g


def _sigmoid(x):
    return 1.0 / (1.0 + jnp.exp(-x))


def _silu(x):
    return x * _sigmoid(x)


def _softplus(x):
    return jnp.maximum(x, 0.0) + jnp.log1p(jnp.exp(-jnp.abs(x)))


def _split3(x):
    hi = x.astype(BF16)
    r = x - hi.astype(F32)
    mid = r.astype(BF16)
    lo = (r - mid.astype(F32)).astype(BF16)
    return hi, mid, lo


def _mod_kernel(cv_ref, w_ref, b_ref, o_ref):
    cv = cv_ref[...]
    o_ref[0] = _dot(_silu(cv).astype(BF16), w_ref[0].astype(BF16)) + b_ref[0]


def _modulation(cv, w_mod, b_mod):
    depth, d, n6 = w_mod.shape
    rows = cv.shape[0]
    tn = 1536
    return pl.pallas_call(
        _mod_kernel,
        grid=(depth, n6 // tn),
        in_specs=[pl.BlockSpec((rows, d), lambda l, j: (0, 0)),
                  pl.BlockSpec((1, d, tn), lambda l, j: (l, 0, j)),
                  pl.BlockSpec((1, 1, tn), lambda l, j: (l, 0, j))],
        out_specs=pl.BlockSpec((1, rows, tn), lambda l, j: (l, 0, j)),
        out_shape=jax.ShapeDtypeStruct((depth, rows, n6), F32),
        compiler_params=_cparams(("arbitrary", "arbitrary")),
    )(cv, w_mod, b_mod.reshape(depth, 1, n6))


def _swap_pairs(y):
    lane = lax.broadcasted_iota(jnp.int32, y.shape, 1)
    return jnp.where(lane % 2 == 0, pltpu.roll(y, LANES - 1, axis=1), pltpu.roll(y, 1, axis=1))


def _shift_rows(x, prev_row, next_row, seq_rows):
    tm = x.shape[0]
    rows = lax.broadcasted_iota(jnp.int32, (tm, 1), 0)
    below, above = pltpu.roll(x, 1, axis=0), pltpu.roll(x, tm - 1, axis=0)
    if seq_rows >= tm:
        return jnp.where(rows == 0, prev_row, below), jnp.where(rows == tm - 1, next_row, above)
    pos = rows % seq_rows
    return jnp.where(pos == 0, 0.0, below), jnp.where(pos == seq_rows - 1, 0.0, above)


def _interleave(*gens):
    gens = list(gens)
    while gens:
        for gen in list(gens):
            try:
                next(gen)
            except StopIteration:
                gens.remove(gen)


def _inproj_kernel(*refs, rope, want_v, tm, tps, seq_rows, tn):
    refs = list(refs)
    x_ref, xp_ref, xn_ref, mod_ref, g_ref, w_ref, wt_ref, gq_ref, gk_ref, cw_ref, ac_ref, bc_ref = refs[:12]
    refs = refs[12:]
    cos_ref, sin_ref = (refs.pop(0), refs.pop(0)) if rope else (None, None)
    q_out, k_out, v_out = refs[:3]
    refs = refs[3:]
    vf_out = refs.pop(0) if want_v else None
    qn_out, kn_out, vn_out, gb_out, gbt_out, rest_out = refs
    i = pl.program_id(0)

    def modnorm(x):
        return (_rms(x, g_ref[...]) * (1.0 + mod_ref[0, 1:2, :]) + mod_ref[0, 0:1, :]).astype(BF16)

    h = modnorm(x_ref[...])
    h_halo = modnorm(jnp.concatenate([xp_ref[...], xn_ref[...]], axis=0))

    def qk_head(x, g):
        y = _rms(x, g)
        if rope:
            y = y * cos_ref[...] + _swap_pairs(y) * sin_ref[...]
        return y

    def attn_epilogue():
        qkv = _dot(h, w_ref[:, COL_Q:COL_DN])
        yield
        for hh in range(N_HEADS):
            sl = slice(hh * HEAD_DIM, (hh + 1) * HEAD_DIM)
            q_out[:, sl] = (qk_head(qkv[:, sl], gq_ref[...]) * Q_SCALE).astype(q_out.dtype)
            if hh % 2 == 1:
                yield
        for hh in range(N_KV_HEADS):
            sl = slice(hh * HEAD_DIM, (hh + 1) * HEAD_DIM)
            k_out[:, sl] = qk_head(qkv[:, ATTN_Q + hh * HEAD_DIM:ATTN_Q + (hh + 1) * HEAD_DIM],
                                   gk_ref[...]).astype(k_out.dtype)
        v = qkv[:, ATTN_Q + ATTN_KV:]
        v_out[...] = v.T.astype(v_out.dtype)
        if want_v:
            vf_out[...] = v

    def dn_epilogue():
        dn = _dot(h, w_ref[:, COL_DN:COL_Z])
        dn_halo = _dot(h_halo, w_ref[:, COL_DN:COL_Z])
        gl = _dot(h, wt_ref[:, TAIL_SMALL:D_TAIL])
        yield
        prev_row = jnp.where((i % tps) == 0, 0.0, dn_halo[SUBLANES - 1:SUBLANES, :])
        next_row = jnp.where((i % tps) == tps - 1, 0.0, dn_halo[SUBLANES:SUBLANES + 1, :])
        dn_prev, dn_next = _shift_rows(dn, prev_row, next_row, seq_rows)
        cw = cw_ref[...]

        def conv_silu(sl):
            return _silu(cw[0:1, sl] * dn_prev[:, sl] + cw[1:2, sl] * dn[:, sl] + cw[2:3, sl] * dn_next[:, sl])

        for hh in range(DN_HEADS):
            sl = slice(hh * DN_DK, (hh + 1) * DN_DK)
            qh = conv_silu(sl)
            kh = conv_silu(slice(DN_QK + hh * DN_DK, DN_QK + (hh + 1) * DN_DK))
            qn_out[:, sl] = (qh * lax.rsqrt(jnp.sum(qh * qh, axis=-1, keepdims=True) + EPS)
                             * (DN_DK ** -0.5)).astype(qn_out.dtype)
            kn_out[:, sl] = (kh * lax.rsqrt(jnp.sum(kh * kh, axis=-1, keepdims=True) + EPS)).astype(kn_out.dtype)
            vn_out[:, sl] = conv_silu(slice(2 * DN_QK + hh * DN_DV, 2 * DN_QK + (hh + 1) * DN_DV)
                                      ).astype(vn_out.dtype)
            yield
        lane = lax.broadcasted_iota(jnp.int32, gl.shape, 1)
        gb = jnp.where(lane < 2 * DN_HEADS, _sigmoid(gl), -jnp.exp(ac_ref[...]) * _softplus(gl + bc_ref[...]))
        gb_out[...] = gb
        gbt_out[...] = gb.T

    def remaining():
        groups = [(wt_ref, TAIL_GATE, REST_GATE, 3 * D_MODEL), (w_ref, COL_CONV, REST_CONV, 3 * CONV_WIDTH),
                  (w_ref, COL_Z, REST_Z, DN_VW)]
        for ref, src, dst, width in groups:
            for j in range(0, width, tn):
                rest_out[:, dst + j:dst + j + tn] = _dot(h, ref[:, src + j:src + j + tn]).astype(rest_out.dtype)
                yield

    _interleave(dn_epilogue(), attn_epilogue(), remaining())


def _inproj(x, t, mod3, mod_base, rows_per_mod, lw, rope_tabs, want_v):
    n = x.shape[0]
    tm = 512 if rope_tabs is None else min(512, t)
    assert n % tm == 0 and (t % tm == 0 or tm % t == 0) and rows_per_mod % tm == 0
    tps = max(t // tm, 1)
    halo = tm // SUBLANES
    w_main, w_tail, layer = lw["w_in"], lw["w_tail"], lw["layer"]
    in_specs = [pl.BlockSpec((tm, D_MODEL), lambda i: (i, 0)),
                pl.BlockSpec((SUBLANES, D_MODEL), lambda i: (jnp.maximum(i * halo - 1, 0), 0)),
                pl.BlockSpec((SUBLANES, D_MODEL), lambda i: (jnp.minimum((i + 1) * halo, n // SUBLANES - 1), 0)),
                pl.BlockSpec((1, 6, D_MODEL), lambda i: (mod_base + (i * tm) // rows_per_mod, 0, 0)),
                _const_spec((1, D_MODEL)),
                _layer_spec((D_MODEL, COL_SMALL), layer), _layer_spec((D_MODEL, D_TAIL), layer),
                _const_spec((1, HEAD_DIM)), _const_spec((1, HEAD_DIM)),
                _const_spec((3, 3 * DN_QK)), _const_spec((1, LANES)), _const_spec((1, LANES))]
    args = [x, x, x, mod3, lw["g_pre1"], w_main, w_tail, lw["g_qn"], lw["g_kn"], lw["dn_conv_w"],
            lw["dn_ac"], lw["dn_bc"]]
    if rope_tabs is not None:
        in_specs += [pl.BlockSpec((tm, HEAD_DIM), lambda i: (i % tps, 0))] * 2
        args += list(rope_tabs)

    outs = [(ATTN_Q, BF16, False), (ATTN_KV, F32 if want_v else BF16, False), (ATTN_KV, BF16, True)]
    if want_v:
        outs.append((ATTN_KV, F32, False))
    outs += [(DN_QK, BF16, False), (DN_QK, BF16, False), (DN_VW, BF16, False), (LANES, F32, False),
             (LANES, F32, True), (D_REST, BF16, False)]
    res = pl.pallas_call(
        functools.partial(_inproj_kernel, rope=rope_tabs is not None, want_v=want_v, tm=tm, tps=tps, seq_rows=t,
                          tn=512),
        grid=(n // tm,),
        in_specs=in_specs,
        out_specs=[pl.BlockSpec((w, tm), lambda i: (0, i)) if tr else pl.BlockSpec((tm, w), lambda i: (i, 0))
                   for w, _, tr in outs],
        out_shape=[jax.ShapeDtypeStruct((w, n) if tr else (n, w), dt) for w, dt, tr in outs],
        compiler_params=_cparams(("arbitrary",)),
    )(*args)
    res = list(res)
    q_p, k_p, vt_p = res[:3]
    v_f32 = res[3] if want_v else None
    qn, kn, vn, gb, gbt, rest = res[-6:]
    return q_p, k_p, vt_p, v_f32, qn, kn, vn, gb, gbt, rest


def _attn_kernel(*refs, has_cache, t, tk, tq, nq, per_step):
    refs = list(refs)
    q_ref = refs.pop(0)
    qn_ref = refs.pop(0) if nq > 1 else None
    kc_ref, vc_ref = (refs.pop(0), refs.pop(0)) if has_cache else (None, None)
    k_ref, v_ref, o_ref = refs[:3]
    refs = refs[3:]
    vct_sc = refs.pop() if has_cache else None
    s_bufs, m_bufs = refs[:len(refs) // 2], refs[len(refs) // 2:]
    qi = pl.program_id(2)

    if has_cache:
        @pl.when(qi == 0)
        def _():
            vct_sc[...] = vc_ref[...].T.astype(BF16)

    segs = []
    if has_cache:
        past = kc_ref.shape[0]
        segs += [(kc_ref, vct_sc, r, min(tk, past - r)) for r in range(0, past, tk)]
    segs += [(k_ref, v_ref, r, tk) for r in range(0, t, tk)]
    offs = [sum(w for _, _, _, w in segs[:i]) for i in range(len(segs))]

    def stack(ref, row0=0):
        q = ref[row0:row0 + tq, :]
        return jnp.concatenate([q[:, g * HEAD_DIM:(g + 1) * HEAD_DIM] for g in range(Q_PER_KV)], axis=0)

    def scores(qs, s_ref, i, m_run):
        kr, _, r, w = segs[i]
        s = _dot_nt(kr[r:r + w, :].astype(BF16), qs)
        s_ref[offs[i]:offs[i] + w, :] = s
        for r0 in range(0, w, SUBLANES):
            blk = s[r0:r0 + SUBLANES, :]
            m_run = blk if m_run is None else jnp.maximum(m_run, blk)
        return m_run

    def weighted(s_ref, i, m, acc):
        _, vr, r, w = segs[i]
        p = jnp.exp2(s_ref[offs[i]:offs[i] + w, :] - m)
        den = p[0:SUBLANES, :]
        for r0 in range(SUBLANES, w, SUBLANES):
            den = den + p[r0:r0 + SUBLANES, :]
        pv = _dot(vr[:, r:r + w].astype(BF16), p.astype(BF16))
        return (pv, den) if acc is None else (acc[0] + pv, acc[1] + den)

    def finish(acc, row0=0):
        o = (acc[0] / jnp.sum(acc[1], axis=0, keepdims=True)).T
        for g in range(Q_PER_KV):
            o_ref[row0:row0 + tq, g * HEAD_DIM:(g + 1) * HEAD_DIM] = o[g * tq:(g + 1) * tq].astype(o_ref.dtype)

    if nq > 1:
        @pl.when(qi == 0)
        def _():
            qs0 = stack(q_ref)
            m_run = None
            for i in range(len(segs)):
                m_run = scores(qs0, s_bufs[0], i, m_run)
            m_bufs[0][...] = m_run

        def step(s_cur, m_cur, s_nxt, m_nxt, qs_next, row0):
            m = jnp.max(m_cur[...], axis=0, keepdims=True)
            acc, m_run = None, None
            for i in range(len(segs)):
                m_run = scores(qs_next, s_nxt, i, m_run)
                acc = weighted(s_cur, i, m, acc)
            m_nxt[...] = m_run
            finish(acc, row0)

        for u in range(per_step):
            qs_next = stack(q_ref, (u + 1) * tq) if u + 1 < per_step else stack(qn_ref)
            step(s_bufs[u % 2], m_bufs[u % 2], s_bufs[1 - u % 2], m_bufs[1 - u % 2], qs_next, u * tq)
    else:
        qs = stack(q_ref)
        m_run = None
        for i in range(len(segs)):
            m_run = scores(qs, s_bufs[0], i, m_run)
        m = jnp.max(m_run, axis=0, keepdims=True)
        acc = None
        for i in range(len(segs)):
            acc = weighted(s_bufs[0], i, m, acc)
        finish(acc)


def _attention(q_p, k_p, vt_p, b, t, cache):
    n = q_p.shape[0]
    tq = 128 if t > 256 else 256
    tk = min(t, 512)
    nq = t // tq
    per_step = 4 if nq % 4 == 0 else (2 if nq % 2 == 0 else 1)
    assert nq == 1 or per_step > 1
    steps = nq // per_step
    qw = Q_PER_KV * HEAD_DIM
    in_specs = [pl.BlockSpec((per_step * tq, qw), lambda bi, j, qi: (bi * steps + qi, j))]
    args = [q_p]
    if nq > 1:
        in_specs.append(pl.BlockSpec((tq, qw),
                                     lambda bi, j, qi: (bi * nq + jnp.minimum(per_step * (qi + 1), nq - 1), j)))
        args.append(q_p)
    if cache is not None:
        cache_k, cache_v, layer = cache
        past = cache_k.shape[2]
        cspec = pl.BlockSpec((None, None, past, HEAD_DIM), lambda bi, j, qi: (bi, layer, 0, j))
        in_specs += [cspec, cspec]
        args += [cache_k, cache_v]
    in_specs += [pl.BlockSpec((t, HEAD_DIM), lambda bi, j, qi: (bi, j)),
                 pl.BlockSpec((HEAD_DIM, t), lambda bi, j, qi: (j, bi))]
    args += [k_p, vt_p]
    n_keys = t + (cache[0].shape[2] if cache is not None else 0)
    slots = 2 if nq > 1 else 1
    return pl.pallas_call(
        functools.partial(_attn_kernel, has_cache=cache is not None, t=t, tk=tk, tq=tq, nq=nq,
                          per_step=per_step),
        grid=(b, N_KV_HEADS, steps),
        in_specs=in_specs,
        out_specs=pl.BlockSpec((per_step * tq, qw), lambda bi, j, qi: (bi * steps + qi, j)),
        out_shape=jax.ShapeDtypeStruct((n, ATTN_Q), BF16),
        scratch_shapes=([pltpu.VMEM((n_keys, Q_PER_KV * tq), F32)] * slots
                        + [pltpu.VMEM((SUBLANES, Q_PER_KV * tq), F32)] * slots
                        + ([pltpu.VMEM((HEAD_DIM, cache[0].shape[2]), BF16)] if cache is not None else [])),
        compiler_params=_cparams(("arbitrary", "arbitrary", "arbitrary")),
    )(*args)


def _lane_pick(x, lane):
    idx = lax.broadcasted_iota(jnp.int32, x.shape, 1)
    return jnp.sum(jnp.where(idx == lane, x, 0.0), axis=-1, keepdims=True)


def _block_diag(x2):
    xb = x2.astype(BF16)
    z = jnp.zeros((DN_CHUNK, DN_CHUNK), BF16)
    return jnp.concatenate([jnp.concatenate([xb[:, :DN_CHUNK], z], axis=1),
                            jnp.concatenate([z, xb[:, DN_CHUNK:]], axis=1)], axis=0)


def _mm_pair(x2, y2):
    return _dot(x2.astype(BF16), _block_diag(y2))


def _unit_tri_inverses(lmats, ri, ci):
    def blk(s):
        return (ri ^ ci) < s

    eye = jnp.where(ri == ci, 1.0, 0.0)
    ps = [jnp.where(blk(DN_BASE), -lm, 0.0) for lm in lmats]
    xs = [eye + p for p in ps]
    s = 2
    while s < DN_BASE:
        ps = [_mm_pair(p, p) for p in ps]
        yield
        xs = [x + _mm_pair(x, p) for x, p in zip(xs, ps)]
        yield
        s *= 2
    s = DN_BASE
    while s < DN_CHUNK:
        sel = blk(2 * s) & jnp.logical_not(blk(s))
        ts = [_mm_pair(jnp.where(sel, lm, 0.0), x) for lm, x in zip(lmats, xs)]
        yield
        xs = [x - _mm_pair(x, t) for x, t in zip(xs, ts)]
        yield
        s *= 2
    return xs


def _dn_kernel(*refs, reverse, has_s0, want_state, n_seq, cps, d, nt):
    nc = n_seq * cps
    refs = list(refs)
    q_ref, k_ref, v_ref, gb_ref, gr_ref = refs[:5]
    refs = refs[5:]
    s0_ref = refs.pop(0) if has_s0 else None
    if want_state:
        refs.pop(0)
    o_ref = refs.pop(0)
    sfin_ref = refs.pop(0) if want_state else None
    s_sc = refs.pop(0)
    bufs = (refs[:5], refs[5:10])
    n_pairs = DN_HEADS // 2
    c2 = 2 * DN_CHUNK
    tt = nc * DN_CHUNK
    insts = [(c, p) for c in range(nc) for p in range(n_pairs)]

    g = pl.program_id(0)
    i_scan = (g - 1) % nt

    def pair_cols(x, lane_a, lane_b):
        shape = (x.shape[0], DN_CHUNK)
        return jnp.concatenate([jnp.broadcast_to(_lane_pick(x, lane_a), shape),
                                jnp.broadcast_to(_lane_pick(x, lane_b), shape)], axis=1)

    def prepare(buf):
        u_buf, wq_buf, attn_buf, kt_buf, gt_buf = buf
        gb = gb_ref[...]
        g_r = gr_ref[...]
        bi = lax.broadcasted_iota(jnp.int32, (tt, tt), 0)
        bj = lax.broadcasted_iota(jnp.int32, (tt, tt), 1)
        same = (bi ^ bj) < DN_CHUNK
        if reverse:
            tri_c = jnp.where(same & (bj >= bi), 1.0, 0.0).astype(BF16)
            tri_r = jnp.where(same & (bi >= bj), 1.0, 0.0).astype(BF16)
        else:
            tri_c = jnp.where(same & (bj <= bi), 1.0, 0.0).astype(BF16)
            tri_r = jnp.where(same & (bi <= bj), 1.0, 0.0).astype(BF16)
        gcum_c = sum(_dot(tri_c, part) for part in _split3(gb))
        gcum_r = sum(_dot(part, tri_r) for part in _split3(g_r))
        yield

        ri = lax.broadcasted_iota(jnp.int32, (DN_CHUNK, c2), 0)
        ci = lax.broadcasted_iota(jnp.int32, (DN_CHUNK, c2), 1) & (DN_CHUNK - 1)
        incl = (ci >= ri) if reverse else (ci <= ri)
        strict = (ci > ri) if reverse else (ci < ri)
        pre = []
        for c, p in insts:
            rs = slice(c * DN_CHUNK, (c + 1) * DN_CHUNK)
            cs = slice(p * c2, (p + 1) * c2)
            lane_b = d * DN_HEADS + 2 * p
            lane_g = 2 * DN_HEADS + d * DN_HEADS + 2 * p
            q2 = q_ref[rs, cs].astype(F32)
            k2 = k_ref[rs, cs].astype(F32)
            v2 = v_ref[rs, cs].astype(F32)
            g_i = pair_cols(gcum_c[rs, :], lane_g, lane_g + 1)
            b_i = pair_cols(gb[rs, :], lane_b, lane_b + 1)
            g_j = jnp.concatenate([gcum_r[lane_g:lane_g + 1, rs], gcum_r[lane_g + 1:lane_g + 2, rs]], axis=1)
            g_tot = g_i[0:1, :] if reverse else g_i[DN_CHUNK - 1:DN_CHUNK, :]
            decay = jnp.where(incl, jnp.exp(jnp.where(incl, g_i - g_j, 0.0)), 0.0)
            pre.append(dict(q2=q2, k2=k2, v2=v2, g_i=g_i, b_i=b_i, g_tot=g_tot, decay=decay, kb=k2 * b_i))
        a2s = [_dot_nt(jnp.concatenate([p["kb"], p["q2"]], axis=0).astype(BF16), _block_diag(p["k2"]))
               for p in pre]
        yield
        lmats = [jnp.where(strict, a2[:DN_CHUNK] * p["decay"], 0.0) for a2, p in zip(a2s, pre)]
        for n, (a2, p) in enumerate(zip(a2s, pre)):
            attn_buf[n] = (a2[DN_CHUNK:] * p["decay"]).astype(BF16)
        tinvs = yield from _unit_tri_inverses(lmats, ri, ci)
        gams = [jnp.exp(p["g_i"]) for p in pre]
        us = [_mm_pair(t, p["v2"] * p["b_i"]) for t, p in zip(tinvs, pre)]
        yield
        ws = [_mm_pair(t, p["kb"] * gam) for t, p, gam in zip(tinvs, pre, gams)]
        yield
        for n, (p, u, w, gam) in enumerate(zip(pre, us, ws, gams)):
            u_buf[n] = u
            wq_buf[n] = jnp.concatenate([w, p["q2"] * gam], axis=0).astype(BF16)
            kt_buf[n] = (p["k2"] * jnp.exp(p["g_tot"] - p["g_i"])).astype(BF16)
            gt_buf[n] = jnp.broadcast_to(jnp.exp(p["g_tot"]), (SUBLANES, c2))

    def scan(buf):
        u_buf, wq_buf, attn_buf, kt_buf, gt_buf = buf
        si = lax.broadcasted_iota(jnp.int32, (c2, c2), 0)
        sj = lax.broadcasted_iota(jnp.int32, (c2, c2), 1)
        on_diag = (si < DN_CHUNK) == (sj < DN_CHUNK)
        chains = [(sq, p) for sq in range(n_seq) for p in range(n_pairs)]
        states = [s_sc[sq * n_pairs + p] for sq, p in chains]
        for k in (range(cps - 1, -1, -1) if reverse else range(cps)):
            cs = [sq * cps + k for sq, _ in chains]
            ns = [c * n_pairs + p for c, (_, p) in zip(cs, chains)]
            m1s = [_dot(wq_buf[n], s.astype(BF16)) for n, s in zip(ns, states)]
            yield
            v_news = [u_buf[n] - m1[:DN_CHUNK] for n, m1 in zip(ns, m1s)]
            outs = [m1[DN_CHUNK:] + _mm_pair(attn_buf[n], v) for n, m1, v in zip(ns, m1s, v_news)]
            yield
            states = [s * gt_buf[n][0:1, :] + jnp.where(on_diag, _dot_tn(kt_buf[n], v.astype(BF16)), 0.0)
                      for n, s, v in zip(ns, states, v_news)]
            for c, (_, p), out in zip(cs, chains, outs):
                o_ref[c * DN_CHUNK:(c + 1) * DN_CHUNK, p * c2:(p + 1) * c2] = out
            yield
        for (sq, p), s in zip(chains, states):
            s_sc[sq * n_pairs + p] = s

    def step(par):
        @pl.when(i_scan == 0)
        def _():
            s_sc[...] = jnp.zeros_like(s_sc)
            if has_s0:
                for h in range(DN_HEADS):
                    lo = (h % 2) * DN_CHUNK
                    s_sc[h // 2, lo:lo + DN_CHUNK, lo:lo + DN_CHUNK] = s0_ref[h]

        _interleave(prepare(bufs[par]), scan(bufs[1 - par]))

        if want_state:
            @pl.when(i_scan == nt - 1)
            def _():
                for sq in range(n_seq):
                    for h in range(DN_HEADS):
                        lo = (h % 2) * DN_CHUNK
                        sfin_ref[sq, h] = s_sc[sq * n_pairs + h // 2, lo:lo + DN_CHUNK, lo:lo + DN_CHUNK]

    pl.when(g == 0)(lambda: _interleave(prepare(bufs[0])))
    pl.when((g > 0) & (g % 2 == 0))(functools.partial(step, 0))
    pl.when(g % 2 == 1)(functools.partial(step, 1))


def _deltanet_dir(qn, kn, vn, gb, gbt, s0, b, t, d, want_state):
    n = qn.shape[0]
    chunks_per_tile = 4
    cps = min(chunks_per_tile, t // DN_CHUNK)
    n_seq = chunks_per_tile // cps
    assert b % n_seq == 0 and (s0 is None or n_seq == 1)
    nc = n_seq * cps
    tt = nc * DN_CHUNK
    nt = t // (cps * DN_CHUNK)
    n_tiles = (b // n_seq) * nt
    reverse = d == 1
    c2 = 2 * DN_CHUNK
    n_inst = nc * (DN_HEADS // 2)

    def seq(gt):
        bi, i = gt // nt, gt % nt
        return bi, ((nt - 1 - i) if reverse else i)

    def prep_tile(g):
        return seq(jnp.minimum(g, n_tiles - 1))

    def scan_tile(g):
        return seq(jnp.maximum(g - 1, 0))

    def row_block(bt):
        return bt[0] * nt + bt[1]

    qkv_spec = pl.BlockSpec((tt, DN_QK), lambda g: (row_block(prep_tile(g)), 0))
    in_specs = [
        qkv_spec, qkv_spec, qkv_spec,
        pl.BlockSpec((tt, LANES), lambda g: (row_block(prep_tile(g)), 0)),
        pl.BlockSpec((4 * DN_HEADS, tt), lambda g: (0, row_block(prep_tile(g)))),
    ]
    args = [qn, kn, vn, gb, gbt]
    if s0 is not None:
        state, layer = s0
        in_specs.append(pl.BlockSpec((None, None, None, DN_HEADS, DN_DK, DN_DV),
                                     lambda g: (scan_tile(g)[0], layer, d, 0, 0, 0)))
        args.append(state)
    out_specs = [pl.BlockSpec((tt, DN_VW), lambda g: (row_block(scan_tile(g)), 0))]
    out_shape = [jax.ShapeDtypeStruct((n, DN_VW), F32)]
    aliases = {}
    if want_state:
        state_out, out_layer = want_state
        in_specs.append(pl.BlockSpec(memory_space=pl.ANY))
        args.append(state_out)
        aliases = {len(args) - 1: 1}
        out_specs.append(pl.BlockSpec((n_seq, None, None, DN_HEADS, DN_DK, DN_DV),
                                      lambda g: (scan_tile(g)[0], out_layer, d, 0, 0, 0)))
        out_shape.append(jax.ShapeDtypeStruct(state_out.shape, state_out.dtype))
    prepared = [pltpu.VMEM((n_inst, DN_CHUNK, c2), F32), pltpu.VMEM((n_inst, c2, c2), BF16),
                pltpu.VMEM((n_inst, DN_CHUNK, c2), BF16), pltpu.VMEM((n_inst, DN_CHUNK, c2), BF16),
                pltpu.VMEM((n_inst, SUBLANES, c2), F32)]
    res = pl.pallas_call(
        functools.partial(_dn_kernel, reverse=reverse, has_s0=s0 is not None, want_state=bool(want_state),
                          n_seq=n_seq, cps=cps, d=d, nt=nt),
        grid=(n_tiles + 1,),
        in_specs=in_specs,
        out_specs=out_specs,
        out_shape=out_shape,
        input_output_aliases=aliases,
        scratch_shapes=[pltpu.VMEM((n_seq * (DN_HEADS // 2), c2, c2), F32)] + prepared + prepared,
        compiler_params=_cparams(("arbitrary",)),
    )(*args)
    return res if want_state else (res[0], None)


def _post_kernel(x_ref, mod_ref, cb_ref, cc_ref, cx_ref, ccp_ref, cxp_ref, ccn_ref, cxn_ref, cw_ref,
                 att_ref, of_ref, ob_ref, z_ref, ng_ref, ga_ref, gb_ref, gc_ref, wpa_ref, wpb_ref, wpc_ref, wo_ref,
                 gpost_ref, o_ref, *, tm, tps, seq_rows):
    i = pl.program_id(0)
    first = (i % tps) == 0
    last = (i % tps) == tps - 1
    u = cc_ref[...].astype(F32) * cx_ref[...].astype(F32)
    hl = BF16_SUBLANES - 1
    prev_row = jnp.where(first, 0.0, ccp_ref[hl:hl + 1, :].astype(F32) * cxp_ref[hl:hl + 1, :].astype(F32))
    next_row = jnp.where(last, 0.0, ccn_ref[0:1, :].astype(F32) * cxn_ref[0:1, :].astype(F32))
    u_prev, u_next = _shift_rows(u, prev_row, next_row, seq_rows)
    cw = cw_ref[...]
    conv =cw[0:1, :] * u_prev + cw[1:2, :] * u + cw[2:3, :] * u_next
    ya = _dot((cb_ref[...].astype(F32) * conv).astype(BF16), wpa_ref[...])
    yb = _dot(att_ref[...], wpb_ref[...])
    o = of_ref[...] + ob_ref[...]
    z = z_ref[...].astype(F32)
    parts = []
    for h in range(DN_HEADS):
        sl = slice(h * DN_DV, (h + 1) * DN_DV)
        parts.append((_rms(o[:, sl], ng_ref[...]) * _silu(z[:, sl])).astype(BF16))
    yc = _dot(jnp.concatenate(parts, axis=1), wpc_ref[...])
    mix_in = (_sigmoid(ga_ref[...].astype(F32)) * ya + _sigmoid(gb_ref[...].astype(F32)) * yb
              + _sigmoid(gc_ref[...].astype(F32)) * yc)
    mix = _dot(mix_in.astype(BF16), wo_ref[...])
    o_ref[...] = x_ref[...] + mod_ref[0, 2:3, :] * _rms(mix, gpost_ref[...])


def _post_mixer(x, mod3, mod_base, rows_per_mod, t, proj, att, o_f, o_b, conv_w, ng, wpa, wpb, wpc, wo, layer,
                g_post1):
    n = x.shape[0]
    tm = 512
    assert n % tm == 0 and (t % tm == 0 or tm % t == 0) and rows_per_mod % tm == 0
    tps = max(t // tm, 1)
    hb = tm // BF16_SUBLANES
    cwid = CONV_WIDTH
    c0 = REST_CONV // cwid

    def prev(i):
        return jnp.maximum(i * hb - 1, 0)

    def nxt(i):
        return jnp.minimum((i + 1) * hb, n // BF16_SUBLANES - 1)

    g0 = REST_GATE // D_MODEL
    in_specs = [
        pl.BlockSpec((tm, D_MODEL), lambda i: (i, 0)),
        pl.BlockSpec((1, 6, D_MODEL), lambda i: (mod_base + (i * tm) // rows_per_mod, 0, 0)),
        pl.BlockSpec((tm, cwid), lambda i: (i, c0)),
        pl.BlockSpec((tm, cwid), lambda i: (i, c0 + 1)),
        pl.BlockSpec((tm, cwid), lambda i: (i, c0 + 2)),
        pl.BlockSpec((BF16_SUBLANES, cwid), lambda i: (prev(i), c0 + 1)),
        pl.BlockSpec((BF16_SUBLANES, cwid), lambda i: (prev(i), c0 + 2)),
        pl.BlockSpec((BF16_SUBLANES, cwid), lambda i: (nxt(i), c0 + 1)),
        pl.BlockSpec((BF16_SUBLANES, cwid), lambda i: (nxt(i), c0 + 2)),
        _const_spec((3, cwid)),
        pl.BlockSpec((tm, ATTN_Q), lambda i: (i, 0)),
        pl.BlockSpec((tm, DN_VW), lambda i: (i, 0)),
        pl.BlockSpec((tm, DN_VW), lambda i: (i, 0)),
        pl.BlockSpec((tm, DN_VW), lambda i: (i, REST_Z // DN_VW)),
        _const_spec((1, DN_DV)),
        pl.BlockSpec((tm, D_MODEL), lambda i: (i, g0)),
        pl.BlockSpec((tm, D_MODEL), lambda i: (i, g0 + 1)),
        pl.BlockSpec((tm, D_MODEL), lambda i: (i, g0 + 2)),
        _layer_spec((CONV_WIDTH, D_MODEL), layer), _layer_spec((ATTN_Q, D_MODEL), layer),
        _layer_spec((DN_VW, D_MODEL), layer), _layer_spec((D_MODEL, D_MODEL), layer),
        _const_spec((1, D_MODEL)),
    ]
    return pl.pallas_call(
        functools.partial(_post_kernel, tm=tm, tps=tps, seq_rows=t),
        grid=(n // tm,),
        in_specs=in_specs,
        out_specs=pl.BlockSpec((tm, D_MODEL), lambda i: (i, 0)),
        out_shape=jax.ShapeDtypeStruct((n, D_MODEL), F32),
        compiler_params=_cparams(("arbitrary",)),
    )(x, mod3, proj, proj, proj, proj, proj, proj, proj, conv_w, att, o_f, o_b, proj, ng,
      proj, proj, proj, wpa, wpb, wpc, wo, g_post1)


def _ffn_kernel(x_ref, mod_ref, gpre_ref, wg_ref, wu_ref, wd_ref, gpost_ref, o_ref, act_sc, *, tf):
    x = x_ref[...]
    h2 = (_rms(x, gpre_ref[...]) * (1.0 + mod_ref[0, 4:5, :]) + mod_ref[0, 3:4, :]).astype(BF16)
    for j in range(0, D_FF, tf):
        gate = _dot(h2, wg_ref[:, j:j + tf])
        up = _dot(h2, wu_ref[:, j:j + tf])
        act_sc[:, j:j + tf] = (_silu(gate) * up).astype(BF16)
    ffn = _dot(act_sc[...], wd_ref[...])
    o_ref[...] = x + mod_ref[0, 5:6, :] * _rms(ffn, gpost_ref[...])


def _ffn(x, mod3, mod_base, rows_per_mod, g_pre2, wg, wu, wd, layer, g_post2):
    n = x.shape[0]
    tm = min(1024, n)
    assert n % tm == 0 and rows_per_mod % tm == 0
    return pl.pallas_call(
        functools.partial(_ffn_kernel, tf=256),
        grid=(n // tm,),
        in_specs=[pl.BlockSpec((tm, D_MODEL), lambda i: (i, 0)),
                  pl.BlockSpec((1, 6, D_MODEL), lambda i: (mod_base + (i * tm) // rows_per_mod, 0, 0)),
                  _const_spec((1, D_MODEL)),
                  _layer_spec((D_MODEL, D_FF), layer), _layer_spec((D_MODEL, D_FF), layer),
                  _layer_spec((D_FF, D_MODEL), layer),
                  _const_spec((1, D_MODEL))],
        out_specs=pl.BlockSpec((tm, D_MODEL), lambda i: (i, 0)),
        out_shape=jax.ShapeDtypeStruct((n, D_MODEL), F32),
        scratch_shapes=[pltpu.VMEM((tm, D_FF), BF16)],
        compiler_params=_cparams(("arbitrary",)),
    )(x, mod3, g_pre2, wg, wu, wd, g_post2)


def _rope_tables(t):
    rows = t // GRID_W
    row_id = np.repeat(np.arange(rows, dtype=np.float32), GRID_W)
    col_id = np.tile(np.arange(GRID_W, dtype=np.float32), rows)
    n_freq = HEAD_DIM // 4
    inv_freq = (np.float32(ROPE_THETA) ** (-np.arange(n_freq, dtype=np.float32) / np.float32(n_freq))).astype(np.float32)
    ang = np.concatenate([row_id[:, None] * inv_freq, col_id[:, None] * inv_freq], axis=-1).astype(np.float32)
    cos = np.repeat(np.cos(ang), 2, axis=-1).astype(np.float32)
    sin = np.repeat(np.sin(ang), 2, axis=-1).astype(np.float32)
    sign = np.tile(np.array([-1.0, 1.0], np.float32), HEAD_DIM // 2)
    return jnp.asarray(cos), jnp.asarray(sin * sign)


def _layer(x, b, t, mod3, mod_base, rows_per_mod, lw, rope_tabs, cache, state0, want_state):
    q_p, k_p, vt_p, v_f32, qn, kn, vn, gb, gbt, proj = _inproj(x, t, mod3, mod_base, rows_per_mod, lw,
                                                               rope_tabs, want_state is not None)
    att = _attention(q_p, k_p, vt_p, b, t, cache)
    dn = []
    for d in range(2):
        o_d, state_buf = _deltanet_dir(qn, kn, vn, gb, gbt, state0, b, t, d,
                                       (want_state, lw["layer"]) if want_state is not None else None)
        want_state = state_buf if want_state is not None else None
        dn.append((o_d, state_buf))
    x = _post_mixer(x, mod3, mod_base, rows_per_mod, t, proj, att, dn[0][0], dn[1][0], lw["conv_w"],
                    lw["dn_norm_g"], lw["w_pa"], lw["w_pb"], lw["w_pc"], lw["w_o"], lw["layer"], lw["g_post1"])
    x = _ffn(x, mod3, mod_base, rows_per_mod, lw["g_pre2"], lw["w_gate"], lw["w_up"], lw["w_down"],
             lw["layer"], lw["g_post2"])
    if want_state is None:
        return x, None
    return x, (k_p, v_f32, want_state)


def kernel(x_prompt, x_sample, cache_k, cache_v, state_dn, c, c_ctx, w_mod, b_mod, g_pre1, g_post1, g_pre2, g_post2, w_in, conv_w, g_qn, g_kn, dn_conv_w, dn_a_log, dn_dt_bias, dn_norm_g, w_pa, w_pb, w_pc, w_o, w_gate, w_up, w_down):
    bp, tp, d = x_prompt.shape
    bs, ts, _ = x_sample.shape
    depth = w_mod.shape[0]
    past = cache_k.shape[2]

    mod_rows = -(-(bs + 1) // SUBLANES) * SUBLANES
    cv = jnp.zeros((mod_rows, d), F32).at[:bs].set(c).at[bs].set(c_ctx)
    mod_all = _modulation(cv, w_mod, b_mod).reshape(depth, mod_rows, 6, d)

    stacked = {"w_pa": w_pa.astype(BF16), "w_pb": w_pb.astype(BF16), "w_pc": w_pc.astype(BF16),
               "w_o": w_o.astype(BF16), "w_gate": w_gate.astype(BF16), "w_up": w_up.astype(BF16),
               "w_down": w_down.astype(BF16)}
    w_main = w_in.astype(BF16)
    w_tail = jnp.concatenate(
        [w_in[:, :, COL_GATE:D_IN], w_in[:, :, COL_SMALL:COL_GATE],
         jnp.zeros((depth, d, LANES - 4 * DN_HEADS), w_in.dtype)], axis=-1).astype(BF16)
    lane_pad = ((0, 0), (2 * DN_HEADS, LANES - 4 * DN_HEADS))
    a_flat = dn_a_log.reshape(depth, 2 * DN_HEADS)
    b_flat = dn_dt_bias.reshape(depth, 2 * DN_HEADS)
    rope_tabs = _rope_tables(ts)
    cache_k4 = cache_k.reshape(bs, depth, past, ATTN_KV)
    cache_v4 = cache_v.reshape(bs, depth, past, ATTN_KV)

    xp = x_prompt.reshape(bp * tp, d)
    xs = x_sample.reshape(bs * ts, d)
    ks, vs = [], []
    new_state = jnp.zeros((bp, depth, 2, DN_HEADS, DN_DK, DN_DV), F32)
    for l in range(depth):
        lw = {
            "g_pre1": g_pre1[l][None], "g_post1": g_post1[l][None], "g_pre2": g_pre2[l][None],
            "g_post2": g_post2[l][None], "layer": l, "w_in": w_main, "w_tail": w_tail, "conv_w": conv_w[l],
            "g_qn": g_qn[l][None], "g_kn": g_kn[l][None], "dn_conv_w": dn_conv_w[l],
            "dn_ac": jnp.pad(a_flat[l][None], lane_pad), "dn_bc": jnp.pad(b_flat[l][None], lane_pad),
            "dn_norm_g": dn_norm_g[l][None],
            **stacked,
        }
        mod3 = mod_all[l]
        xp, (k_l, v_l, new_state) = _layer(xp, bp, tp, mod3, bs, bp * tp, lw, None, None, None, new_state)
        ks.append(k_l.reshape(bp, tp, N_KV_HEADS, HEAD_DIM))
        vs.append(v_l.reshape(bp, tp, N_KV_HEADS, HEAD_DIM))
        xs, _ = _layer(xs, bs, ts, mod3, 0, ts, lw, rope_tabs, (cache_k4, cache_v4, l), (state_dn, l), None)
    return (xp.reshape(bp, tp, d), xs.reshape(bs, ts, d), jnp.stack(ks, axis=1), jnp.stack(vs, axis=1),
            new_state)
```

```python
import functools

import jax
import jax.numpy as jnp
import numpy as np
from jax import lax
from jax.experimental import pallas as pl
from jax.experimental.pallas import tpu as pltpu

F32 = jnp.float32
BF16 = jnp.bfloat16

D_MODEL = 1024
EPS = 1e-6
GRID_W = 64
N_HEADS = 8
N_KV_HEADS = 2
Q_PER_KV = N_HEADS // N_KV_HEADS
HEAD_DIM = 128
ATTN_Q = N_HEADS * HEAD_DIM
ATTN_KV = N_KV_HEADS * HEAD_DIM
ROPE_THETA = 10000.0
CONV_WIDTH = 512
DN_HEADS = 4
DN_DK = 128
DN_DV = 128
DN_QK = DN_HEADS * DN_DK
DN_VW = DN_HEADS * DN_DV
D_FF = 2816

SUBLANES = 8
BF16_SUBLANES = 16
LANES = 128

COL_CONV = 0
COL_Q = COL_CONV + 3 * CONV_WIDTH
COL_K = COL_Q + ATTN_Q
COL_V = COL_K + ATTN_KV
COL_DN = COL_V + ATTN_KV
COL_Z = COL_DN + 3 * DN_QK
COL_SMALL = COL_Z + DN_VW
COL_GATE = COL_SMALL + 4 * DN_HEADS
D_IN = COL_GATE + 3 * D_MODEL
TAIL_GATE = 0
TAIL_SMALL = TAIL_GATE + 3 * D_MODEL
D_TAIL = TAIL_SMALL + LANES
REST_GATE = 0
REST_CONV = REST_GATE + 3 * D_MODEL
REST_Z = REST_CONV + 3 * CONV_WIDTH
D_REST = REST_Z + DN_VW

Q_SCALE = 1.4426950408889634 * HEAD_DIM ** -0.5
DN_CHUNK = 128
DN_BASE = 16
VMEM_LIMIT = 56 * 1024 * 1024


def _cparams(sem):
    return pltpu.CompilerParams(dimension_semantics=sem, vmem_limit_bytes=VMEM_LIMIT)


def _const_spec(shape):
    nd = len(shape)
    return pl.BlockSpec(shape, lambda *_: (0,) * nd, pipeline_mode=pl.Buffered(1))


def _layer_spec(shape, layer):
    return pl.BlockSpec((None,) + tuple(shape), lambda *_: (layer,) + (0,) * len(shape),
                        pipeline_mode=pl.Buffered(1))


def _dot(a, b):
    return jnp.dot(a, b, preferred_element_type=F32)


def _dot_nt(a, b):
    return lax.dot_general(a, b, (((1,), (1,)), ((), ())), preferred_element_type=F32)


def _dot_tn(a, b):
    return lax.dot_general(a, b, (((0,), (0,)), ((), ())), preferred_element_type=F32)


def _rms(x, g):
    return x * lax.rsqrt(jnp.mean(x * x, axis=-1, keepdims=True) + EPS) * g


def _sigmoid(x):
    return 1.0 / (1.0 + jnp.exp(-x))


def _silu(x):
    return x * _sigmoid(x)


def _softplus(x):
    return jnp.maximum(x, 0.0) + jnp.log1p(jnp.exp(-jnp.abs(x)))


def _split3(x):
    hi = x.astype(BF16)
    r = x - hi.astype(F32)
    mid = r.astype(BF16)
    lo = (r - mid.astype(F32)).astype(BF16)
    return hi, mid, lo


def _mod_kernel(cv_ref, w_ref, b_ref, o_ref):
    cv = cv_ref[...]
    o_ref[0] = _dot(_silu(cv).astype(BF16), w_ref[0].astype(BF16)) + b_ref[0]


def _modulation(cv, w_mod, b_mod):
    depth, d, n6 = w_mod.shape
    rows = cv.shape[0]
    tn = 1536
    return pl.pallas_call(
        _mod_kernel,
        grid=(depth, n6 // tn),
        in_specs=[pl.BlockSpec((rows, d), lambda l, j: (0, 0)),
                  pl.BlockSpec((1, d, tn), lambda l, j: (l, 0, j)),
                  pl.BlockSpec((1, 1, tn), lambda l, j: (l, 0, j))],
        out_specs=pl.BlockSpec((1, rows, tn), lambda l, j: (l, 0, j)),
        out_shape=jax.ShapeDtypeStruct((depth, rows, n6), F32),
        compiler_params=_cparams(("arbitrary", "arbitrary")),
    )(cv, w_mod, b_mod.reshape(depth, 1, n6))


def _swap_pairs(y):
    lane = lax.broadcasted_iota(jnp.int32, y.shape, 1)
    return jnp.where(lane % 2 == 0, pltpu.roll(y, LANES - 1, axis=1), pltpu.roll(y, 1, axis=1))


def _shift_rows(x, prev_row, next_row, seq_rows):
    tm = x.shape[0]
    rows = lax.broadcasted_iota(jnp.int32, (tm, 1), 0)
    below, above = pltpu.roll(x, 1, axis=0), pltpu.roll(x, tm - 1, axis=0)
    if seq_rows >= tm:
        return jnp.where(rows == 0, prev_row, below), jnp.where(rows == tm - 1, next_row, above)
    pos = rows % seq_rows
    return jnp.where(pos == 0, 0.0, below), jnp.where(pos == seq_rows - 1, 0.0, above)


def _interleave(*gens):
    gens = list(gens)
    while gens:
        for gen in list(gens):
            try:
                next(gen)
            except StopIteration:
                gens.remove(gen)


def _inproj_kernel(*refs, rope, want_v, tm, tps, seq_rows, tn):
    refs = list(refs)
    x_ref, xp_ref, xn_ref, mod_ref, g_ref, w_ref, wt_ref, gq_ref, gk_ref, cw_ref, ac_ref, bc_ref = refs[:12]
    refs = refs[12:]
    cos_ref, sin_ref = (refs.pop(0), refs.pop(0)) if rope else (None, None)
    q_out, k_out, v_out = refs[:3]
    refs = refs[3:]
    vf_out = refs.pop(0) if want_v else None
    qn_out, kn_out, vn_out, gb_out, gbt_out, rest_out = refs
    i = pl.program_id(0)

    def modnorm(x):
        return (_rms(x, g_ref[...]) * (1.0 + mod_ref[0, 1:2, :]) + mod_ref[0, 0:1, :]).astype(BF16)

    h = modnorm(x_ref[...])
    h_halo = modnorm(jnp.concatenate([xp_ref[...], xn_ref[...]], axis=0))

    def qk_head(x, g):
        y = _rms(x, g)
        if rope:
            y = y * cos_ref[...] + _swap_pairs(y) * sin_ref[...]
        return y

    def attn_epilogue():
        qkv = _dot(h, w_ref[:, COL_Q:COL_DN])
        yield
        for hh in range(N_HEADS):
            sl = slice(hh * HEAD_DIM, (hh + 1) * HEAD_DIM)
            q_out[:, sl] = (qk_head(qkv[:, sl], gq_ref[...]) * Q_SCALE).astype(q_out.dtype)
            if hh % 2 == 1:
                yield
        for hh in range(N_KV_HEADS):
            sl = slice(hh * HEAD_DIM, (hh + 1) * HEAD_DIM)
            k_out[:, sl] = qk_head(qkv[:, ATTN_Q + hh * HEAD_DIM:ATTN_Q + (hh + 1) * HEAD_DIM],
                                   gk_ref[...]).astype(k_out.dtype)
        v = qkv[:, ATTN_Q + ATTN_KV:]
        v_out[...] = v.T.astype(v_out.dtype)
        if want_v:
            vf_out[...] = v

    def dn_epilogue():
        dn = _dot(h, w_ref[:, COL_DN:COL_Z])
        dn_halo = _dot(h_halo, w_ref[:, COL_DN:COL_Z])
        gl = _dot(h, wt_ref[:, TAIL_SMALL:D_TAIL])
        yield
        prev_row = jnp.where((i % tps) == 0, 0.0, dn_halo[SUBLANES - 1:SUBLANES, :])
        next_row = jnp.where((i % tps) == tps - 1, 0.0, dn_halo[SUBLANES:SUBLANES + 1, :])
        dn_prev, dn_next = _shift_rows(dn, prev_row, next_row, seq_rows)
        cw = cw_ref[...]

        def conv_silu(sl):
            return _silu(cw[0:1, sl] * dn_prev[:, sl] + cw[1:2, sl] * dn[:, sl] + cw[2:3, sl] * dn_next[:, sl])

        for hh in range(DN_HEADS):
            sl = slice(hh * DN_DK, (hh + 1) * DN_DK)
            qh = conv_silu(sl)
            kh = conv_silu(slice(DN_QK + hh * DN_DK, DN_QK + (hh + 1) * DN_DK))
            qn_out[:, sl] = (qh * lax.rsqrt(jnp.sum(qh * qh, axis=-1, keepdims=True) + EPS)
                             * (DN_DK ** -0.5)).astype(qn_out.dtype)
            kn_out[:, sl] = (kh * lax.rsqrt(jnp.sum(kh * kh, axis=-1, keepdims=True) + EPS)).astype(kn_out.dtype)
            vn_out[:, sl] = conv_silu(slice(2 * DN_QK + hh * DN_DV, 2 * DN_QK + (hh + 1) * DN_DV)
                                      ).astype(vn_out.dtype)
            yield
        lane = lax.broadcasted_iota(jnp.int32, gl.shape, 1)
        gb = jnp.where(lane < 2 * DN_HEADS, _sigmoid(gl), -jnp.exp(ac_ref[...]) * _softplus(gl + bc_ref[...]))
        gb_out[...] = gb
        gbt_out[...] = gb.T

    def remaining():
        groups = [(wt_ref, TAIL_GATE, REST_GATE, 3 * D_MODEL), (w_ref, COL_CONV, REST_CONV, 3 * CONV_WIDTH),
                  (w_ref, COL_Z, REST_Z, DN_VW)]
        for ref, src, dst, width in groups:
            for j in range(0, width, tn):
                rest_out[:, dst + j:dst + j + tn] = _dot(h, ref[:, src + j:src + j + tn]).astype(rest_out.dtype)
                yield

    _interleave(dn_epilogue(), attn_epilogue(), remaining())


def _inproj(x, t, mod3, mod_base, rows_per_mod, lw, rope_tabs, want_v):
    n = x.shape[0]
    tm = 512 if rope_tabs is None else min(512, t)
    assert n % tm == 0 and (t % tm == 0 or tm % t == 0) and rows_per_mod % tm == 0
    tps = max(t // tm, 1)
    halo = tm // SUBLANES
    w_main, w_tail, layer = lw["w_in"], lw["w_tail"], lw["layer"]
    in_specs = [pl.BlockSpec((tm, D_MODEL), lambda i: (i, 0)),
                pl.BlockSpec((SUBLANES, D_MODEL), lambda i: (jnp.maximum(i * halo - 1, 0), 0)),
                pl.BlockSpec((SUBLANES, D_MODEL), lambda i: (jnp.minimum((i + 1) * halo, n // SUBLANES - 1), 0)),
                pl.BlockSpec((1, 6, D_MODEL), lambda i: (mod_base + (i * tm) // rows_per_mod, 0, 0)),
                _const_spec((1, D_MODEL)),
                _layer_spec((D_MODEL, COL_SMALL), layer), _layer_spec((D_MODEL, D_TAIL), layer),
                _const_spec((1, HEAD_DIM)), _const_spec((1, HEAD_DIM)),
                _const_spec((3, 3 * DN_QK)), _const_spec((1, LANES)), _const_spec((1, LANES))]
    args = [x, x, x, mod3, lw["g_pre1"], w_main, w_tail, lw["g_qn"], lw["g_kn"], lw["dn_conv_w"],
            lw["dn_ac"], lw["dn_bc"]]
    if rope_tabs is not None:
        in_specs += [pl.BlockSpec((tm, HEAD_DIM), lambda i: (i % tps, 0))] * 2
        args += list(rope_tabs)

    outs = [(ATTN_Q, BF16, False), (ATTN_KV, F32 if want_v else BF16, False), (ATTN_KV, BF16, True)]
    if want_v:
        outs.append((ATTN_KV, F32, False))
    outs += [(DN_QK, BF16, False), (DN_QK, BF16, False), (DN_VW, BF16, False), (LANES, F32, False),
             (LANES, F32, True), (D_REST, BF16, False)]
    res = pl.pallas_call(
        functools.partial(_inproj_kernel, rope=rope_tabs is not None, want_v=want_v, tm=tm, tps=tps, seq_rows=t,
                          tn=512),
        grid=(n // tm,),
        in_specs=in_specs,
        out_specs=[pl.BlockSpec((w, tm), lambda i: (0, i)) if tr else pl.BlockSpec((tm, w), lambda i: (i, 0))
                   for w, _, tr in outs],
        out_shape=[jax.ShapeDtypeStruct((w, n) if tr else (n, w), dt) for w, dt, tr in outs],
        compiler_params=_cparams(("arbitrary",)),
    )(*args)
    res = list(res)
    q_p, k_p, vt_p = res[:3]
    v_f32 = res[3] if want_v else None
    qn, kn, vn, gb, gbt, rest = res[-6:]
    return q_p, k_p, vt_p, v_f32, qn, kn, vn, gb, gbt, rest


def _attn_kernel(*refs, has_cache, t, tk, tq, nq, per_step):
    refs = list(refs)
    q_ref = refs.pop(0)
    qn_ref = refs.pop(0) if nq > 1 else None
    kc_ref, vc_ref = (refs.pop(0), refs.pop(0)) if has_cache else (None, None)
    k_ref, v_ref, o_ref = refs[:3]
    refs = refs[3:]
    vct_sc = refs.pop() if has_cache else None
    s_bufs, m_bufs = refs[:len(refs) // 2], refs[len(refs) // 2:]
    qi = pl.program_id(2)

    if has_cache:
        @pl.when(qi == 0)
        def _():
            vct_sc[...] = vc_ref[...].T.astype(BF16)

    segs = []
    if has_cache:
        past = kc_ref.shape[0]
        segs += [(kc_ref, vct_sc, r, min(tk, past - r)) for r in range(0, past, tk)]
    segs += [(k_ref, v_ref, r, tk) for r in range(0, t, tk)]
    offs = [sum(w for _, _, _, w in segs[:i]) for i in range(len(segs))]

    def stack(ref, row0=0):
        q = ref[row0:row0 + tq, :]
        return jnp.concatenate([q[:, g * HEAD_DIM:(g + 1) * HEAD_DIM] for g in range(Q_PER_KV)], axis=0)

    def scores(qs, s_ref, i, m_run):
        kr, _, r, w = segs[i]
        s = _dot_nt(kr[r:r + w, :].astype(BF16), qs)
        s_ref[offs[i]:offs[i] + w, :] = s
        for r0 in range(0, w, SUBLANES):
            blk = s[r0:r0 + SUBLANES, :]
            m_run = blk if m_run is None else jnp.maximum(m_run, blk)
        return m_run

    def weighted(s_ref, i, m, acc):
        _, vr, r, w = segs[i]
        p = jnp.exp2(s_ref[offs[i]:offs[i] + w, :] - m)
        den = p[0:SUBLANES, :]
        for r0 in range(SUBLANES, w, SUBLANES):
            den = den + p[r0:r0 + SUBLANES, :]
        pv = _dot(vr[:, r:r + w].astype(BF16), p.astype(BF16))
        return (pv, den) if acc is None else (acc[0] + pv, acc[1] + den)

    def finish(acc, row0=0):
        o = (acc[0] / jnp.sum(acc[1], axis=0, keepdims=True)).T
        for g in range(Q_PER_KV):
            o_ref[row0:row0 + tq, g * HEAD_DIM:(g + 1) * HEAD_DIM] = o[g * tq:(g + 1) * tq].astype(o_ref.dtype)

    if nq > 1:
        @pl.when(qi == 0)
        def _():
            qs0 = stack(q_ref)
            m_run = None
            for i in range(len(segs)):
                m_run = scores(qs0, s_bufs[0], i, m_run)
            m_bufs[0][...] = m_run

        def step(s_cur, m_cur, s_nxt, m_nxt, qs_next, row0):
            m = jnp.max(m_cur[...], axis=0, keepdims=True)
            acc, m_run = None, None
            for i in range(len(segs)):
                m_run = scores(qs_next, s_nxt, i, m_run)
                acc = weighted(s_cur, i, m, acc)
            m_nxt[...] = m_run
            finish(acc, row0)

        for u in range(per_step):
            qs_next = stack(q_ref, (u + 1) * tq) if u + 1 < per_step else stack(qn_ref)
            step(s_bufs[u % 2], m_bufs[u % 2], s_bufs[1 - u % 2], m_bufs[1 - u % 2], qs_next, u * tq)
    else:
        qs = stack(q_ref)
        m_run = None
        for i in range(len(segs)):
            m_run = scores(qs, s_bufs[0], i, m_run)
        m = jnp.max(m_run, axis=0, keepdims=True)
        acc = None
        for i in range(len(segs)):
            acc = weighted(s_bufs[0], i, m, acc)
        finish(acc)


def _attention(q_p, k_p, vt_p, b, t, cache):
    n = q_p.shape[0]
    tq = 128 if t > 256 else 256
    tk = min(t, 512)
    nq = t // tq
    per_step = next(p for p in (8, 4, 2, 1) if nq % p == 0)
    assert nq == 1 or per_step > 1
    steps = nq // per_step
    qw = Q_PER_KV * HEAD_DIM
    in_specs = [pl.BlockSpec((per_step * tq, qw), lambda bi, j, qi: (bi * steps + qi, j))]
    args = [q_p]
    if nq > 1:
        in_specs.append(pl.BlockSpec((tq, qw),
                                     lambda bi, j, qi: (bi * nq + jnp.minimum(per_step * (qi + 1), nq - 1), j)))
        args.append(q_p)
    if cache is not None:
        cache_k, cache_v, layer = cache
        past = cache_k.shape[2]
        cspec = pl.BlockSpec((None, None, past, HEAD_DIM), lambda bi, j, qi: (bi, layer, 0, j))
        in_specs += [cspec, cspec]
        args += [cache_k, cache_v]
    in_specs += [pl.BlockSpec((t, HEAD_DIM), lambda bi, j, qi: (bi, j)),
                 pl.BlockSpec((HEAD_DIM, t), lambda bi, j, qi: (j, bi))]
    args += [k_p, vt_p]
    n_keys = t + (cache[0].shape[2] if cache is not None else 0)
    slots = 2 if nq > 1 else 1
    return pl.pallas_call(
        functools.partial(_attn_kernel, has_cache=cache is not None, t=t, tk=tk, tq=tq, nq=nq,
                          per_step=per_step),
        grid=(b, N_KV_HEADS, steps),
        in_specs=in_specs,
        out_specs=pl.BlockSpec((per_step * tq, qw), lambda bi, j, qi: (bi * steps + qi, j)),
        out_shape=jax.ShapeDtypeStruct((n, ATTN_Q), BF16),
        scratch_shapes=([pltpu.VMEM((n_keys, Q_PER_KV * tq), F32)] * slots
                        + [pltpu.VMEM((SUBLANES, Q_PER_KV * tq), F32)] * slots
                        + ([pltpu.VMEM((HEAD_DIM, cache[0].shape[2]), BF16)] if cache is not None else [])),
        compiler_params=_cparams(("arbitrary", "arbitrary", "arbitrary")),
    )(*args)


def _lane_pick(x, lane):
    idx = lax.broadcasted_iota(jnp.int32, x.shape, 1)
    return jnp.sum(jnp.where(idx == lane, x, 0.0), axis=-1, keepdims=True)


def _block_diag(x2):
    xb = x2.astype(BF16)
    z = jnp.zeros((DN_CHUNK, DN_CHUNK), BF16)
    return jnp.concatenate([jnp.concatenate([xb[:, :DN_CHUNK], z], axis=1),
                            jnp.concatenate([z, xb[:, DN_CHUNK:]], axis=1)], axis=0)


def _mm_pair(x2, y2):
    return _dot(x2.astype(BF16), _block_diag(y2))


def _unit_tri_inverses(lmats, ri, ci):
    def blk(s):
        return (ri ^ ci) < s

    eye = jnp.where(ri == ci, 1.0, 0.0)
    ps = [jnp.where(blk(DN_BASE), -lm, 0.0) for lm in lmats]
    xs = [eye + p for p in ps]
    s = 2
    while s < DN_BASE:
        ps = [_mm_pair(p, p) for p in ps]
        yield
        xs = [x + _mm_pair(x, p) for x, p in zip(xs, ps)]
        yield
        s *= 2
    s = DN_BASE
    while s < DN_CHUNK:
        sel = blk(2 * s) & jnp.logical_not(blk(s))
        ts = [_mm_pair(jnp.where(sel, lm, 0.0), x) for lm, x in zip(lmats, xs)]
        yield
        xs = [x - _mm_pair(x, t) for x, t in zip(xs, ts)]
        yield
        s *= 2
    return xs


def _dn_kernel(*refs, reverse, has_s0, want_state, n_seq, cps, d, nt):
    nc = n_seq * cps
    refs = list(refs)
    q_ref, k_ref, v_ref, gb_ref, gr_ref = refs[:5]
    refs = refs[5:]
    s0_ref = refs.pop(0) if has_s0 else None
    if want_state:
        refs.pop(0)
    o_ref = refs.pop(0)
    sfin_ref = refs.pop(0) if want_state else None
    s_sc = refs.pop(0)
    bufs = (refs[:5], refs[5:10])
    n_pairs = DN_HEADS // 2
    c2 = 2 * DN_CHUNK
    tt = nc * DN_CHUNK
    insts = [(c, p) for c in range(nc) for p in range(n_pairs)]

    g = pl.program_id(0)
    i_scan = (g - 1) % nt

    def pair_cols(x, lane_a, lane_b):
        shape = (x.shape[0], DN_CHUNK)
        return jnp.concatenate([jnp.broadcast_to(_lane_pick(x, lane_a), shape),
                                jnp.broadcast_to(_lane_pick(x, lane_b), shape)], axis=1)

    def prepare(buf):
        u_buf, wq_buf, attn_buf, kt_buf, gt_buf = buf
        gb = gb_ref[...]
        g_r = gr_ref[...]
        bi = lax.broadcasted_iota(jnp.int32, (tt, tt), 0)
        bj = lax.broadcasted_iota(jnp.int32, (tt, tt), 1)
        same = (bi ^ bj) < DN_CHUNK
        if reverse:
            tri_c = jnp.where(same & (bj >= bi), 1.0, 0.0).astype(BF16)
            tri_r = jnp.where(same & (bi >= bj), 1.0, 0.0).astype(BF16)
        else:
            tri_c = jnp.where(same & (bj <= bi), 1.0, 0.0).astype(BF16)
            tri_r = jnp.where(same & (bi <= bj), 1.0, 0.0).astype(BF16)
        gcum_c = sum(_dot(tri_c, part) for part in _split3(gb))
        gcum_r = sum(_dot(part, tri_r) for part in _split3(g_r))
        yield

        ri = lax.broadcasted_iota(jnp.int32, (DN_CHUNK, c2), 0)
        ci = lax.broadcasted_iota(jnp.int32, (DN_CHUNK, c2), 1) & (DN_CHUNK - 1)
        incl = (ci >= ri) if reverse else (ci <= ri)
        strict = (ci > ri) if reverse else (ci < ri)
        pre = []
        for c, p in insts:
            rs = slice(c * DN_CHUNK, (c + 1) * DN_CHUNK)
            cs = slice(p * c2, (p + 1) * c2)
            lane_b = d * DN_HEADS + 2 * p
            lane_g = 2 * DN_HEADS + d * DN_HEADS + 2 * p
            q2 = q_ref[rs, cs].astype(F32)
            k2 = k_ref[rs, cs].astype(F32)
            v2 = v_ref[rs, cs].astype(F32)
            g_i = pair_cols(gcum_c[rs, :], lane_g, lane_g + 1)
            b_i = pair_cols(gb[rs, :], lane_b, lane_b + 1)
            g_j = jnp.concatenate([gcum_r[lane_g:lane_g + 1, rs], gcum_r[lane_g + 1:lane_g + 2, rs]], axis=1)
            g_tot = g_i[0:1, :] if reverse else g_i[DN_CHUNK - 1:DN_CHUNK, :]
            decay = jnp.where(incl, jnp.exp(jnp.where(incl, g_i - g_j, 0.0)), 0.0)
            pre.append(dict(q2=q2, k2=k2, v2=v2, g_i=g_i, b_i=b_i, g_tot=g_tot, decay=decay, kb=k2 * b_i))
        a2s = [_dot_nt(jnp.concatenate([p["kb"], p["q2"]], axis=0).astype(BF16), _block_diag(p["k2"]))
               for p in pre]
        yield
        lmats = [jnp.where(strict, a2[:DN_CHUNK] * p["decay"], 0.0) for a2, p in zip(a2s, pre)]
        for n, (a2, p) in enumerate(zip(a2s, pre)):
            attn_buf[n] = (a2[DN_CHUNK:] * p["decay"]).astype(BF16)
        tinvs = yield from _unit_tri_inverses(lmats, ri, ci)
        gams = [jnp.exp(p["g_i"]) for p in pre]
        us = [_mm_pair(t, p["v2"] * p["b_i"]) for t, p in zip(tinvs, pre)]
        yield
        ws = [_mm_pair(t, p["kb"] * gam) for t, p, gam in zip(tinvs, pre, gams)]
        yield
        for n, (p, u, w, gam) in enumerate(zip(pre, us, ws, gams)):
            u_buf[n] = u
            wq_buf[n] = jnp.concatenate([w, p["q2"] * gam], axis=0).astype(BF16)
            kt_buf[n] = (p["k2"] * jnp.exp(p["g_tot"] - p["g_i"])).astype(BF16)
            gt_buf[n] = jnp.broadcast_to(jnp.exp(p["g_tot"]), (SUBLANES, c2))

    def scan(buf):
        u_buf, wq_buf, attn_buf, kt_buf, gt_buf = buf
        si = lax.broadcasted_iota(jnp.int32, (c2, c2), 0)
        sj = lax.broadcasted_iota(jnp.int32, (c2, c2), 1)
        on_diag = (si < DN_CHUNK) == (sj < DN_CHUNK)
        chains = [(sq, p) for sq in range(n_seq) for p in range(n_pairs)]
        states = [s_sc[sq * n_pairs + p] for sq, p in chains]
        for k in (range(cps - 1, -1, -1) if reverse else range(cps)):
            cs = [sq * cps + k for sq, _ in chains]
            ns = [c * n_pairs + p for c, (_, p) in zip(cs, chains)]
            m1s = [_dot(wq_buf[n], s.astype(BF16)) for n, s in zip(ns, states)]
            yield
            v_news = [u_buf[n] - m1[:DN_CHUNK] for n, m1 in zip(ns, m1s)]
            outs = [m1[DN_CHUNK:] + _mm_pair(attn_buf[n], v) for n, m1, v in zip(ns, m1s, v_news)]
            yield
            states = [s * gt_buf[n][0:1, :] + jnp.where(on_diag, _dot_tn(kt_buf[n], v.astype(BF16)), 0.0)
                      for n, s, v in zip(ns, states, v_news)]
            for c, (_, p), out in zip(cs, chains, outs):
                o_ref[c * DN_CHUNK:(c + 1) * DN_CHUNK, p * c2:(p + 1) * c2] = out
            yield
        for (sq, p), s in zip(chains, states):
            s_sc[sq * n_pairs + p] = s

    def step(par):
        @pl.when(i_scan == 0)
        def _():
            s_sc[...] = jnp.zeros_like(s_sc)
            if has_s0:
                for h in range(DN_HEADS):
                    lo = (h % 2) * DN_CHUNK
                    s_sc[h // 2, lo:lo + DN_CHUNK, lo:lo + DN_CHUNK] = s0_ref[h]

        _interleave(prepare(bufs[par]), scan(bufs[1 - par]))

        if want_state:
            @pl.when(i_scan == nt - 1)
            def _():
                for sq in range(n_seq):
                    for h in range(DN_HEADS):
                        lo = (h % 2) * DN_CHUNK
                        sfin_ref[sq, h] = s_sc[sq * n_pairs + h // 2, lo:lo + DN_CHUNK, lo:lo + DN_CHUNK]

    pl.when(g == 0)(lambda: _interleave(prepare(bufs[0])))
    pl.when((g > 0) & (g % 2 == 0))(functools.partial(step, 0))
    pl.when(g % 2 == 1)(functools.partial(step, 1))


def _deltanet_dir(qn, kn, vn, gb, gbt, s0, b, t, d, want_state):
    n = qn.shape[0]
    chunks_per_tile = 4
    cps = min(chunks_per_tile, t // DN_CHUNK)
    n_seq = chunks_per_tile // cps
    assert b % n_seq == 0 and (s0 is None or n_seq == 1)
    nc = n_seq * cps
    tt = nc * DN_CHUNK
    nt = t // (cps * DN_CHUNK)
    n_tiles = (b // n_seq) * nt
    reverse = d == 1
    c2 = 2 * DN_CHUNK
    n_inst = nc * (DN_HEADS // 2)

    def seq(gt):
        bi, i = gt // nt, gt % nt
        return bi, ((nt - 1 - i) if reverse else i)

    def prep_tile(g):
        return seq(jnp.minimum(g, n_tiles - 1))

    def scan_tile(g):
        return seq(jnp.maximum(g - 1, 0))

    def row_block(bt):
        return bt[0] * nt + bt[1]

    qkv_spec = pl.BlockSpec((tt, DN_QK), lambda g: (row_block(prep_tile(g)), 0))
    in_specs = [
        qkv_spec, qkv_spec, qkv_spec,
        pl.BlockSpec((tt, LANES), lambda g: (row_block(prep_tile(g)), 0)),
        pl.BlockSpec((4 * DN_HEADS, tt), lambda g: (0, row_block(prep_tile(g)))),
    ]
    args = [qn, kn, vn, gb, gbt]
    if s0 is not None:
        state, layer = s0
        in_specs.append(pl.BlockSpec((None, None, None, DN_HEADS, DN_DK, DN_DV),
                                     lambda g: (scan_tile(g)[0], layer, d, 0, 0, 0)))
        args.append(state)
    out_specs = [pl.BlockSpec((tt, DN_VW), lambda g: (row_block(scan_tile(g)), 0))]
    out_shape = [jax.ShapeDtypeStruct((n, DN_VW), F32)]
    aliases = {}
    if want_state:
        state_out, out_layer = want_state
        in_specs.append(pl.BlockSpec(memory_space=pl.ANY))
        args.append(state_out)
        aliases = {len(args) - 1: 1}
        out_specs.append(pl.BlockSpec((n_seq, None, None, DN_HEADS, DN_DK, DN_DV),
                                      lambda g: (scan_tile(g)[0], out_layer, d, 0, 0, 0)))
        out_shape.append(jax.ShapeDtypeStruct(state_out.shape, state_out.dtype))
    prepared = [pltpu.VMEM((n_inst, DN_CHUNK, c2), F32), pltpu.VMEM((n_inst, c2, c2), BF16),
                pltpu.VMEM((n_inst, DN_CHUNK, c2), BF16), pltpu.VMEM((n_inst, DN_CHUNK, c2), BF16),
                pltpu.VMEM((n_inst, SUBLANES, c2), F32)]
    res = pl.pallas_call(
        functools.partial(_dn_kernel, reverse=reverse, has_s0=s0 is not None, want_state=bool(want_state),
                          n_seq=n_seq, cps=cps, d=d, nt=nt),
        grid=(n_tiles + 1,),
        in_specs=in_specs,
        out_specs=out_specs,
        out_shape=out_shape,
        input_output_aliases=aliases,
        scratch_shapes=[pltpu.VMEM((n_seq * (DN_HEADS // 2), c2, c2), F32)] + prepared + prepared,
        compiler_params=_cparams(("arbitrary",)),
    )(*args)
    return res if want_state else (res[0], None)


def _post_kernel(x_ref, mod_ref, cb_ref, cc_ref, cx_ref, ccp_ref, cxp_ref, ccn_ref, cxn_ref, cw_ref,
                 att_ref, of_ref, ob_ref, z_ref, ng_ref, ga_ref, gb_ref, gc_ref, wpa_ref, wpb_ref, wpc_ref, wo_ref,
                 gpost_ref, o_ref, *, tm, tps, seq_rows):
    i = pl.program_id(0)
    first = (i % tps) == 0
    last = (i % tps) == tps - 1
    u = cc_ref[...].astype(F32) * cx_ref[...].astype(F32)
    hl = BF16_SUBLANES - 1
    prev_row = jnp.where(first, 0.0, ccp_ref[hl:hl + 1, :].astype(F32) * cxp_ref[hl:hl + 1, :].astype(F32))
    next_row = jnp.where(last, 0.0, ccn_ref[0:1, :].astype(F32) * cxn_ref[0:1, :].astype(F32))
    u_prev, u_next = _shift_rows(u, prev_row, next_row, seq_rows)
    cw = cw_ref[...]
    conv =cw[0:1, :] * u_prev + cw[1:2, :] * u + cw[2:3, :] * u_next
    ya = _dot((cb_ref[...].astype(F32) * conv).astype(BF16), wpa_ref[...])
    yb = _dot(att_ref[...], wpb_ref[...])
    o = of_ref[...] + ob_ref[...]
    z = z_ref[...].astype(F32)
    parts = []
    for h in range(DN_HEADS):
        sl = slice(h * DN_DV, (h + 1) * DN_DV)
        parts.append((_rms(o[:, sl], ng_ref[...]) * _silu(z[:, sl])).astype(BF16))
    yc = _dot(jnp.concatenate(parts, axis=1), wpc_ref[...])
    mix_in = (_sigmoid(ga_ref[...].astype(F32)) * ya + _sigmoid(gb_ref[...].astype(F32)) * yb
              + _sigmoid(gc_ref[...].astype(F32)) * yc)
    mix = _dot(mix_in.astype(BF16), wo_ref[...])
    o_ref[...] = x_ref[...] + mod_ref[0, 2:3, :] * _rms(mix, gpost_ref[...])


def _post_mixer(x, mod3, mod_base, rows_per_mod, t, proj, att, o_f, o_b, conv_w, ng, wpa, wpb, wpc, wo, layer,
                g_post1):
    n = x.shape[0]
    tm = 512
    assert n % tm == 0 and (t % tm == 0 or tm % t == 0) and rows_per_mod % tm == 0
    tps = max(t // tm, 1)
    hb = tm // BF16_SUBLANES
    cwid = CONV_WIDTH
    c0 = REST_CONV // cwid

    def prev(i):
        return jnp.maximum(i * hb - 1, 0)

    def nxt(i):
        return jnp.minimum((i + 1) * hb, n // BF16_SUBLANES - 1)

    g0 = REST_GATE // D_MODEL
    in_specs = [
        pl.BlockSpec((tm, D_MODEL), lambda i: (i, 0)),
        pl.BlockSpec((1, 6, D_MODEL), lambda i: (mod_base + (i * tm) // rows_per_mod, 0, 0)),
        pl.BlockSpec((tm, cwid), lambda i: (i, c0)),
        pl.BlockSpec((tm, cwid), lambda i: (i, c0 + 1)),
        pl.BlockSpec((tm, cwid), lambda i: (i, c0 + 2)),
        pl.BlockSpec((BF16_SUBLANES, cwid), lambda i: (prev(i), c0 + 1)),
        pl.BlockSpec((BF16_SUBLANES, cwid), lambda i: (prev(i), c0 + 2)),
        pl.BlockSpec((BF16_SUBLANES, cwid), lambda i: (nxt(i), c0 + 1)),
        pl.BlockSpec((BF16_SUBLANES, cwid), lambda i: (nxt(i), c0 + 2)),
        _const_spec((3, cwid)),
        pl.BlockSpec((tm, ATTN_Q), lambda i: (i, 0)),
        pl.BlockSpec((tm, DN_VW), lambda i: (i, 0)),
        pl.BlockSpec((tm, DN_VW), lambda i: (i, 0)),
        pl.BlockSpec((tm, DN_VW), lambda i: (i, REST_Z // DN_VW)),
        _const_spec((1, DN_DV)),
        pl.BlockSpec((tm, D_MODEL), lambda i: (i, g0)),
        pl.BlockSpec((tm, D_MODEL), lambda i: (i, g0 + 1)),
        pl.BlockSpec((tm, D_MODEL), lambda i: (i, g0 + 2)),
        _layer_spec((CONV_WIDTH, D_MODEL), layer), _layer_spec((ATTN_Q, D_MODEL), layer),
        _layer_spec((DN_VW, D_MODEL), layer), _layer_spec((D_MODEL, D_MODEL), layer),
        _const_spec((1, D_MODEL)),
    ]
    return pl.pallas_call(
        functools.partial(_post_kernel, tm=tm, tps=tps, seq_rows=t),
        grid=(n // tm,),
        in_specs=in_specs,
        out_specs=pl.BlockSpec((tm, D_MODEL), lambda i: (i, 0)),
        out_shape=jax.ShapeDtypeStruct((n, D_MODEL), F32),
        compiler_params=_cparams(("arbitrary",)),
    )(x, mod3, proj, proj, proj, proj, proj, proj, proj, conv_w, att, o_f, o_b, proj, ng,
      proj, proj, proj, wpa, wpb, wpc, wo, g_post1)


def _ffn_kernel(x_ref, mod_ref, gpre_ref, wg_ref, wu_ref, wd_ref, gpost_ref, o_ref, act_sc, *, tf):
    x = x_ref[...]
    h2 = (_rms(x, gpre_ref[...]) * (1.0 + mod_ref[0, 4:5, :]) + mod_ref[0, 3:4, :]).astype(BF16)
    for j in range(0, D_FF, tf):
        gate = _dot(h2, wg_ref[:, j:j + tf])
        up = _dot(h2, wu_ref[:, j:j + tf])
        act_sc[:, j:j + tf] = (_silu(gate) * up).astype(BF16)
    ffn = _dot(act_sc[...], wd_ref[...])
    o_ref[...] = x + mod_ref[0, 5:6, :] * _rms(ffn, gpost_ref[...])


def _ffn(x, mod3, mod_base, rows_per_mod, g_pre2, wg, wu, wd, layer, g_post2):
    n = x.shape[0]
    tm = min(1024, n)
    assert n % tm == 0 and rows_per_mod % tm == 0
    return pl.pallas_call(
        functools.partial(_ffn_kernel, tf=256),
        grid=(n // tm,),
        in_specs=[pl.BlockSpec((tm, D_MODEL), lambda i: (i, 0)),
                  pl.BlockSpec((1, 6, D_MODEL), lambda i: (mod_base + (i * tm) // rows_per_mod, 0, 0)),
                  _const_spec((1, D_MODEL)),
                  _layer_spec((D_MODEL, D_FF), layer), _layer_spec((D_MODEL, D_FF), layer),
                  _layer_spec((D_FF, D_MODEL), layer),
                  _const_spec((1, D_MODEL))],
        out_specs=pl.BlockSpec((tm, D_MODEL), lambda i: (i, 0)),
        out_shape=jax.ShapeDtypeStruct((n, D_MODEL), F32),
        scratch_shapes=[pltpu.VMEM((tm, D_FF), BF16)],
        compiler_params=_cparams(("arbitrary",)),
    )(x, mod3, g_pre2, wg, wu, wd, g_post2)


def _rope_tables(t):
    rows = t // GRID_W
    row_id = np.repeat(np.arange(rows, dtype=np.float32), GRID_W)
    col_id = np.tile(np.arange(GRID_W, dtype=np.float32), rows)
    n_freq = HEAD_DIM // 4
    inv_freq = (np.float32(ROPE_THETA) ** (-np.arange(n_freq, dtype=np.float32) / np.float32(n_freq))).astype(np.float32)
    ang = np.concatenate([row_id[:, None] * inv_freq, col_id[:, None] * inv_freq], axis=-1).astype(np.float32)
    cos = np.repeat(np.cos(ang), 2, axis=-1).astype(np.float32)
    sin = np.repeat(np.sin(ang), 2, axis=-1).astype(np.float32)
    sign = np.tile(np.array([-1.0, 1.0], np.float32), HEAD_DIM // 2)
    return jnp.asarray(cos), jnp.asarray(sin * sign)


def _layer(x, b, t, mod3, mod_base, rows_per_mod, lw, rope_tabs, cache, state0, want_state):
    q_p, k_p, vt_p, v_f32, qn, kn, vn, gb, gbt, proj = _inproj(x, t, mod3, mod_base, rows_per_mod, lw,
                                                               rope_tabs, want_state is not None)
    att = _attention(q_p, k_p, vt_p, b, t, cache)
    dn = []
    for d in range(2):
        o_d, state_buf = _deltanet_dir(qn, kn, vn, gb, gbt, state0, b, t, d,
                                       (want_state, lw["layer"]) if want_state is not None else None)
        want_state = state_buf if want_state is not None else None
        dn.append((o_d, state_buf))
    x = _post_mixer(x, mod3, mod_base, rows_per_mod, t, proj, att, dn[0][0], dn[1][0], lw["conv_w"],
                    lw["dn_norm_g"], lw["w_pa"], lw["w_pb"], lw["w_pc"], lw["w_o"], lw["layer"], lw["g_post1"])
    x = _ffn(x, mod3, mod_base, rows_per_mod, lw["g_pre2"], lw["w_gate"], lw["w_up"], lw["w_down"],
             lw["layer"], lw["g_post2"])
    if want_state is None:
        return x, None
    return x, (k_p, v_f32, want_state)


def kernel(x_prompt, x_sample, cache_k, cache_v, state_dn, c, c_ctx, w_mod, b_mod, g_pre1, g_post1, g_pre2, g_post2, w_in, conv_w, g_qn, g_kn, dn_conv_w, dn_a_log, dn_dt_bias, dn_norm_g, w_pa, w_pb, w_pc, w_o, w_gate, w_up, w_down):
    bp, tp, d = x_prompt.shape
    bs, ts, _ = x_sample.shape
    depth = w_mod.shape[0]
    past = cache_k.shape[2]

    mod_rows = -(-(bs + 1) // SUBLANES) * SUBLANES
    cv = jnp.zeros((mod_rows, d), F32).at[:bs].set(c).at[bs].set(c_ctx)
    mod_all = _modulation(cv, w_mod, b_mod).reshape(depth, mod_rows, 6, d)

    stacked = {"w_pa": w_pa.astype(BF16), "w_pb": w_pb.astype(BF16), "w_pc": w_pc.astype(BF16),
               "w_o": w_o.astype(BF16), "w_gate": w_gate.astype(BF16), "w_up": w_up.astype(BF16),
               "w_down": w_down.astype(BF16)}
    w_main = w_in.astype(BF16)
    w_tail = jnp.concatenate(
        [w_in[:, :, COL_GATE:D_IN], w_in[:, :, COL_SMALL:COL_GATE],
         jnp.zeros((depth, d, LANES - 4 * DN_HEADS), w_in.dtype)], axis=-1).astype(BF16)
    lane_pad = ((0, 0), (2 * DN_HEADS, LANES - 4 * DN_HEADS))
    a_flat = dn_a_log.reshape(depth, 2 * DN_HEADS)
    b_flat = dn_dt_bias.reshape(depth, 2 * DN_HEADS)
    rope_tabs = _rope_tables(ts)
    cache_k4 = cache_k.reshape(bs, depth, past, ATTN_KV)
    cache_v4 = cache_v.reshape(bs, depth, past, ATTN_KV)

    xp = x_prompt.reshape(bp * tp, d)
    xs = x_sample.reshape(bs * ts, d)
    ks, vs = [], []
    new_state = jnp.zeros((bp, depth, 2, DN_HEADS, DN_DK, DN_DV), F32)
    for l in range(depth):
        lw = {
            "g_pre1": g_pre1[l][None], "g_post1": g_post1[l][None], "g_pre2": g_pre2[l][None],
            "g_post2": g_post2[l][None], "layer": l, "w_in": w_main, "w_tail": w_tail, "conv_w": conv_w[l],
            "g_qn": g_qn[l][None], "g_kn": g_kn[l][None], "dn_conv_w": dn_conv_w[l],
            "dn_ac": jnp.pad(a_flat[l][None], lane_pad), "dn_bc": jnp.pad(b_flat[l][None], lane_pad),
            "dn_norm_g": dn_norm_g[l][None],
            **stacked,
        }
        mod3 = mod_all[l]
        xp, (k_l, v_l, new_state) = _layer(xp, bp, tp, mod3, bs, bp * tp, lw, None, None, None, new_state)
        ks.append(k_l.reshape(bp, tp, N_KV_HEADS, HEAD_DIM))
        vs.append(v_l.reshape(bp, tp, N_KV_HEADS, HEAD_DIM))
        xs, _ = _layer(xs, bs, ts, mod3, 0, ts, lw, rope_tabs, (cache_k4, cache_v4, l), (state_dn, l), None)
    return (xp.reshape(bp, tp, d), xs.reshape(bs, ts, d), jnp.stack(ks, axis=1), jnp.stack(vs, axis=1),
            new_state)
```
